```python
import jax, jax.numpy as jnp
from jax import lax
import numpy as np

D_MODEL = 1024
BATCH = 16
SEQ = 4096
DEPTH = 4

CHUNK = 128
A_WIDTH = D_MODEL
A_HEADS = 8
A_HEAD_DIM = A_WIDTH // A_HEADS
B_WIDTH = D_MODEL
CONV_WIDTH = 31
FFN_HIDDEN = 4 * D_MODEL
N_BRANCH = 2
N_MOD = 6
IN_COLS = 2 * A_WIDTH + 2 * B_WIDTH + N_BRANCH * D_MODEL
EPS = 1e-6

kernel_name = "hybrid_gmlp_conformer_gated_adaln"


def rmsnorm(x, g):
    xf = x.astype(jnp.float32)
    y = xf * lax.rsqrt(jnp.mean(xf * xf, axis=-1, keepdims=True) + EPS)
    return (y * g.astype(jnp.float32)).astype(x.dtype)


def layernorm(x, g, b):
    xf = x.astype(jnp.float32)
    mu = jnp.mean(xf, axis=-1, keepdims=True)
    xc = xf - mu
    var = jnp.mean(xc * xc, axis=-1, keepdims=True)
    y = xc * lax.rsqrt(var + EPS) * g.astype(jnp.float32) + b.astype(jnp.float32)
    return y.astype(x.dtype)


def chunked_spatial_gating(u, v, ln_g, ln_b, w_s, b_s):
    bsz, t, _ = v.shape
    v = layernorm(v, ln_g, ln_b)
    v = v.reshape(bsz, t // CHUNK, CHUNK, A_HEADS, A_HEAD_DIM)
    causal = jnp.tril(jnp.ones((CHUNK, CHUNK), dtype=bool))
    w = jnp.where(causal[None], w_s, 0).astype(v.dtype)
    s = jnp.einsum('hij,bnjhd->bnihd', w, v) + b_s.T.astype(v.dtype)[None, None, :, :, None]
    return u * s.reshape(bsz, t, A_WIDTH)


def conformer_conv(p, conv_w, conv_b, ln_g, ln_b):
    a, g = jnp.split(p, 2, axis=-1)
    z = a * jax.nn.sigmoid(g)
    z = lax.conv_general_dilated(
        z, conv_w.astype(z.dtype), window_strides=(1,),
        padding=[(CONV_WIDTH - 1, 0)],
        dimension_numbers=('NWC', 'WIO', 'NWC'),
        feature_group_count=B_WIDTH) + conv_b.astype(z.dtype)
    z = layernorm(z, ln_g, ln_b)
    return jax.nn.silu(z)


def _fwd_setup_inputs(seed: int = 0) -> dict:
    key = jax.random.key(seed)
    ks = jax.random.split(key, 24)
    f32 = jnp.float32
    L, D = DEPTH, D_MODEL

    def nrm(k, shape, scale):
        return jax.random.normal(k, shape, f32) * scale

    return {
        "x": nrm(ks[0], (BATCH, SEQ, D), 1.0),
        "c": nrm(ks[1], (BATCH, D), 1.0),
        "w_ada": nrm(ks[2], (L, D, N_MOD * D), 0.5 * D ** -0.5),
        "b_ada": nrm(ks[3], (L, N_MOD * D), 0.02),
        "norm1_g": 1.0 + nrm(ks[4], (L, D), 0.05),
        "w_in": nrm(ks[5], (L, D, IN_COLS), D ** -0.5),
        "a_ln_g": 1.0 + nrm(ks[6], (L, A_WIDTH), 0.05),
        "a_ln_b": nrm(ks[7], (L, A_WIDTH), 0.02),
        "a_ws": nrm(ks[8], (L, A_HEADS, CHUNK, CHUNK), CHUNK ** -0.5),
        "a_bs": 1.0 + nrm(ks[9], (L, A_HEADS, CHUNK), 0.1),
        "w_pa": nrm(ks[10], (L, A_WIDTH, D), A_WIDTH ** -0.5),
        "b_conv_w": nrm(ks[11], (L, CONV_WIDTH, 1, B_WIDTH), CONV_WIDTH ** -0.5),
        "b_conv_b": nrm(ks[12], (L, B_WIDTH), 0.02),
        "b_ln_g": 1.0 + nrm(ks[13], (L, B_WIDTH), 0.05),
        "b_ln_b": nrm(ks[14], (L, B_WIDTH), 0.02),
        "w_pb": nrm(ks[15], (L, B_WIDTH, D), B_WIDTH ** -0.5),
        "w_out": nrm(ks[16], (L, D, D), D ** -0.5),
        "norm2_g": 1.0 + nrm(ks[17], (L, D), 0.05),
        "w_ff1": nrm(ks[18], (L, D, FFN_HIDDEN), D ** -0.5),
        "w_ff2": nrm(ks[19], (L, FFN_HIDDEN, D), FFN_HIDDEN ** -0.5),
        "final_g": 1.0 + nrm(ks[20], (D,), 0.05),
    }


def _fwd_reference(x, c, w_ada, b_ada, norm1_g, w_in, a_ln_g, a_ln_b, a_ws, a_bs, w_pa,
              b_conv_w, b_conv_b, b_ln_g, b_ln_b, w_pb, w_out, norm2_g,
              w_ff1, w_ff2, final_g):
    split_at = [A_WIDTH, 2 * A_WIDTH, 2 * A_WIDTH + 2 * B_WIDTH,
                2 * A_WIDTH + 2 * B_WIDTH + D_MODEL]
    c_act = jax.nn.silu(c)
    for l in range(DEPTH):
        mod = (c_act @ w_ada[l] + b_ada[l])[:, None, :]
        sh1, sc1, gt1, sh2, sc2, gt2 = jnp.split(mod, N_MOD, axis=-1)

        h = rmsnorm(x, norm1_g[l]) * (1 + sc1) + sh1
        proj = h @ w_in[l]
        u, v, p_b, g_a, g_b = jnp.split(proj, split_at, axis=-1)
        y_a = chunked_spatial_gating(u, v, a_ln_g[l], a_ln_b[l], a_ws[l], a_bs[l]) @ w_pa[l]
        y_b = conformer_conv(p_b, b_conv_w[l], b_conv_b[l], b_ln_g[l], b_ln_b[l]) @ w_pb[l]
        merged = jax.nn.sigmoid(g_a) * y_a + jax.nn.sigmoid(g_b) * y_b
        x = x + gt1 * (merged @ w_out[l])

        h = rmsnorm(x, norm2_g[l]) * (1 + sc2) + sh2
        x = x + gt2 * (jnp.square(jax.nn.relu(h @ w_ff1[l])) @ w_ff2[l])

    return rmsnorm(x, final_g)


import jax as _jax
import jax.numpy as _jnp

TWIN_FORMAT = 'train_step'
FWD_PARAMS = ['x', 'c', 'w_ada', 'b_ada', 'norm1_g', 'w_in', 'a_ln_g', 'a_ln_b', 'a_ws', 'a_bs', 'w_pa', 'b_conv_w', 'b_conv_b', 'b_ln_g', 'b_ln_b', 'w_pb', 'w_out', 'norm2_g', 'w_ff1', 'w_ff2', 'final_g']
TWIN_WEIGHTS = ['w_ada', 'b_ada', 'norm1_g', 'w_in', 'a_ln_g', 'a_ln_b', 'a_ws', 'a_bs', 'w_pa', 'b_conv_w', 'b_conv_b', 'b_ln_g', 'b_ln_b', 'w_pb', 'w_out', 'norm2_g', 'w_ff1', 'w_ff2', 'final_g']
TWIN_DIFF_INPUT = 'x'
TWIN_INPUTS = ['x', 'c', 'w_ada', 'b_ada', 'norm1_g', 'w_in', 'a_ln_g', 'a_ln_b', 'a_ws', 'a_bs', 'w_pa', 'b_conv_w', 'b_conv_b', 'b_ln_g', 'b_ln_b', 'w_pb', 'w_out', 'norm2_g', 'w_ff1', 'w_ff2', 'final_g', 'loss_target', 'm_w_ada', 'm_b_ada', 'm_norm1_g', 'm_w_in', 'm_a_ln_g', 'm_a_ln_b', 'm_a_ws', 'm_a_bs', 'm_w_pa', 'm_b_conv_w', 'm_b_conv_b', 'm_b_ln_g', 'm_b_ln_b', 'm_w_pb', 'm_w_out', 'm_norm2_g', 'm_w_ff1', 'm_w_ff2', 'm_final_g', 'v_w_ada', 'v_b_ada', 'v_norm1_g', 'v_w_in', 'v_a_ln_g', 'v_a_ln_b', 'v_a_ws', 'v_a_bs', 'v_w_pa', 'v_b_conv_w', 'v_b_conv_b', 'v_b_ln_g', 'v_b_ln_b', 'v_w_pb', 'v_w_out', 'v_norm2_g', 'v_w_ff1', 'v_w_ff2', 'v_final_g']
TWIN_OUTPUTS = ['loss', 'grad_x', 'grad_w_ada', 'grad_b_ada', 'grad_norm1_g', 'grad_w_in', 'grad_a_ln_g', 'grad_a_ln_b', 'grad_a_ws', 'grad_a_bs', 'grad_w_pa', 'grad_b_conv_w', 'grad_b_conv_b', 'grad_b_ln_g', 'grad_b_ln_b', 'grad_w_pb', 'grad_w_out', 'grad_norm2_g', 'grad_w_ff1', 'grad_w_ff2', 'grad_final_g', 'delta_w_ada', 'delta_b_ada', 'delta_norm1_g', 'delta_w_in', 'delta_a_ln_g', 'delta_a_ln_b', 'delta_a_ws', 'delta_a_bs', 'delta_w_pa', 'delta_b_conv_w', 'delta_b_conv_b', 'delta_b_ln_g', 'delta_b_ln_b', 'delta_w_pb', 'delta_w_out', 'delta_norm2_g', 'delta_w_ff1', 'delta_w_ff2', 'delta_final_g', 'new_m_w_ada', 'new_m_b_ada', 'new_m_norm1_g', 'new_m_w_in', 'new_m_a_ln_g', 'new_m_a_ln_b', 'new_m_a_ws', 'new_m_a_bs', 'new_m_w_pa', 'new_m_b_conv_w', 'new_m_b_conv_b', 'new_m_b_ln_g', 'new_m_b_ln_b', 'new_m_w_pb', 'new_m_w_out', 'new_m_norm2_g', 'new_m_w_ff1', 'new_m_w_ff2', 'new_m_final_g', 'new_v_w_ada', 'new_v_b_ada', 'new_v_norm1_g', 'new_v_w_in', 'new_v_a_ln_g', 'new_v_a_ln_b', 'new_v_a_ws', 'new_v_a_bs', 'new_v_w_pa', 'new_v_b_conv_w', 'new_v_b_conv_b', 'new_v_b_ln_g', 'new_v_b_ln_b', 'new_v_w_pb', 'new_v_w_out', 'new_v_norm2_g', 'new_v_w_ff1', 'new_v_w_ff2', 'new_v_final_g']
TWIN_LEAF_KINDS = {'loss': 'loss', 'grad_x': 'grad_x', 'grad_w_ada': 'grad_w', 'grad_b_ada': 'grad_w', 'grad_norm1_g': 'grad_w', 'grad_w_in': 'grad_w', 'grad_a_ln_g': 'grad_w', 'grad_a_ln_b': 'grad_w', 'grad_a_ws': 'grad_w', 'grad_a_bs': 'grad_w', 'grad_w_pa': 'grad_w', 'grad_b_conv_w': 'grad_w', 'grad_b_conv_b': 'grad_w', 'grad_b_ln_g': 'grad_w', 'grad_b_ln_b': 'grad_w', 'grad_w_pb': 'grad_w', 'grad_w_out': 'grad_w', 'grad_norm2_g': 'grad_w', 'grad_w_ff1': 'grad_w', 'grad_w_ff2': 'grad_w', 'grad_final_g': 'grad_w', 'delta_w_ada': 'delta_w', 'delta_b_ada': 'delta_w', 'delta_norm1_g': 'delta_w', 'delta_w_in': 'delta_w', 'delta_a_ln_g': 'delta_w', 'delta_a_ln_b': 'delta_w', 'delta_a_ws': 'delta_w', 'delta_a_bs': 'delta_w', 'delta_w_pa': 'delta_w', 'delta_b_conv_w': 'delta_w', 'delta_b_conv_b': 'delta_w', 'delta_b_ln_g': 'delta_w', 'delta_b_ln_b': 'delta_w', 'delta_w_pb': 'delta_w', 'delta_w_out': 'delta_w', 'delta_norm2_g': 'delta_w', 'delta_w_ff1': 'delta_w', 'delta_w_ff2': 'delta_w', 'delta_final_g': 'delta_w', 'new_m_w_ada': 'new_m', 'new_m_b_ada': 'new_m', 'new_m_norm1_g': 'new_m', 'new_m_w_in': 'new_m', 'new_m_a_ln_g': 'new_m', 'new_m_a_ln_b': 'new_m', 'new_m_a_ws': 'new_m', 'new_m_a_bs': 'new_m', 'new_m_w_pa': 'new_m', 'new_m_b_conv_w': 'new_m', 'new_m_b_conv_b': 'new_m', 'new_m_b_ln_g': 'new_m', 'new_m_b_ln_b': 'new_m', 'new_m_w_pb': 'new_m', 'new_m_w_out': 'new_m', 'new_m_norm2_g': 'new_m', 'new_m_w_ff1': 'new_m', 'new_m_w_ff2': 'new_m', 'new_m_final_g': 'new_m', 'new_v_w_ada': 'new_v', 'new_v_b_ada': 'new_v', 'new_v_norm1_g': 'new_v', 'new_v_w_in': 'new_v', 'new_v_a_ln_g': 'new_v', 'new_v_a_ln_b': 'new_v', 'new_v_a_ws': 'new_v', 'new_v_a_bs': 'new_v', 'new_v_w_pa': 'new_v', 'new_v_b_conv_w': 'new_v', 'new_v_b_conv_b': 'new_v', 'new_v_b_ln_g': 'new_v', 'new_v_b_ln_b': 'new_v', 'new_v_w_pb': 'new_v', 'new_v_w_out': 'new_v', 'new_v_norm2_g': 'new_v', 'new_v_w_ff1': 'new_v', 'new_v_w_ff2': 'new_v', 'new_v_final_g': 'new_v'}


def _forward(args):
    return _fwd_reference(*[args[k] for k in FWD_PARAMS])


def _output_shape():
    out = _jax.eval_shape(lambda: _forward(_fwd_setup_inputs(0)))
    return out.shape, out.dtype

N_MICROBATCH = 1
ADAM_LR = 0.001
ADAM_B1 = 0.9
ADAM_B2 = 0.999
ADAM_EPS = 1e-08
ADAM_WD = 0.01
ADAM_STEP = 10
PER_EXAMPLE_BATCH_AXIS = {'x': 0, 'c': 0, 'loss_target': 0}
SHARED_INPUTS = []
_WEIGHT_DTYPES = {'w_ada': _jnp.float32, 'b_ada': _jnp.float32, 'norm1_g': _jnp.float32, 'w_in': _jnp.float32, 'a_ln_g': _jnp.float32, 'a_ln_b': _jnp.float32, 'a_ws': _jnp.float32, 'a_bs': _jnp.float32, 'w_pa': _jnp.float32, 'b_conv_w': _jnp.float32, 'b_conv_b': _jnp.float32, 'b_ln_g': _jnp.float32, 'b_ln_b': _jnp.float32, 'w_pb': _jnp.float32, 'w_out': _jnp.float32, 'norm2_g': _jnp.float32, 'w_ff1': _jnp.float32, 'w_ff2': _jnp.float32, 'final_g': _jnp.float32}
MOMENT_SCALE = {'w_ada': 1.731152e-01, 'b_ada': 3.135962e-01, 'norm1_g': 6.518534e-02, 'w_in': 2.916214e-02, 'a_ln_g': 3.070348e-02, 'a_ln_b': 2.944098e-02, 'a_ws': 2.967766e-02, 'a_bs': 4.440198e-02, 'w_pa': 5.426567e-02, 'b_conv_w': 2.520786e-02, 'b_conv_b': 4.884084e-02, 'b_ln_g': 3.398297e-02, 'b_ln_b': 3.470222e-02, 'w_pb': 2.604105e-02, 'w_out': 6.030684e-02, 'norm2_g': 1.073149e-01, 'w_ff1': 5.699780e-02, 'w_ff2': 1.235638e-01, 'final_g': 6.490610e+01}


def _to_microbatches(a, axis):
    t = _jnp.moveaxis(a, axis, 0)
    t = t.reshape((N_MICROBATCH, t.shape[0] // N_MICROBATCH) + t.shape[1:])
    return _jnp.moveaxis(t, 1, axis + 1)


def setup_inputs(seed: int = 0) -> dict:
    inp = _fwd_setup_inputs(seed)
    key = _jax.random.fold_in(_jax.random.key(seed), 7919)
    shape, _ = _output_shape()
    out = dict(inp)
    out["loss_target"] = _jax.random.normal(_jax.random.fold_in(key, 0), shape, _jnp.float32)
    for i, name in enumerate(TWIN_WEIGHTS):
        w = inp[name].astype(_jnp.float32)
        if MOMENT_SCALE is None:
            s = _jnp.sqrt(_jnp.mean(_jnp.square(w)) + 1e-30)
        else:
            s = MOMENT_SCALE[name]
        km, kv = _jax.random.split(_jax.random.fold_in(key, i + 1))
        out[name] = w
        out["m_" + name] = s * _jax.random.normal(km, w.shape, _jnp.float32)
        out["v_" + name] = (s * s) * _jax.random.uniform(kv, w.shape, _jnp.float32, 0.5, 1.5)
    if N_MICROBATCH > 1:
        for name, axis in PER_EXAMPLE_BATCH_AXIS.items():
            out[name] = _to_microbatches(out[name], axis)
    return {'x': out['x'], 'c': out['c'], 'w_ada': out['w_ada'], 'b_ada': out['b_ada'], 'norm1_g': out['norm1_g'], 'w_in': out['w_in'], 'a_ln_g': out['a_ln_g'], 'a_ln_b': out['a_ln_b'], 'a_ws': out['a_ws'], 'a_bs': out['a_bs'], 'w_pa': out['w_pa'], 'b_conv_w': out['b_conv_w'], 'b_conv_b': out['b_conv_b'], 'b_ln_g': out['b_ln_g'], 'b_ln_b': out['b_ln_b'], 'w_pb': out['w_pb'], 'w_out': out['w_out'], 'norm2_g': out['norm2_g'], 'w_ff1': out['w_ff1'], 'w_ff2': out['w_ff2'], 'final_g': out['final_g'], 'loss_target': out['loss_target'], 'm_w_ada': out['m_w_ada'], 'm_b_ada': out['m_b_ada'], 'm_norm1_g': out['m_norm1_g'], 'm_w_in': out['m_w_in'], 'm_a_ln_g': out['m_a_ln_g'], 'm_a_ln_b': out['m_a_ln_b'], 'm_a_ws': out['m_a_ws'], 'm_a_bs': out['m_a_bs'], 'm_w_pa': out['m_w_pa'], 'm_b_conv_w': out['m_b_conv_w'], 'm_b_conv_b': out['m_b_conv_b'], 'm_b_ln_g': out['m_b_ln_g'], 'm_b_ln_b': out['m_b_ln_b'], 'm_w_pb': out['m_w_pb'], 'm_w_out': out['m_w_out'], 'm_norm2_g': out['m_norm2_g'], 'm_w_ff1': out['m_w_ff1'], 'm_w_ff2': out['m_w_ff2'], 'm_final_g': out['m_final_g'], 'v_w_ada': out['v_w_ada'], 'v_b_ada': out['v_b_ada'], 'v_norm1_g': out['v_norm1_g'], 'v_w_in': out['v_w_in'], 'v_a_ln_g': out['v_a_ln_g'], 'v_a_ln_b': out['v_a_ln_b'], 'v_a_ws': out['v_a_ws'], 'v_a_bs': out['v_a_bs'], 'v_w_pa': out['v_w_pa'], 'v_b_conv_w': out['v_b_conv_w'], 'v_b_conv_b': out['v_b_conv_b'], 'v_b_ln_g': out['v_b_ln_g'], 'v_b_ln_b': out['v_b_ln_b'], 'v_w_pb': out['v_w_pb'], 'v_w_out': out['v_w_out'], 'v_norm2_g': out['v_norm2_g'], 'v_w_ff1': out['v_w_ff1'], 'v_w_ff2': out['v_w_ff2'], 'v_final_g': out['v_final_g']}


def _loss(weights, diff, rest, loss_target):
    with _jax.named_scope("forward"):
        args = {**rest, TWIN_DIFF_INPUT: diff, **{k: w.astype(_WEIGHT_DTYPES[k]) for k, w in weights.items()}}
        y = _forward(args)
    with _jax.named_scope("loss_head"):
        err = _jnp.square(y.astype(_jnp.float32) - loss_target)
        return 0.5 * _jnp.sum(_jnp.mean(err, axis=-1)) if err.ndim else 0.5 * err


def _adamw(w, g, m, v):
    m = ADAM_B1 * m + (1.0 - ADAM_B1) * g
    v = ADAM_B2 * v + (1.0 - ADAM_B2) * _jnp.square(g)
    m_hat = m / (1.0 - ADAM_B1 ** ADAM_STEP)
    v_hat = v / (1.0 - ADAM_B2 ** ADAM_STEP)
    delta = -ADAM_LR * (m_hat / (_jnp.sqrt(v_hat) + ADAM_EPS) + ADAM_WD * w)
    return delta, m, v


def reference(x, c, w_ada, b_ada, norm1_g, w_in, a_ln_g, a_ln_b, a_ws, a_bs, w_pa, b_conv_w, b_conv_b, b_ln_g, b_ln_b, w_pb, w_out, norm2_g, w_ff1, w_ff2, final_g, loss_target, m_w_ada, m_b_ada, m_norm1_g, m_w_in, m_a_ln_g, m_a_ln_b, m_a_ws, m_a_bs, m_w_pa, m_b_conv_w, m_b_conv_b, m_b_ln_g, m_b_ln_b, m_w_pb, m_w_out, m_norm2_g, m_w_ff1, m_w_ff2, m_final_g, v_w_ada, v_b_ada, v_norm1_g, v_w_in, v_a_ln_g, v_a_ln_b, v_a_ws, v_a_bs, v_w_pa, v_b_conv_w, v_b_conv_b, v_b_ln_g, v_b_ln_b, v_w_pb, v_w_out, v_norm2_g, v_w_ff1, v_w_ff2, v_final_g):
    given = dict(x=x, c=c, w_ada=w_ada, b_ada=b_ada, norm1_g=norm1_g, w_in=w_in, a_ln_g=a_ln_g, a_ln_b=a_ln_b, a_ws=a_ws, a_bs=a_bs, w_pa=w_pa, b_conv_w=b_conv_w, b_conv_b=b_conv_b, b_ln_g=b_ln_g, b_ln_b=b_ln_b, w_pb=w_pb, w_out=w_out, norm2_g=norm2_g, w_ff1=w_ff1, w_ff2=w_ff2, final_g=final_g, loss_target=loss_target, m_w_ada=m_w_ada, m_b_ada=m_b_ada, m_norm1_g=m_norm1_g, m_w_in=m_w_in, m_a_ln_g=m_a_ln_g, m_a_ln_b=m_a_ln_b, m_a_ws=m_a_ws, m_a_bs=m_a_bs, m_w_pa=m_w_pa, m_b_conv_w=m_b_conv_w, m_b_conv_b=m_b_conv_b, m_b_ln_g=m_b_ln_g, m_b_ln_b=m_b_ln_b, m_w_pb=m_w_pb, m_w_out=m_w_out, m_norm2_g=m_norm2_g, m_w_ff1=m_w_ff1, m_w_ff2=m_w_ff2, m_final_g=m_final_g, v_w_ada=v_w_ada, v_b_ada=v_b_ada, v_norm1_g=v_norm1_g, v_w_in=v_w_in, v_a_ln_g=v_a_ln_g, v_a_ln_b=v_a_ln_b, v_a_ws=v_a_ws, v_a_bs=v_a_bs, v_w_pa=v_w_pa, v_b_conv_w=v_b_conv_w, v_b_conv_b=v_b_conv_b, v_b_ln_g=v_b_ln_g, v_b_ln_b=v_b_ln_b, v_w_pb=v_w_pb, v_w_out=v_w_out, v_norm2_g=v_norm2_g, v_w_ff1=v_w_ff1, v_w_ff2=v_w_ff2, v_final_g=v_final_g)
    weights = {n: given[n] for n in TWIN_WEIGHTS}
    shared = {n: given[n] for n in SHARED_INPUTS}
    per_example = {n: given[n] for n in ['x', 'c']}
    grad_fn = _jax.value_and_grad(_loss, argnums=(0, 1))

    def one_microbatch(ex, loss_target):
        ex = dict(ex)
        diff = ex.pop(TWIN_DIFF_INPUT)
        return grad_fn(weights, diff, {**shared, **ex}, loss_target)

    if N_MICROBATCH == 1:
        loss, (grad_w, grad_x) = one_microbatch(per_example, given["loss_target"])
    else:
        def body(carry, xs):
            loss_sum, grad_sum = carry
            l_k, (gw_k, gx_k) = one_microbatch(xs[0], xs[1])
            with _jax.named_scope("update"):
                return (loss_sum + l_k, _jax.tree.map(_jnp.add, grad_sum, gw_k)), gx_k

        init = (_jnp.zeros((), _jnp.float32), _jax.tree.map(_jnp.zeros_like, weights))
        (loss, grad_w), grad_x = _jax.lax.scan(body, init, (per_example, given["loss_target"]))
    with _jax.named_scope("update"):
        delta_w, new_m, new_v = {}, {}, {}
        for n in TWIN_WEIGHTS:
            delta_w[n], new_m[n], new_v[n] = _adamw(weights[n], grad_w[n], given["m_" + n], given["v_" + n])
    return (loss, grad_x, *[grad_w[n] for n in TWIN_WEIGHTS], *[delta_w[n] for n in TWIN_WEIGHTS],
            *[new_m[n] for n in TWIN_WEIGHTS], *[new_v[n] for n in TWIN_WEIGHTS])
```

```python
import functools

import jax
import jax.numpy as jnp
from jax import lax
from jax.experimental import pallas as pl
from jax.experimental.pallas import tpu as pltpu

F32 = jnp.float32
BF16 = jnp.bfloat16
MESH = pl.DeviceIdType.MESH
ANY = pl.BlockSpec(memory_space=pl.ANY)

N_DEV = 8
N_CHIP = 4
EPS = 1e-6
CHUNK = 128
HEADS = 8
CONV_TAPS = 31
HALO = 32
CONV_ROWS = 16
TM_BIG = 512
TM_MIX = 256
TK_WGRAD = 1024
VMEM_LIMIT = 56 * 1024 * 1024

ADAM_LR = 0.001
ADAM_B1 = 0.9
ADAM_B2 = 0.999
ADAM_EPS = 1e-08
ADAM_WD = 0.01
ADAM_STEP = 10


def _sds(shape, dtype):
    return jax.ShapeDtypeStruct(tuple(shape), dtype)


def _params(n_grid, vmem=VMEM_LIMIT):
    return pltpu.CompilerParams(dimension_semantics=("arbitrary",) * n_grid, vmem_limit_bytes=vmem)


def _nn(a, b):
    return jnp.dot(a, b, preferred_element_type=F32)


def _nt(a, b):
    return lax.dot_general(a, b, (((1,), (1,)), ((), ())), preferred_element_type=F32)


def _tn(a, b):
    return lax.dot_general(a, b, (((0,), (0,)), ((), ())), preferred_element_type=F32)


def _rowsum(v):
    return jnp.sum(v, axis=0, keepdims=True)


def _mean(v):
    return jnp.mean(v, axis=-1, keepdims=True)


def _add_row(ref, idx, val):
    ref[idx] = ref[idx] + val


def _ln_stats(v):
    mu = _mean(v)
    xc = v - mu
    rs = lax.rsqrt(_mean(xc * xc) + EPS)
    return xc * rs, rs


def _ln_bwd(dout, g, vhat, rs):
    dvh = dout * g
    return rs * (dvh - _mean(dvh) - vhat * _mean(dvh * vhat))


def _rms_bwd(dn, g, x, r):
    gd = dn * g
    return r * gd - x * (r * r * r) * _mean(x * gd)


class _Pack:
    def __init__(self, d):
        self.d = d
        self.n_in = 6 * d // N_DEV
        self.n_ff = 4 * d // N_DEV
        self.n_p = d // N_DEV
        self.off_in = 0
        self.off_ff1 = self.off_in + self.n_in
        self.off_ff2 = self.off_ff1 + self.n_ff
        self.off_pa = self.off_ff2 + self.n_ff
        self.off_pb = self.off_pa + self.n_p
        self.off_out = self.off_pb + self.n_p
        self.rows = self.off_out + self.n_p


def _load_rows(g_hbm, w_vm, sems, sem0, off, rows):
    cps = [
        pltpu.make_async_copy(g_hbm.at[k, pl.ds(off, rows), :], w_vm.at[pl.ds(k * rows, rows), :], sems.at[sem0 + k])
        for k in range(N_DEV)
    ]
    for cp in cps:
        cp.start()
    for cp in cps:
        cp.wait()


def _row_spec(tm, cols, colblk=0):
    return pl.BlockSpec((tm, cols), lambda i: (i, colblk))


def _const_spec(shape):
    nd = len(shape)
    return pl.BlockSpec(tuple(shape), lambda i: (0,) * nd)


def _mod_spec(tps, cols):
    return pl.BlockSpec((1, 1, cols), lambda i: (i // tps, 0, 0))


def _mstat_spec(tps, d):
    return pl.BlockSpec((1, 8, d), lambda i: (i // tps, 0, 0))


def _fwd_in(x2d, mod3, g1, gw, pk, seq, tm):
    t, d = x2d.shape
    nc = 6 * d
    tps = seq // tm

    def body(x_ref, mod_ref, g_ref, gw_hbm, proj_ref, w_vm, sems):
        @pl.when(pl.program_id(0) == 0)
        def _():
            _load_rows(gw_hbm, w_vm, sems, 0, pk.off_in, pk.n_in)

        x = x_ref[...]
        m = mod_ref[0]
        r = lax.rsqrt(_mean(x * x) + EPS)
        h = (x * r * g_ref[...] * (1.0 + m[:, d:2 * d]) + m[:, 0:d]).astype(BF16)
        for j in range(nc // 512):
            proj_ref[:, j * 512:(j + 1) * 512] = _nt(h, w_vm[j * 512:(j + 1) * 512, :]).astype(BF16)

    return pl.pallas_call(
        body, grid=(t // tm,),
        in_specs=[_row_spec(tm, d), _mod_spec(tps, nc), _const_spec((1, d)), ANY],
        out_specs=_row_spec(tm, nc), out_shape=_sds((t, nc), BF16),
        scratch_shapes=[pltpu.VMEM((nc, d), BF16), pltpu.SemaphoreType.DMA((N_DEV,))],
        compiler_params=_params(1), name="fwd_in",
    )(x2d, mod3, g1, gw)


def _conv_causal(zext, cw_ref, bias, out_ref, tm, d):
    for rb in range(tm // CONV_ROWS):
        acc = jnp.broadcast_to(bias, (CONV_ROWS, d))
        for k in range(CONV_TAPS):
            acc = acc + cw_ref[k:k + 1, :] * zext[pl.ds(rb * CONV_ROWS + HALO - (CONV_TAPS - 1) + k, CONV_ROWS), :]
        out_ref[pl.ds(rb * CONV_ROWS, CONV_ROWS), :] = acc


def _fill_zext(zext, z, halo_ref, first, tm, d):
    hz = halo_ref[:, 0:d].astype(F32) * jax.nn.sigmoid(halo_ref[:, d:2 * d].astype(F32))
    zext[pl.ds(0, HALO), :] = jnp.where(first, 0.0, hz)
    zext[pl.ds(HALO, tm), :] = z


def _fwd_mix(proj, x2d, mod3, vecs, wm, bsx, cw, gw, pk, seq, tm):
    t, d = x2d.shape
    tps = seq // tm
    hb = tm // HALO

    def body(proj_ref, halo_ref, x_ref, mod_ref, vec_ref, wm_ref, bs_ref, cw_ref, gw_hbm,
             x1_ref, ya_ref, yb_ref, o1_ref, zc_ref,
             wpa, wpb, wout, sems, zext, zc_buf, vn_buf, a_buf):
        i = pl.program_id(0)

        @pl.when(i == 0)
        def _():
            _load_rows(gw_hbm, wpa, sems, 0, pk.off_pa, pk.n_p)
            _load_rows(gw_hbm, wpb, sems, N_DEV, pk.off_pb, pk.n_p)
            _load_rows(gw_hbm, wout, sems, 2 * N_DEV, pk.off_out, pk.n_p)

        first = (i % tps) == 0
        m = mod_ref[0]
        vhat, _ = _ln_stats(proj_ref[:, d:2 * d].astype(F32))
        vn_buf[...] = (vhat * vec_ref[0:1, :] + vec_ref[1:2, :]).astype(BF16)
        for c in range(tm // CHUNK):
            rs_ = slice(c * CHUNK, (c + 1) * CHUNK)
            for h in range(HEADS):
                cs_ = slice(h * CHUNK, (h + 1) * CHUNK)
                s_b = _nn(wm_ref[h], vn_buf[rs_, cs_]) + bs_ref[:, cs_]
                a_buf[rs_, cs_] = (proj_ref[rs_, cs_].astype(F32) * s_b).astype(BF16)
        y_a = _nn(a_buf[...], wpa[...])
        ya_ref[...] = y_a.astype(BF16)
        z = proj_ref[:, 2 * d:3 * d].astype(F32) * jax.nn.sigmoid(proj_ref[:, 3 * d:4 * d].astype(F32))
        _fill_zext(zext, z, halo_ref, first, tm, d)
        _conv_causal(zext, cw_ref, vec_ref[2:3, :], zc_buf, tm, d)
        zc = zc_buf[...]
        zc_ref[...] = zc.astype(BF16)
        zhat, _ = _ln_stats(zc)
        zn = zhat * vec_ref[3:4, :] + vec_ref[4:5, :]
        b_act = (zn * jax.nn.sigmoid(zn)).astype(BF16)
        y_b = _nn(b_act, wpb[...])
        yb_ref[...] = y_b.astype(BF16)
        merged = (jax.nn.sigmoid(proj_ref[:, 4 * d:5 * d].astype(F32)) * y_a
                  + jax.nn.sigmoid(proj_ref[:, 5 * d:6 * d].astype(F32)) * y_b).astype(BF16)
        o1 = _nn(merged, wout[...])
        o1_ref[...] = o1.astype(BF16)
        x1_ref[...] = x_ref[...] + m[:, 2 * d:3 * d] * o1

    halo_spec = pl.BlockSpec((HALO, 2 * d), lambda i: (jnp.maximum(i * hb - 1, 0), 1))
    act = _sds((t, d), BF16)
    return pl.pallas_call(
        body, grid=(t // tm,),
        in_specs=[_row_spec(tm, 6 * d), halo_spec, _row_spec(tm, d), _mod_spec(tps, 6 * d), _const_spec((8, d)),
                  _const_spec((HEADS, CHUNK, CHUNK)), _const_spec((CHUNK, d)), _const_spec((HALO, d)), ANY],
        out_specs=[_row_spec(tm, d)] * 5,
        out_shape=[_sds((t, d), F32), act, act, act, act],
        scratch_shapes=[pltpu.VMEM((d, d), BF16), pltpu.VMEM((d, d), BF16), pltpu.VMEM((d, d), BF16),
                        pltpu.SemaphoreType.DMA((3 * N_DEV,)),
                        pltpu.VMEM((tm + HALO, d), F32), pltpu.VMEM((tm, d), F32),
                        pltpu.VMEM((tm, d), BF16), pltpu.VMEM((tm, d), BF16)],
        compiler_params=_params(1), name="fwd_mix",
    )(proj, proj, x2d, mod3, vecs, wm, bsx, cw, gw)


def _fwd_ffn(x1, mod3, g2, gw, pk, seq, tm):
    t, d = x1.shape
    nf = 4 * d
    tps = seq // tm

    def body(x_ref, mod_ref, g_ref, gw_hbm, x2_ref, f_ref, o2_ref, w1, w2, sems):
        @pl.when(pl.program_id(0) == 0)
        def _():
            _load_rows(gw_hbm, w1, sems, 0, pk.off_ff1, pk.n_ff)
            _load_rows(gw_hbm, w2, sems, N_DEV, pk.off_ff2, pk.n_ff)

        x = x_ref[...]
        m = mod_ref[0]
        r = lax.rsqrt(_mean(x * x) + EPS)
        h2 = (x * r * g_ref[...] * (1.0 + m[:, 4 * d:5 * d]) + m[:, 3 * d:4 * d]).astype(BF16)
        acc = jnp.zeros(x.shape, F32)
        for j in range(nf // 512):
            js = slice(j * 512, (j + 1) * 512)
            f = _nt(h2, w1[js, :])
            f_ref[:, js] = f.astype(BF16)
            acc = acc + _nn(jnp.square(jnp.maximum(f, 0.0)).astype(BF16), w2[js, :])
        o2_ref[...] = acc.astype(BF16)
        x2_ref[...] = x + m[:, 5 * d:6 * d] * acc

    return pl.pallas_call(
        body, grid=(t // tm,),
        in_specs=[_row_spec(tm, d), _mod_spec(tps, 6 * d), _const_spec((1, d)), ANY],
        out_specs=[_row_spec(tm, d), _row_spec(tm, nf), _row_spec(tm, d)],
        out_shape=[_sds((t, d), F32), _sds((t, nf), BF16), _sds((t, d), BF16)],
        scratch_shapes=[pltpu.VMEM((nf, d), BF16), pltpu.VMEM((nf, d), BF16), pltpu.SemaphoreType.DMA((2 * N_DEV,))],
        compiler_params=_params(1), name="fwd_ffn",
    )(x1, mod3, g2, gw)


def _loss_head(x, tgt, fg, tm):
    t, d = x.shape
    n = t // tm

    def body(x_ref, t_ref, g_ref, dx_ref, loss_ref, dg_ref, lacc):
        i = pl.program_id(0)

        @pl.when(i == 0)
        def _():
            lacc[...] = jnp.zeros(lacc.shape, F32)
            dg_ref[...] = jnp.zeros(dg_ref.shape, F32)

        xv = x_ref[...]
        g = g_ref[...]
        r = lax.rsqrt(_mean(xv * xv) + EPS)
        err = xv * r * g - t_ref[...]
        lacc[...] = lacc[...] + _rowsum(err * err)
        dy = err * (1.0 / d)
        _add_row(dg_ref, (slice(0, 1), slice(None)), _rowsum(dy * xv * r))
        dx_ref[...] = _rms_bwd(dy, g, xv, r)

        @pl.when(i == n - 1)
        def _():
            loss_ref[...] = jnp.broadcast_to(jnp.sum(lacc[...], keepdims=True) * (0.5 / d), loss_ref.shape)

    return pl.pallas_call(
        body, grid=(n,),
        in_specs=[_row_spec(tm, d), _row_spec(tm, d), _const_spec((1, d))],
        out_specs=[_row_spec(tm, d), _const_spec((8, 128)), _const_spec((8, d))],
        out_shape=[_sds((t, d), F32), _sds((8, 128), F32), _sds((8, d), F32)],
        scratch_shapes=[pltpu.VMEM((1, d), F32)],
        compiler_params=_params(1), name="loss_head",
    )(x, tgt, fg)


def _bwd_ffn(dx2, x1, f, o2, mod3, g2, gw, pk, seq, tm):
    t, d = x1.shape
    nf = 4 * d
    tps = seq // tm
    nb = t // seq

    def body(dx2_ref, x_ref, f_ref, o2_ref, mod_ref, g_ref, gw_hbm,
             dx1_ref, df_ref, do2_ref, h2_ref, ms_ref, ps_ref, w1, w2, sems):
        i = pl.program_id(0)

        @pl.when(i == 0)
        def _():
            _load_rows(gw_hbm, w1, sems, 0, pk.off_ff1, pk.n_ff)
            _load_rows(gw_hbm, w2, sems, N_DEV, pk.off_ff2, pk.n_ff)
            ps_ref[...] = jnp.zeros(ps_ref.shape, F32)

        @pl.when((i % tps) == 0)
        def _():
            ms_ref[...] = jnp.zeros(ms_ref.shape, F32)

        dx2 = dx2_ref[...]
        x = x_ref[...]
        m = mod_ref[0]
        g = g_ref[...]
        sh2, sc2, gt2 = m[:, 3 * d:4 * d], m[:, 4 * d:5 * d], m[:, 5 * d:6 * d]
        _add_row(ms_ref, (0, slice(2, 3), slice(None)), _rowsum(dx2 * o2_ref[...].astype(F32)))
        do2 = (gt2 * dx2).astype(BF16)
        do2_ref[...] = do2
        r = lax.rsqrt(_mean(x * x) + EPS)
        n = x * r * g
        h2_ref[...] = (n * (1.0 + sc2) + sh2).astype(BF16)
        dh = jnp.zeros(x.shape, F32)
        for j in range(nf // 512):
            js = slice(j * 512, (j + 1) * 512)
            dr = _nt(do2, w2[js, :])
            df = (dr * (2.0 * jnp.maximum(f_ref[:, js].astype(F32), 0.0))).astype(BF16)
            df_ref[:, js] = df
            dh = dh + _nn(df, w1[js, :])
        _add_row(ms_ref, (0, slice(0, 1), slice(None)), _rowsum(dh))
        _add_row(ms_ref, (0, slice(1, 2), slice(None)), _rowsum(dh * n))
        dn = dh * (1.0 + sc2)
        _add_row(ps_ref, (slice(0, 1), slice(None)), _rowsum(dn * x * r))
        dx1_ref[...] = dx2 + _rms_bwd(dn, g, x, r)

    act = _sds((t, d), BF16)
    return pl.pallas_call(
        body, grid=(t // tm,),
        in_specs=[_row_spec(tm, d), _row_spec(tm, d), _row_spec(tm, nf), _row_spec(tm, d), _mod_spec(tps, 6 * d),
                  _const_spec((1, d)), ANY],
        out_specs=[_row_spec(tm, d), _row_spec(tm, nf), _row_spec(tm, d), _row_spec(tm, d), _mstat_spec(tps, d),
                   _const_spec((8, d))],
        out_shape=[_sds((t, d), F32), _sds((t, nf), BF16), act, act, _sds((nb, 8, d), F32), _sds((8, d), F32)],
        scratch_shapes=[pltpu.VMEM((nf, d), BF16), pltpu.VMEM((nf, d), BF16), pltpu.SemaphoreType.DMA((2 * N_DEV,))],
        compiler_params=_params(1), name="bwd_ffn",
    )(dx2, x1, f, o2, mod3, g2, gw)


def _bwd_mix(dx1, proj, ya, yb, o1, zc, mod3, vecs, wm, wmt, bsx, gw, pk, seq, tm):
    t, d = dx1.shape
    tps = seq // tm
    nb = t // seq
    n = t // tm

    def body(dx1_ref, proj_ref, ya_ref, yb_ref, o1_ref, zc_ref, mod_ref, vec_ref, wm_ref, wmt_ref, bs_ref, gw_hbm,
             dp_ref, dzc_ref, do1_ref, dya_ref, dyb_ref, mg_ref, aa_ref, ba_ref, ms_ref, ps_ref, dws_ref, dbs_ref,
             wpa, wpb, wout, sems, vn_buf, da_buf, dvn_buf):
        i = pl.program_id(0)

        @pl.when(i == 0)
        def _():
            _load_rows(gw_hbm, wpa, sems, 0, pk.off_pa, pk.n_p)
            _load_rows(gw_hbm, wpb, sems, N_DEV, pk.off_pb, pk.n_p)
            _load_rows(gw_hbm, wout, sems, 2 * N_DEV, pk.off_out, pk.n_p)
            ps_ref[...] = jnp.zeros(ps_ref.shape, F32)
            dws_ref[...] = jnp.zeros(dws_ref.shape, F32)
            dbs_ref[...] = jnp.zeros(dbs_ref.shape, F32)

        @pl.when((i % tps) == 0)
        def _():
            ms_ref[...] = jnp.zeros(ms_ref.shape, F32)

        m = mod_ref[0]
        dx1v = dx1_ref[...]
        _add_row(ms_ref, (0, slice(0, 1), slice(None)), _rowsum(dx1v * o1_ref[...].astype(F32)))
        do1 = (m[:, 2 * d:3 * d] * dx1v).astype(BF16)
        do1_ref[...] = do1
        dmg = _nt(do1, wout[...])
        sa = jax.nn.sigmoid(proj_ref[:, 4 * d:5 * d].astype(F32))
        sb = jax.nn.sigmoid(proj_ref[:, 5 * d:6 * d].astype(F32))
        y_a = ya_ref[...].astype(F32)
        y_b = yb_ref[...].astype(F32)
        dya = (dmg * sa).astype(BF16)
        dyb = (dmg * sb).astype(BF16)
        dya_ref[...] = dya
        dyb_ref[...] = dyb
        dp_ref[:, 4 * d:5 * d] = (dmg * y_a * sa * (1.0 - sa)).astype(BF16)
        dp_ref[:, 5 * d:6 * d] = (dmg * y_b * sb * (1.0 - sb)).astype(BF16)
        mg_ref[...] = (sa * y_a + sb * y_b).astype(BF16)
        da_buf[...] = _nt(dya, wpa[...])
        db = _nt(dyb, wpb[...])
        vhat, rs = _ln_stats(proj_ref[:, d:2 * d].astype(F32))
        alg = vec_ref[0:1, :]
        vn_buf[...] = (vhat * alg + vec_ref[1:2, :]).astype(BF16)
        for c in range(tm // CHUNK):
            rs_ = slice(c * CHUNK, (c + 1) * CHUNK)
            for h in range(HEADS):
                cs_ = slice(h * CHUNK, (h + 1) * CHUNK)
                vn_b = vn_buf[rs_, cs_]
                s_b = _nn(wm_ref[h], vn_b) + bs_ref[:, cs_]
                u_b = proj_ref[rs_, cs_].astype(F32)
                da_b = da_buf[rs_, cs_]
                aa_ref[rs_, cs_] = (u_b * s_b).astype(BF16)
                dp_ref[rs_, cs_] = (da_b * s_b).astype(BF16)
                ds_b = da_b * u_b
                dbs_ref[:, cs_] = dbs_ref[:, cs_] + ds_b
                ds_bf = ds_b.astype(BF16)
                dvn_buf[rs_, cs_] = _nn(wmt_ref[h], ds_bf)
                dws_ref[:, cs_] = dws_ref[:, cs_] + _nt(ds_bf, vn_b)
        dvn = dvn_buf[...]
        _add_row(ps_ref, (slice(0, 1), slice(None)), _rowsum(dvn * vhat))
        _add_row(ps_ref, (slice(1, 2), slice(None)), _rowsum(dvn))
        dp_ref[:, d:2 * d] = _ln_bwd(dvn, alg, vhat, rs).astype(BF16)
        dp_ref[:, 2 * d:4 * d] = jnp.zeros((tm, 2 * d), BF16)
        zhat, rsb = _ln_stats(zc_ref[...].astype(F32))
        blg = vec_ref[3:4, :]
        zn = zhat * blg + vec_ref[4:5, :]
        sg = jax.nn.sigmoid(zn)
        ba_ref[...] = (zn * sg).astype(BF16)
        dzn = db * (sg * (1.0 + zn * (1.0 - sg)))
        _add_row(ps_ref, (slice(2, 3), slice(None)), _rowsum(dzn * zhat))
        _add_row(ps_ref, (slice(3, 4), slice(None)), _rowsum(dzn))
        dzc = _ln_bwd(dzn, blg, zhat, rsb)
        _add_row(ps_ref, (slice(4, 5), slice(None)), _rowsum(dzc))
        dzc_ref[...] = dzc.astype(BF16)

        @pl.when(i == n - 1)
        def _():
            causal = (lax.broadcasted_iota(jnp.int32, (CHUNK, CHUNK), 0)
                      >= lax.broadcasted_iota(jnp.int32, (CHUNK, CHUNK), 1))
            for h in range(HEADS):
                cs_ = slice(h * CHUNK, (h + 1) * CHUNK)
                dws_ref[:, cs_] = jnp.where(causal, dws_ref[:, cs_], 0.0)
                dbs_ref[:, cs_] = jnp.broadcast_to(jnp.sum(dbs_ref[:, cs_], axis=1, keepdims=True), (CHUNK, CHUNK))

    act = _sds((t, d), BF16)
    return pl.pallas_call(
        body, grid=(n,),
        in_specs=[_row_spec(tm, d), _row_spec(tm, 6 * d), _row_spec(tm, d), _row_spec(tm, d), _row_spec(tm, d),
                  _row_spec(tm, d), _mod_spec(tps, 6 * d), _const_spec((8, d)), _const_spec((HEADS, CHUNK, CHUNK)),
                  _const_spec((HEADS, CHUNK, CHUNK)), _const_spec((CHUNK, d)), ANY],
        out_specs=[_row_spec(tm, 6 * d)] + [_row_spec(tm, d)] * 7
        + [_mstat_spec(tps, d), _const_spec((8, d)), _const_spec((CHUNK, d)), _const_spec((CHUNK, d))],
        out_shape=[_sds((t, 6 * d), BF16)] + [act] * 7
        + [_sds((nb, 8, d), F32), _sds((8, d), F32), _sds((CHUNK, d), F32), _sds((CHUNK, d), F32)],
        scratch_shapes=[pltpu.VMEM((d, d), BF16), pltpu.VMEM((d, d), BF16), pltpu.VMEM((d, d), BF16),
                        pltpu.SemaphoreType.DMA((3 * N_DEV,)),
                        pltpu.VMEM((tm, d), BF16), pltpu.VMEM((tm, d), F32), pltpu.VMEM((tm, d), F32)],
        compiler_params=_params(1), name="bwd_mix",
    )(dx1, proj, ya, yb, o1, zc, mod3, vecs, wm, wmt, bsx, gw)


def _bwd_in(dproj, dzc, proj, x2d, dx1, mod3, g1, cw, gw, pk, seq, tm):
    t, d = x2d.shape
    tps = seq // tm
    nb = t // seq
    n = t // tm
    hb = tm // HALO
    lo = HALO - (CONV_TAPS - 1)

    def body(dpi_ref, dzc_ref, dzn_ref, pp_ref, halo_ref, x_ref, dx1_ref, mod_ref, g_ref, cw_ref, gw_hbm,
             dpo_ref, dx_ref, h_ref, ms_ref, ps_ref, dcw_ref,
             w_vm, sems, zext, dzext, dz_buf, dcw_acc):
        i = pl.program_id(0)

        @pl.when(i == 0)
        def _():
            _load_rows(gw_hbm, w_vm, sems, 0, pk.off_in, pk.n_in)
            ps_ref[...] = jnp.zeros(ps_ref.shape, F32)
            dcw_acc[...] = jnp.zeros(dcw_acc.shape, F32)

        first = (i % tps) == 0
        last = (i % tps) == tps - 1

        @pl.when(first)
        def _():
            ms_ref[...] = jnp.zeros(ms_ref.shape, F32)

        pa = pp_ref[:, 0:d].astype(F32)
        sgp = jax.nn.sigmoid(pp_ref[:, d:2 * d].astype(F32))
        _fill_zext(zext, pa * sgp, halo_ref, first, tm, d)
        dzext[pl.ds(0, tm), :] = dzc_ref[...].astype(F32)
        dzext[pl.ds(tm, HALO), :] = jnp.where(last, 0.0, dzn_ref[...].astype(F32))
        for rb in range(tm // CONV_ROWS):
            acc = jnp.zeros((CONV_ROWS, d), F32)
            for k in range(CONV_TAPS):
                acc = acc + cw_ref[k:k + 1, :] * dzext[pl.ds(rb * CONV_ROWS + CONV_TAPS - 1 - k, CONV_ROWS), :]
            dz_buf[pl.ds(rb * CONV_ROWS, CONV_ROWS), :] = acc
        for k in range(CONV_TAPS):
            acc = jnp.zeros((8, d), F32)
            for rb in range(tm // 8):
                acc = acc + dzext[pl.ds(rb * 8, 8), :] * zext[pl.ds(rb * 8 + lo + k, 8), :]
            dcw_acc[pl.ds(k * 8, 8), :] = dcw_acc[pl.ds(k * 8, 8), :] + acc
        dz = dz_buf[...]
        dpa = (dz * sgp).astype(BF16)
        dpg = (dz * pa * sgp * (1.0 - sgp)).astype(BF16)
        dpo_ref[:, 0:d] = dpa
        dpo_ref[:, d:2 * d] = dpg
        dh = (_nn(dpi_ref[:, 0:2 * d], w_vm[0:2 * d, :]) + _nn(dpa, w_vm[2 * d:3 * d, :])
              + _nn(dpg, w_vm[3 * d:4 * d, :]) + _nn(dpi_ref[:, 4 * d:6 * d], w_vm[4 * d:6 * d, :]))
        x = x_ref[...]
        m = mod_ref[0]
        g = g_ref[...]
        sh1, sc1 = m[:, 0:d], m[:, d:2 * d]
        r = lax.rsqrt(_mean(x * x) + EPS)
        nrm = x * r * g
        h_ref[...] = (nrm * (1.0 + sc1) + sh1).astype(BF16)
        _add_row(ms_ref, (0, slice(0, 1), slice(None)), _rowsum(dh))
        _add_row(ms_ref, (0, slice(1, 2), slice(None)), _rowsum(dh * nrm))
        dn = dh * (1.0 + sc1)
        _add_row(ps_ref, (slice(0, 1), slice(None)), _rowsum(dn * x * r))
        dx_ref[...] = dx1_ref[...] + _rms_bwd(dn, g, x, r)

        @pl.when(i == n - 1)
        def _():
            for k in range(CONV_TAPS):
                dcw_ref[k:k + 1, :] = _rowsum(dcw_acc[pl.ds(k * 8, 8), :])
            dcw_ref[CONV_TAPS:HALO, :] = jnp.zeros((HALO - CONV_TAPS, d), F32)

    halo_prev = pl.BlockSpec((HALO, 2 * d), lambda i: (jnp.maximum(i * hb - 1, 0), 1))
    halo_next = pl.BlockSpec((HALO, d), lambda i: (jnp.minimum((i + 1) * hb, t // HALO - 1), 0))
    return pl.pallas_call(
        body, grid=(n,),
        in_specs=[_row_spec(tm, 6 * d), _row_spec(tm, d), halo_next, _row_spec(tm, 2 * d, 1), halo_prev,
                  _row_spec(tm, d), _row_spec(tm, d), _mod_spec(tps, 6 * d), _const_spec((1, d)),
                  _const_spec((HALO, d)), ANY],
        out_specs=[_row_spec(tm, 2 * d, 1), _row_spec(tm, d), _row_spec(tm, d), _mstat_spec(tps, d),
                   _const_spec((8, d)), _const_spec((HALO, d))],
        out_shape=[_sds((t, 6 * d), BF16), _sds((t, d), F32), _sds((t, d), BF16), _sds((nb, 8, d), F32),
                   _sds((8, d), F32), _sds((HALO, d), F32)],
        scratch_shapes=[pltpu.VMEM((6 * d, d), BF16), pltpu.SemaphoreType.DMA((N_DEV,)),
                        pltpu.VMEM((tm + HALO, d), F32), pltpu.VMEM((tm + HALO, d), F32),
                        pltpu.VMEM((tm, d), F32), pltpu.VMEM((CONV_TAPS * 8, d), F32)],
        input_output_aliases={0: 0},
        compiler_params=_params(1), name="bwd_in",
    )(dproj, dzc, dzc, proj, proj, x2d, dx1, mod3, g1, cw, gw)


def _wgrad(a, b, tmo, relu2, name):
    t, mo = a.shape
    nn_ = b.shape[1]
    tk = min(TK_WGRAD, t)
    nk = t // tk

    def body(a_ref, b_ref, o_ref, acc):
        k = pl.program_id(1)

        @pl.when(k == 0)
        def _():
            acc[...] = jnp.zeros(acc.shape, F32)

        av = a_ref[...]
        if relu2:
            av = jnp.square(jnp.maximum(av, 0.0))
        acc[...] = acc[...] + _tn(av, b_ref[...])

        @pl.when(k == nk - 1)
        def _():
            o_ref[...] = acc[...].astype(BF16)

    return pl.pallas_call(
        body, grid=(mo // tmo, nk),
        in_specs=[pl.BlockSpec((tk, tmo), lambda i, k: (k, i)), pl.BlockSpec((tk, nn_), lambda i, k: (k, 0))],
        out_specs=pl.BlockSpec((tmo, nn_), lambda i, k: (i, 0)),
        out_shape=_sds((mo, nn_), BF16),
        scratch_shapes=[pltpu.VMEM((tmo, nn_), F32)],
        compiler_params=_params(2), name=name,
    )(a, b)


def _mod_fwd(c_all, w_ada, b_cols):
    nl, d, cols = w_ada.shape
    bsz = c_all.shape[0]

    def body(c_ref, w_ref, b_ref, o_ref):
        cv = c_ref[...]
        ca = cv * jax.nn.sigmoid(cv)
        o_ref[0] = jnp.dot(ca, w_ref[0], preferred_element_type=F32, precision=lax.Precision.HIGHEST) + b_ref[0]

    return pl.pallas_call(
        body, grid=(nl,),
        in_specs=[_const_spec((bsz, d)), pl.BlockSpec((1, d, cols), lambda l: (l, 0, 0)),
                  pl.BlockSpec((1, 1, cols), lambda l: (l, 0, 0))],
        out_specs=pl.BlockSpec((1, bsz, cols), lambda l: (l, 0, 0)),
        out_shape=_sds((nl, bsz, cols), F32),
        compiler_params=_params(1), name="mod_fwd",
    )(c_all, w_ada, b_cols)


def _mod_bwd(c_all, dmod_cols, dmod_all):
    nl, bsz, cols = dmod_cols.shape
    d = c_all.shape[1]
    ncol = dmod_all.shape[2]

    def body(c_ref, dc_ref, da_ref, dw_ref, db_ref):
        cv = c_ref[...]
        ca = cv * jax.nn.sigmoid(cv)
        dw_ref[0] = lax.dot_general(ca, dc_ref[0], (((0,), (0,)), ((), ())), preferred_element_type=F32,
                                    precision=lax.Precision.HIGHEST)
        db_ref[0] = _rowsum(da_ref[0])

    return pl.pallas_call(
        body, grid=(nl,),
        in_specs=[_const_spec((bsz, d)), pl.BlockSpec((1, bsz, cols), lambda l: (l, 0, 0)),
                  pl.BlockSpec((1, bsz, ncol), lambda l: (l, 0, 0))],
        out_specs=[pl.BlockSpec((1, d, cols), lambda l: (l, 0, 0)), pl.BlockSpec((1, 1, ncol), lambda l: (l, 0, 0))],
        out_shape=[_sds((nl, d, cols), F32), _sds((nl, 1, ncol), F32)],
        compiler_params=_params(1), name="mod_bwd",
    )(c_all, dmod_cols, dmod_all)


def _row_tile(rows, cols, nbuf, itemsize=4, budget=24 * 1024 * 1024):
    cap = max(16, budget // (2 * nbuf * cols * itemsize))
    if rows <= cap:
        return rows
    best = None
    for tr in range(16, cap + 1, 16):
        if rows % tr == 0:
            best = tr
    assert best is not None, (rows, cols)
    return best


def _sum_blocks(xs, name):
    nblk, rows, cols = xs.shape
    tr = _row_tile(rows, cols, nblk + 1)

    def body(x_ref, o_ref):
        acc = x_ref[0].astype(F32)
        for j in range(1, nblk):
            acc = acc + x_ref[j].astype(F32)
        o_ref[...] = acc

    return pl.pallas_call(
        body, grid=(rows // tr,),
        in_specs=[pl.BlockSpec((nblk, tr, cols), lambda i: (0, i, 0))],
        out_specs=pl.BlockSpec((tr, cols), lambda i: (i, 0)),
        out_shape=_sds((rows, cols), F32),
        compiler_params=_params(1), name=name,
    )(xs)


def _add_sibling(dp, recv, core):
    nq, _, rows, cols = dp.shape
    tr = _row_tile(rows, cols, 3, itemsize=2)

    def body(c_ref, a_ref, b_ref, o_ref):
        o_ref[...] = (a_ref[...].astype(F32) + b_ref[...].astype(F32)).astype(BF16)

    return pl.pallas_call(
        body,
        grid_spec=pltpu.PrefetchScalarGridSpec(
            num_scalar_prefetch=1, grid=(nq, rows // tr),
            in_specs=[pl.BlockSpec((1, 1, tr, cols), lambda q, i, c: (q, c[0], i, 0)),
                      pl.BlockSpec((1, 1, tr, cols), lambda q, i, c: (q, 0, i, 0))],
            out_specs=pl.BlockSpec((1, 1, tr, cols), lambda q, i, c: (q, 0, i, 0))),
        out_shape=_sds((nq, 1, rows, cols), BF16),
        compiler_params=_params(2), name="add_sibling",
    )(core, dp, recv.reshape(nq, 1, rows, cols)).reshape(nq, rows, cols)


def _adamw(w, g, m, v, name):
    rows, cols = w.shape
    tr = _row_tile(rows, cols, 7)
    c1 = 1.0 - ADAM_B1 ** ADAM_STEP
    c2 = 1.0 - ADAM_B2 ** ADAM_STEP

    def body(w_ref, g_ref, m_ref, v_ref, d_ref, nm_ref, nv_ref):
        gv = g_ref[...]
        nm = ADAM_B1 * m_ref[...] + (1.0 - ADAM_B1) * gv
        nv = ADAM_B2 * v_ref[...] + (1.0 - ADAM_B2) * (gv * gv)
        nm_ref[...] = nm
        nv_ref[...] = nv
        d_ref[...] = -ADAM_LR * ((nm / c1) / (jnp.sqrt(nv / c2) + ADAM_EPS) + ADAM_WD * w_ref[...])

    spec = pl.BlockSpec((tr, cols), lambda i: (i, 0))
    out = _sds((rows, cols), F32)
    return pl.pallas_call(
        body, grid=(rows // tr,), in_specs=[spec] * 4, out_specs=[spec] * 3, out_shape=[out] * 3,
        compiler_params=_params(1), name=name,
    )(w, g, m, v)


def _position():
    return lax.axis_index("x"), lax.axis_index("y"), lax.axis_index("c")


def _all_gather(xs, name):
    rows, cols = xs.shape

    def body(x_ref, out_ref, send_sems, recv_sems, local_sem):
        x, y, c = _position()
        me, sibling = (x, y, c), (x, y, 1 - c)
        chips = [(1 - x, y), (x, 1 - y), (1 - x, 1 - y)]

        def slot(px, py, pc):
            return out_ref.at[4 * px + 2 * py + pc]

        def copy(k, block, to, src=None):
            return pltpu.make_async_remote_copy(
                src_ref=slot(*block) if src is None else src, dst_ref=slot(*block),
                send_sem=send_sems.at[k], recv_sem=recv_sems.at[k], device_id=to, device_id_type=MESH)

        mine = pltpu.make_async_copy(x_ref, slot(*me), local_sem)
        mine.start()
        first = [copy(0, me, sibling, src=x_ref)]
        first += [copy(1 + j, me, (*chip, c), src=x_ref) for j, chip in enumerate(chips)]
        for cp in first:
            cp.start()
        passed = [copy(4 + j, (*chip, c), sibling) for j, chip in enumerate(chips)]
        for j, chip in enumerate(chips):
            copy(1 + j, (*chip, c), me).wait_recv()
            passed[j].start()
        copy(0, sibling, me).wait_recv()
        for j, chip in enumerate(chips):
            copy(4 + j, (*chip, 1 - c), me).wait_recv()
        for cp in first + passed:
            cp.wait_send()
        mine.wait()

    return pl.pallas_call(
        body, out_shape=_sds((N_DEV, rows, cols), xs.dtype), in_specs=[ANY], out_specs=ANY,
        scratch_shapes=[pltpu.SemaphoreType.DMA((7,)), pltpu.SemaphoreType.DMA((7,)), pltpu.SemaphoreType.DMA(())],
        name=name,
    )(xs)


def _sibling_exchange(dp):
    nq, _, rows, cols = dp.shape

    def body(x_ref, out_ref, send_sem, recv_sem):
        x, y, c = _position()
        cp = pltpu.make_async_remote_copy(
            src_ref=x_ref.at[pl.ds(0, nq), 1 - c], dst_ref=out_ref, send_sem=send_sem, recv_sem=recv_sem,
            device_id=(x, y, 1 - c), device_id_type=MESH)
        cp.start()
        cp.wait()

    return pl.pallas_call(
        body, out_shape=_sds((nq, rows, cols), dp.dtype), in_specs=[ANY], out_specs=ANY,
        scratch_shapes=[pltpu.SemaphoreType.DMA(()), pltpu.SemaphoreType.DMA(())],
        name="rs_sibling",
    )(dp)


def _chip_all_to_all(xs):
    nq, rows, cols = xs.shape

    def body(x_ref, out_ref, send_sems, recv_sems, local_sem):
        x, y, c = _position()
        q_me = 2 * x + y
        chips = [(1 - x, y), (x, 1 - y), (1 - x, 1 - y)]
        mine = pltpu.make_async_copy(x_ref.at[q_me], out_ref.at[q_me], local_sem)
        mine.start()

        def copy(j, src_q, dst_q, to):
            return pltpu.make_async_remote_copy(
                src_ref=x_ref.at[src_q], dst_ref=out_ref.at[dst_q], send_sem=send_sems.at[j],
                recv_sem=recv_sems.at[j], device_id=to, device_id_type=MESH)

        sends = [copy(j, 2 * px + py, q_me, (px, py, c)) for j, (px, py) in enumerate(chips)]
        for cp in sends:
            cp.start()
        for j, (px, py) in enumerate(chips):
            copy(j, q_me, 2 * px + py, (px, py, c)).wait_recv()
        for cp in sends:
            cp.wait_send()
        mine.wait()

    return pl.pallas_call(
        body, out_shape=_sds((nq, rows, cols), xs.dtype), in_specs=[ANY], out_specs=ANY,
        scratch_shapes=[pltpu.SemaphoreType.DMA((3,)), pltpu.SemaphoreType.DMA((3,)), pltpu.SemaphoreType.DMA(())],
        name="rs_chips",
    )(xs)


def _pad_rows(a, rows):
    return jnp.pad(a, ((0, rows - a.shape[0]), (0, 0)))


def kernel(x, c, w_ada, b_ada, norm1_g, w_in, a_ln_g, a_ln_b, a_ws, a_bs, w_pa, b_conv_w, b_conv_b, b_ln_g, b_ln_b, w_pb, w_out, norm2_g, w_ff1, w_ff2, final_g, loss_target, m_w_ada, m_b_ada, m_norm1_g, m_w_in, m_a_ln_g, m_a_ln_b, m_a_ws, m_a_bs, m_w_pa, m_b_conv_w, m_b_conv_b, m_b_ln_g, m_b_ln_b, m_w_pb, m_w_out, m_norm2_g, m_w_ff1, m_w_ff2, m_final_g, v_w_ada, v_b_ada, v_norm1_g, v_w_in, v_a_ln_g, v_a_ln_b, v_a_ws, v_a_bs, v_w_pa, v_b_conv_w, v_b_conv_b, v_b_ln_g, v_b_ln_b, v_w_pb, v_w_out, v_norm2_g, v_w_ff1, v_w_ff2, v_final_g):
    nb, seq, d = x.shape
    nl = w_in.shape[0]
    t = nb * seq
    pk = _Pack(d)
    assert d % (N_DEV * CHUNK) == 0 and d // HEADS == CHUNK and seq % CHUNK == 0
    tm_big = min(TM_BIG, seq)
    tm_mix = min(TM_MIX, seq)
    ax, ay, ac = _position()
    dev = 4 * ax + 2 * ay + ac
    ncol = 6 * d
    cols = ncol // N_DEV
    cpd = d // N_DEV
    bsz = nb * N_DEV

    cw_rows = nl * HALO
    small = jnp.concatenate([
        c.reshape(nb * d // CHUNK, CHUNK),
        jnp.pad(b_conv_w.reshape(nl, CONV_TAPS, cpd), ((0, 0), (0, HALO - CONV_TAPS), (0, 0))).reshape(cw_rows, cpd),
    ], axis=0)
    c_rows = nb * d // CHUNK
    small_all = _all_gather(small, "ag_small")
    c_all = small_all[:, :c_rows].reshape(bsz, d)
    cw_all = small_all[:, c_rows:].reshape(N_DEV, nl, HALO, cpd).transpose(1, 2, 0, 3).reshape(nl, HALO, d)
    b_cols = lax.dynamic_slice_in_dim(b_ada, dev * cols, cols, axis=1).reshape(nl, 1, cols)
    mod_cols = _mod_fwd(c_all, w_ada, b_cols)
    mod_all = _all_gather(mod_cols.reshape(nl * bsz, cols), "ag_mod")
    mod_all = mod_all.reshape(N_DEV, nl, bsz, cols).transpose(1, 2, 0, 3).reshape(nl, bsz, ncol)
    mod_mine = lax.dynamic_slice_in_dim(mod_all, dev * nb, nb, axis=1)

    causal = jnp.tril(jnp.ones((CHUNK, CHUNK), bool))
    wm_all = jnp.where(causal[None, None], a_ws, 0.0)
    wm_bf = wm_all.astype(BF16)
    wmt_bf = jnp.swapaxes(wm_all, 2, 3).astype(BF16)
    bsx_all = jnp.broadcast_to(jnp.swapaxes(a_bs, 1, 2)[:, :, :, None], (nl, CHUNK, HEADS, CHUNK)).reshape(nl, CHUNK, d)

    def vec_rows(l):
        return _pad_rows(jnp.stack([a_ln_g[l], a_ln_b[l], b_conv_b[l], b_ln_g[l], b_ln_b[l]]), 8)

    def weight_block(l):
        return jnp.concatenate([
            w_in[l].T, w_ff1[l].T, w_ff2[l], w_pa[l], w_pb[l], w_out[l]], axis=0).astype(BF16)

    xs = x.reshape(t, d)
    saved = []
    for l in range(nl):
        gw = _all_gather(weight_block(l), "ag_weights")
        mod3 = mod_mine[l].reshape(nb, 1, ncol)
        vecs = vec_rows(l)
        proj = _fwd_in(xs, mod3, norm1_g[l].reshape(1, d), gw, pk, seq, tm_big)
        x1, ya, yb, o1, zc = _fwd_mix(proj, xs, mod3, vecs, wm_bf[l], bsx_all[l], cw_all[l], gw, pk, seq, tm_mix)
        x2, f, o2 = _fwd_ffn(x1, mod3, norm2_g[l].reshape(1, d), gw, pk, seq, tm_big)
        saved.append((xs, x1, proj, ya, yb, o1, zc, f, o2, gw, mod3, vecs))
        xs = x2

    dx, loss_blk, dfg = _loss_head(xs, loss_target.reshape(t, d), final_g.reshape(1, d), tm_big)
    loss = lax.psum(loss_blk[0, 0], ("x", "y", "c"))

    core = ac.reshape(1).astype(jnp.int32)
    wg = {k: [None] * nl for k in ("w_in", "w_ff1", "w_ff2", "w_pa", "w_pb", "w_out")}
    small_rows = []
    dmod_rows = []
    for l in reversed(range(nl)):
        x0, x1, proj, ya, yb, o1, zc, f, o2, gw, mod3, vecs = saved[l]
        dx1, df, do2, h2, ms2, ps2 = _bwd_ffn(dx, x1, f, o2, mod3, norm2_g[l].reshape(1, d), gw, pk, seq, tm_mix)
        (dproj, dzc, do1, dya, dyb, mg, aa, ba, ms1, ps1, dws, dbs) = _bwd_mix(
            dx1, proj, ya, yb, o1, zc, mod3, vecs, wm_bf[l], wmt_bf[l], bsx_all[l], gw, pk, seq, tm_mix)
        dproj, dx, h, ms0, ps0, dcw = _bwd_in(
            dproj, dzc, proj, x0, dx1, mod3, norm1_g[l].reshape(1, d), cw_all[l], gw, pk, seq, tm_mix)
        grads = jnp.concatenate([
            _wgrad(dproj, h, pk.n_in, False, "wgrad_in").reshape(N_DEV, pk.n_in, d),
            _wgrad(df, h2, d, False, "wgrad_ff1").reshape(N_DEV, pk.n_ff, d),
            _wgrad(f, do2, d, True, "wgrad_ff2").reshape(N_DEV, pk.n_ff, d),
            _wgrad(aa, dya, d, False, "wgrad_pa").reshape(N_DEV, pk.n_p, d),
            _wgrad(ba, dyb, d, False, "wgrad_pb").reshape(N_DEV, pk.n_p, d),
            _wgrad(mg, do1, d, False, "wgrad_out").reshape(N_DEV, pk.n_p, d),
        ], axis=1)
        dp = grads.reshape(N_CHIP, 2, pk.rows, d)
        part = _add_sibling(dp, _sibling_exchange(dp), core)
        red = _sum_blocks(_chip_all_to_all(part), "sum_chips")
        wg["w_in"][l] = red[pk.off_in:pk.off_ff1].T
        wg["w_ff1"][l] = red[pk.off_ff1:pk.off_ff2].T
        wg["w_ff2"][l] = red[pk.off_ff2:pk.off_pa]
        wg["w_pa"][l] = red[pk.off_pa:pk.off_pb]
        wg["w_pb"][l] = red[pk.off_pb:pk.off_out]
        wg["w_out"][l] = red[pk.off_out:pk.rows]
        vec_g = jnp.concatenate([ps0[0:1], ps1[0:5], ps2[0:1], jnp.zeros((1, d), F32)], axis=0)
        small_rows.append(jnp.concatenate([vec_g, dws, dbs, dcw], axis=0))
        dmod_rows.append(jnp.concatenate([ms0[:, 0], ms0[:, 1], ms1[:, 0], ms2[:, 0], ms2[:, 1], ms2[:, 2]], axis=1))
    small_rows = small_rows[::-1]
    dmod_rows = dmod_rows[::-1]
    per_layer = 8 + 2 * CHUNK + HALO

    small_pack = jnp.concatenate(small_rows + [dfg], axis=0)
    small_pack = _pad_rows(small_pack, -(-small_pack.shape[0] // 256) * 256)
    small_sum = _sum_blocks(_all_gather(small_pack, "ag_small_grads"), "sum_small")
    dmod_mine = jnp.stack(dmod_rows).reshape(nl * nb, ncol)
    dmod_all = _all_gather(dmod_mine, "ag_dmod").reshape(N_DEV, nl, nb, ncol).transpose(1, 0, 2, 3).reshape(nl, bsz, ncol)
    dmod_cols = lax.dynamic_slice_in_dim(dmod_all, dev * cols, cols, axis=2)
    g_w_ada, g_b_ada = _mod_bwd(c_all, dmod_cols, dmod_all)

    lay = small_sum[:nl * per_layer].reshape(nl, per_layer, d)
    g_small = {
        "norm1_g": lay[:, 0], "a_ln_g": lay[:, 1], "a_ln_b": lay[:, 2], "b_ln_g": lay[:, 3], "b_ln_b": lay[:, 4],
        "b_conv_b": lay[:, 5], "norm2_g": lay[:, 6],
        "a_ws": lay[:, 8:8 + CHUNK].reshape(nl, CHUNK, HEADS, CHUNK).transpose(0, 2, 1, 3),
        "a_bs": jnp.swapaxes(lay[:, 8 + CHUNK:8 + 2 * CHUNK, ::CHUNK], 1, 2),
        "b_conv_w": lax.dynamic_slice_in_dim(
            lay[:, 8 + 2 * CHUNK:8 + 2 * CHUNK + CONV_TAPS], dev * cpd, cpd, axis=2).reshape(nl, CONV_TAPS, 1, cpd),
        "final_g": small_sum[nl * per_layer],
    }
    grads = dict(g_small)
    grads["w_ada"] = g_w_ada
    grads["b_ada"] = g_b_ada.reshape(nl, ncol)
    for k, v in wg.items():
        grads[k] = jnp.stack(v)

    names = ["w_ada", "b_ada", "norm1_g", "w_in", "a_ln_g", "a_ln_b", "a_ws", "a_bs", "w_pa", "b_conv_w", "b_conv_b",
             "b_ln_g", "b_ln_b", "w_pb", "w_out", "norm2_g", "w_ff1", "w_ff2", "final_g"]
    weights = dict(w_ada=w_ada, b_ada=b_ada, norm1_g=norm1_g, w_in=w_in, a_ln_g=a_ln_g, a_ln_b=a_ln_b, a_ws=a_ws,
                   a_bs=a_bs, w_pa=w_pa, b_conv_w=b_conv_w, b_conv_b=b_conv_b, b_ln_g=b_ln_g, b_ln_b=b_ln_b,
                   w_pb=w_pb, w_out=w_out, norm2_g=norm2_g, w_ff1=w_ff1, w_ff2=w_ff2, final_g=final_g)
    m_in = dict(w_ada=m_w_ada, b_ada=m_b_ada, norm1_g=m_norm1_g, w_in=m_w_in, a_ln_g=m_a_ln_g, a_ln_b=m_a_ln_b,
                a_ws=m_a_ws, a_bs=m_a_bs, w_pa=m_w_pa, b_conv_w=m_b_conv_w, b_conv_b=m_b_conv_b, b_ln_g=m_b_ln_g,
                b_ln_b=m_b_ln_b, w_pb=m_w_pb, w_out=m_w_out, norm2_g=m_norm2_g, w_ff1=m_w_ff1, w_ff2=m_w_ff2,
                final_g=m_final_g)
    v_in = dict(w_ada=v_w_ada, b_ada=v_b_ada, norm1_g=v_norm1_g, w_in=v_w_in, a_ln_g=v_a_ln_g, a_ln_b=v_a_ln_b,
                a_ws=v_a_ws, a_bs=v_a_bs, w_pa=v_w_pa, b_conv_w=v_b_conv_w, b_conv_b=v_b_conv_b, b_ln_g=v_b_ln_g,
                b_ln_b=v_b_ln_b, w_pb=v_w_pb, w_out=v_w_out, norm2_g=v_norm2_g, w_ff1=v_w_ff1, w_ff2=v_w_ff2,
                final_g=v_final_g)

    deltas, new_m, new_v = {}, {}, {}
    for k in names:
        shape = weights[k].shape
        two_d = (-1, shape[-1])
        g2d = grads[k].reshape(shape).reshape(two_d)
        grads[k] = grads[k].reshape(shape)
        dl, nm, nv = _adamw(weights[k].reshape(two_d), g2d, m_in[k].reshape(two_d), v_in[k].reshape(two_d),
                            "adamw_" + k)
        deltas[k], new_m[k], new_v[k] = dl.reshape(shape), nm.reshape(shape), nv.reshape(shape)

    return (loss, dx.reshape(nb, seq, d), *[grads[k] for k in names], *[deltas[k] for k in names],
            *[new_m[k] for k in names], *[new_v[k] for k in names])
```

```python
import functools

import jax
import jax.numpy as jnp
from jax import lax
from jax.experimental import pallas as pl
from jax.experimental.pallas import tpu as pltpu

F32 = jnp.float32
BF16 = jnp.bfloat16
MESH = pl.DeviceIdType.MESH
ANY = pl.BlockSpec(memory_space=pl.ANY)

N_DEV = 8
N_CHIP = 4
EPS = 1e-6
CHUNK = 128
HEADS = 8
CONV_TAPS = 31
HALO = 32
SMALL_ROWS = 40
SMALL_SLOT = 128
CONV_ROWS = 16
TM_BIG = 512
TM_MIX = 256
TK_WGRAD = 1024
VMEM_LIMIT = 56 * 1024 * 1024

ADAM_LR = 0.001
ADAM_B1 = 0.9
ADAM_B2 = 0.999
ADAM_EPS = 1e-08
ADAM_WD = 0.01
ADAM_STEP = 10


def _sds(shape, dtype):
    return jax.ShapeDtypeStruct(tuple(shape), dtype)


def _params(n_grid, vmem=VMEM_LIMIT):
    return pltpu.CompilerParams(dimension_semantics=("arbitrary",) * n_grid, vmem_limit_bytes=vmem)


def _nn(a, b):
    return jnp.dot(a, b, preferred_element_type=F32)


def _nt(a, b):
    return lax.dot_general(a, b, (((1,), (1,)), ((), ())), preferred_element_type=F32)


def _tn(a, b):
    return lax.dot_general(a, b, (((0,), (0,)), ((), ())), preferred_element_type=F32)


def _rowsum(v):
    return jnp.sum(v, axis=0, keepdims=True)


def _mean(v):
    return jnp.mean(v, axis=-1, keepdims=True)


def _add_row(ref, idx, val):
    ref[idx] = ref[idx] + val


def _ln_stats(v):
    mu = _mean(v)
    xc = v - mu
    rs = lax.rsqrt(_mean(xc * xc) + EPS)
    return xc * rs, rs


def _ln_bwd(dout, g, vhat, rs):
    dvh = dout * g
    return rs * (dvh - _mean(dvh) - vhat * _mean(dvh * vhat))


def _rms_bwd(dn, g, x, r):
    gd = dn * g
    return r * gd - x * (r * r * r) * _mean(x * gd)


class _Pack:
    def __init__(self, d):
        self.d = d
        self.n_in = 6 * d // N_DEV
        self.n_ff = 4 * d // N_DEV
        self.n_p = d // N_DEV
        self.off_in = 0
        self.off_ff1 = self.off_in + self.n_in
        self.off_ff2 = self.off_ff1 + self.n_ff
        self.off_pa = self.off_ff2 + self.n_ff
        self.off_pb = self.off_pa + self.n_p
        self.off_out = self.off_pb + self.n_p
        self.rows = self.off_out + self.n_p


def _load_rows(g_hbm, w_vm, sems, sem0, off, rows):
    cps = [
        pltpu.make_async_copy(g_hbm.at[k, pl.ds(off, rows), :], w_vm.at[pl.ds(k * rows, rows), :], sems.at[sem0 + k])
        for k in range(N_DEV)
    ]
    for cp in cps:
        cp.start()
    for cp in cps:
        cp.wait()


def _row_spec(tm, cols, colblk=0):
    return pl.BlockSpec((tm, cols), lambda i: (i, colblk))


def _const_spec(shape):
    nd = len(shape)
    return pl.BlockSpec(tuple(shape), lambda i: (0,) * nd)


def _mod_spec(tps, cols):
    return pl.BlockSpec((1, 1, cols), lambda i: (i // tps, 0, 0))


def _mstat_spec(tps, d):
    return pl.BlockSpec((1, 8, d), lambda i: (i // tps, 0, 0))


def _position():
    return lax.axis_index("x"), lax.axis_index("y"), lax.axis_index("c")


def _other_chips(x, y):
    return [(1 - x, y), (x, 1 - y), (1 - x, 1 - y)]


def _remote(src, dst, send_sems, recv_sems, k, to):
    return pltpu.make_async_remote_copy(src_ref=src, dst_ref=dst, send_sem=send_sems.at[k], recv_sem=recv_sems.at[k],
                                        device_id=to, device_id_type=MESH)


def _slot(ref, p):
    return ref.at[4 * p[0] + 2 * p[1] + p[2]]


def _ag_stage1(x_ref, out_ref, send_sems, recv_sems, local_sem):
    x, y, c = _position()
    me = (x, y, c)
    peers = [(x, y, 1 - c)] + [(*chip, c) for chip in _other_chips(x, y)]
    sends = [_remote(x_ref, _slot(out_ref, me), send_sems, recv_sems, k, p) for k, p in enumerate(peers)]
    recvs = [_remote(x_ref, _slot(out_ref, p), send_sems, recv_sems, k, p) for k, p in enumerate(peers)]
    return pltpu.make_async_copy(x_ref, _slot(out_ref, me), local_sem), sends, recvs


def _ag_stage2(in_ref, out_ref, send_sems, recv_sems):
    x, y, c = _position()
    sibling = (x, y, 1 - c)
    chips = _other_chips(x, y)
    sends = [_remote(_slot(in_ref, (*ch, c)), _slot(out_ref, (*ch, c)), send_sems, recv_sems, j, sibling)
             for j, ch in enumerate(chips)]
    recvs = [_remote(_slot(in_ref, (*ch, c)), _slot(out_ref, (*ch, 1 - c)), send_sems, recv_sems, j, sibling)
             for j, ch in enumerate(chips)]
    return sends, recvs


def _rs_chip_copies(x_ref, out_ref, send_sems, recv_sems, local_sem):
    x, y, c = _position()
    q_me = 2 * x + y
    chips = _other_chips(x, y)
    sends = [_remote(x_ref.at[2 * px + py], out_ref.at[q_me], send_sems, recv_sems, j, (px, py, c))
             for j, (px, py) in enumerate(chips)]
    recvs = [_remote(x_ref.at[q_me], out_ref.at[2 * px + py], send_sems, recv_sems, j, (px, py, c))
             for j, (px, py) in enumerate(chips)]
    return pltpu.make_async_copy(x_ref.at[q_me], out_ref.at[q_me], local_sem), sends, recvs


def _start_all(local, sends):
    if local is not None:
        local.start()
    for cp in sends:
        cp.start()


def _finish_all(local, sends, recvs):
    for cp in recvs:
        cp.wait_recv()
    for cp in sends:
        cp.wait_send()
    if local is not None:
        local.wait()


def _sem_scratch(n, local):
    out = [pltpu.SemaphoreType.DMA((n,)), pltpu.SemaphoreType.DMA((n,))]
    return out + ([pltpu.SemaphoreType.DMA(())] if local else [])


def _fwd_in(x2d, mod3, g1, gw, pk, seq, tm):
    t, d = x2d.shape
    nc = 6 * d
    tps = seq // tm

    def body(x_ref, mod_ref, g_ref, gw_hbm, proj_ref, w_vm, sems):
        @pl.when(pl.program_id(0) == 0)
        def _():
            _load_rows(gw_hbm, w_vm, sems, 0, pk.off_in, pk.n_in)

        x = x_ref[...]
        m = mod_ref[0]
        r = lax.rsqrt(_mean(x * x) + EPS)
        h = (x * r * g_ref[...] * (1.0 + m[:, d:2 * d]) + m[:, 0:d]).astype(BF16)
        for j in range(nc // 512):
            proj_ref[:, j * 512:(j + 1) * 512] = _nt(h, w_vm[j * 512:(j + 1) * 512, :]).astype(BF16)

    return pl.pallas_call(
        body, grid=(t // tm,),
        in_specs=[_row_spec(tm, d), _mod_spec(tps, nc), _const_spec((1, d)), ANY],
        out_specs=_row_spec(tm, nc), out_shape=_sds((t, nc), BF16),
        scratch_shapes=[pltpu.VMEM((nc, d), BF16), pltpu.SemaphoreType.DMA((N_DEV,))],
        compiler_params=_params(1), name="fwd_in",
    )(x2d, mod3, g1, gw)


def _conv_causal(zext, cw_ref, bias, out_ref, tm, d):
    for rb in range(tm // CONV_ROWS):
        acc = jnp.broadcast_to(bias, (CONV_ROWS, d))
        for k in range(CONV_TAPS):
            acc = acc + cw_ref[k:k + 1, :] * zext[pl.ds(rb * CONV_ROWS + HALO - (CONV_TAPS - 1) + k, CONV_ROWS), :]
        out_ref[pl.ds(rb * CONV_ROWS, CONV_ROWS), :] = acc


def _fill_zext(zext, z, halo_ref, first, tm, d):
    hz = halo_ref[:, 0:d].astype(F32) * jax.nn.sigmoid(halo_ref[:, d:2 * d].astype(F32))
    zext[pl.ds(0, HALO), :] = jnp.where(first, 0.0, hz)
    zext[pl.ds(HALO, tm), :] = z


def _fwd_mix(proj, x2d, mod3, vecs, wm, bsx, cw, gw, pk, seq, tm, nxt=None):
    t, d = x2d.shape
    tps = seq // tm
    hb = tm // HALO
    n = t // tm
    comm = nxt is not None

    def body(*refs):
        proj_ref, halo_ref, x_ref, mod_ref, vec_ref, wm_ref, bs_ref, cw_ref, gw_hbm = refs[:9]
        refs = refs[9:]
        if comm:
            nx_ref, refs = refs[0], refs[1:]
        x1_ref, ya_ref, yb_ref, o1_ref, zc_ref = refs[:5]
        refs = refs[5:]
        if comm:
            gwn_ref, refs = refs[0], refs[1:]
        wpa, wpb, wout, sems, zext, zc_buf, vn_buf, a_buf = refs[:8]
        i = pl.program_id(0)

        if comm:
            ag = functools.partial(_ag_stage1, nx_ref, gwn_ref, *refs[8:11])

            @pl.when(i == 0)
            def _():
                local, sends, _ = ag()
                _start_all(local, sends)

        @pl.when(i == 0)
        def _():
            _load_rows(gw_hbm, wpa, sems, 0, pk.off_pa, pk.n_p)
            _load_rows(gw_hbm, wpb, sems, N_DEV, pk.off_pb, pk.n_p)
            _load_rows(gw_hbm, wout, sems, 2 * N_DEV, pk.off_out, pk.n_p)

        first = (i % tps) == 0
        m = mod_ref[0]
        vhat, _ = _ln_stats(proj_ref[:, d:2 * d].astype(F32))
        vn_buf[...] = (vhat * vec_ref[0:1, :] + vec_ref[1:2, :]).astype(BF16)
        for c in range(tm // CHUNK):
            rs_ = slice(c * CHUNK, (c + 1) * CHUNK)
            for h in range(HEADS):
                cs_ = slice(h * CHUNK, (h + 1) * CHUNK)
                s_b = _nn(wm_ref[h], vn_buf[rs_, cs_]) + bs_ref[:, cs_]
                a_buf[rs_, cs_] = (proj_ref[rs_, cs_].astype(F32) * s_b).astype(BF16)
        y_a = _nn(a_buf[...], wpa[...])
        ya_ref[...] = y_a.astype(BF16)
        z = proj_ref[:, 2 * d:3 * d].astype(F32) * jax.nn.sigmoid(proj_ref[:, 3 * d:4 * d].astype(F32))
        _fill_zext(zext, z, halo_ref, first, tm, d)
        _conv_causal(zext, cw_ref, vec_ref[2:3, :], zc_buf, tm, d)
        zc = zc_buf[...]
        zc_ref[...] = zc.astype(BF16)
        zhat, _ = _ln_stats(zc)
        zn = zhat * vec_ref[3:4, :] + vec_ref[4:5, :]
        b_act = (zn * jax.nn.sigmoid(zn)).astype(BF16)
        y_b = _nn(b_act, wpb[...])
        yb_ref[...] = y_b.astype(BF16)
        merged = (jax.nn.sigmoid(proj_ref[:, 4 * d:5 * d].astype(F32)) * y_a
                  + jax.nn.sigmoid(proj_ref[:, 5 * d:6 * d].astype(F32)) * y_b).astype(BF16)
        o1 = _nn(merged, wout[...])
        o1_ref[...] = o1.astype(BF16)
        x1_ref[...] = x_ref[...] + m[:, 2 * d:3 * d] * o1

        if comm:
            @pl.when(i == n - 1)
            def _():
                _finish_all(*ag())

    halo_spec = pl.BlockSpec((HALO, 2 * d), lambda i: (jnp.maximum(i * hb - 1, 0), 1))
    act = _sds((t, d), BF16)
    return pl.pallas_call(
        body, grid=(n,),
        in_specs=[_row_spec(tm, 6 * d), halo_spec, _row_spec(tm, d), _mod_spec(tps, 6 * d), _const_spec((8, d)),
                  _const_spec((HEADS, CHUNK, CHUNK)), _const_spec((CHUNK, d)), _const_spec((HALO, d)), ANY]
        + ([ANY] if comm else []),
        out_specs=[_row_spec(tm, d)] * 5 + ([ANY] if comm else []),
        out_shape=[_sds((t, d), F32), act, act, act, act]
        + ([_sds((N_DEV,) + nxt.shape, nxt.dtype)] if comm else []),
        scratch_shapes=[pltpu.VMEM((d, d), BF16), pltpu.VMEM((d, d), BF16), pltpu.VMEM((d, d), BF16),
                        pltpu.SemaphoreType.DMA((3 * N_DEV,)),
                        pltpu.VMEM((tm + HALO, d), F32), pltpu.VMEM((tm, d), F32),
                        pltpu.VMEM((tm, d), BF16), pltpu.VMEM((tm, d), BF16)]
        + (_sem_scratch(4, True) if comm else []),
        compiler_params=_params(1), name="fwd_mix_ag" if comm else "fwd_mix",
    )(proj, proj, x2d, mod3, vecs, wm, bsx, cw, gw, *([nxt] if comm else []))


def _fwd_ffn(x1, mod3, g2, gw, pk, seq, tm, gw_next=None):
    t, d = x1.shape
    nf = 4 * d
    tps = seq // tm
    n = t // tm
    comm = gw_next is not None

    def body(*refs):
        x_ref, mod_ref, g_ref, gw_hbm = refs[:4]
        refs = refs[4:]
        if comm:
            gwn_in, refs = refs[0], refs[1:]
        x2_ref, f_ref, o2_ref = refs[:3]
        refs = refs[3:]
        if comm:
            gwn_out, refs = refs[0], refs[1:]
        w1, w2, sems = refs[:3]
        i = pl.program_id(0)

        if comm:
            ag = functools.partial(_ag_stage2, gwn_in, gwn_out, *refs[3:5])

            @pl.when(i == 0)
            def _():
                _start_all(None, ag()[0])

        @pl.when(i == 0)
        def _():
            _load_rows(gw_hbm, w1, sems, 0, pk.off_ff1, pk.n_ff)
            _load_rows(gw_hbm, w2, sems, N_DEV, pk.off_ff2, pk.n_ff)

        x = x_ref[...]
        m = mod_ref[0]
        r = lax.rsqrt(_mean(x * x) + EPS)
        h2 = (x * r * g_ref[...] * (1.0 + m[:, 4 * d:5 * d]) + m[:, 3 * d:4 * d]).astype(BF16)
        acc = jnp.zeros(x.shape, F32)
        for j in range(nf // 512):
            js = slice(j * 512, (j + 1) * 512)
            f = _nt(h2, w1[js, :])
            f_ref[:, js] = f.astype(BF16)
            acc = acc + _nn(jnp.square(jnp.maximum(f, 0.0)).astype(BF16), w2[js, :])
        o2_ref[...] = acc.astype(BF16)
        x2_ref[...] = x + m[:, 5 * d:6 * d] * acc

        if comm:
            @pl.when(i == n - 1)
            def _():
                _finish_all(None, *ag())

    return pl.pallas_call(
        body, grid=(n,),
        in_specs=[_row_spec(tm, d), _mod_spec(tps, 6 * d), _const_spec((1, d)), ANY] + ([ANY] if comm else []),
        out_specs=[_row_spec(tm, d), _row_spec(tm, nf), _row_spec(tm, d)] + ([ANY] if comm else []),
        out_shape=[_sds((t, d), F32), _sds((t, nf), BF16), _sds((t, d), BF16)]
        + ([_sds(gw_next.shape, gw_next.dtype)] if comm else []),
        scratch_shapes=[pltpu.VMEM((nf, d), BF16), pltpu.VMEM((nf, d), BF16), pltpu.SemaphoreType.DMA((2 * N_DEV,))]
        + (_sem_scratch(3, False) if comm else []),
        input_output_aliases={4: 3} if comm else {},
        compiler_params=_params(1), name="fwd_ffn_ag" if comm else "fwd_ffn",
    )(x1, mod3, g2, gw, *([gw_next] if comm else []))


def _loss_head(x, tgt, fg, tm):
    t, d = x.shape
    n = t // tm

    def body(x_ref, t_ref, g_ref, dx_ref, loss_ref, dg_ref, lacc):
        i = pl.program_id(0)

        @pl.when(i == 0)
        def _():
            lacc[...] = jnp.zeros(lacc.shape, F32)
            dg_ref[...] = jnp.zeros(dg_ref.shape, F32)

        xv = x_ref[...]
        g = g_ref[...]
        r = lax.rsqrt(_mean(xv * xv) + EPS)
        err = xv * r * g - t_ref[...]
        lacc[...] = lacc[...] + _rowsum(err * err)
        dy = err * (1.0 / d)
        _add_row(dg_ref, (slice(0, 1), slice(None)), _rowsum(dy * xv * r))
        dx_ref[...] = _rms_bwd(dy, g, xv, r)

        @pl.when(i == n - 1)
        def _():
            loss_ref[...] = jnp.broadcast_to(jnp.sum(lacc[...], keepdims=True) * (0.5 / d), loss_ref.shape)

    return pl.pallas_call(
        body, grid=(n,),
        in_specs=[_row_spec(tm, d), _row_spec(tm, d), _const_spec((1, d))],
        out_specs=[_row_spec(tm, d), _const_spec((8, 128)), _const_spec((8, d))],
        out_shape=[_sds((t, d), F32), _sds((8, 128), F32), _sds((8, d), F32)],
        scratch_shapes=[pltpu.VMEM((1, d), F32)],
        compiler_params=_params(1), name="loss_head",
    )(x, tgt, fg)


def _bwd_ffn(dx2, x1, f, o2, mod3, g2, gw, pk, seq, tm):
    t, d = x1.shape
    nf = 4 * d
    tps = seq // tm
    nb = t // seq

    def body(dx2_ref, x_ref, f_ref, o2_ref, mod_ref, g_ref, gw_hbm,
             dx1_ref, df_ref, do2_ref, h2_ref, ms_ref, ps_ref, w1, w2, sems):
        i = pl.program_id(0)

        @pl.when(i == 0)
        def _():
            _load_rows(gw_hbm, w1, sems, 0, pk.off_ff1, pk.n_ff)
            _load_rows(gw_hbm, w2, sems, N_DEV, pk.off_ff2, pk.n_ff)
            ps_ref[...] = jnp.zeros(ps_ref.shape, F32)

        @pl.when((i % tps) == 0)
        def _():
            ms_ref[...] = jnp.zeros(ms_ref.shape, F32)

        dx2 = dx2_ref[...]
        x = x_ref[...]
        m = mod_ref[0]
        g = g_ref[...]
        sh2, sc2, gt2 = m[:, 3 * d:4 * d], m[:, 4 * d:5 * d], m[:, 5 * d:6 * d]
        _add_row(ms_ref, (0, slice(2, 3), slice(None)), _rowsum(dx2 * o2_ref[...].astype(F32)))
        do2 = (gt2 * dx2).astype(BF16)
        do2_ref[...] = do2
        r = lax.rsqrt(_mean(x * x) + EPS)
        n = x * r * g
        h2_ref[...] = (n * (1.0 + sc2) + sh2).astype(BF16)
        dh = jnp.zeros(x.shape, F32)
        for j in range(nf // 512):
            js = slice(j * 512, (j + 1) * 512)
            dr = _nt(do2, w2[js, :])
            df = (dr * (2.0 * jnp.maximum(f_ref[:, js].astype(F32), 0.0))).astype(BF16)
            df_ref[:, js] = df
            dh = dh + _nn(df, w1[js, :])
        _add_row(ms_ref, (0, slice(0, 1), slice(None)), _rowsum(dh))
        _add_row(ms_ref, (0, slice(1, 2), slice(None)), _rowsum(dh * n))
        dn = dh * (1.0 + sc2)
        _add_row(ps_ref, (slice(0, 1), slice(None)), _rowsum(dn * x * r))
        dx1_ref[...] = dx2 + _rms_bwd(dn, g, x, r)

    act = _sds((t, d), BF16)
    return pl.pallas_call(
        body, grid=(t // tm,),
        in_specs=[_row_spec(tm, d), _row_spec(tm, d), _row_spec(tm, nf), _row_spec(tm, d), _mod_spec(tps, 6 * d),
                  _const_spec((1, d)), ANY],
        out_specs=[_row_spec(tm, d), _row_spec(tm, nf), _row_spec(tm, d), _row_spec(tm, d), _mstat_spec(tps, d),
                   _const_spec((8, d))],
        out_shape=[_sds((t, d), F32), _sds((t, nf), BF16), act, act, _sds((nb, 8, d), F32), _sds((8, d), F32)],
        scratch_shapes=[pltpu.VMEM((nf, d), BF16), pltpu.VMEM((nf, d), BF16), pltpu.SemaphoreType.DMA((2 * N_DEV,))],
        compiler_params=_params(1), name="bwd_ffn",
    )(dx2, x1, f, o2, mod3, g2, gw)


def _bwd_mix(dx1, proj, ya, yb, o1, zc, mod3, vecs, wm, wmt, bsx, gw, pk, seq, tm):
    t, d = dx1.shape
    tps = seq // tm
    nb = t // seq
    n = t // tm

    def body(dx1_ref, proj_ref, ya_ref, yb_ref, o1_ref, zc_ref, mod_ref, vec_ref, wm_ref, wmt_ref, bs_ref, gw_hbm,
             dp_ref, dzc_ref, do1_ref, dya_ref, dyb_ref, mg_ref, aa_ref, ba_ref, ms_ref, ps_ref, dws_ref, dbs_ref,
             wpa, wpb, wout, sems, vn_buf, da_buf, dvn_buf):
        i = pl.program_id(0)

        @pl.when(i == 0)
        def _():
            _load_rows(gw_hbm, wpa, sems, 0, pk.off_pa, pk.n_p)
            _load_rows(gw_hbm, wpb, sems, N_DEV, pk.off_pb, pk.n_p)
            _load_rows(gw_hbm, wout, sems, 2 * N_DEV, pk.off_out, pk.n_p)
            ps_ref[...] = jnp.zeros(ps_ref.shape, F32)
            dws_ref[...] = jnp.zeros(dws_ref.shape, F32)
            dbs_ref[...] = jnp.zeros(dbs_ref.shape, F32)

        @pl.when((i % tps) == 0)
        def _():
            ms_ref[...] = jnp.zeros(ms_ref.shape, F32)

        m = mod_ref[0]
        dx1v = dx1_ref[...]
        _add_row(ms_ref, (0, slice(0, 1), slice(None)), _rowsum(dx1v * o1_ref[...].astype(F32)))
        do1 = (m[:, 2 * d:3 * d] * dx1v).astype(BF16)
        do1_ref[...] = do1
        dmg = _nt(do1, wout[...])
        sa = jax.nn.sigmoid(proj_ref[:, 4 * d:5 * d].astype(F32))
        sb = jax.nn.sigmoid(proj_ref[:, 5 * d:6 * d].astype(F32))
        y_a = ya_ref[...].astype(F32)
        y_b = yb_ref[...].astype(F32)
        dya = (dmg * sa).astype(BF16)
        dyb = (dmg * sb).astype(BF16)
        dya_ref[...] = dya
        dyb_ref[...] = dyb
        dp_ref[:, 4 * d:5 * d] = (dmg * y_a * sa * (1.0 - sa)).astype(BF16)
        dp_ref[:, 5 * d:6 * d] = (dmg * y_b * sb * (1.0 - sb)).astype(BF16)
        mg_ref[...] = (sa * y_a + sb * y_b).astype(BF16)
        da_buf[...] = _nt(dya, wpa[...])
        db = _nt(dyb, wpb[...])
        vhat, rs = _ln_stats(proj_ref[:, d:2 * d].astype(F32))
        alg = vec_ref[0:1, :]
        vn_buf[...] = (vhat * alg + vec_ref[1:2, :]).astype(BF16)
        for c in range(tm // CHUNK):
            rs_ = slice(c * CHUNK, (c + 1) * CHUNK)
            for h in range(HEADS):
                cs_ = slice(h * CHUNK, (h + 1) * CHUNK)
                vn_b = vn_buf[rs_, cs_]
                s_b = _nn(wm_ref[h], vn_b) + bs_ref[:, cs_]
                u_b = proj_ref[rs_, cs_].astype(F32)
                da_b = da_buf[rs_, cs_]
                aa_ref[rs_, cs_] = (u_b * s_b).astype(BF16)
                dp_ref[rs_, cs_] = (da_b * s_b).astype(BF16)
                ds_b = da_b * u_b
                dbs_ref[:, cs_] = dbs_ref[:, cs_] + ds_b
                ds_bf = ds_b.astype(BF16)
                dvn_buf[rs_, cs_] = _nn(wmt_ref[h], ds_bf)
                dws_ref[:, cs_] = dws_ref[:, cs_] + _nt(ds_bf, vn_b)
        dvn = dvn_buf[...]
        _add_row(ps_ref, (slice(0, 1), slice(None)), _rowsum(dvn * vhat))
        _add_row(ps_ref, (slice(1, 2), slice(None)), _rowsum(dvn))
        dp_ref[:, d:2 * d] = _ln_bwd(dvn, alg, vhat, rs).astype(BF16)
        dp_ref[:, 2 * d:4 * d] = jnp.zeros((tm, 2 * d), BF16)
        zhat, rsb = _ln_stats(zc_ref[...].astype(F32))
        blg = vec_ref[3:4, :]
        zn = zhat * blg + vec_ref[4:5, :]
        sg = jax.nn.sigmoid(zn)
        ba_ref[...] = (zn * sg).astype(BF16)
        dzn = db * (sg * (1.0 + zn * (1.0 - sg)))
        _add_row(ps_ref, (slice(2, 3), slice(None)), _rowsum(dzn * zhat))
        _add_row(ps_ref, (slice(3, 4), slice(None)), _rowsum(dzn))
        dzc = _ln_bwd(dzn, blg, zhat, rsb)
        _add_row(ps_ref, (slice(4, 5), slice(None)), _rowsum(dzc))
        dzc_ref[...] = dzc.astype(BF16)

        @pl.when(i == n - 1)
        def _():
            causal = (lax.broadcasted_iota(jnp.int32, (CHUNK, CHUNK), 0)
                      >= lax.broadcasted_iota(jnp.int32, (CHUNK, CHUNK), 1))
            for h in range(HEADS):
                cs_ = slice(h * CHUNK, (h + 1) * CHUNK)
                dws_ref[:, cs_] = jnp.where(causal, dws_ref[:, cs_], 0.0)
                dbs_ref[:, cs_] = jnp.broadcast_to(jnp.sum(dbs_ref[:, cs_], axis=1, keepdims=True), (CHUNK, CHUNK))

    act = _sds((t, d), BF16)
    return pl.pallas_call(
        body, grid=(n,),
        in_specs=[_row_spec(tm, d), _row_spec(tm, 6 * d), _row_spec(tm, d), _row_spec(tm, d), _row_spec(tm, d),
                  _row_spec(tm, d), _mod_spec(tps, 6 * d), _const_spec((8, d)), _const_spec((HEADS, CHUNK, CHUNK)),
                  _const_spec((HEADS, CHUNK, CHUNK)), _const_spec((CHUNK, d)), ANY],
        out_specs=[_row_spec(tm, 6 * d)] + [_row_spec(tm, d)] * 7
        + [_mstat_spec(tps, d), _const_spec((8, d)), _const_spec((CHUNK, d)), _const_spec((CHUNK, d))],
        out_shape=[_sds((t, 6 * d), BF16)] + [act] * 7
        + [_sds((nb, 8, d), F32), _sds((8, d), F32), _sds((CHUNK, d), F32), _sds((CHUNK, d), F32)],
        scratch_shapes=[pltpu.VMEM((d, d), BF16), pltpu.VMEM((d, d), BF16), pltpu.VMEM((d, d), BF16),
                        pltpu.SemaphoreType.DMA((3 * N_DEV,)),
                        pltpu.VMEM((tm, d), BF16), pltpu.VMEM((tm, d), F32), pltpu.VMEM((tm, d), F32)],
        compiler_params=_params(1), name="bwd_mix",
    )(dx1, proj, ya, yb, o1, zc, mod3, vecs, wm, wmt, bsx, gw)


def _bwd_in(dproj, dzc, proj, x2d, dx1, mod3, g1, cw, gw, pk, seq, tm, part=None):
    t, d = x2d.shape
    tps = seq // tm
    nb = t // seq
    n = t // tm
    hb = tm // HALO
    lo = HALO - (CONV_TAPS - 1)
    comm = part is not None

    def body(*refs):
        dpi_ref, dzc_ref, dzn_ref, pp_ref, halo_ref, x_ref, dx1_ref, mod_ref, g_ref, cw_ref, gw_hbm = refs[:11]
        refs = refs[11:]
        if comm:
            part_ref, refs = refs[0], refs[1:]
        dpo_ref, dx_ref, h_ref, ms_ref, ps_ref, dcw_ref = refs[:6]
        refs = refs[6:]
        if comm:
            got_ref, refs = refs[0], refs[1:]
        w_vm, sems, zext, dzext, dz_buf, dcw_acc = refs[:6]
        i = pl.program_id(0)

        if comm:
            rs = functools.partial(_rs_chip_copies, part_ref, got_ref, *refs[6:9])

            @pl.when(i == 0)
            def _():
                local, sends, _ = rs()
                _start_all(local, sends)

        @pl.when(i == 0)
        def _():
            _load_rows(gw_hbm, w_vm, sems, 0, pk.off_in, pk.n_in)
            ps_ref[...] = jnp.zeros(ps_ref.shape, F32)
            dcw_acc[...] = jnp.zeros(dcw_acc.shape, F32)

        first = (i % tps) == 0
        last = (i % tps) == tps - 1

        @pl.when(first)
        def _():
            ms_ref[...] = jnp.zeros(ms_ref.shape, F32)

        pa = pp_ref[:, 0:d].astype(F32)
        sgp = jax.nn.sigmoid(pp_ref[:, d:2 * d].astype(F32))
        _fill_zext(zext, pa * sgp, halo_ref, first, tm, d)
        dzext[pl.ds(0, tm), :] = dzc_ref[...].astype(F32)
        dzext[pl.ds(tm, HALO), :] = jnp.where(last, 0.0, dzn_ref[...].astype(F32))
        for rb in range(tm // CONV_ROWS):
            acc = jnp.zeros((CONV_ROWS, d), F32)
            for k in range(CONV_TAPS):
                acc = acc + cw_ref[k:k + 1, :] * dzext[pl.ds(rb * CONV_ROWS + CONV_TAPS - 1 - k, CONV_ROWS), :]
            dz_buf[pl.ds(rb * CONV_ROWS, CONV_ROWS), :] = acc
        for k in range(CONV_TAPS):
            acc = jnp.zeros((8, d), F32)
            for rb in range(tm // 8):
                acc = acc + dzext[pl.ds(rb * 8, 8), :] * zext[pl.ds(rb * 8 + lo + k, 8), :]
            dcw_acc[pl.ds(k * 8, 8), :] = dcw_acc[pl.ds(k * 8, 8), :] + acc
        dz = dz_buf[...]
        dpa = (dz * sgp).astype(BF16)
        dpg = (dz * pa * sgp * (1.0 - sgp)).astype(BF16)
        dpo_ref[:, 0:d] = dpa
        dpo_ref[:, d:2 * d] = dpg
        dh = (_nn(dpi_ref[:, 0:2 * d], w_vm[0:2 * d, :]) + _nn(dpa, w_vm[2 * d:3 * d, :])
              + _nn(dpg, w_vm[3 * d:4 * d, :]) + _nn(dpi_ref[:, 4 * d:6 * d], w_vm[4 * d:6 * d, :]))
        x = x_ref[...]
        m = mod_ref[0]
        g = g_ref[...]
        sh1, sc1 = m[:, 0:d], m[:, d:2 * d]
        r = lax.rsqrt(_mean(x * x) + EPS)
        nrm = x * r * g
        h_ref[...] = (nrm * (1.0 + sc1) + sh1).astype(BF16)
        _add_row(ms_ref, (0, slice(0, 1), slice(None)), _rowsum(dh))
        _add_row(ms_ref, (0, slice(1, 2), slice(None)), _rowsum(dh * nrm))
        dn = dh * (1.0 + sc1)
        _add_row(ps_ref, (slice(0, 1), slice(None)), _rowsum(dn * x * r))
        dx_ref[...] = dx1_ref[...] + _rms_bwd(dn, g, x, r)

        @pl.when(i == n - 1)
        def _():
            for k in range(CONV_TAPS):
                dcw_ref[k:k + 1, :] = _rowsum(dcw_acc[pl.ds(k * 8, 8), :])
            dcw_ref[CONV_TAPS:HALO, :] = jnp.zeros((HALO - CONV_TAPS, d), F32)

        if comm:
            @pl.when(i == n - 1)
            def _():
                _finish_all(*rs())

    halo_prev = pl.BlockSpec((HALO, 2 * d), lambda i: (jnp.maximum(i * hb - 1, 0), 1))
    halo_next = pl.BlockSpec((HALO, d), lambda i: (jnp.minimum((i + 1) * hb, t // HALO - 1), 0))
    return pl.pallas_call(
        body, grid=(n,),
        in_specs=[_row_spec(tm, 6 * d), _row_spec(tm, d), halo_next, _row_spec(tm, 2 * d, 1), halo_prev,
                  _row_spec(tm, d), _row_spec(tm, d), _mod_spec(tps, 6 * d), _const_spec((1, d)),
                  _const_spec((HALO, d)), ANY] + ([ANY] if comm else []),
        out_specs=[_row_spec(tm, 2 * d, 1), _row_spec(tm, d), _row_spec(tm, d), _mstat_spec(tps, d),
                   _const_spec((8, d)), _const_spec((HALO, d))] + ([ANY] if comm else []),
        out_shape=[_sds((t, 6 * d), BF16), _sds((t, d), F32), _sds((t, d), BF16), _sds((nb, 8, d), F32),
                   _sds((8, d), F32), _sds((HALO, d), F32)] + ([_sds(part.shape, part.dtype)] if comm else []),
        scratch_shapes=[pltpu.VMEM((6 * d, d), BF16), pltpu.SemaphoreType.DMA((N_DEV,)),
                        pltpu.VMEM((tm + HALO, d), F32), pltpu.VMEM((tm + HALO, d), F32),
                        pltpu.VMEM((tm, d), F32), pltpu.VMEM((CONV_TAPS * 8, d), F32)]
        + (_sem_scratch(3, True) if comm else []),
        input_output_aliases={0: 0},
        compiler_params=_params(1), name="bwd_in_rs" if comm else "bwd_in",
    )(dproj, dzc, dzc, proj, proj, x2d, dx1, mod3, g1, cw, gw, *([part] if comm else []))


def _wgrad(a, b, tmo, relu2, name):
    t, mo = a.shape
    nn_ = b.shape[1]
    tk = min(TK_WGRAD, t)
    nk = t // tk

    def body(a_ref, b_ref, o_ref, acc):
        k = pl.program_id(1)

        @pl.when(k == 0)
        def _():
            acc[...] = jnp.zeros(acc.shape, F32)

        av = a_ref[...]
        if relu2:
            av = jnp.square(jnp.maximum(av, 0.0))
        acc[...] = acc[...] + _tn(av, b_ref[...])

        @pl.when(k == nk - 1)
        def _():
            o_ref[...] = acc[...].astype(BF16)

    return pl.pallas_call(
        body, grid=(mo // tmo, nk),
        in_specs=[pl.BlockSpec((tk, tmo), lambda i, k: (k, i)), pl.BlockSpec((tk, nn_), lambda i, k: (k, 0))],
        out_specs=pl.BlockSpec((tmo, nn_), lambda i, k: (i, 0)),
        out_shape=_sds((mo, nn_), BF16),
        scratch_shapes=[pltpu.VMEM((tmo, nn_), F32)],
        compiler_params=_params(2), name=name,
    )(a, b)


def _mod_fwd(c_all, w_ada, b_cols):
    nl, d, cols = w_ada.shape
    bsz = c_all.shape[0]

    def body(c_ref, w_ref, b_ref, o_ref):
        cv = c_ref[...]
        ca = cv * jax.nn.sigmoid(cv)
        o_ref[0] = jnp.dot(ca, w_ref[0], preferred_element_type=F32, precision=lax.Precision.HIGHEST) + b_ref[0]

    return pl.pallas_call(
        body, grid=(nl,),
        in_specs=[_const_spec((bsz, d)), pl.BlockSpec((1, d, cols), lambda l: (l, 0, 0)),
                  pl.BlockSpec((1, 1, cols), lambda l: (l, 0, 0))],
        out_specs=pl.BlockSpec((1, bsz, cols), lambda l: (l, 0, 0)),
        out_shape=_sds((nl, bsz, cols), F32),
        compiler_params=_params(1), name="mod_fwd",
    )(c_all, w_ada, b_cols)


def _mod_bwd(c_all, dmod_cols, dmod_all):
    nl, bsz, cols = dmod_cols.shape
    d = c_all.shape[1]
    ncol = dmod_all.shape[2]

    def body(c_ref, dc_ref, da_ref, dw_ref, db_ref):
        cv = c_ref[...]
        ca = cv * jax.nn.sigmoid(cv)
        dw_ref[0] = lax.dot_general(ca, dc_ref[0], (((0,), (0,)), ((), ())), preferred_element_type=F32,
                                    precision=lax.Precision.HIGHEST)
        db_ref[0] = _rowsum(da_ref[0])

    return pl.pallas_call(
        body, grid=(nl,),
        in_specs=[_const_spec((bsz, d)), pl.BlockSpec((1, bsz, cols), lambda l: (l, 0, 0)),
                  pl.BlockSpec((1, bsz, ncol), lambda l: (l, 0, 0))],
        out_specs=[pl.BlockSpec((1, d, cols), lambda l: (l, 0, 0)), pl.BlockSpec((1, 1, ncol), lambda l: (l, 0, 0))],
        out_shape=[_sds((nl, d, cols), F32), _sds((nl, 1, ncol), F32)],
        compiler_params=_params(1), name="mod_bwd",
    )(c_all, dmod_cols, dmod_all)


def _row_tile(rows, cols, nbuf, itemsize=4, budget=24 * 1024 * 1024):
    cap = max(16, budget // (2 * nbuf * cols * itemsize))
    if rows <= cap:
        return rows
    best = None
    for tr in range(16, cap + 1, 16):
        if rows % tr == 0:
            best = tr
    assert best is not None, (rows, cols)
    return best


def _sum_blocks(xs, name):
    nblk, rows, cols = xs.shape
    tr = _row_tile(rows, cols, nblk + 1)

    def body(x_ref, o_ref):
        acc = x_ref[0].astype(F32)
        for j in range(1, nblk):
            acc = acc + x_ref[j].astype(F32)
        o_ref[...] = acc

    return pl.pallas_call(
        body, grid=(rows // tr,),
        in_specs=[pl.BlockSpec((nblk, tr, cols), lambda i: (0, i, 0))],
        out_specs=pl.BlockSpec((tr, cols), lambda i: (i, 0)),
        out_shape=_sds((rows, cols), F32),
        compiler_params=_params(1), name=name,
    )(xs)


def _add_sibling(dp, recv, core):
    nq, _, rows, cols = dp.shape
    tr = _row_tile(rows, cols, 3, itemsize=2)

    def body(c_ref, a_ref, b_ref, o_ref):
        o_ref[...] = (a_ref[...].astype(F32) + b_ref[...].astype(F32)).astype(BF16)

    return pl.pallas_call(
        body,
        grid_spec=pltpu.PrefetchScalarGridSpec(
            num_scalar_prefetch=1, grid=(nq, rows // tr),
            in_specs=[pl.BlockSpec((1, 1, tr, cols), lambda q, i, c: (q, c[0], i, 0)),
                      pl.BlockSpec((1, 1, tr, cols), lambda q, i, c: (q, 0, i, 0))],
            out_specs=pl.BlockSpec((1, 1, tr, cols), lambda q, i, c: (q, 0, i, 0))),
        out_shape=_sds((nq, 1, rows, cols), BF16),
        compiler_params=_params(2), name="add_sibling",
    )(core, dp, recv.reshape(nq, 1, rows, cols)).reshape(nq, rows, cols)


def _adamw(w, g, m, v, name):
    rows, cols = w.shape
    tr = _row_tile(rows, cols, 7)
    c1 = 1.0 - ADAM_B1 ** ADAM_STEP
    c2 = 1.0 - ADAM_B2 ** ADAM_STEP

    def body(w_ref, g_ref, m_ref, v_ref, d_ref, nm_ref, nv_ref):
        gv = g_ref[...]
        nm = ADAM_B1 * m_ref[...] + (1.0 - ADAM_B1) * gv
        nv = ADAM_B2 * v_ref[...] + (1.0 - ADAM_B2) * (gv * gv)
        nm_ref[...] = nm
        nv_ref[...] = nv
        d_ref[...] = -ADAM_LR * ((nm / c1) / (jnp.sqrt(nv / c2) + ADAM_EPS) + ADAM_WD * w_ref[...])

    spec = pl.BlockSpec((tr, cols), lambda i: (i, 0))
    out = _sds((rows, cols), F32)
    return pl.pallas_call(
        body, grid=(rows // tr,), in_specs=[spec] * 4, out_specs=[spec] * 3, out_shape=[out] * 3,
        compiler_params=_params(1), name=name,
    )(w, g, m, v)


def _all_gather(xs, name):
    rows, cols = xs.shape

    def body(x_ref, out_ref, send1, recv1, local_sem, send2, recv2):
        local, first, arrivals = _ag_stage1(x_ref, out_ref, send1, recv1, local_sem)
        _start_all(local, first)
        passed, from_sibling = _ag_stage2(out_ref, out_ref, send2, recv2)
        for arrival, onward in zip(arrivals[1:], passed):
            arrival.wait_recv()
            onward.start()
        arrivals[0].wait_recv()
        _finish_all(local, first + passed, from_sibling)

    return pl.pallas_call(
        body, out_shape=_sds((N_DEV, rows, cols), xs.dtype), in_specs=[ANY], out_specs=ANY,
        scratch_shapes=_sem_scratch(4, True) + _sem_scratch(3, False), name=name,
    )(xs)


def _sibling_exchange(dp):
    nq, _, rows, cols = dp.shape

    def body(x_ref, out_ref, send_sem, recv_sem):
        x, y, c = _position()
        cp = pltpu.make_async_remote_copy(
            src_ref=x_ref.at[pl.ds(0, nq), 1 - c], dst_ref=out_ref, send_sem=send_sem, recv_sem=recv_sem,
            device_id=(x, y, 1 - c), device_id_type=MESH)
        cp.start()
        cp.wait()

    return pl.pallas_call(
        body, out_shape=_sds((nq, rows, cols), dp.dtype), in_specs=[ANY], out_specs=ANY,
        scratch_shapes=[pltpu.SemaphoreType.DMA(()), pltpu.SemaphoreType.DMA(())],
        name="rs_sibling",
    )(dp)


def _chip_all_to_all(xs):
    nq, rows, cols = xs.shape

    def body(x_ref, out_ref, send_sems, recv_sems, local_sem):
        local, sends, recvs = _rs_chip_copies(x_ref, out_ref, send_sems, recv_sems, local_sem)
        _start_all(local, sends)
        _finish_all(local, sends, recvs)

    return pl.pallas_call(
        body, out_shape=_sds((nq, rows, cols), xs.dtype), in_specs=[ANY], out_specs=ANY,
        scratch_shapes=_sem_scratch(3, True), name="rs_chips",
    )(xs)


def _pad_rows(a, rows):
    return jnp.pad(a, ((0, rows - a.shape[0]), (0, 0)))


def kernel(x, c, w_ada, b_ada, norm1_g, w_in, a_ln_g, a_ln_b, a_ws, a_bs, w_pa, b_conv_w, b_conv_b, b_ln_g, b_ln_b, w_pb, w_out, norm2_g, w_ff1, w_ff2, final_g, loss_target, m_w_ada, m_b_ada, m_norm1_g, m_w_in, m_a_ln_g, m_a_ln_b, m_a_ws, m_a_bs, m_w_pa, m_b_conv_w, m_b_conv_b, m_b_ln_g, m_b_ln_b, m_w_pb, m_w_out, m_norm2_g, m_w_ff1, m_w_ff2, m_final_g, v_w_ada, v_b_ada, v_norm1_g, v_w_in, v_a_ln_g, v_a_ln_b, v_a_ws, v_a_bs, v_w_pa, v_b_conv_w, v_b_conv_b, v_b_ln_g, v_b_ln_b, v_w_pb, v_w_out, v_norm2_g, v_w_ff1, v_w_ff2, v_final_g):
    nb, seq, d = x.shape
    nl = w_in.shape[0]
    t = nb * seq
    pk = _Pack(d)
    assert d % (N_DEV * CHUNK) == 0 and d // HEADS == CHUNK and seq % CHUNK == 0
    tm_big = min(TM_BIG, seq)
    tm_mix = min(TM_MIX, seq)
    ax, ay, ac = _position()
    dev = 4 * ax + 2 * ay + ac
    ncol = 6 * d
    cols = ncol // N_DEV
    cpd = d // N_DEV
    bsz = nb * N_DEV

    cw_rows = nl * HALO
    small = jnp.concatenate([
        c.reshape(nb * d // CHUNK, CHUNK),
        jnp.pad(b_conv_w.reshape(nl, CONV_TAPS, cpd), ((0, 0), (0, HALO - CONV_TAPS), (0, 0))).reshape(cw_rows, cpd),
    ], axis=0)
    c_rows = nb * d // CHUNK
    small_all = _all_gather(small, "ag_small")
    c_all = small_all[:, :c_rows].reshape(bsz, d)
    cw_all = small_all[:, c_rows:].reshape(N_DEV, nl, HALO, cpd).transpose(1, 2, 0, 3).reshape(nl, HALO, d)
    b_cols = lax.dynamic_slice_in_dim(b_ada, dev * cols, cols, axis=1).reshape(nl, 1, cols)
    mod_cols = _mod_fwd(c_all, w_ada, b_cols)
    mod_all = _all_gather(mod_cols.reshape(nl * bsz, cols), "ag_mod")
    mod_all = mod_all.reshape(N_DEV, nl, bsz, cols).transpose(1, 2, 0, 3).reshape(nl, bsz, ncol)
    mod_mine = lax.dynamic_slice_in_dim(mod_all, dev * nb, nb, axis=1)

    causal = jnp.tril(jnp.ones((CHUNK, CHUNK), bool))
    wm_all = jnp.where(causal[None, None], a_ws, 0.0)
    wm_bf = wm_all.astype(BF16)
    wmt_bf = jnp.swapaxes(wm_all, 2, 3).astype(BF16)
    bsx_all = jnp.broadcast_to(jnp.swapaxes(a_bs, 1, 2)[:, :, :, None], (nl, CHUNK, HEADS, CHUNK)).reshape(nl, CHUNK, d)

    def vec_rows(l):
        return _pad_rows(jnp.stack([a_ln_g[l], a_ln_b[l], b_conv_b[l], b_ln_g[l], b_ln_b[l]]), 8)

    def weight_block(l):
        return jnp.concatenate([
            w_in[l].T, w_ff1[l].T, w_ff2[l], w_pa[l], w_pb[l], w_out[l]], axis=0).astype(BF16)

    xs = x.reshape(t, d)
    saved = []
    gw = _all_gather(weight_block(0), "ag_weights")
    for l in range(nl):
        nxt = weight_block(l + 1) if l + 1 < nl else None
        mod3 = mod_mine[l].reshape(nb, 1, ncol)
        vecs = vec_rows(l)
        proj = _fwd_in(xs, mod3, norm1_g[l].reshape(1, d), gw, pk, seq, tm_big)
        x1, ya, yb, o1, zc, *gw_next = _fwd_mix(
            proj, xs, mod3, vecs, wm_bf[l], bsx_all[l], cw_all[l], gw, pk, seq, tm_mix, nxt)
        x2, f, o2, *gw_next = _fwd_ffn(x1, mod3, norm2_g[l].reshape(1, d), gw, pk, seq, tm_big, *gw_next)
        saved.append((xs, x1, proj, ya, yb, o1, zc, f, o2, gw, mod3, vecs))
        xs = x2
        if gw_next:
            gw = gw_next[0]

    dx, loss_blk, dfg = _loss_head(xs, loss_target.reshape(t, d), final_g.reshape(1, d), tm_big)
    loss = lax.psum(loss_blk[0, 0], ("x", "y", "c"))

    core = ac.reshape(1).astype(jnp.int32)
    wg = {k: [None] * nl for k in ("w_in", "w_ff1", "w_ff2", "w_pa", "w_pb", "w_out")}
    small_red = [None] * nl
    dmod_rows = [None] * nl
    per_layer = 8 + 2 * CHUNK + HALO
    assert per_layer <= N_DEV * SMALL_ROWS <= N_DEV * SMALL_SLOT

    def reduced(l, red):
        wg["w_in"][l] = red[pk.off_in:pk.off_ff1].T
        wg["w_ff1"][l] = red[pk.off_ff1:pk.off_ff2].T
        wg["w_ff2"][l] = red[pk.off_ff2:pk.off_pa]
        wg["w_pa"][l] = red[pk.off_pa:pk.off_pb]
        wg["w_pb"][l] = red[pk.off_pb:pk.off_out]
        wg["w_out"][l] = red[pk.off_out:pk.rows]
        small_red[l] = red[pk.rows:pk.rows + SMALL_ROWS]

    pending = None
    for l in reversed(range(nl)):
        x0, x1, proj, ya, yb, o1, zc, f, o2, gw, mod3, vecs = saved[l]
        dx1, df, do2, h2, ms2, ps2 = _bwd_ffn(dx, x1, f, o2, mod3, norm2_g[l].reshape(1, d), gw, pk, seq, tm_mix)
        (dproj, dzc, do1, dya, dyb, mg, aa, ba, ms1, ps1, dws, dbs) = _bwd_mix(
            dx1, proj, ya, yb, o1, zc, mod3, vecs, wm_bf[l], wmt_bf[l], bsx_all[l], gw, pk, seq, tm_mix)
        dproj, dx, h, ms0, ps0, dcw, *got = _bwd_in(
            dproj, dzc, proj, x0, dx1, mod3, norm1_g[l].reshape(1, d), cw_all[l], gw, pk, seq, tm_mix,
            None if pending is None else pending[1])
        if pending is not None:
            reduced(pending[0], _sum_blocks(got[0], "sum_chips"))
        vec_g = jnp.concatenate([ps0[0:1], ps1[0:5], ps2[0:1], jnp.zeros((1, d), F32)], axis=0)
        small = _pad_rows(jnp.concatenate([vec_g, dws, dbs, dcw], axis=0), N_DEV * SMALL_ROWS)
        small = jnp.pad(small.reshape(N_DEV, SMALL_ROWS, d).astype(BF16), ((0, 0), (0, SMALL_SLOT - SMALL_ROWS), (0, 0)))
        grads = jnp.concatenate([
            _wgrad(dproj, h, pk.n_in, False, "wgrad_in").reshape(N_DEV, pk.n_in, d),
            _wgrad(df, h2, d, False, "wgrad_ff1").reshape(N_DEV, pk.n_ff, d),
            _wgrad(f, do2, d, True, "wgrad_ff2").reshape(N_DEV, pk.n_ff, d),
            _wgrad(aa, dya, d, False, "wgrad_pa").reshape(N_DEV, pk.n_p, d),
            _wgrad(ba, dyb, d, False, "wgrad_pb").reshape(N_DEV, pk.n_p, d),
            _wgrad(mg, do1, d, False, "wgrad_out").reshape(N_DEV, pk.n_p, d),
            small,
        ], axis=1)
        dp = grads.reshape(N_CHIP, 2, pk.rows + SMALL_SLOT, d)
        pending = (l, _add_sibling(dp, _sibling_exchange(dp), core))
        dmod_rows[l] = jnp.concatenate([ms0[:, 0], ms0[:, 1], ms1[:, 0], ms2[:, 0], ms2[:, 1], ms2[:, 2]], axis=1)
    reduced(pending[0], _sum_blocks(_chip_all_to_all(pending[1]), "sum_chips"))

    small_all = _all_gather(jnp.concatenate(small_red, axis=0), "ag_small_grads")
    lay = small_all.reshape(N_DEV, nl, SMALL_ROWS, d).transpose(1, 0, 2, 3).reshape(nl, N_DEV * SMALL_ROWS, d)
    n_dm = nl * nb * 6
    tail = jnp.concatenate([jnp.stack(dmod_rows).reshape(n_dm, d), dfg], axis=0)
    tail_all = _all_gather(tail, "ag_dmod")
    dmod_all = tail_all[:, :n_dm].reshape(N_DEV, nl, nb, ncol).transpose(1, 0, 2, 3).reshape(nl, bsz, ncol)
    dmod_cols = lax.dynamic_slice_in_dim(dmod_all, dev * cols, cols, axis=2)
    g_w_ada, g_b_ada = _mod_bwd(c_all, dmod_cols, dmod_all)
    g_final = _sum_blocks(tail_all[:, n_dm:], "sum_final_g")[0]

    g_small = {
        "norm1_g": lay[:, 0], "a_ln_g": lay[:, 1], "a_ln_b": lay[:, 2], "b_ln_g": lay[:, 3], "b_ln_b": lay[:, 4],
        "b_conv_b": lay[:, 5], "norm2_g": lay[:, 6],
        "a_ws": lay[:, 8:8 + CHUNK].reshape(nl, CHUNK, HEADS, CHUNK).transpose(0, 2, 1, 3),
        "a_bs": jnp.swapaxes(lay[:, 8 + CHUNK:8 + 2 * CHUNK, ::CHUNK], 1, 2),
        "b_conv_w": lax.dynamic_slice_in_dim(
            lay[:, 8 + 2 * CHUNK:8 + 2 * CHUNK + CONV_TAPS], dev * cpd, cpd, axis=2).reshape(nl, CONV_TAPS, 1, cpd),
        "final_g": g_final,
    }
    grads = dict(g_small)
    grads["w_ada"] = g_w_ada
    grads["b_ada"] = g_b_ada.reshape(nl, ncol)
    for k, v in wg.items():
        grads[k] = jnp.stack(v)

    names = ["w_ada", "b_ada", "norm1_g", "w_in", "a_ln_g", "a_ln_b", "a_ws", "a_bs", "w_pa", "b_conv_w", "b_conv_b",
             "b_ln_g", "b_ln_b", "w_pb", "w_out", "norm2_g", "w_ff1", "w_ff2", "final_g"]
    weights = dict(w_ada=w_ada, b_ada=b_ada, norm1_g=norm1_g, w_in=w_in, a_ln_g=a_ln_g, a_ln_b=a_ln_b, a_ws=a_ws,
                   a_bs=a_bs, w_pa=w_pa, b_conv_w=b_conv_w, b_conv_b=b_conv_b, b_ln_g=b_ln_g, b_ln_b=b_ln_b,
                   w_pb=w_pb, w_out=w_out, norm2_g=norm2_g, w_ff1=w_ff1, w_ff2=w_ff2, final_g=final_g)
    m_in = dict(w_ada=m_w_ada, b_ada=m_b_ada, norm1_g=m_norm1_g, w_in=m_w_in, a_ln_g=m_a_ln_g, a_ln_b=m_a_ln_b,
                a_ws=m_a_ws, a_bs=m_a_bs, w_pa=m_w_pa, b_conv_w=m_b_conv_w, b_conv_b=m_b_conv_b, b_ln_g=m_b_ln_g,
                b_ln_b=m_b_ln_b, w_pb=m_w_pb, w_out=m_w_out, norm2_g=m_norm2_g, w_ff1=m_w_ff1, w_ff2=m_w_ff2,
                final_g=m_final_g)
    v_in = dict(w_ada=v_w_ada, b_ada=v_b_ada, norm1_g=v_norm1_g, w_in=v_w_in, a_ln_g=v_a_ln_g, a_ln_b=v_a_ln_b,
                a_ws=v_a_ws, a_bs=v_a_bs, w_pa=v_w_pa, b_conv_w=v_b_conv_w, b_conv_b=v_b_conv_b, b_ln_g=v_b_ln_g,
                b_ln_b=v_b_ln_b, w_pb=v_w_pb, w_out=v_w_out, norm2_g=v_norm2_g, w_ff1=v_w_ff1, w_ff2=v_w_ff2,
                final_g=v_final_g)

    deltas, new_m, new_v = {}, {}, {}
    for k in names:
        shape = weights[k].shape
        two_d = (-1, shape[-1])
        g2d = grads[k].reshape(shape).reshape(two_d)
        grads[k] = grads[k].reshape(shape)
        dl, nm, nv = _adamw(weights[k].reshape(two_d), g2d, m_in[k].reshape(two_d), v_in[k].reshape(two_d),
                            "adamw_" + k)
        deltas[k], new_m[k], new_v[k] = dl.reshape(shape), nm.reshape(shape), nv.reshape(shape)

    return (loss, dx.reshape(nb, seq, d), *[grads[k] for k in names], *[deltas[k] for k in names],
            *[new_m[k] for k in names], *[new_v[k] for k in names])
```

```python
import functools

import jax
import jax.numpy as jnp
from jax import lax
from jax.experimental import pallas as pl
from jax.experimental.pallas import tpu as pltpu

F32 = jnp.float32
BF16 = jnp.bfloat16
MESH = pl.DeviceIdType.MESH
ANY = pl.BlockSpec(memory_space=pl.ANY)

N_DEV = 8
N_CHIP = 4
EPS = 1e-6
CHUNK = 128
HEADS = 8
CONV_TAPS = 31
HALO = 32
SMALL_ROWS = 40
SMALL_SLOT = 128
CONV_ROWS = 32
TM_BIG = 512
TM_MIX = 256
TK_WGRAD = 1024
VMEM_LIMIT = 56 * 1024 * 1024

ADAM_LR = 0.001
ADAM_B1 = 0.9
ADAM_B2 = 0.999
ADAM_EPS = 1e-08
ADAM_WD = 0.01
ADAM_STEP = 10


def _sds(shape, dtype):
    return jax.ShapeDtypeStruct(tuple(shape), dtype)


def _params(n_grid, vmem=VMEM_LIMIT):
    return pltpu.CompilerParams(dimension_semantics=("arbitrary",) * n_grid, vmem_limit_bytes=vmem)


def _nn(a, b):
    return jnp.dot(a, b, preferred_element_type=F32)


def _nt(a, b):
    return lax.dot_general(a, b, (((1,), (1,)), ((), ())), preferred_element_type=F32)


def _tn(a, b):
    return lax.dot_general(a, b, (((0,), (0,)), ((), ())), preferred_element_type=F32)


def _rowsum(v):
    return jnp.sum(v, axis=0, keepdims=True)


def _mean(v):
    return jnp.mean(v, axis=-1, keepdims=True)


def _add_row(ref, idx, val):
    ref[idx] = ref[idx] + val


def _ln_stats(v):
    mu = _mean(v)
    xc = v - mu
    rs = lax.rsqrt(_mean(xc * xc) + EPS)
    return xc * rs, rs


def _ln_bwd(dout, g, vhat, rs):
    dvh = dout * g
    return rs * (dvh - _mean(dvh) - vhat * _mean(dvh * vhat))


def _rms_bwd(dn, g, x, r):
    gd = dn * g
    return r * gd - x * (r * r * r) * _mean(x * gd)


class _Pack:
    def __init__(self, d):
        self.d = d
        self.n_in = 6 * d // N_DEV
        self.n_ff = 4 * d // N_DEV
        self.n_p = d // N_DEV
        self.off_in = 0
        self.off_ff1 = self.off_in + self.n_in
        self.off_ff2 = self.off_ff1 + self.n_ff
        self.off_pa = self.off_ff2 + self.n_ff
        self.off_pb = self.off_pa + self.n_p
        self.off_out = self.off_pb + self.n_p
        self.rows = self.off_out + self.n_p


def _load_rows(g_hbm, w_vm, sems, sem0, off, rows):
    cps = [
        pltpu.make_async_copy(g_hbm.at[k, pl.ds(off, rows), :], w_vm.at[pl.ds(k * rows, rows), :], sems.at[sem0 + k])
        for k in range(N_DEV)
    ]
    for cp in cps:
        cp.start()
    for cp in cps:
        cp.wait()


def _row_spec(tm, cols, colblk=0):
    return pl.BlockSpec((tm, cols), lambda i: (i, colblk))


def _const_spec(shape):
    nd = len(shape)
    return pl.BlockSpec(tuple(shape), lambda i: (0,) * nd)


def _mod_spec(tps, cols):
    return pl.BlockSpec((1, 1, cols), lambda i: (i // tps, 0, 0))


def _mstat_spec(tps, d):
    return pl.BlockSpec((1, 8, d), lambda i: (i // tps, 0, 0))


def _position():
    return lax.axis_index("x"), lax.axis_index("y"), lax.axis_index("c")


def _other_chips(x, y):
    return [(1 - x, y), (x, 1 - y), (1 - x, 1 - y)]


def _remote(src, dst, send_sems, recv_sems, k, to):
    return pltpu.make_async_remote_copy(src_ref=src, dst_ref=dst, send_sem=send_sems.at[k], recv_sem=recv_sems.at[k],
                                        device_id=to, device_id_type=MESH)


def _slot(ref, p):
    return ref.at[4 * p[0] + 2 * p[1] + p[2]]


def _ag_stage1(x_ref, out_ref, send_sems, recv_sems, local_sem):
    x, y, c = _position()
    me = (x, y, c)
    peers = [(x, y, 1 - c)] + [(*chip, c) for chip in _other_chips(x, y)]
    sends = [_remote(x_ref, _slot(out_ref, me), send_sems, recv_sems, k, p) for k, p in enumerate(peers)]
    recvs = [_remote(x_ref, _slot(out_ref, p), send_sems, recv_sems, k, p) for k, p in enumerate(peers)]
    return pltpu.make_async_copy(x_ref, _slot(out_ref, me), local_sem), sends, recvs


def _ag_stage2(in_ref, out_ref, send_sems, recv_sems):
    x, y, c = _position()
    sibling = (x, y, 1 - c)
    chips = _other_chips(x, y)
    sends = [_remote(_slot(in_ref, (*ch, c)), _slot(out_ref, (*ch, c)), send_sems, recv_sems, j, sibling)
             for j, ch in enumerate(chips)]
    recvs = [_remote(_slot(in_ref, (*ch, c)), _slot(out_ref, (*ch, 1 - c)), send_sems, recv_sems, j, sibling)
             for j, ch in enumerate(chips)]
    return sends, recvs


def _rs_chip_copies(x_ref, out_ref, send_sems, recv_sems, local_sem):
    x, y, c = _position()
    q_me = 2 * x + y
    chips = _other_chips(x, y)
    sends = [_remote(x_ref.at[2 * px + py], out_ref.at[q_me], send_sems, recv_sems, j, (px, py, c))
             for j, (px, py) in enumerate(chips)]
    recvs = [_remote(x_ref.at[q_me], out_ref.at[2 * px + py], send_sems, recv_sems, j, (px, py, c))
             for j, (px, py) in enumerate(chips)]
    return pltpu.make_async_copy(x_ref.at[q_me], out_ref.at[q_me], local_sem), sends, recvs


def _start_all(local, sends):
    if local is not None:
        local.start()
    for cp in sends:
        cp.start()


def _finish_all(local, sends, recvs):
    for cp in recvs:
        cp.wait_recv()
    for cp in sends:
        cp.wait_send()
    if local is not None:
        local.wait()


def _sem_scratch(n, local):
    out = [pltpu.SemaphoreType.DMA((n,)), pltpu.SemaphoreType.DMA((n,))]
    return out + ([pltpu.SemaphoreType.DMA(())] if local else [])


def _fwd_in(x2d, mod3, g1, gw, pk, seq, tm):
    t, d = x2d.shape
    nc = 6 * d
    tps = seq // tm

    def body(x_ref, mod_ref, g_ref, gw_hbm, proj_ref, w_vm, sems):
        @pl.when(pl.program_id(0) == 0)
        def _():
            _load_rows(gw_hbm, w_vm, sems, 0, pk.off_in, pk.n_in)

        x = x_ref[...]
        m = mod_ref[0]
        r = lax.rsqrt(_mean(x * x) + EPS)
        h = (x * r * g_ref[...] * (1.0 + m[:, d:2 * d]) + m[:, 0:d]).astype(BF16)
        for j in range(nc // 512):
            proj_ref[:, j * 512:(j + 1) * 512] = _nt(h, w_vm[j * 512:(j + 1) * 512, :]).astype(BF16)

    return pl.pallas_call(
        body, grid=(t // tm,),
        in_specs=[_row_spec(tm, d), _mod_spec(tps, nc), _const_spec((1, d)), ANY],
        out_specs=_row_spec(tm, nc), out_shape=_sds((t, nc), BF16),
        scratch_shapes=[pltpu.VMEM((nc, d), BF16), pltpu.SemaphoreType.DMA((N_DEV,))],
        compiler_params=_params(1), name="fwd_in",
    )(x2d, mod3, g1, gw)


def _shift_copies(src, sh, rows):
    for r in range(1, 8):
        sh[r - 1, pl.ds(0, rows), :] = src[pl.ds(r, rows), :]


def _window(src, sh, offset, start, size):
    r, q = offset % 8, offset // 8
    if r == 0:
        return src[pl.ds(start + 8 * q, size), :]
    return sh[r - 1, pl.ds(start + 8 * q, size), :]


def _conv_taps(src, sh, cwb_ref, offsets, bias, out_ref, tm, d):
    nsub = CONV_ROWS // 8
    for rb in range(tm // CONV_ROWS):
        accs = [jnp.broadcast_to(bias, (8, d))] * nsub
        for k in range(CONV_TAPS):
            w8 = cwb_ref[pl.ds(8 * k, 8), :]
            accs = [a + w8 * _window(src, sh, offsets[k], rb * CONV_ROWS + 8 * j, 8) for j, a in enumerate(accs)]
        for j, a in enumerate(accs):
            out_ref[pl.ds(rb * CONV_ROWS + 8 * j, 8), :] = a


def _conv_wgrad(dsrc, zsrc, zsh, acc_ref, lo, tm, d):
    for r in range(8):
        taps = [k for k in range(CONV_TAPS) if (lo + k) % 8 == r]
        accs = [jnp.zeros((8, d), F32)] * len(taps)
        for rb in range(tm // 8):
            dblk = dsrc[pl.ds(rb * 8, 8), :]
            accs = [a + dblk * _window(zsrc, zsh, lo + k, rb * 8, 8) for a, k in zip(accs, taps)]
        for a, k in zip(accs, taps):
            acc_ref[pl.ds(8 * k, 8), :] = acc_ref[pl.ds(8 * k, 8), :] + a


def _fill_zext(zext, z, halo_ref, first, tm, d):
    hz = halo_ref[:, 0:d].astype(F32) * jax.nn.sigmoid(halo_ref[:, d:2 * d].astype(F32))
    zext[pl.ds(0, HALO), :] = jnp.where(first, 0.0, hz)
    zext[pl.ds(HALO, tm), :] = z


def _fwd_mix(proj, x2d, mod3, vecs, wm, bsx, cw, gw, pk, seq, tm, nxt=None):
    t, d = x2d.shape
    tps = seq // tm
    hb = tm // HALO
    n = t // tm
    lo = HALO - (CONV_TAPS - 1)
    comm = nxt is not None

    def body(*refs):
        proj_ref, halo_ref, x_ref, mod_ref, vec_ref, wm_ref, bs_ref, cw_ref, gw_hbm = refs[:9]
        refs = refs[9:]
        if comm:
            nx_ref, refs = refs[0], refs[1:]
        x1_ref, ya_ref, yb_ref, o1_ref, zc_ref = refs[:5]
        refs = refs[5:]
        if comm:
            gwn_ref, refs = refs[0], refs[1:]
        wpa, wpb, wout, sems, zext, zsh, zc_buf, vn_buf, a_buf = refs[:9]
        i = pl.program_id(0)

        if comm:
            ag = functools.partial(_ag_stage1, nx_ref, gwn_ref, *refs[9:12])

            @pl.when(i == 0)
            def _():
                local, sends, _ = ag()
                _start_all(local, sends)

        @pl.when(i == 0)
        def _():
            _load_rows(gw_hbm, wpa, sems, 0, pk.off_pa, pk.n_p)
            _load_rows(gw_hbm, wpb, sems, N_DEV, pk.off_pb, pk.n_p)
            _load_rows(gw_hbm, wout, sems, 2 * N_DEV, pk.off_out, pk.n_p)

        first = (i % tps) == 0
        m = mod_ref[0]
        vhat, _ = _ln_stats(proj_ref[:, d:2 * d].astype(F32))
        vn_buf[...] = (vhat * vec_ref[0:1, :] + vec_ref[1:2, :]).astype(BF16)
        for c in range(tm // CHUNK):
            rs_ = slice(c * CHUNK, (c + 1) * CHUNK)
            for h in range(HEADS):
                cs_ = slice(h * CHUNK, (h + 1) * CHUNK)
                s_b = _nn(wm_ref[h], vn_buf[rs_, cs_]) + bs_ref[:, cs_]
                a_buf[rs_, cs_] = (proj_ref[rs_, cs_].astype(F32) * s_b).astype(BF16)
        y_a = _nn(a_buf[...], wpa[...])
        ya_ref[...] = y_a.astype(BF16)
        z = proj_ref[:, 2 * d:3 * d].astype(F32) * jax.nn.sigmoid(proj_ref[:, 3 * d:4 * d].astype(F32))
        _fill_zext(zext, z, halo_ref, first, tm, d)
        _shift_copies(zext, zsh, tm + HALO - 8)
        _conv_taps(zext, zsh, cw_ref, [lo + k for k in range(CONV_TAPS)], vec_ref[2:3, :], zc_buf, tm, d)
        zc = zc_buf[...]
        zc_ref[...] = zc.astype(BF16)
        zhat, _ = _ln_stats(zc)
        zn = zhat * vec_ref[3:4, :] + vec_ref[4:5, :]
        b_act = (zn * jax.nn.sigmoid(zn)).astype(BF16)
        y_b = _nn(b_act, wpb[...])
        yb_ref[...] = y_b.astype(BF16)
        merged = (jax.nn.sigmoid(proj_ref[:, 4 * d:5 * d].astype(F32)) * y_a
                  + jax.nn.sigmoid(proj_ref[:, 5 * d:6 * d].astype(F32)) * y_b).astype(BF16)
        o1 = _nn(merged, wout[...])
        o1_ref[...] = o1.astype(BF16)
        x1_ref[...] = x_ref[...] + m[:, 2 * d:3 * d] * o1

        if comm:
            @pl.when(i == n - 1)
            def _():
                _finish_all(*ag())

    halo_spec = pl.BlockSpec((HALO, 2 * d), lambda i: (jnp.maximum(i * hb - 1, 0), 1))
    act = _sds((t, d), BF16)
    return pl.pallas_call(
        body, grid=(n,),
        in_specs=[_row_spec(tm, 6 * d), halo_spec, _row_spec(tm, d), _mod_spec(tps, 6 * d), _const_spec((8, d)),
                  _const_spec((HEADS, CHUNK, CHUNK)), _const_spec((CHUNK, d)), _const_spec((8 * HALO, d)), ANY]
        + ([ANY] if comm else []),
        out_specs=[_row_spec(tm, d)] * 5 + ([ANY] if comm else []),
        out_shape=[_sds((t, d), F32), act, act, act, act]
        + ([_sds((N_DEV,) + nxt.shape, nxt.dtype)] if comm else []),
        scratch_shapes=[pltpu.VMEM((d, d), BF16), pltpu.VMEM((d, d), BF16), pltpu.VMEM((d, d), BF16),
                        pltpu.SemaphoreType.DMA((3 * N_DEV,)),
                        pltpu.VMEM((tm + HALO, d), F32), pltpu.VMEM((7, tm + HALO, d), F32),
                        pltpu.VMEM((tm, d), F32), pltpu.VMEM((tm, d), BF16), pltpu.VMEM((tm, d), BF16)]
        + (_sem_scratch(4, True) if comm else []),
        compiler_params=_params(1), name="fwd_mix_ag" if comm else "fwd_mix",
    )(proj, proj, x2d, mod3, vecs, wm, bsx, cw, gw, *([nxt] if comm else []))


def _fwd_ffn(x1, mod3, g2, gw, pk, seq, tm, gw_next=None):
    t, d = x1.shape
    nf = 4 * d
    tps = seq // tm
    n = t // tm
    comm = gw_next is not None

    def body(*refs):
        x_ref, mod_ref, g_ref, gw_hbm = refs[:4]
        refs = refs[4:]
        if comm:
            gwn_in, refs = refs[0], refs[1:]
        x2_ref, f_ref, o2_ref = refs[:3]
        refs = refs[3:]
        if comm:
            gwn_out, refs = refs[0], refs[1:]
        w1, w2, sems = refs[:3]
        i = pl.program_id(0)

        if comm:
            ag = functools.partial(_ag_stage2, gwn_in, gwn_out, *refs[3:5])

            @pl.when(i == 0)
            def _():
                _start_all(None, ag()[0])

        @pl.when(i == 0)
        def _():
            _load_rows(gw_hbm, w1, sems, 0, pk.off_ff1, pk.n_ff)
            _load_rows(gw_hbm, w2, sems, N_DEV, pk.off_ff2, pk.n_ff)

        x = x_ref[...]
        m = mod_ref[0]
        r = lax.rsqrt(_mean(x * x) + EPS)
        h2 = (x * r * g_ref[...] * (1.0 + m[:, 4 * d:5 * d]) + m[:, 3 * d:4 * d]).astype(BF16)
        acc = jnp.zeros(x.shape, F32)
        for j in range(nf // 512):
            js = slice(j * 512, (j + 1) * 512)
            f = _nt(h2, w1[js, :])
            f_ref[:, js] = f.astype(BF16)
            acc = acc + _nn(jnp.square(jnp.maximum(f, 0.0)).astype(BF16), w2[js, :])
        o2_ref[...] = acc.astype(BF16)
        x2_ref[...] = x + m[:, 5 * d:6 * d] * acc

        if comm:
            @pl.when(i == n - 1)
            def _():
                _finish_all(None, *ag())

    return pl.pallas_call(
        body, grid=(n,),
        in_specs=[_row_spec(tm, d), _mod_spec(tps, 6 * d), _const_spec((1, d)), ANY] + ([ANY] if comm else []),
        out_specs=[_row_spec(tm, d), _row_spec(tm, nf), _row_spec(tm, d)] + ([ANY] if comm else []),
        out_shape=[_sds((t, d), F32), _sds((t, nf), BF16), _sds((t, d), BF16)]
        + ([_sds(gw_next.shape, gw_next.dtype)] if comm else []),
        scratch_shapes=[pltpu.VMEM((nf, d), BF16), pltpu.VMEM((nf, d), BF16), pltpu.SemaphoreType.DMA((2 * N_DEV,))]
        + (_sem_scratch(3, False) if comm else []),
        input_output_aliases={4: 3} if comm else {},
        compiler_params=_params(1), name="fwd_ffn_ag" if comm else "fwd_ffn",
    )(x1, mod3, g2, gw, *([gw_next] if comm else []))


def _loss_head(x, tgt, fg, tm):
    t, d = x.shape
    n = t // tm

    def body(x_ref, t_ref, g_ref, dx_ref, loss_ref, dg_ref, lacc):
        i = pl.program_id(0)

        @pl.when(i == 0)
        def _():
            lacc[...] = jnp.zeros(lacc.shape, F32)
            dg_ref[...] = jnp.zeros(dg_ref.shape, F32)

        xv = x_ref[...]
        g = g_ref[...]
        r = lax.rsqrt(_mean(xv * xv) + EPS)
        err = xv * r * g - t_ref[...]
        lacc[...] = lacc[...] + _rowsum(err * err)
        dy = err * (1.0 / d)
        _add_row(dg_ref, (slice(0, 1), slice(None)), _rowsum(dy * xv * r))
        dx_ref[...] = _rms_bwd(dy, g, xv, r)

        @pl.when(i == n - 1)
        def _():
            loss_ref[...] = jnp.broadcast_to(jnp.sum(lacc[...], keepdims=True) * (0.5 / d), loss_ref.shape)

    return pl.pallas_call(
        body, grid=(n,),
        in_specs=[_row_spec(tm, d), _row_spec(tm, d), _const_spec((1, d))],
        out_specs=[_row_spec(tm, d), _const_spec((8, 128)), _const_spec((8, d))],
        out_shape=[_sds((t, d), F32), _sds((8, 128), F32), _sds((8, d), F32)],
        scratch_shapes=[pltpu.VMEM((1, d), F32)],
        compiler_params=_params(1), name="loss_head",
    )(x, tgt, fg)


def _bwd_ffn(dx2, x1, f, o2, mod3, g2, gw, pk, seq, tm):
    t, d = x1.shape
    nf = 4 * d
    tps = seq // tm
    nb = t // seq

    def body(dx2_ref, x_ref, f_ref, o2_ref, mod_ref, g_ref, gw_hbm,
             dx1_ref, df_ref, do2_ref, h2_ref, ms_ref, ps_ref, w1, w2, sems):
        i = pl.program_id(0)

        @pl.when(i == 0)
        def _():
            _load_rows(gw_hbm, w1, sems, 0, pk.off_ff1, pk.n_ff)
            _load_rows(gw_hbm, w2, sems, N_DEV, pk.off_ff2, pk.n_ff)
            ps_ref[...] = jnp.zeros(ps_ref.shape, F32)

        @pl.when((i % tps) == 0)
        def _():
            ms_ref[...] = jnp.zeros(ms_ref.shape, F32)

        dx2 = dx2_ref[...]
        x = x_ref[...]
        m = mod_ref[0]
        g = g_ref[...]
        sh2, sc2, gt2 = m[:, 3 * d:4 * d], m[:, 4 * d:5 * d], m[:, 5 * d:6 * d]
        _add_row(ms_ref, (0, slice(2, 3), slice(None)), _rowsum(dx2 * o2_ref[...].astype(F32)))
        do2 = (gt2 * dx2).astype(BF16)
        do2_ref[...] = do2
        r = lax.rsqrt(_mean(x * x) + EPS)
        n = x * r * g
        h2_ref[...] = (n * (1.0 + sc2) + sh2).astype(BF16)
        dh = jnp.zeros(x.shape, F32)
        for j in range(nf // 512):
            js = slice(j * 512, (j + 1) * 512)
            dr = _nt(do2, w2[js, :])
            df = (dr * (2.0 * jnp.maximum(f_ref[:, js].astype(F32), 0.0))).astype(BF16)
            df_ref[:, js] = df
            dh = dh + _nn(df, w1[js, :])
        _add_row(ms_ref, (0, slice(0, 1), slice(None)), _rowsum(dh))
        _add_row(ms_ref, (0, slice(1, 2), slice(None)), _rowsum(dh * n))
        dn = dh * (1.0 + sc2)
        _add_row(ps_ref, (slice(0, 1), slice(None)), _rowsum(dn * x * r))
        dx1_ref[...] = dx2 + _rms_bwd(dn, g, x, r)

    act = _sds((t, d), BF16)
    return pl.pallas_call(
        body, grid=(t // tm,),
        in_specs=[_row_spec(tm, d), _row_spec(tm, d), _row_spec(tm, nf), _row_spec(tm, d), _mod_spec(tps, 6 * d),
                  _const_spec((1, d)), ANY],
        out_specs=[_row_spec(tm, d), _row_spec(tm, nf), _row_spec(tm, d), _row_spec(tm, d), _mstat_spec(tps, d),
                   _const_spec((8, d))],
        out_shape=[_sds((t, d), F32), _sds((t, nf), BF16), act, act, _sds((nb, 8, d), F32), _sds((8, d), F32)],
        scratch_shapes=[pltpu.VMEM((nf, d), BF16), pltpu.VMEM((nf, d), BF16), pltpu.SemaphoreType.DMA((2 * N_DEV,))],
        compiler_params=_params(1), name="bwd_ffn",
    )(dx2, x1, f, o2, mod3, g2, gw)


def _bwd_mix(dx1, proj, ya, yb, o1, zc, mod3, vecs, wm, wmt, bsx, gw, pk, seq, tm):
    t, d = dx1.shape
    tps = seq // tm
    nb = t // seq
    n = t // tm

    def body(dx1_ref, proj_ref, ya_ref, yb_ref, o1_ref, zc_ref, mod_ref, vec_ref, wm_ref, wmt_ref, bs_ref, gw_hbm,
             dp_ref, dzc_ref, do1_ref, dya_ref, dyb_ref, mg_ref, aa_ref, ba_ref, ms_ref, ps_ref, dws_ref, dbs_ref,
             wpa, wpb, wout, sems, vn_buf, da_buf, dvn_buf):
        i = pl.program_id(0)

        @pl.when(i == 0)
        def _():
            _load_rows(gw_hbm, wpa, sems, 0, pk.off_pa, pk.n_p)
            _load_rows(gw_hbm, wpb, sems, N_DEV, pk.off_pb, pk.n_p)
            _load_rows(gw_hbm, wout, sems, 2 * N_DEV, pk.off_out, pk.n_p)
            ps_ref[...] = jnp.zeros(ps_ref.shape, F32)
            dws_ref[...] = jnp.zeros(dws_ref.shape, F32)
            dbs_ref[...] = jnp.zeros(dbs_ref.shape, F32)

        @pl.when((i % tps) == 0)
        def _():
            ms_ref[...] = jnp.zeros(ms_ref.shape, F32)

        m = mod_ref[0]
        dx1v = dx1_ref[...]
        _add_row(ms_ref, (0, slice(0, 1), slice(None)), _rowsum(dx1v * o1_ref[...].astype(F32)))
        do1 = (m[:, 2 * d:3 * d] * dx1v).astype(BF16)
        do1_ref[...] = do1
        dmg = _nt(do1, wout[...])
        sa = jax.nn.sigmoid(proj_ref[:, 4 * d:5 * d].astype(F32))
        sb = jax.nn.sigmoid(proj_ref[:, 5 * d:6 * d].astype(F32))
        y_a = ya_ref[...].astype(F32)
        y_b = yb_ref[...].astype(F32)
        dya = (dmg * sa).astype(BF16)
        dyb = (dmg * sb).astype(BF16)
        dya_ref[...] = dya
        dyb_ref[...] = dyb
        dp_ref[:, 4 * d:5 * d] = (dmg * y_a * sa * (1.0 - sa)).astype(BF16)
        dp_ref[:, 5 * d:6 * d] = (dmg * y_b * sb * (1.0 - sb)).astype(BF16)
        mg_ref[...] = (sa * y_a + sb * y_b).astype(BF16)
        da_buf[...] = _nt(dya, wpa[...])
        db = _nt(dyb, wpb[...])
        vhat, rs = _ln_stats(proj_ref[:, d:2 * d].astype(F32))
        alg = vec_ref[0:1, :]
        vn_buf[...] = (vhat * alg + vec_ref[1:2, :]).astype(BF16)
        for c in range(tm // CHUNK):
            rs_ = slice(c * CHUNK, (c + 1) * CHUNK)
            for h in range(HEADS):
                cs_ = slice(h * CHUNK, (h + 1) * CHUNK)
                vn_b = vn_buf[rs_, cs_]
                s_b = _nn(wm_ref[h], vn_b) + bs_ref[:, cs_]
                u_b = proj_ref[rs_, cs_].astype(F32)
                da_b = da_buf[rs_, cs_]
                aa_ref[rs_, cs_] = (u_b * s_b).astype(BF16)
                dp_ref[rs_, cs_] = (da_b * s_b).astype(BF16)
                ds_b = da_b * u_b
                dbs_ref[:, cs_] = dbs_ref[:, cs_] + ds_b
                ds_bf = ds_b.astype(BF16)
                dvn_buf[rs_, cs_] = _nn(wmt_ref[h], ds_bf)
                dws_ref[:, cs_] = dws_ref[:, cs_] + _nt(ds_bf, vn_b)
        dvn = dvn_buf[...]
        _add_row(ps_ref, (slice(0, 1), slice(None)), _rowsum(dvn * vhat))
        _add_row(ps_ref, (slice(1, 2), slice(None)), _rowsum(dvn))
        dp_ref[:, d:2 * d] = _ln_bwd(dvn, alg, vhat, rs).astype(BF16)
        dp_ref[:, 2 * d:4 * d] = jnp.zeros((tm, 2 * d), BF16)
        zhat, rsb = _ln_stats(zc_ref[...].astype(F32))
        blg = vec_ref[3:4, :]
        zn = zhat * blg + vec_ref[4:5, :]
        sg = jax.nn.sigmoid(zn)
        ba_ref[...] = (zn * sg).astype(BF16)
        dzn = db * (sg * (1.0 + zn * (1.0 - sg)))
        _add_row(ps_ref, (slice(2, 3), slice(None)), _rowsum(dzn * zhat))
        _add_row(ps_ref, (slice(3, 4), slice(None)), _rowsum(dzn))
        dzc = _ln_bwd(dzn, blg, zhat, rsb)
        _add_row(ps_ref, (slice(4, 5), slice(None)), _rowsum(dzc))
        dzc_ref[...] = dzc.astype(BF16)

        @pl.when(i == n - 1)
        def _():
            causal = (lax.broadcasted_iota(jnp.int32, (CHUNK, CHUNK), 0)
                      >= lax.broadcasted_iota(jnp.int32, (CHUNK, CHUNK), 1))
            for h in range(HEADS):
                cs_ = slice(h * CHUNK, (h + 1) * CHUNK)
                dws_ref[:, cs_] = jnp.where(causal, dws_ref[:, cs_], 0.0)
                dbs_ref[:, cs_] = jnp.broadcast_to(jnp.sum(dbs_ref[:, cs_], axis=1, keepdims=True), (CHUNK, CHUNK))

    act = _sds((t, d), BF16)
    return pl.pallas_call(
        body, grid=(n,),
        in_specs=[_row_spec(tm, d), _row_spec(tm, 6 * d), _row_spec(tm, d), _row_spec(tm, d), _row_spec(tm, d),
                  _row_spec(tm, d), _mod_spec(tps, 6 * d), _const_spec((8, d)), _const_spec((HEADS, CHUNK, CHUNK)),
                  _const_spec((HEADS, CHUNK, CHUNK)), _const_spec((CHUNK, d)), ANY],
        out_specs=[_row_spec(tm, 6 * d)] + [_row_spec(tm, d)] * 7
        + [_mstat_spec(tps, d), _const_spec((8, d)), _const_spec((CHUNK, d)), _const_spec((CHUNK, d))],
        out_shape=[_sds((t, 6 * d), BF16)] + [act] * 7
        + [_sds((nb, 8, d), F32), _sds((8, d), F32), _sds((CHUNK, d), F32), _sds((CHUNK, d), F32)],
        scratch_shapes=[pltpu.VMEM((d, d), BF16), pltpu.VMEM((d, d), BF16), pltpu.VMEM((d, d), BF16),
                        pltpu.SemaphoreType.DMA((3 * N_DEV,)),
                        pltpu.VMEM((tm, d), BF16), pltpu.VMEM((tm, d), F32), pltpu.VMEM((tm, d), F32)],
        compiler_params=_params(1), name="bwd_mix",
    )(dx1, proj, ya, yb, o1, zc, mod3, vecs, wm, wmt, bsx, gw)


def _bwd_in(dproj, dzc, proj, x2d, dx1, mod3, g1, cw, gw, pk, seq, tm, part=None):
    t, d = x2d.shape
    tps = seq // tm
    nb = t // seq
    n = t // tm
    hb = tm // HALO
    lo = HALO - (CONV_TAPS - 1)
    comm = part is not None

    def body(*refs):
        dpi_ref, dzc_ref, dzn_ref, pp_ref, halo_ref, x_ref, dx1_ref, mod_ref, g_ref, cw_ref, gw_hbm = refs[:11]
        refs = refs[11:]
        if comm:
            part_ref, refs = refs[0], refs[1:]
        dpo_ref, dx_ref, h_ref, ms_ref, ps_ref, dcw_ref = refs[:6]
        refs = refs[6:]
        if comm:
            got_ref, refs = refs[0], refs[1:]
        w_vm, sems, zext, dzext, zsh, dz_buf, dcw_acc = refs[:7]
        i = pl.program_id(0)

        if comm:
            rs = functools.partial(_rs_chip_copies, part_ref, got_ref, *refs[7:10])

            @pl.when(i == 0)
            def _():
                local, sends, _ = rs()
                _start_all(local, sends)

        @pl.when(i == 0)
        def _():
            _load_rows(gw_hbm, w_vm, sems, 0, pk.off_in, pk.n_in)
            ps_ref[...] = jnp.zeros(ps_ref.shape, F32)
            dcw_acc[...] = jnp.zeros(dcw_acc.shape, F32)

        first = (i % tps) == 0
        last = (i % tps) == tps - 1

        @pl.when(first)
        def _():
            ms_ref[...] = jnp.zeros(ms_ref.shape, F32)

        pa = pp_ref[:, 0:d].astype(F32)
        sgp = jax.nn.sigmoid(pp_ref[:, d:2 * d].astype(F32))
        _fill_zext(zext, pa * sgp, halo_ref, first, tm, d)
        dzext[pl.ds(0, tm), :] = dzc_ref[...].astype(F32)
        dzext[pl.ds(tm, HALO), :] = jnp.where(last, 0.0, dzn_ref[...].astype(F32))
        _shift_copies(zext, zsh, tm + HALO - 8)
        _conv_wgrad(dzext, zext, zsh, dcw_acc, lo, tm, d)
        _shift_copies(dzext, zsh, tm + HALO - 8)
        _conv_taps(dzext, zsh, cw_ref, [CONV_TAPS - 1 - k for k in range(CONV_TAPS)], jnp.zeros((1, d), F32),
                   dz_buf, tm, d)
        dz = dz_buf[...]
        dpa = (dz * sgp).astype(BF16)
        dpg = (dz * pa * sgp * (1.0 - sgp)).astype(BF16)
        dpo_ref[:, 0:d] = dpa
        dpo_ref[:, d:2 * d] = dpg
        dh = (_nn(dpi_ref[:, 0:2 * d], w_vm[0:2 * d, :]) + _nn(dpa, w_vm[2 * d:3 * d, :])
              + _nn(dpg, w_vm[3 * d:4 * d, :]) + _nn(dpi_ref[:, 4 * d:6 * d], w_vm[4 * d:6 * d, :]))
        x = x_ref[...]
        m = mod_ref[0]
        g = g_ref[...]
        sh1, sc1 = m[:, 0:d], m[:, d:2 * d]
        r = lax.rsqrt(_mean(x * x) + EPS)
        nrm = x * r * g
        h_ref[...] = (nrm * (1.0 + sc1) + sh1).astype(BF16)
        _add_row(ms_ref, (0, slice(0, 1), slice(None)), _rowsum(dh))
        _add_row(ms_ref, (0, slice(1, 2), slice(None)), _rowsum(dh * nrm))
        dn = dh * (1.0 + sc1)
        _add_row(ps_ref, (slice(0, 1), slice(None)), _rowsum(dn * x * r))
        dx_ref[...] = dx1_ref[...] + _rms_bwd(dn, g, x, r)

        @pl.when(i == n - 1)
        def _():
            for k in range(CONV_TAPS):
                dcw_ref[k:k + 1, :] = _rowsum(dcw_acc[pl.ds(k * 8, 8), :])
            dcw_ref[CONV_TAPS:HALO, :] = jnp.zeros((HALO - CONV_TAPS, d), F32)

        if comm:
            @pl.when(i == n - 1)
            def _():
                _finish_all(*rs())

    halo_prev = pl.BlockSpec((HALO, 2 * d), lambda i: (jnp.maximum(i * hb - 1, 0), 1))
    halo_next = pl.BlockSpec((HALO, d), lambda i: (jnp.minimum((i + 1) * hb, t // HALO - 1), 0))
    return pl.pallas_call(
        body, grid=(n,),
        in_specs=[_row_spec(tm, 6 * d), _row_spec(tm, d), halo_next, _row_spec(tm, 2 * d, 1), halo_prev,
                  _row_spec(tm, d), _row_spec(tm, d), _mod_spec(tps, 6 * d), _const_spec((1, d)),
                  _const_spec((8 * HALO, d)), ANY] + ([ANY] if comm else []),
        out_specs=[_row_spec(tm, 2 * d, 1), _row_spec(tm, d), _row_spec(tm, d), _mstat_spec(tps, d),
                   _const_spec((8, d)), _const_spec((HALO, d))] + ([ANY] if comm else []),
        out_shape=[_sds((t, 6 * d), BF16), _sds((t, d), F32), _sds((t, d), BF16), _sds((nb, 8, d), F32),
                   _sds((8, d), F32), _sds((HALO, d), F32)] + ([_sds(part.shape, part.dtype)] if comm else []),
        scratch_shapes=[pltpu.VMEM((6 * d, d), BF16), pltpu.SemaphoreType.DMA((N_DEV,)),
                        pltpu.VMEM((tm + HALO, d), F32), pltpu.VMEM((tm + HALO, d), F32),
                        pltpu.VMEM((7, tm + HALO, d), F32),
                        pltpu.VMEM((tm, d), F32), pltpu.VMEM((CONV_TAPS * 8, d), F32)]
        + (_sem_scratch(3, True) if comm else []),
        input_output_aliases={0: 0},
        compiler_params=_params(1), name="bwd_in_rs" if comm else "bwd_in",
    )(dproj, dzc, dzc, proj, proj, x2d, dx1, mod3, g1, cw, gw, *([part] if comm else []))


def _wgrad(a, b, tmo, relu2, name):
    t, mo = a.shape
    nn_ = b.shape[1]
    tk = min(TK_WGRAD, t)
    nk = t // tk

    def body(a_ref, b_ref, o_ref, acc):
        k = pl.program_id(1)

        @pl.when(k == 0)
        def _():
            acc[...] = jnp.zeros(acc.shape, F32)

        av = a_ref[...]
        if relu2:
            av = jnp.square(jnp.maximum(av, 0.0))
        acc[...] = acc[...] + _tn(av, b_ref[...])

        @pl.when(k == nk - 1)
        def _():
            o_ref[...] = acc[...].astype(BF16)

    return pl.pallas_call(
        body, grid=(mo // tmo, nk),
        in_specs=[pl.BlockSpec((tk, tmo), lambda i, k: (k, i)), pl.BlockSpec((tk, nn_), lambda i, k: (k, 0))],
        out_specs=pl.BlockSpec((tmo, nn_), lambda i, k: (i, 0)),
        out_shape=_sds((mo, nn_), BF16),
        scratch_shapes=[pltpu.VMEM((tmo, nn_), F32)],
        compiler_params=_params(2), name=name,
    )(a, b)


def _mod_fwd(c_all, w_ada, b_cols):
    nl, d, cols = w_ada.shape
    bsz = c_all.shape[0]

    def body(c_ref, w_ref, b_ref, o_ref):
        cv = c_ref[...]
        ca = cv * jax.nn.sigmoid(cv)
        o_ref[0] = jnp.dot(ca, w_ref[0], preferred_element_type=F32, precision=lax.Precision.HIGHEST) + b_ref[0]

    return pl.pallas_call(
        body, grid=(nl,),
        in_specs=[_const_spec((bsz, d)), pl.BlockSpec((1, d, cols), lambda l: (l, 0, 0)),
                  pl.BlockSpec((1, 1, cols), lambda l: (l, 0, 0))],
        out_specs=pl.BlockSpec((1, bsz, cols), lambda l: (l, 0, 0)),
        out_shape=_sds((nl, bsz, cols), F32),
        compiler_params=_params(1), name="mod_fwd",
    )(c_all, w_ada, b_cols)


def _mod_bwd(c_all, dmod_cols, dmod_all):
    nl, bsz, cols = dmod_cols.shape
    d = c_all.shape[1]
    ncol = dmod_all.shape[2]

    def body(c_ref, dc_ref, da_ref, dw_ref, db_ref):
        cv = c_ref[...]
        ca = cv * jax.nn.sigmoid(cv)
        dw_ref[0] = lax.dot_general(ca, dc_ref[0], (((0,), (0,)), ((), ())), preferred_element_type=F32,
                                    precision=lax.Precision.HIGHEST)
        db_ref[0] = _rowsum(da_ref[0])

    return pl.pallas_call(
        body, grid=(nl,),
        in_specs=[_const_spec((bsz, d)), pl.BlockSpec((1, bsz, cols), lambda l: (l, 0, 0)),
                  pl.BlockSpec((1, bsz, ncol), lambda l: (l, 0, 0))],
        out_specs=[pl.BlockSpec((1, d, cols), lambda l: (l, 0, 0)), pl.BlockSpec((1, 1, ncol), lambda l: (l, 0, 0))],
        out_shape=[_sds((nl, d, cols), F32), _sds((nl, 1, ncol), F32)],
        compiler_params=_params(1), name="mod_bwd",
    )(c_all, dmod_cols, dmod_all)


def _row_tile(rows, cols, nbuf, itemsize=4, budget=24 * 1024 * 1024):
    cap = max(16, budget // (2 * nbuf * cols * itemsize))
    if rows <= cap:
        return rows
    best = None
    for tr in range(16, cap + 1, 16):
        if rows % tr == 0:
            best = tr
    assert best is not None, (rows, cols)
    return best


def _sum_blocks(xs, name):
    nblk, rows, cols = xs.shape
    tr = _row_tile(rows, cols, nblk + 1)

    def body(x_ref, o_ref):
        acc = x_ref[0].astype(F32)
        for j in range(1, nblk):
            acc = acc + x_ref[j].astype(F32)
        o_ref[...] = acc

    return pl.pallas_call(
        body, grid=(rows // tr,),
        in_specs=[pl.BlockSpec((nblk, tr, cols), lambda i: (0, i, 0))],
        out_specs=pl.BlockSpec((tr, cols), lambda i: (i, 0)),
        out_shape=_sds((rows, cols), F32),
        compiler_params=_params(1), name=name,
    )(xs)


def _add_sibling(dp, recv, core):
    nq, _, rows, cols = dp.shape
    tr = _row_tile(rows, cols, 3, itemsize=2)

    def body(c_ref, a_ref, b_ref, o_ref):
        o_ref[...] = (a_ref[...].astype(F32) + b_ref[...].astype(F32)).astype(BF16)

    return pl.pallas_call(
        body,
        grid_spec=pltpu.PrefetchScalarGridSpec(
            num_scalar_prefetch=1, grid=(nq, rows // tr),
            in_specs=[pl.BlockSpec((1, 1, tr, cols), lambda q, i, c: (q, c[0], i, 0)),
                      pl.BlockSpec((1, 1, tr, cols), lambda q, i, c: (q, 0, i, 0))],
            out_specs=pl.BlockSpec((1, 1, tr, cols), lambda q, i, c: (q, 0, i, 0))),
        out_shape=_sds((nq, 1, rows, cols), BF16),
        compiler_params=_params(2), name="add_sibling",
    )(core, dp, recv.reshape(nq, 1, rows, cols)).reshape(nq, rows, cols)


def _adamw(w, g, m, v, name):
    rows, cols = w.shape
    tr = _row_tile(rows, cols, 7)
    c1 = 1.0 - ADAM_B1 ** ADAM_STEP
    c2 = 1.0 - ADAM_B2 ** ADAM_STEP

    def body(w_ref, g_ref, m_ref, v_ref, d_ref, nm_ref, nv_ref):
        gv = g_ref[...]
        nm = ADAM_B1 * m_ref[...] + (1.0 - ADAM_B1) * gv
        nv = ADAM_B2 * v_ref[...] + (1.0 - ADAM_B2) * (gv * gv)
        nm_ref[...] = nm
        nv_ref[...] = nv
        d_ref[...] = -ADAM_LR * ((nm / c1) / (jnp.sqrt(nv / c2) + ADAM_EPS) + ADAM_WD * w_ref[...])

    spec = pl.BlockSpec((tr, cols), lambda i: (i, 0))
    out = _sds((rows, cols), F32)
    return pl.pallas_call(
        body, grid=(rows // tr,), in_specs=[spec] * 4, out_specs=[spec] * 3, out_shape=[out] * 3,
        compiler_params=_params(1), name=name,
    )(w, g, m, v)


def _all_gather(xs, name):
    rows, cols = xs.shape

    def body(x_ref, out_ref, send1, recv1, local_sem, send2, recv2):
        local, first, arrivals = _ag_stage1(x_ref, out_ref, send1, recv1, local_sem)
        _start_all(local, first)
        passed, from_sibling = _ag_stage2(out_ref, out_ref, send2, recv2)
        for arrival, onward in zip(arrivals[1:], passed):
            arrival.wait_recv()
            onward.start()
        arrivals[0].wait_recv()
        _finish_all(local, first + passed, from_sibling)

    return pl.pallas_call(
        body, out_shape=_sds((N_DEV, rows, cols), xs.dtype), in_specs=[ANY], out_specs=ANY,
        scratch_shapes=_sem_scratch(4, True) + _sem_scratch(3, False), name=name,
    )(xs)


def _sibling_exchange(dp):
    nq, _, rows, cols = dp.shape

    def body(x_ref, out_ref, send_sem, recv_sem):
        x, y, c = _position()
        cp = pltpu.make_async_remote_copy(
            src_ref=x_ref.at[pl.ds(0, nq), 1 - c], dst_ref=out_ref, send_sem=send_sem, recv_sem=recv_sem,
            device_id=(x, y, 1 - c), device_id_type=MESH)
        cp.start()
        cp.wait()

    return pl.pallas_call(
        body, out_shape=_sds((nq, rows, cols), dp.dtype), in_specs=[ANY], out_specs=ANY,
        scratch_shapes=[pltpu.SemaphoreType.DMA(()), pltpu.SemaphoreType.DMA(())],
        name="rs_sibling",
    )(dp)


def _chip_all_to_all(xs):
    nq, rows, cols = xs.shape

    def body(x_ref, out_ref, send_sems, recv_sems, local_sem):
        local, sends, recvs = _rs_chip_copies(x_ref, out_ref, send_sems, recv_sems, local_sem)
        _start_all(local, sends)
        _finish_all(local, sends, recvs)

    return pl.pallas_call(
        body, out_shape=_sds((nq, rows, cols), xs.dtype), in_specs=[ANY], out_specs=ANY,
        scratch_shapes=_sem_scratch(3, True), name="rs_chips",
    )(xs)


def _pad_rows(a, rows):
    return jnp.pad(a, ((0, rows - a.shape[0]), (0, 0)))


def kernel(x, c, w_ada, b_ada, norm1_g, w_in, a_ln_g, a_ln_b, a_ws, a_bs, w_pa, b_conv_w, b_conv_b, b_ln_g, b_ln_b, w_pb, w_out, norm2_g, w_ff1, w_ff2, final_g, loss_target, m_w_ada, m_b_ada, m_norm1_g, m_w_in, m_a_ln_g, m_a_ln_b, m_a_ws, m_a_bs, m_w_pa, m_b_conv_w, m_b_conv_b, m_b_ln_g, m_b_ln_b, m_w_pb, m_w_out, m_norm2_g, m_w_ff1, m_w_ff2, m_final_g, v_w_ada, v_b_ada, v_norm1_g, v_w_in, v_a_ln_g, v_a_ln_b, v_a_ws, v_a_bs, v_w_pa, v_b_conv_w, v_b_conv_b, v_b_ln_g, v_b_ln_b, v_w_pb, v_w_out, v_norm2_g, v_w_ff1, v_w_ff2, v_final_g):
    nb, seq, d = x.shape
    nl = w_in.shape[0]
    t = nb * seq
    pk = _Pack(d)
    assert d % (N_DEV * CHUNK) == 0 and d // HEADS == CHUNK and seq % CHUNK == 0
    tm_big = min(TM_BIG, seq)
    tm_mix = min(TM_MIX, seq)
    ax, ay, ac = _position()
    dev = 4 * ax + 2 * ay + ac
    ncol = 6 * d
    cols = ncol // N_DEV
    cpd = d // N_DEV
    bsz = nb * N_DEV

    cw_rows = nl * HALO
    small = jnp.concatenate([
        c.reshape(nb * d // CHUNK, CHUNK),
        jnp.pad(b_conv_w.reshape(nl, CONV_TAPS, cpd), ((0, 0), (0, HALO - CONV_TAPS), (0, 0))).reshape(cw_rows, cpd),
    ], axis=0)
    c_rows = nb * d // CHUNK
    small_all = _all_gather(small, "ag_small")
    c_all = small_all[:, :c_rows].reshape(bsz, d)
    cw_all = small_all[:, c_rows:].reshape(N_DEV, nl, HALO, cpd).transpose(1, 2, 0, 3).reshape(nl, HALO, d)
    cwb_all = jnp.repeat(cw_all, 8, axis=1)
    b_cols = lax.dynamic_slice_in_dim(b_ada, dev * cols, cols, axis=1).reshape(nl, 1, cols)
    mod_cols = _mod_fwd(c_all, w_ada, b_cols)
    mod_all = _all_gather(mod_cols.reshape(nl * bsz, cols), "ag_mod")
    mod_all = mod_all.reshape(N_DEV, nl, bsz, cols).transpose(1, 2, 0, 3).reshape(nl, bsz, ncol)
    mod_mine = lax.dynamic_slice_in_dim(mod_all, dev * nb, nb, axis=1)

    causal = jnp.tril(jnp.ones((CHUNK, CHUNK), bool))
    wm_all = jnp.where(causal[None, None], a_ws, 0.0)
    wm_bf = wm_all.astype(BF16)
    wmt_bf = jnp.swapaxes(wm_all, 2, 3).astype(BF16)
    bsx_all = jnp.broadcast_to(jnp.swapaxes(a_bs, 1, 2)[:, :, :, None], (nl, CHUNK, HEADS, CHUNK)).reshape(nl, CHUNK, d)

    def vec_rows(l):
        return _pad_rows(jnp.stack([a_ln_g[l], a_ln_b[l], b_conv_b[l], b_ln_g[l], b_ln_b[l]]), 8)

    def weight_block(l):
        return jnp.concatenate([
            w_in[l].T, w_ff1[l].T, w_ff2[l], w_pa[l], w_pb[l], w_out[l]], axis=0).astype(BF16)

    xs = x.reshape(t, d)
    saved = []
    gw = _all_gather(weight_block(0), "ag_weights")
    for l in range(nl):
        nxt = weight_block(l + 1) if l + 1 < nl else None
        mod3 = mod_mine[l].reshape(nb, 1, ncol)
        vecs = vec_rows(l)
        proj = _fwd_in(xs, mod3, norm1_g[l].reshape(1, d), gw, pk, seq, tm_big)
        x1, ya, yb, o1, zc, *gw_next = _fwd_mix(
            proj, xs, mod3, vecs, wm_bf[l], bsx_all[l], cwb_all[l], gw, pk, seq, tm_mix, nxt)
        x2, f, o2, *gw_next = _fwd_ffn(x1, mod3, norm2_g[l].reshape(1, d), gw, pk, seq, tm_big, *gw_next)
        saved.append((xs, x1, proj, ya, yb, o1, zc, f, o2, gw, mod3, vecs))
        xs = x2
        if gw_next:
            gw = gw_next[0]

    dx, loss_blk, dfg = _loss_head(xs, loss_target.reshape(t, d), final_g.reshape(1, d), tm_big)
    loss = lax.psum(loss_blk[0, 0], ("x", "y", "c"))

    core = ac.reshape(1).astype(jnp.int32)
    wg = {k: [None] * nl for k in ("w_in", "w_ff1", "w_ff2", "w_pa", "w_pb", "w_out")}
    small_red = [None] * nl
    dmod_rows = [None] * nl
    per_layer = 8 + 2 * CHUNK + HALO
    assert per_layer <= N_DEV * SMALL_ROWS <= N_DEV * SMALL_SLOT

    def reduced(l, red):
        wg["w_in"][l] = red[pk.off_in:pk.off_ff1].T
        wg["w_ff1"][l] = red[pk.off_ff1:pk.off_ff2].T
        wg["w_ff2"][l] = red[pk.off_ff2:pk.off_pa]
        wg["w_pa"][l] = red[pk.off_pa:pk.off_pb]
        wg["w_pb"][l] = red[pk.off_pb:pk.off_out]
        wg["w_out"][l] = red[pk.off_out:pk.rows]
        small_red[l] = red[pk.rows:pk.rows + SMALL_ROWS]

    pending = None
    for l in reversed(range(nl)):
        x0, x1, proj, ya, yb, o1, zc, f, o2, gw, mod3, vecs = saved[l]
        dx1, df, do2, h2, ms2, ps2 = _bwd_ffn(dx, x1, f, o2, mod3, norm2_g[l].reshape(1, d), gw, pk, seq, tm_mix)
        (dproj, dzc, do1, dya, dyb, mg, aa, ba, ms1, ps1, dws, dbs) = _bwd_mix(
            dx1, proj, ya, yb, o1, zc, mod3, vecs, wm_bf[l], wmt_bf[l], bsx_all[l], gw, pk, seq, tm_mix)
        dproj, dx, h, ms0, ps0, dcw, *got = _bwd_in(
            dproj, dzc, proj, x0, dx1, mod3, norm1_g[l].reshape(1, d), cwb_all[l], gw, pk, seq, tm_mix,
            None if pending is None else pending[1])
        if pending is not None:
            reduced(pending[0], _sum_blocks(got[0], "sum_chips"))
        vec_g = jnp.concatenate([ps0[0:1], ps1[0:5], ps2[0:1], jnp.zeros((1, d), F32)], axis=0)
        small = _pad_rows(jnp.concatenate([vec_g, dws, dbs, dcw], axis=0), N_DEV * SMALL_ROWS)
        small = jnp.pad(small.reshape(N_DEV, SMALL_ROWS, d).astype(BF16), ((0, 0), (0, SMALL_SLOT - SMALL_ROWS), (0, 0)))
        grads = jnp.concatenate([
            _wgrad(dproj, h, pk.n_in, False, "wgrad_in").reshape(N_DEV, pk.n_in, d),
            _wgrad(df, h2, d, False, "wgrad_ff1").reshape(N_DEV, pk.n_ff, d),
            _wgrad(f, do2, d, True, "wgrad_ff2").reshape(N_DEV, pk.n_ff, d),
            _wgrad(aa, dya, d, False, "wgrad_pa").reshape(N_DEV, pk.n_p, d),
            _wgrad(ba, dyb, d, False, "wgrad_pb").reshape(N_DEV, pk.n_p, d),
            _wgrad(mg, do1, d, False, "wgrad_out").reshape(N_DEV, pk.n_p, d),
            small,
        ], axis=1)
        dp = grads.reshape(N_CHIP, 2, pk.rows + SMALL_SLOT, d)
        pending = (l, _add_sibling(dp, _sibling_exchange(dp), core))
        dmod_rows[l] = jnp.concatenate([ms0[:, 0], ms0[:, 1], ms1[:, 0], ms2[:, 0], ms2[:, 1], ms2[:, 2]], axis=1)
    reduced(pending[0], _sum_blocks(_chip_all_to_all(pending[1]), "sum_chips"))

    small_all = _all_gather(jnp.concatenate(small_red, axis=0), "ag_small_grads")
    lay = small_all.reshape(N_DEV, nl, SMALL_ROWS, d).transpose(1, 0, 2, 3).reshape(nl, N_DEV * SMALL_ROWS, d)
    n_dm = nl * nb * 6
    tail = jnp.concatenate([jnp.stack(dmod_rows).reshape(n_dm, d), dfg], axis=0)
    tail_all = _all_gather(tail, "ag_dmod")
    dmod_all = tail_all[:, :n_dm].reshape(N_DEV, nl, nb, ncol).transpose(1, 0, 2, 3).reshape(nl, bsz, ncol)
    dmod_cols = lax.dynamic_slice_in_dim(dmod_all, dev * cols, cols, axis=2)
    g_w_ada, g_b_ada = _mod_bwd(c_all, dmod_cols, dmod_all)
    g_final = _sum_blocks(tail_all[:, n_dm:], "sum_final_g")[0]

    g_small = {
        "norm1_g": lay[:, 0], "a_ln_g": lay[:, 1], "a_ln_b": lay[:, 2], "b_ln_g": lay[:, 3], "b_ln_b": lay[:, 4],
        "b_conv_b": lay[:, 5], "norm2_g": lay[:, 6],
        "a_ws": lay[:, 8:8 + CHUNK].reshape(nl, CHUNK, HEADS, CHUNK).transpose(0, 2, 1, 3),
        "a_bs": jnp.swapaxes(lay[:, 8 + CHUNK:8 + 2 * CHUNK, ::CHUNK], 1, 2),
        "b_conv_w": lax.dynamic_slice_in_dim(
            lay[:, 8 + 2 * CHUNK:8 + 2 * CHUNK + CONV_TAPS], dev * cpd, cpd, axis=2).reshape(nl, CONV_TAPS, 1, cpd),
        "final_g": g_final,
    }
    grads = dict(g_small)
    grads["w_ada"] = g_w_ada
    grads["b_ada"] = g_b_ada.reshape(nl, ncol)
    for k, v in wg.items():
        grads[k] = jnp.stack(v)

    names = ["w_ada", "b_ada", "norm1_g", "w_in", "a_ln_g", "a_ln_b", "a_ws", "a_bs", "w_pa", "b_conv_w", "b_conv_b",
             "b_ln_g", "b_ln_b", "w_pb", "w_out", "norm2_g", "w_ff1", "w_ff2", "final_g"]
    weights = dict(w_ada=w_ada, b_ada=b_ada, norm1_g=norm1_g, w_in=w_in, a_ln_g=a_ln_g, a_ln_b=a_ln_b, a_ws=a_ws,
                   a_bs=a_bs, w_pa=w_pa, b_conv_w=b_conv_w, b_conv_b=b_conv_b, b_ln_g=b_ln_g, b_ln_b=b_ln_b,
                   w_pb=w_pb, w_out=w_out, norm2_g=norm2_g, w_ff1=w_ff1, w_ff2=w_ff2, final_g=final_g)
    m_in = dict(w_ada=m_w_ada, b_ada=m_b_ada, norm1_g=m_norm1_g, w_in=m_w_in, a_ln_g=m_a_ln_g, a_ln_b=m_a_ln_b,
                a_ws=m_a_ws, a_bs=m_a_bs, w_pa=m_w_pa, b_conv_w=m_b_conv_w, b_conv_b=m_b_conv_b, b_ln_g=m_b_ln_g,
                b_ln_b=m_b_ln_b, w_pb=m_w_pb, w_out=m_w_out, norm2_g=m_norm2_g, w_ff1=m_w_ff1, w_ff2=m_w_ff2,
                final_g=m_final_g)
    v_in = dict(w_ada=v_w_ada, b_ada=v_b_ada, norm1_g=v_norm1_g, w_in=v_w_in, a_ln_g=v_a_ln_g, a_ln_b=v_a_ln_b,
                a_ws=v_a_ws, a_bs=v_a_bs, w_pa=v_w_pa, b_conv_w=v_b_conv_w, b_conv_b=v_b_conv_b, b_ln_g=v_b_ln_g,
                b_ln_b=v_b_ln_b, w_pb=v_w_pb, w_out=v_w_out, norm2_g=v_norm2_g, w_ff1=v_w_ff1, w_ff2=v_w_ff2,
                final_g=v_final_g)

    deltas, new_m, new_v = {}, {}, {}
    for k in names:
        shape = weights[k].shape
        two_d = (-1, shape[-1])
        g2d = grads[k].reshape(shape).reshape(two_d)
        grads[k] = grads[k].reshape(shape)
        dl, nm, nv = _adamw(weights[k].reshape(two_d), g2d, m_in[k].reshape(two_d), v_in[k].reshape(two_d),
                            "adamw_" + k)
        deltas[k], new_m[k], new_v[k] = dl.reshape(shape), nm.reshape(shape), nv.reshape(shape)

    return (loss, dx.reshape(nb, seq, d), *[grads[k] for k in names], *[deltas[k] for k in names],
            *[new_m[k] for k in names], *[new_v[k] for k in names])
```

```python
import functools

import jax
import jax.numpy as jnp
from jax import lax
from jax.experimental import pallas as pl
from jax.experimental.pallas import tpu as pltpu

F32 = jnp.float32
BF16 = jnp.bfloat16
MESH = pl.DeviceIdType.MESH
ANY = pl.BlockSpec(memory_space=pl.ANY)

N_DEV = 8
N_CHIP = 4
EPS = 1e-6
CHUNK = 128
HEADS = 8
CONV_TAPS = 31
HALO = 32
SMALL_ROWS = 40
SMALL_SLOT = 128
CONV_ROWS = 32
TM_BIG = 512
TM_MIX = 256
TK_WGRAD = 4096
VMEM_LIMIT = 56 * 1024 * 1024

ADAM_LR = 0.001
ADAM_B1 = 0.9
ADAM_B2 = 0.999
ADAM_EPS = 1e-08
ADAM_WD = 0.01
ADAM_STEP = 10


def _sds(shape, dtype):
    return jax.ShapeDtypeStruct(tuple(shape), dtype)


def _params(n_grid, vmem=VMEM_LIMIT):
    return pltpu.CompilerParams(dimension_semantics=("arbitrary",) * n_grid, vmem_limit_bytes=vmem)


def _nn(a, b):
    return jnp.dot(a, b, preferred_element_type=F32)


def _nt(a, b):
    return lax.dot_general(a, b, (((1,), (1,)), ((), ())), preferred_element_type=F32)


def _tn(a, b):
    return lax.dot_general(a, b, (((0,), (0,)), ((), ())), preferred_element_type=F32)


def _rowsum(v):
    return jnp.sum(v, axis=0, keepdims=True)


def _mean(v):
    return jnp.mean(v, axis=-1, keepdims=True)


def _add_row(ref, idx, val):
    ref[idx] = ref[idx] + val


def _ln_stats(v):
    mu = _mean(v)
    xc = v - mu
    rs = lax.rsqrt(_mean(xc * xc) + EPS)
    return xc * rs, rs


def _ln_bwd(dout, g, vhat, rs):
    dvh = dout * g
    return rs * (dvh - _mean(dvh) - vhat * _mean(dvh * vhat))


def _rms_bwd(dn, g, x, r):
    gd = dn * g
    return r * gd - x * (r * r * r) * _mean(x * gd)


class _Pack:
    def __init__(self, d, small_slot):
        self.n_in = 6 * d // N_DEV
        self.n_ff = 4 * d // N_DEV
        self.n_p = d // N_DEV
        self.off_ff1 = 0
        self.off_ff2 = self.off_ff1 + self.n_ff
        self.off_pa = self.off_ff2 + self.n_ff
        self.off_pb = self.off_pa + self.n_p
        self.off_out = self.off_pb + self.n_p
        self.off_small = self.off_out + self.n_p
        self.off_in = self.off_small + small_slot
        self.rows = self.off_in + self.n_in
        if small_slot:
            assert self.off_in % self.n_in == 0 and self.off_ff2 % self.n_ff == 0 and self.off_small % small_slot == 0


def _load_rows(g_hbm, w_vm, sems, sem0, off, rows):
    cps = [
        pltpu.make_async_copy(g_hbm.at[k, pl.ds(off, rows), :], w_vm.at[pl.ds(k * rows, rows), :], sems.at[sem0 + k])
        for k in range(N_DEV)
    ]
    for cp in cps:
        cp.start()
    for cp in cps:
        cp.wait()


def _row_spec(tm, cols, colblk=0):
    return pl.BlockSpec((tm, cols), lambda i: (i, colblk))


def _const_spec(shape):
    nd = len(shape)
    return pl.BlockSpec(tuple(shape), lambda i: (0,) * nd)


def _mod_spec(tps, cols):
    return pl.BlockSpec((1, 1, cols), lambda i: (i // tps, 0, 0))


def _mstat_spec(tps, d):
    return pl.BlockSpec((1, 8, d), lambda i: (i // tps, 0, 0))


def _position():
    return lax.axis_index("x"), lax.axis_index("y"), lax.axis_index("c")


def _other_chips(x, y):
    return [(1 - x, y), (x, 1 - y), (1 - x, 1 - y)]


def _remote(src, dst, send_sems, recv_sems, k, to):
    return pltpu.make_async_remote_copy(src_ref=src, dst_ref=dst, send_sem=send_sems.at[k], recv_sem=recv_sems.at[k],
                                        device_id=to, device_id_type=MESH)


def _slot(ref, p):
    return ref.at[4 * p[0] + 2 * p[1] + p[2]]


def _ag_stage1(x_ref, out_ref, send_sems, recv_sems, local_sem):
    x, y, c = _position()
    me = (x, y, c)
    peers = [(x, y, 1 - c)] + [(*chip, c) for chip in _other_chips(x, y)]
    sends = [_remote(x_ref, _slot(out_ref, me), send_sems, recv_sems, k, p) for k, p in enumerate(peers)]
    recvs = [_remote(x_ref, _slot(out_ref, p), send_sems, recv_sems, k, p) for k, p in enumerate(peers)]
    return pltpu.make_async_copy(x_ref, _slot(out_ref, me), local_sem), sends, recvs


def _ag_stage2(in_ref, out_ref, send_sems, recv_sems):
    x, y, c = _position()
    sibling = (x, y, 1 - c)
    chips = _other_chips(x, y)
    sends = [_remote(_slot(in_ref, (*ch, c)), _slot(out_ref, (*ch, c)), send_sems, recv_sems, j, sibling)
             for j, ch in enumerate(chips)]
    recvs = [_remote(_slot(in_ref, (*ch, c)), _slot(out_ref, (*ch, 1 - c)), send_sems, recv_sems, j, sibling)
             for j, ch in enumerate(chips)]
    return sends, recvs


def _rs_chip_copies(x_ref, out_ref, send_sems, recv_sems, local_sem):
    x, y, c = _position()
    q_me = 2 * x + y
    chips = _other_chips(x, y)
    sends = [_remote(x_ref.at[2 * px + py], out_ref.at[q_me], send_sems, recv_sems, j, (px, py, c))
             for j, (px, py) in enumerate(chips)]
    recvs = [_remote(x_ref.at[q_me], out_ref.at[2 * px + py], send_sems, recv_sems, j, (px, py, c))
             for j, (px, py) in enumerate(chips)]
    return pltpu.make_async_copy(x_ref.at[q_me], out_ref.at[q_me], local_sem), sends, recvs


def _start_all(local, sends):
    if local is not None:
        local.start()
    for cp in sends:
        cp.start()


def _finish_all(local, sends, recvs):
    for cp in recvs:
        cp.wait_recv()
    for cp in sends:
        cp.wait_send()
    if local is not None:
        local.wait()


def _sem_scratch(n, local):
    out = [pltpu.SemaphoreType.DMA((n,)), pltpu.SemaphoreType.DMA((n,))]
    return out + ([pltpu.SemaphoreType.DMA(())] if local else [])


def _fwd_in(x2d, mod3, g1, gw, pk, seq, tm):
    t, d = x2d.shape
    nc = 6 * d
    tps = seq // tm

    def body(x_ref, mod_ref, g_ref, gw_hbm, proj_ref, w_vm, sems):
        @pl.when(pl.program_id(0) == 0)
        def _():
            _load_rows(gw_hbm, w_vm, sems, 0, pk.off_in, pk.n_in)

        x = x_ref[...]
        m = mod_ref[0]
        r = lax.rsqrt(_mean(x * x) + EPS)
        h = (x * r * g_ref[...] * (1.0 + m[:, d:2 * d]) + m[:, 0:d]).astype(BF16)
        for j in range(nc // 512):
            proj_ref[:, j * 512:(j + 1) * 512] = _nt(h, w_vm[j * 512:(j + 1) * 512, :]).astype(BF16)

    return pl.pallas_call(
        body, grid=(t // tm,),
        in_specs=[_row_spec(tm, d), _mod_spec(tps, nc), _const_spec((1, d)), ANY],
        out_specs=_row_spec(tm, nc), out_shape=_sds((t, nc), BF16),
        scratch_shapes=[pltpu.VMEM((nc, d), BF16), pltpu.SemaphoreType.DMA((N_DEV,))],
        compiler_params=_params(1), name="fwd_in",
    )(x2d, mod3, g1, gw)


def _shift_copies(src, sh, rows):
    for r in range(1, 8):
        sh[r - 1, pl.ds(0, rows), :] = src[pl.ds(r, rows), :]


def _window(src, sh, offset, start, size):
    r, q = offset % 8, offset // 8
    if r == 0:
        return src[pl.ds(start + 8 * q, size), :]
    return sh[r - 1, pl.ds(start + 8 * q, size), :]


def _conv_taps(src, sh, cwb_ref, offsets, bias, out_ref, tm, d):
    nsub = CONV_ROWS // 8
    for rb in range(tm // CONV_ROWS):
        accs = [jnp.broadcast_to(bias, (8, d))] * nsub
        for k in range(CONV_TAPS):
            w8 = cwb_ref[pl.ds(8 * k, 8), :]
            accs = [a + w8 * _window(src, sh, offsets[k], rb * CONV_ROWS + 8 * j, 8) for j, a in enumerate(accs)]
        for j, a in enumerate(accs):
            out_ref[pl.ds(rb * CONV_ROWS + 8 * j, 8), :] = a


def _conv_wgrad(dsrc, zsrc, zsh, acc_ref, lo, tm, d):
    for r in range(8):
        taps = [k for k in range(CONV_TAPS) if (lo + k) % 8 == r]
        accs = [jnp.zeros((8, d), F32)] * len(taps)
        for rb in range(tm // 8):
            dblk = dsrc[pl.ds(rb * 8, 8), :]
            accs = [a + dblk * _window(zsrc, zsh, lo + k, rb * 8, 8) for a, k in zip(accs, taps)]
        for a, k in zip(accs, taps):
            acc_ref[pl.ds(8 * k, 8), :] = acc_ref[pl.ds(8 * k, 8), :] + a


def _fill_zext(zext, z, halo_ref, first, tm, d):
    hz = halo_ref[:, 0:d].astype(F32) * jax.nn.sigmoid(halo_ref[:, d:2 * d].astype(F32))
    zext[pl.ds(0, HALO), :] = jnp.where(first, 0.0, hz)
    zext[pl.ds(HALO, tm), :] = z


def _fwd_mix(proj, x2d, mod3, vecs, wm, bsx, cw, gw, pk, seq, tm, nxt=None):
    t, d = x2d.shape
    tps = seq // tm
    hb = tm // HALO
    n = t // tm
    lo = HALO - (CONV_TAPS - 1)
    comm = nxt is not None

    def body(*refs):
        proj_ref, halo_ref, x_ref, mod_ref, vec_ref, wm_ref, bs_ref, cw_ref, gw_hbm = refs[:9]
        refs = refs[9:]
        if comm:
            nx_ref, refs = refs[0], refs[1:]
        x1_ref, ya_ref, yb_ref, o1_ref, zc_ref = refs[:5]
        refs = refs[5:]
        if comm:
            gwn_ref, refs = refs[0], refs[1:]
        wpa, wpb, wout, sems, zext, zsh, zc_buf, vn_buf, a_buf = refs[:9]
        i = pl.program_id(0)

        if comm:
            ag = functools.partial(_ag_stage1, nx_ref, gwn_ref, *refs[9:12])

            @pl.when(i == 0)
            def _():
                local, sends, _ = ag()
                _start_all(local, sends)

        @pl.when(i == 0)
        def _():
            _load_rows(gw_hbm, wpa, sems, 0, pk.off_pa, pk.n_p)
            _load_rows(gw_hbm, wpb, sems, N_DEV, pk.off_pb, pk.n_p)
            _load_rows(gw_hbm, wout, sems, 2 * N_DEV, pk.off_out, pk.n_p)

        first = (i % tps) == 0
        m = mod_ref[0]
        vhat, _ = _ln_stats(proj_ref[:, d:2 * d].astype(F32))
        vn_buf[...] = (vhat * vec_ref[0:1, :] + vec_ref[1:2, :]).astype(BF16)
        for c in range(tm // CHUNK):
            rs_ = slice(c * CHUNK, (c + 1) * CHUNK)
            for h in range(HEADS):
                cs_ = slice(h * CHUNK, (h + 1) * CHUNK)
                s_b = _nn(wm_ref[h], vn_buf[rs_, cs_]) + bs_ref[:, cs_]
                a_buf[rs_, cs_] = (proj_ref[rs_, cs_].astype(F32) * s_b).astype(BF16)
        y_a = _nn(a_buf[...], wpa[...])
        ya_ref[...] = y_a.astype(BF16)
        z = proj_ref[:, 2 * d:3 * d].astype(F32) * jax.nn.sigmoid(proj_ref[:, 3 * d:4 * d].astype(F32))
        _fill_zext(zext, z, halo_ref, first, tm, d)
        _shift_copies(zext, zsh, tm + HALO - 8)
        _conv_taps(zext, zsh, cw_ref, [lo + k for k in range(CONV_TAPS)], vec_ref[2:3, :], zc_buf, tm, d)
        zc = zc_buf[...]
        zc_ref[...] = zc.astype(BF16)
        zhat, _ = _ln_stats(zc)
        zn = zhat * vec_ref[3:4, :] + vec_ref[4:5, :]
        b_act = (zn * jax.nn.sigmoid(zn)).astype(BF16)
        y_b = _nn(b_act, wpb[...])
        yb_ref[...] = y_b.astype(BF16)
        merged = (jax.nn.sigmoid(proj_ref[:, 4 * d:5 * d].astype(F32)) * y_a
                  + jax.nn.sigmoid(proj_ref[:, 5 * d:6 * d].astype(F32)) * y_b).astype(BF16)
        o1 = _nn(merged, wout[...])
        o1_ref[...] = o1.astype(BF16)
        x1_ref[...] = x_ref[...] + m[:, 2 * d:3 * d] * o1

        if comm:
            @pl.when(i == n - 1)
            def _():
                _finish_all(*ag())

    halo_spec = pl.BlockSpec((HALO, 2 * d), lambda i: (jnp.maximum(i * hb - 1, 0), 1))
    act = _sds((t, d), BF16)
    return pl.pallas_call(
        body, grid=(n,),
        in_specs=[_row_spec(tm, 6 * d), halo_spec, _row_spec(tm, d), _mod_spec(tps, 6 * d), _const_spec((8, d)),
                  _const_spec((HEADS, CHUNK, CHUNK)), _const_spec((CHUNK, d)), _const_spec((8 * HALO, d)), ANY]
        + ([ANY] if comm else []),
        out_specs=[_row_spec(tm, d)] * 5 + ([ANY] if comm else []),
        out_shape=[_sds((t, d), F32), act, act, act, act]
        + ([_sds((N_DEV,) + nxt.shape, nxt.dtype)] if comm else []),
        scratch_shapes=[pltpu.VMEM((d, d), BF16), pltpu.VMEM((d, d), BF16), pltpu.VMEM((d, d), BF16),
                        pltpu.SemaphoreType.DMA((3 * N_DEV,)),
                        pltpu.VMEM((tm + HALO, d), F32), pltpu.VMEM((7, tm + HALO, d), F32),
                        pltpu.VMEM((tm, d), F32), pltpu.VMEM((tm, d), BF16), pltpu.VMEM((tm, d), BF16)]
        + (_sem_scratch(4, True) if comm else []),
        compiler_params=_params(1), name="fwd_mix_ag" if comm else "fwd_mix",
    )(proj, proj, x2d, mod3, vecs, wm, bsx, cw, gw, *([nxt] if comm else []))


def _fwd_ffn(x1, mod3, g2, gw, pk, seq, tm, gw_next=None):
    t, d = x1.shape
    nf = 4 * d
    tps = seq // tm
    n = t // tm
    comm = gw_next is not None

    def body(*refs):
        x_ref, mod_ref, g_ref, gw_hbm = refs[:4]
        refs = refs[4:]
        if comm:
            gwn_in, refs = refs[0], refs[1:]
        x2_ref, f_ref, o2_ref = refs[:3]
        refs = refs[3:]
        if comm:
            gwn_out, refs = refs[0], refs[1:]
        w1, w2, sems = refs[:3]
        i = pl.program_id(0)

        if comm:
            ag = functools.partial(_ag_stage2, gwn_in, gwn_out, *refs[3:5])

            @pl.when(i == 0)
            def _():
                _start_all(None, ag()[0])

        @pl.when(i == 0)
        def _():
            _load_rows(gw_hbm, w1, sems, 0, pk.off_ff1, pk.n_ff)
            _load_rows(gw_hbm, w2, sems, N_DEV, pk.off_ff2, pk.n_ff)

        x = x_ref[...]
        m = mod_ref[0]
        r = lax.rsqrt(_mean(x * x) + EPS)
        h2 = (x * r * g_ref[...] * (1.0 + m[:, 4 * d:5 * d]) + m[:, 3 * d:4 * d]).astype(BF16)
        acc = jnp.zeros(x.shape, F32)
        for j in range(nf // 512):
            js = slice(j * 512, (j + 1) * 512)
            f = _nt(h2, w1[js, :])
            f_ref[:, js] = f.astype(BF16)
            acc = acc + _nn(jnp.square(jnp.maximum(f, 0.0)).astype(BF16), w2[js, :])
        o2_ref[...] = acc.astype(BF16)
        x2_ref[...] = x + m[:, 5 * d:6 * d] * acc

        if comm:
            @pl.when(i == n - 1)
            def _():
                _finish_all(None, *ag())

    return pl.pallas_call(
        body, grid=(n,),
        in_specs=[_row_spec(tm, d), _mod_spec(tps, 6 * d), _const_spec((1, d)), ANY] + ([ANY] if comm else []),
        out_specs=[_row_spec(tm, d), _row_spec(tm, nf), _row_spec(tm, d)] + ([ANY] if comm else []),
        out_shape=[_sds((t, d), F32), _sds((t, nf), BF16), _sds((t, d), BF16)]
        + ([_sds(gw_next.shape, gw_next.dtype)] if comm else []),
        scratch_shapes=[pltpu.VMEM((nf, d), BF16), pltpu.VMEM((nf, d), BF16), pltpu.SemaphoreType.DMA((2 * N_DEV,))]
        + (_sem_scratch(3, False) if comm else []),
        input_output_aliases={4: 3} if comm else {},
        compiler_params=_params(1), name="fwd_ffn_ag" if comm else "fwd_ffn",
    )(x1, mod3, g2, gw, *([gw_next] if comm else []))


def _loss_head(x, tgt, fg, tm):
    t, d = x.shape
    n = t // tm

    def body(x_ref, t_ref, g_ref, dx_ref, loss_ref, dg_ref, lacc):
        i = pl.program_id(0)

        @pl.when(i == 0)
        def _():
            lacc[...] = jnp.zeros(lacc.shape, F32)
            dg_ref[...] = jnp.zeros(dg_ref.shape, F32)

        xv = x_ref[...]
        g = g_ref[...]
        r = lax.rsqrt(_mean(xv * xv) + EPS)
        err = xv * r * g - t_ref[...]
        lacc[...] = lacc[...] + _rowsum(err * err)
        dy = err * (1.0 / d)
        _add_row(dg_ref, (slice(0, 1), slice(None)), _rowsum(dy * xv * r))
        dx_ref[...] = _rms_bwd(dy, g, xv, r)

        @pl.when(i == n - 1)
        def _():
            loss_ref[...] = jnp.broadcast_to(jnp.sum(lacc[...], keepdims=True) * (0.5 / d), loss_ref.shape)

    return pl.pallas_call(
        body, grid=(n,),
        in_specs=[_row_spec(tm, d), _row_spec(tm, d), _const_spec((1, d))],
        out_specs=[_row_spec(tm, d), _const_spec((8, 128)), _const_spec((8, d))],
        out_shape=[_sds((t, d), F32), _sds((8, 128), F32), _sds((8, d), F32)],
        scratch_shapes=[pltpu.VMEM((1, d), F32)],
        compiler_params=_params(1), name="loss_head",
    )(x, tgt, fg)


def _bwd_ffn(dx2, x1, f, o2, mod3, g2, gw, pk, seq, tm):
    t, d = x1.shape
    nf = 4 * d
    tps = seq // tm
    nb = t // seq

    def body(dx2_ref, x_ref, f_ref, o2_ref, mod_ref, g_ref, gw_hbm,
             dx1_ref, df_ref, do2_ref, h2_ref, ms_ref, ps_ref, w1, w2, sems):
        i = pl.program_id(0)

        @pl.when(i == 0)
        def _():
            _load_rows(gw_hbm, w1, sems, 0, pk.off_ff1, pk.n_ff)
            _load_rows(gw_hbm, w2, sems, N_DEV, pk.off_ff2, pk.n_ff)
            ps_ref[...] = jnp.zeros(ps_ref.shape, F32)

        @pl.when((i % tps) == 0)
        def _():
            ms_ref[...] = jnp.zeros(ms_ref.shape, F32)

        dx2 = dx2_ref[...]
        x = x_ref[...]
        m = mod_ref[0]
        g = g_ref[...]
        sh2, sc2, gt2 = m[:, 3 * d:4 * d], m[:, 4 * d:5 * d], m[:, 5 * d:6 * d]
        _add_row(ms_ref, (0, slice(2, 3), slice(None)), _rowsum(dx2 * o2_ref[...].astype(F32)))
        do2 = (gt2 * dx2).astype(BF16)
        do2_ref[...] = do2
        r = lax.rsqrt(_mean(x * x) + EPS)
        n = x * r * g
        h2_ref[...] = (n * (1.0 + sc2) + sh2).astype(BF16)
        dh = jnp.zeros(x.shape, F32)
        for j in range(nf // 512):
            js = slice(j * 512, (j + 1) * 512)
            dr = _nt(do2, w2[js, :])
            df = (dr * (2.0 * jnp.maximum(f_ref[:, js].astype(F32), 0.0))).astype(BF16)
            df_ref[:, js] = df
            dh = dh + _nn(df, w1[js, :])
        _add_row(ms_ref, (0, slice(0, 1), slice(None)), _rowsum(dh))
        _add_row(ms_ref, (0, slice(1, 2), slice(None)), _rowsum(dh * n))
        dn = dh * (1.0 + sc2)
        _add_row(ps_ref, (slice(0, 1), slice(None)), _rowsum(dn * x * r))
        dx1_ref[...] = dx2 + _rms_bwd(dn, g, x, r)

    act = _sds((t, d), BF16)
    return pl.pallas_call(
        body, grid=(t // tm,),
        in_specs=[_row_spec(tm, d), _row_spec(tm, d), _row_spec(tm, nf), _row_spec(tm, d), _mod_spec(tps, 6 * d),
                  _const_spec((1, d)), ANY],
        out_specs=[_row_spec(tm, d), _row_spec(tm, nf), _row_spec(tm, d), _row_spec(tm, d), _mstat_spec(tps, d),
                   _const_spec((8, d))],
        out_shape=[_sds((t, d), F32), _sds((t, nf), BF16), act, act, _sds((nb, 8, d), F32), _sds((8, d), F32)],
        scratch_shapes=[pltpu.VMEM((nf, d), BF16), pltpu.VMEM((nf, d), BF16), pltpu.SemaphoreType.DMA((2 * N_DEV,))],
        compiler_params=_params(1), name="bwd_ffn",
    )(dx2, x1, f, o2, mod3, g2, gw)


def _bwd_mix(dx1, proj, ya, yb, o1, zc, mod3, vecs, wm, wmt, bsx, gw, pk, seq, tm):
    t, d = dx1.shape
    tps = seq // tm
    nb = t // seq
    n = t // tm

    def body(dx1_ref, proj_ref, ya_ref, yb_ref, o1_ref, zc_ref, mod_ref, vec_ref, wm_ref, wmt_ref, bs_ref, gw_hbm,
             dp_ref, dzc_ref, do1_ref, dya_ref, dyb_ref, mg_ref, aa_ref, ba_ref, ms_ref, ps_ref, dws_ref, dbs_ref,
             wpa, wpb, wout, sems, vn_buf, da_buf, dvn_buf):
        i = pl.program_id(0)

        @pl.when(i == 0)
        def _():
            _load_rows(gw_hbm, wpa, sems, 0, pk.off_pa, pk.n_p)
            _load_rows(gw_hbm, wpb, sems, N_DEV, pk.off_pb, pk.n_p)
            _load_rows(gw_hbm, wout, sems, 2 * N_DEV, pk.off_out, pk.n_p)
            ps_ref[...] = jnp.zeros(ps_ref.shape, F32)
            dws_ref[...] = jnp.zeros(dws_ref.shape, F32)
            dbs_ref[...] = jnp.zeros(dbs_ref.shape, F32)

        @pl.when((i % tps) == 0)
        def _():
            ms_ref[...] = jnp.zeros(ms_ref.shape, F32)

        m = mod_ref[0]
        dx1v = dx1_ref[...]
        _add_row(ms_ref, (0, slice(0, 1), slice(None)), _rowsum(dx1v * o1_ref[...].astype(F32)))
        do1 = (m[:, 2 * d:3 * d] * dx1v).astype(BF16)
        do1_ref[...] = do1
        dmg = _nt(do1, wout[...])
        sa = jax.nn.sigmoid(proj_ref[:, 4 * d:5 * d].astype(F32))
        sb = jax.nn.sigmoid(proj_ref[:, 5 * d:6 * d].astype(F32))
        y_a = ya_ref[...].astype(F32)
        y_b = yb_ref[...].astype(F32)
        dya = (dmg * sa).astype(BF16)
        dyb = (dmg * sb).astype(BF16)
        dya_ref[...] = dya
        dyb_ref[...] = dyb
        dp_ref[:, 4 * d:5 * d] = (dmg * y_a * sa * (1.0 - sa)).astype(BF16)
        dp_ref[:, 5 * d:6 * d] = (dmg * y_b * sb * (1.0 - sb)).astype(BF16)
        mg_ref[...] = (sa * y_a + sb * y_b).astype(BF16)
        da_buf[...] = _nt(dya, wpa[...])
        db = _nt(dyb, wpb[...])
        vhat, rs = _ln_stats(proj_ref[:, d:2 * d].astype(F32))
        alg = vec_ref[0:1, :]
        vn_buf[...] = (vhat * alg + vec_ref[1:2, :]).astype(BF16)
        for c in range(tm // CHUNK):
            rs_ = slice(c * CHUNK, (c + 1) * CHUNK)
            for h in range(HEADS):
                cs_ = slice(h * CHUNK, (h + 1) * CHUNK)
                vn_b = vn_buf[rs_, cs_]
                s_b = _nn(wm_ref[h], vn_b) + bs_ref[:, cs_]
                u_b = proj_ref[rs_, cs_].astype(F32)
                da_b = da_buf[rs_, cs_]
                aa_ref[rs_, cs_] = (u_b * s_b).astype(BF16)
                dp_ref[rs_, cs_] = (da_b * s_b).astype(BF16)
                ds_b = da_b * u_b
                dbs_ref[:, cs_] = dbs_ref[:, cs_] + ds_b
                ds_bf = ds_b.astype(BF16)
                dvn_buf[rs_, cs_] = _nn(wmt_ref[h], ds_bf)
                dws_ref[:, cs_] = dws_ref[:, cs_] + _nt(ds_bf, vn_b)
        dvn = dvn_buf[...]
        _add_row(ps_ref, (slice(0, 1), slice(None)), _rowsum(dvn * vhat))
        _add_row(ps_ref, (slice(1, 2), slice(None)), _rowsum(dvn))
        dp_ref[:, d:2 * d] = _ln_bwd(dvn, alg, vhat, rs).astype(BF16)
        dp_ref[:, 2 * d:4 * d] = jnp.zeros((tm, 2 * d), BF16)
        zhat, rsb = _ln_stats(zc_ref[...].astype(F32))
        blg = vec_ref[3:4, :]
        zn = zhat * blg + vec_ref[4:5, :]
        sg = jax.nn.sigmoid(zn)
        ba_ref[...] = (zn * sg).astype(BF16)
        dzn = db * (sg * (1.0 + zn * (1.0 - sg)))
        _add_row(ps_ref, (slice(2, 3), slice(None)), _rowsum(dzn * zhat))
        _add_row(ps_ref, (slice(3, 4), slice(None)), _rowsum(dzn))
        dzc = _ln_bwd(dzn, blg, zhat, rsb)
        _add_row(ps_ref, (slice(4, 5), slice(None)), _rowsum(dzc))
        dzc_ref[...] = dzc.astype(BF16)

        @pl.when(i == n - 1)
        def _():
            causal = (lax.broadcasted_iota(jnp.int32, (CHUNK, CHUNK), 0)
                      >= lax.broadcasted_iota(jnp.int32, (CHUNK, CHUNK), 1))
            for h in range(HEADS):
                cs_ = slice(h * CHUNK, (h + 1) * CHUNK)
                dws_ref[:, cs_] = jnp.where(causal, dws_ref[:, cs_], 0.0)
                dbs_ref[:, cs_] = jnp.broadcast_to(jnp.sum(dbs_ref[:, cs_], axis=1, keepdims=True), (CHUNK, CHUNK))

    act = _sds((t, d), BF16)
    return pl.pallas_call(
        body, grid=(n,),
        in_specs=[_row_spec(tm, d), _row_spec(tm, 6 * d), _row_spec(tm, d), _row_spec(tm, d), _row_spec(tm, d),
                  _row_spec(tm, d), _mod_spec(tps, 6 * d), _const_spec((8, d)), _const_spec((HEADS, CHUNK, CHUNK)),
                  _const_spec((HEADS, CHUNK, CHUNK)), _const_spec((CHUNK, d)), ANY],
        out_specs=[_row_spec(tm, 6 * d)] + [_row_spec(tm, d)] * 7
        + [_mstat_spec(tps, d), _const_spec((8, d)), _const_spec((CHUNK, d)), _const_spec((CHUNK, d))],
        out_shape=[_sds((t, 6 * d), BF16)] + [act] * 7
        + [_sds((nb, 8, d), F32), _sds((8, d), F32), _sds((CHUNK, d), F32), _sds((CHUNK, d), F32)],
        scratch_shapes=[pltpu.VMEM((d, d), BF16), pltpu.VMEM((d, d), BF16), pltpu.VMEM((d, d), BF16),
                        pltpu.SemaphoreType.DMA((3 * N_DEV,)),
                        pltpu.VMEM((tm, d), BF16), pltpu.VMEM((tm, d), F32), pltpu.VMEM((tm, d), F32)],
        compiler_params=_params(1), name="bwd_mix",
    )(dx1, proj, ya, yb, o1, zc, mod3, vecs, wm, wmt, bsx, gw)


def _bwd_in(dproj, dzc, proj, x2d, dx1, mod3, g1, cw, gw, pk, seq, tm, part=None):
    t, d = x2d.shape
    tps = seq // tm
    nb = t // seq
    n = t // tm
    hb = tm // HALO
    lo = HALO - (CONV_TAPS - 1)
    comm = part is not None

    def body(*refs):
        dpi_ref, dzc_ref, dzn_ref, pp_ref, halo_ref, x_ref, dx1_ref, mod_ref, g_ref, cw_ref, gw_hbm = refs[:11]
        refs = refs[11:]
        if comm:
            part_ref, refs = refs[0], refs[1:]
        dpo_ref, dx_ref, h_ref, ms_ref, ps_ref, dcw_ref = refs[:6]
        refs = refs[6:]
        if comm:
            got_ref, refs = refs[0], refs[1:]
        w_vm, sems, zext, dzext, zsh, dz_buf, dcw_acc = refs[:7]
        i = pl.program_id(0)

        if comm:
            rs = functools.partial(_rs_chip_copies, part_ref, got_ref, *refs[7:10])

            @pl.when(i == 0)
            def _():
                local, sends, _ = rs()
                _start_all(local, sends)

        @pl.when(i == 0)
        def _():
            _load_rows(gw_hbm, w_vm, sems, 0, pk.off_in, pk.n_in)
            ps_ref[...] = jnp.zeros(ps_ref.shape, F32)
            dcw_acc[...] = jnp.zeros(dcw_acc.shape, F32)

        first = (i % tps) == 0
        last = (i % tps) == tps - 1

        @pl.when(first)
        def _():
            ms_ref[...] = jnp.zeros(ms_ref.shape, F32)

        pa = pp_ref[:, 0:d].astype(F32)
        sgp = jax.nn.sigmoid(pp_ref[:, d:2 * d].astype(F32))
        _fill_zext(zext, pa * sgp, halo_ref, first, tm, d)
        dzext[pl.ds(0, tm), :] = dzc_ref[...].astype(F32)
        dzext[pl.ds(tm, HALO), :] = jnp.where(last, 0.0, dzn_ref[...].astype(F32))
        _shift_copies(zext, zsh, tm + HALO - 8)
        _conv_wgrad(dzext, zext, zsh, dcw_acc, lo, tm, d)
        _shift_copies(dzext, zsh, tm + HALO - 8)
        _conv_taps(dzext, zsh, cw_ref, [CONV_TAPS - 1 - k for k in range(CONV_TAPS)], jnp.zeros((1, d), F32),
                   dz_buf, tm, d)
        dz = dz_buf[...]
        dpa = (dz * sgp).astype(BF16)
        dpg = (dz * pa * sgp * (1.0 - sgp)).astype(BF16)
        dpo_ref[:, 0:d] = dpa
        dpo_ref[:, d:2 * d] = dpg
        dh = (_nn(dpi_ref[:, 0:2 * d], w_vm[0:2 * d, :]) + _nn(dpa, w_vm[2 * d:3 * d, :])
              + _nn(dpg, w_vm[3 * d:4 * d, :]) + _nn(dpi_ref[:, 4 * d:6 * d], w_vm[4 * d:6 * d, :]))
        x = x_ref[...]
        m = mod_ref[0]
        g = g_ref[...]
        sh1, sc1 = m[:, 0:d], m[:, d:2 * d]
        r = lax.rsqrt(_mean(x * x) + EPS)
        nrm = x * r * g
        h_ref[...] = (nrm * (1.0 + sc1) + sh1).astype(BF16)
        _add_row(ms_ref, (0, slice(0, 1), slice(None)), _rowsum(dh))
        _add_row(ms_ref, (0, slice(1, 2), slice(None)), _rowsum(dh * nrm))
        dn = dh * (1.0 + sc1)
        _add_row(ps_ref, (slice(0, 1), slice(None)), _rowsum(dn * x * r))
        dx_ref[...] = dx1_ref[...] + _rms_bwd(dn, g, x, r)

        @pl.when(i == n - 1)
        def _():
            for k in range(CONV_TAPS):
                dcw_ref[k:k + 1, :] = _rowsum(dcw_acc[pl.ds(k * 8, 8), :])
            dcw_ref[CONV_TAPS:HALO, :] = jnp.zeros((HALO - CONV_TAPS, d), F32)

        if comm:
            @pl.when(i == n - 1)
            def _():
                _finish_all(*rs())

    halo_prev = pl.BlockSpec((HALO, 2 * d), lambda i: (jnp.maximum(i * hb - 1, 0), 1))
    halo_next = pl.BlockSpec((HALO, d), lambda i: (jnp.minimum((i + 1) * hb, t // HALO - 1), 0))
    return pl.pallas_call(
        body, grid=(n,),
        in_specs=[_row_spec(tm, 6 * d), _row_spec(tm, d), halo_next, _row_spec(tm, 2 * d, 1), halo_prev,
                  _row_spec(tm, d), _row_spec(tm, d), _mod_spec(tps, 6 * d), _const_spec((1, d)),
                  _const_spec((8 * HALO, d)), ANY] + ([ANY] if comm else []),
        out_specs=[_row_spec(tm, 2 * d, 1), _row_spec(tm, d), _row_spec(tm, d), _mstat_spec(tps, d),
                   _const_spec((8, d)), _const_spec((HALO, d))] + ([ANY] if comm else []),
        out_shape=[_sds((t, 6 * d), BF16), _sds((t, d), F32), _sds((t, d), BF16), _sds((nb, 8, d), F32),
                   _sds((8, d), F32), _sds((HALO, d), F32)] + ([_sds(part.shape, part.dtype)] if comm else []),
        scratch_shapes=[pltpu.VMEM((6 * d, d), BF16), pltpu.SemaphoreType.DMA((N_DEV,)),
                        pltpu.VMEM((tm + HALO, d), F32), pltpu.VMEM((tm + HALO, d), F32),
                        pltpu.VMEM((7, tm + HALO, d), F32),
                        pltpu.VMEM((tm, d), F32), pltpu.VMEM((CONV_TAPS * 8, d), F32)]
        + (_sem_scratch(3, True) if comm else []),
        input_output_aliases={0: 0},
        compiler_params=_params(1), name="bwd_in_rs" if comm else "bwd_in",
    )(dproj, dzc, dzc, proj, proj, x2d, dx1, mod3, g1, cw, gw, *([part] if comm else []))


def _wgrad(a, b, pack, kb, off, relu2, name):
    t, mo = a.shape
    nn_ = b.shape[1]
    rows = mo // N_DEV
    tk = min(TK_WGRAD, t)
    nk = t // tk
    assert off % rows == 0 and N_DEV % kb == 0

    def body(a_ref, b_ref, pack_hbm, o_ref, acc):
        k = pl.program_id(1)

        @pl.when(k == 0)
        def _():
            acc[...] = jnp.zeros(acc.shape, F32)

        av = a_ref[...]
        if relu2:
            av = jnp.square(jnp.maximum(av, 0.0))
        acc[...] = acc[...] + _tn(av, b_ref[...])

        @pl.when(k == nk - 1)
        def _():
            for j in range(kb):
                o_ref[j] = acc[pl.ds(j * rows, rows), :].astype(BF16)

    return pl.pallas_call(
        body, grid=(N_DEV // kb, nk),
        in_specs=[pl.BlockSpec((tk, kb * rows), lambda i, k: (k, i)), pl.BlockSpec((tk, nn_), lambda i, k: (k, 0)), ANY],
        out_specs=pl.BlockSpec((kb, rows, nn_), lambda i, k: (i, off // rows, 0)),
        out_shape=_sds(pack.shape, BF16),
        scratch_shapes=[pltpu.VMEM((kb * rows, nn_), F32)],
        input_output_aliases={2: 0},
        compiler_params=_params(2), name=name,
    )(a, b, pack)


def _place_rows(pack, rows_blk, off):
    nblk, r, nn_ = rows_blk.shape
    assert off % r == 0

    def body(pack_hbm, s_ref, o_ref):
        o_ref[...] = s_ref[...]

    return pl.pallas_call(
        body, grid=(1,),
        in_specs=[ANY, pl.BlockSpec((nblk, r, nn_), lambda i: (0, 0, 0))],
        out_specs=pl.BlockSpec((nblk, r, nn_), lambda i: (0, off // r, 0)),
        out_shape=_sds(pack.shape, pack.dtype),
        input_output_aliases={0: 0},
        compiler_params=_params(1), name="place_small",
    )(pack, rows_blk)


def _mod_fwd(c_all, w_ada, b_cols):
    nl, d, cols = w_ada.shape
    bsz = c_all.shape[0]

    def body(c_ref, w_ref, b_ref, o_ref):
        cv = c_ref[...]
        ca = cv * jax.nn.sigmoid(cv)
        o_ref[0] = jnp.dot(ca, w_ref[0], preferred_element_type=F32, precision=lax.Precision.HIGHEST) + b_ref[0]

    return pl.pallas_call(
        body, grid=(nl,),
        in_specs=[_const_spec((bsz, d)), pl.BlockSpec((1, d, cols), lambda l: (l, 0, 0)),
                  pl.BlockSpec((1, 1, cols), lambda l: (l, 0, 0))],
        out_specs=pl.BlockSpec((1, bsz, cols), lambda l: (l, 0, 0)),
        out_shape=_sds((nl, bsz, cols), F32),
        compiler_params=_params(1), name="mod_fwd",
    )(c_all, w_ada, b_cols)


def _mod_bwd(c_all, dmod_cols, dmod_all):
    nl, bsz, cols = dmod_cols.shape
    d = c_all.shape[1]
    ncol = dmod_all.shape[2]

    def body(c_ref, dc_ref, da_ref, dw_ref, db_ref):
        cv = c_ref[...]
        ca = cv * jax.nn.sigmoid(cv)
        dw_ref[0] = lax.dot_general(ca, dc_ref[0], (((0,), (0,)), ((), ())), preferred_element_type=F32,
                                    precision=lax.Precision.HIGHEST)
        db_ref[0] = _rowsum(da_ref[0])

    return pl.pallas_call(
        body, grid=(nl,),
        in_specs=[_const_spec((bsz, d)), pl.BlockSpec((1, bsz, cols), lambda l: (l, 0, 0)),
                  pl.BlockSpec((1, bsz, ncol), lambda l: (l, 0, 0))],
        out_specs=[pl.BlockSpec((1, d, cols), lambda l: (l, 0, 0)), pl.BlockSpec((1, 1, ncol), lambda l: (l, 0, 0))],
        out_shape=[_sds((nl, d, cols), F32), _sds((nl, 1, ncol), F32)],
        compiler_params=_params(1), name="mod_bwd",
    )(c_all, dmod_cols, dmod_all)


def _row_tile(rows, cols, nbuf, itemsize=4, budget=24 * 1024 * 1024):
    cap = max(16, budget // (2 * nbuf * cols * itemsize))
    if rows <= cap:
        return rows
    best = None
    for tr in range(16, cap + 1, 16):
        if rows % tr == 0:
            best = tr
    assert best is not None, (rows, cols)
    return best


def _sum_blocks(xs, name):
    nblk, rows, cols = xs.shape
    tr = _row_tile(rows, cols, nblk + 1)

    def body(x_ref, o_ref):
        acc = x_ref[0].astype(F32)
        for j in range(1, nblk):
            acc = acc + x_ref[j].astype(F32)
        o_ref[...] = acc

    return pl.pallas_call(
        body, grid=(rows // tr,),
        in_specs=[pl.BlockSpec((nblk, tr, cols), lambda i: (0, i, 0))],
        out_specs=pl.BlockSpec((tr, cols), lambda i: (i, 0)),
        out_shape=_sds((rows, cols), F32),
        compiler_params=_params(1), name=name,
    )(xs)


def _add_sibling(dp, recv, core):
    nq, _, rows, cols = dp.shape
    tr = _row_tile(rows, cols, 3, itemsize=2)

    def body(c_ref, a_ref, b_ref, o_ref):
        o_ref[...] = (a_ref[...].astype(F32) + b_ref[...].astype(F32)).astype(BF16)

    return pl.pallas_call(
        body,
        grid_spec=pltpu.PrefetchScalarGridSpec(
            num_scalar_prefetch=1, grid=(nq, rows // tr),
            in_specs=[pl.BlockSpec((1, 1, tr, cols), lambda q, i, c: (q, c[0], i, 0)),
                      pl.BlockSpec((1, 1, tr, cols), lambda q, i, c: (q, 0, i, 0))],
            out_specs=pl.BlockSpec((1, 1, tr, cols), lambda q, i, c: (q, 0, i, 0))),
        out_shape=_sds((nq, 1, rows, cols), BF16),
        compiler_params=_params(2), name="add_sibling",
    )(core, dp, recv.reshape(nq, 1, rows, cols)).reshape(nq, rows, cols)


def _adamw(w, g, m, v, name):
    rows, cols = w.shape
    tr = _row_tile(rows, cols, 7)
    c1 = 1.0 - ADAM_B1 ** ADAM_STEP
    c2 = 1.0 - ADAM_B2 ** ADAM_STEP

    def body(w_ref, g_ref, m_ref, v_ref, d_ref, nm_ref, nv_ref):
        gv = g_ref[...]
        nm = ADAM_B1 * m_ref[...] + (1.0 - ADAM_B1) * gv
        nv = ADAM_B2 * v_ref[...] + (1.0 - ADAM_B2) * (gv * gv)
        nm_ref[...] = nm
        nv_ref[...] = nv
        d_ref[...] = -ADAM_LR * ((nm / c1) / (jnp.sqrt(nv / c2) + ADAM_EPS) + ADAM_WD * w_ref[...])

    spec = pl.BlockSpec((tr, cols), lambda i: (i, 0))
    out = _sds((rows, cols), F32)
    return pl.pallas_call(
        body, grid=(rows // tr,), in_specs=[spec] * 4, out_specs=[spec] * 3, out_shape=[out] * 3,
        compiler_params=_params(1), name=name,
    )(w, g, m, v)


def _all_gather(xs, name):
    rows, cols = xs.shape

    def body(x_ref, out_ref, send1, recv1, local_sem, send2, recv2):
        local, first, arrivals = _ag_stage1(x_ref, out_ref, send1, recv1, local_sem)
        _start_all(local, first)
        passed, from_sibling = _ag_stage2(out_ref, out_ref, send2, recv2)
        for arrival, onward in zip(arrivals[1:], passed):
            arrival.wait_recv()
            onward.start()
        arrivals[0].wait_recv()
        _finish_all(local, first + passed, from_sibling)

    return pl.pallas_call(
        body, out_shape=_sds((N_DEV, rows, cols), xs.dtype), in_specs=[ANY], out_specs=ANY,
        scratch_shapes=_sem_scratch(4, True) + _sem_scratch(3, False), name=name,
    )(xs)


def _sibling_exchange(dp):
    nq, _, rows, cols = dp.shape

    def body(x_ref, out_ref, send_sem, recv_sem):
        x, y, c = _position()
        cp = pltpu.make_async_remote_copy(
            src_ref=x_ref.at[pl.ds(0, nq), 1 - c], dst_ref=out_ref, send_sem=send_sem, recv_sem=recv_sem,
            device_id=(x, y, 1 - c), device_id_type=MESH)
        cp.start()
        cp.wait()

    return pl.pallas_call(
        body, out_shape=_sds((nq, rows, cols), dp.dtype), in_specs=[ANY], out_specs=ANY,
        scratch_shapes=[pltpu.SemaphoreType.DMA(()), pltpu.SemaphoreType.DMA(())],
        name="rs_sibling",
    )(dp)


def _chip_all_to_all(xs):
    nq, rows, cols = xs.shape

    def body(x_ref, out_ref, send_sems, recv_sems, local_sem):
        local, sends, recvs = _rs_chip_copies(x_ref, out_ref, send_sems, recv_sems, local_sem)
        _start_all(local, sends)
        _finish_all(local, sends, recvs)

    return pl.pallas_call(
        body, out_shape=_sds((nq, rows, cols), xs.dtype), in_specs=[ANY], out_specs=ANY,
        scratch_shapes=_sem_scratch(3, True), name="rs_chips",
    )(xs)


def _pad_rows(a, rows):
    return jnp.pad(a, ((0, rows - a.shape[0]), (0, 0)))


def kernel(x, c, w_ada, b_ada, norm1_g, w_in, a_ln_g, a_ln_b, a_ws, a_bs, w_pa, b_conv_w, b_conv_b, b_ln_g, b_ln_b, w_pb, w_out, norm2_g, w_ff1, w_ff2, final_g, loss_target, m_w_ada, m_b_ada, m_norm1_g, m_w_in, m_a_ln_g, m_a_ln_b, m_a_ws, m_a_bs, m_w_pa, m_b_conv_w, m_b_conv_b, m_b_ln_g, m_b_ln_b, m_w_pb, m_w_out, m_norm2_g, m_w_ff1, m_w_ff2, m_final_g, v_w_ada, v_b_ada, v_norm1_g, v_w_in, v_a_ln_g, v_a_ln_b, v_a_ws, v_a_bs, v_w_pa, v_b_conv_w, v_b_conv_b, v_b_ln_g, v_b_ln_b, v_w_pb, v_w_out, v_norm2_g, v_w_ff1, v_w_ff2, v_final_g):
    nb, seq, d = x.shape
    nl = w_in.shape[0]
    t = nb * seq
    pk = _Pack(d, 0)
    gk = _Pack(d, SMALL_SLOT)
    assert d % (N_DEV * CHUNK) == 0 and d // HEADS == CHUNK and seq % CHUNK == 0
    tm_big = min(TM_BIG, seq)
    tm_mix = min(TM_MIX, seq)
    ax, ay, ac = _position()
    dev = 4 * ax + 2 * ay + ac
    ncol = 6 * d
    cols = ncol // N_DEV
    cpd = d // N_DEV
    bsz = nb * N_DEV

    cw_rows = nl * HALO
    small = jnp.concatenate([
        c.reshape(nb * d // CHUNK, CHUNK),
        jnp.pad(b_conv_w.reshape(nl, CONV_TAPS, cpd), ((0, 0), (0, HALO - CONV_TAPS), (0, 0))).reshape(cw_rows, cpd),
    ], axis=0)
    c_rows = nb * d // CHUNK
    small_all = _all_gather(small, "ag_small")
    c_all = small_all[:, :c_rows].reshape(bsz, d)
    cw_all = small_all[:, c_rows:].reshape(N_DEV, nl, HALO, cpd).transpose(1, 2, 0, 3).reshape(nl, HALO, d)
    cwb_all = jnp.repeat(cw_all, 8, axis=1)
    b_cols = lax.dynamic_slice_in_dim(b_ada, dev * cols, cols, axis=1).reshape(nl, 1, cols)
    mod_cols = _mod_fwd(c_all, w_ada, b_cols)
    mod_all = _all_gather(mod_cols.reshape(nl * bsz, cols), "ag_mod")
    mod_all = mod_all.reshape(N_DEV, nl, bsz, cols).transpose(1, 2, 0, 3).reshape(nl, bsz, ncol)
    mod_mine = lax.dynamic_slice_in_dim(mod_all, dev * nb, nb, axis=1)

    causal = jnp.tril(jnp.ones((CHUNK, CHUNK), bool))
    wm_all = jnp.where(causal[None, None], a_ws, 0.0)
    wm_bf = wm_all.astype(BF16)
    wmt_bf = jnp.swapaxes(wm_all, 2, 3).astype(BF16)
    bsx_all = jnp.broadcast_to(jnp.swapaxes(a_bs, 1, 2)[:, :, :, None], (nl, CHUNK, HEADS, CHUNK)).reshape(nl, CHUNK, d)

    def vec_rows(l):
        return _pad_rows(jnp.stack([a_ln_g[l], a_ln_b[l], b_conv_b[l], b_ln_g[l], b_ln_b[l]]), 8)

    def weight_block(l):
        return jnp.concatenate([
            w_ff1[l].T, w_ff2[l], w_pa[l], w_pb[l], w_out[l], w_in[l].T], axis=0).astype(BF16)

    xs = x.reshape(t, d)
    saved = []
    gw = _all_gather(weight_block(0), "ag_weights")
    for l in range(nl):
        nxt = weight_block(l + 1) if l + 1 < nl else None
        mod3 = mod_mine[l].reshape(nb, 1, ncol)
        vecs = vec_rows(l)
        proj = _fwd_in(xs, mod3, norm1_g[l].reshape(1, d), gw, pk, seq, tm_big)
        x1, ya, yb, o1, zc, *gw_next = _fwd_mix(
            proj, xs, mod3, vecs, wm_bf[l], bsx_all[l], cwb_all[l], gw, pk, seq, tm_mix, nxt)
        x2, f, o2, *gw_next = _fwd_ffn(x1, mod3, norm2_g[l].reshape(1, d), gw, pk, seq, tm_big, *gw_next)
        saved.append((xs, x1, proj, ya, yb, o1, zc, f, o2, gw, mod3, vecs))
        xs = x2
        if gw_next:
            gw = gw_next[0]

    dx, loss_blk, dfg = _loss_head(xs, loss_target.reshape(t, d), final_g.reshape(1, d), tm_big)
    loss = lax.psum(loss_blk[0, 0], ("x", "y", "c"))

    core = ac.reshape(1).astype(jnp.int32)
    wg = {k: [None] * nl for k in ("w_in", "w_ff1", "w_ff2", "w_pa", "w_pb", "w_out")}
    small_red = [None] * nl
    dmod_rows = [None] * nl
    per_layer = 8 + 2 * CHUNK + HALO
    assert per_layer <= N_DEV * SMALL_ROWS <= N_DEV * SMALL_SLOT

    def reduced(l, red):
        wg["w_in"][l] = red[gk.off_in:gk.off_in + gk.n_in].T
        wg["w_ff1"][l] = red[gk.off_ff1:gk.off_ff1 + gk.n_ff].T
        wg["w_ff2"][l] = red[gk.off_ff2:gk.off_ff2 + gk.n_ff]
        wg["w_pa"][l] = red[gk.off_pa:gk.off_pa + gk.n_p]
        wg["w_pb"][l] = red[gk.off_pb:gk.off_pb + gk.n_p]
        wg["w_out"][l] = red[gk.off_out:gk.off_out + gk.n_p]
        small_red[l] = red[gk.off_small:gk.off_small + SMALL_ROWS]

    pending = None
    for l in reversed(range(nl)):
        x0, x1, proj, ya, yb, o1, zc, f, o2, gw, mod3, vecs = saved[l]
        dx1, df, do2, h2, ms2, ps2 = _bwd_ffn(dx, x1, f, o2, mod3, norm2_g[l].reshape(1, d), gw, pk, seq, tm_big)
        (dproj, dzc, do1, dya, dyb, mg, aa, ba, ms1, ps1, dws, dbs) = _bwd_mix(
            dx1, proj, ya, yb, o1, zc, mod3, vecs, wm_bf[l], wmt_bf[l], bsx_all[l], gw, pk, seq, tm_mix)
        dproj, dx, h, ms0, ps0, dcw, *got = _bwd_in(
            dproj, dzc, proj, x0, dx1, mod3, norm1_g[l].reshape(1, d), cwb_all[l], gw, pk, seq, tm_mix,
            None if pending is None else pending[1])
        if pending is not None:
            reduced(pending[0], _sum_blocks(got[0], "sum_chips"))
        vec_g = jnp.concatenate([ps0[0:1], ps1[0:5], ps2[0:1], jnp.zeros((1, d), F32)], axis=0)
        small = _pad_rows(jnp.concatenate([vec_g, dws, dbs, dcw], axis=0), N_DEV * SMALL_ROWS)
        small = jnp.pad(small.reshape(N_DEV, SMALL_ROWS, d).astype(BF16), ((0, 0), (0, SMALL_SLOT - SMALL_ROWS), (0, 0)))
        grads = _place_rows(lax.empty((N_DEV, gk.rows, d), BF16), small, gk.off_small)
        grads = _wgrad(df, h2, grads, 2, gk.off_ff1, False, "wgrad_ff1")
        grads = _wgrad(f, do2, grads, 2, gk.off_ff2, True, "wgrad_ff2")
        grads = _wgrad(aa, dya, grads, N_DEV, gk.off_pa, False, "wgrad_pa")
        grads = _wgrad(ba, dyb, grads, N_DEV, gk.off_pb, False, "wgrad_pb")
        grads = _wgrad(mg, do1, grads, N_DEV, gk.off_out, False, "wgrad_out")
        grads = _wgrad(dproj, h, grads, 1, gk.off_in, False, "wgrad_in")
        dp = grads.reshape(N_CHIP, 2, gk.rows, d)
        pending = (l, _add_sibling(dp, _sibling_exchange(dp), core))
        dmod_rows[l] = jnp.concatenate([ms0[:, 0], ms0[:, 1], ms1[:, 0], ms2[:, 0], ms2[:, 1], ms2[:, 2]], axis=1)
    reduced(pending[0], _sum_blocks(_chip_all_to_all(pending[1]), "sum_chips"))

    small_all = _all_gather(jnp.concatenate(small_red, axis=0), "ag_small_grads")
    lay = small_all.reshape(N_DEV, nl, SMALL_ROWS, d).transpose(1, 0, 2, 3).reshape(nl, N_DEV * SMALL_ROWS, d)
    n_dm = nl * nb * 6
    tail = jnp.concatenate([jnp.stack(dmod_rows).reshape(n_dm, d), dfg], axis=0)
    tail_all = _all_gather(tail, "ag_dmod")
    dmod_all = tail_all[:, :n_dm].reshape(N_DEV, nl, nb, ncol).transpose(1, 0, 2, 3).reshape(nl, bsz, ncol)
    dmod_cols = lax.dynamic_slice_in_dim(dmod_all, dev * cols, cols, axis=2)
    g_w_ada, g_b_ada = _mod_bwd(c_all, dmod_cols, dmod_all)
    g_final = _sum_blocks(tail_all[:, n_dm:], "sum_final_g")[0]

    g_small = {
        "norm1_g": lay[:, 0], "a_ln_g": lay[:, 1], "a_ln_b": lay[:, 2], "b_ln_g": lay[:, 3], "b_ln_b": lay[:, 4],
        "b_conv_b": lay[:, 5], "norm2_g": lay[:, 6],
        "a_ws": lay[:, 8:8 + CHUNK].reshape(nl, CHUNK, HEADS, CHUNK).transpose(0, 2, 1, 3),
        "a_bs": jnp.swapaxes(lay[:, 8 + CHUNK:8 + 2 * CHUNK, ::CHUNK], 1, 2),
        "b_conv_w": lax.dynamic_slice_in_dim(
            lay[:, 8 + 2 * CHUNK:8 + 2 * CHUNK + CONV_TAPS], dev * cpd, cpd, axis=2).reshape(nl, CONV_TAPS, 1, cpd),
        "final_g": g_final,
    }
    grads = dict(g_small)
    grads["w_ada"] = g_w_ada
    grads["b_ada"] = g_b_ada.reshape(nl, ncol)
    for k, v in wg.items():
        grads[k] = jnp.stack(v)

    names = ["w_ada", "b_ada", "norm1_g", "w_in", "a_ln_g", "a_ln_b", "a_ws", "a_bs", "w_pa", "b_conv_w", "b_conv_b",
             "b_ln_g", "b_ln_b", "w_pb", "w_out", "norm2_g", "w_ff1", "w_ff2", "final_g"]
    weights = dict(w_ada=w_ada, b_ada=b_ada, norm1_g=norm1_g, w_in=w_in, a_ln_g=a_ln_g, a_ln_b=a_ln_b, a_ws=a_ws,
                   a_bs=a_bs, w_pa=w_pa, b_conv_w=b_conv_w, b_conv_b=b_conv_b, b_ln_g=b_ln_g, b_ln_b=b_ln_b,
                   w_pb=w_pb, w_out=w_out, norm2_g=norm2_g, w_ff1=w_ff1, w_ff2=w_ff2, final_g=final_g)
    m_in = dict(w_ada=m_w_ada, b_ada=m_b_ada, norm1_g=m_norm1_g, w_in=m_w_in, a_ln_g=m_a_ln_g, a_ln_b=m_a_ln_b,
                a_ws=m_a_ws, a_bs=m_a_bs, w_pa=m_w_pa, b_conv_w=m_b_conv_w, b_conv_b=m_b_conv_b, b_ln_g=m_b_ln_g,
                b_ln_b=m_b_ln_b, w_pb=m_w_pb, w_out=m_w_out, norm2_g=m_norm2_g, w_ff1=m_w_ff1, w_ff2=m_w_ff2,
                final_g=m_final_g)
    v_in = dict(w_ada=v_w_ada, b_ada=v_b_ada, norm1_g=v_norm1_g, w_in=v_w_in, a_ln_g=v_a_ln_g, a_ln_b=v_a_ln_b,
                a_ws=v_a_ws, a_bs=v_a_bs, w_pa=v_w_pa, b_conv_w=v_b_conv_w, b_conv_b=v_b_conv_b, b_ln_g=v_b_ln_g,
                b_ln_b=v_b_ln_b, w_pb=v_w_pb, w_out=v_w_out, norm2_g=v_norm2_g, w_ff1=v_w_ff1, w_ff2=v_w_ff2,
                final_g=v_final_g)

    deltas, new_m, new_v = {}, {}, {}
    for k in names:
        shape = weights[k].shape
        two_d = (-1, shape[-1])
        g2d = grads[k].reshape(shape).reshape(two_d)
        grads[k] = grads[k].reshape(shape)
        dl, nm, nv = _adamw(weights[k].reshape(two_d), g2d, m_in[k].reshape(two_d), v_in[k].reshape(two_d),
                            "adamw_" + k)
        deltas[k], new_m[k], new_v[k] = dl.reshape(shape), nm.reshape(shape), nv.reshape(shape)

    return (loss, dx.reshape(nb, seq, d), *[grads[k] for k in names], *[deltas[k] for k in names],
            *[new_m[k] for k in names], *[new_v[k] for k in names])
```

```python
import functools

import jax
import jax.numpy as jnp
from jax import lax
from jax.experimental import pallas as pl
from jax.experimental.pallas import tpu as pltpu

F32 = jnp.float32
BF16 = jnp.bfloat16
MESH = pl.DeviceIdType.MESH
ANY = pl.BlockSpec(memory_space=pl.ANY)

N_DEV = 8
N_CHIP = 4
EPS = 1e-6
CHUNK = 128
HEADS = 8
CONV_TAPS = 31
HALO = 32
SMALL_ROWS = 40
SMALL_SLOT = 128
CONV_ROWS = 32
TM_BIG = 512
TM_MIX = 256
CONV_SUB = 256
TK_WGRAD = 4096
TK_WGRAD_CONV = 2048
VMEM_LIMIT = 56 * 1024 * 1024

ADAM_LR = 0.001
ADAM_B1 = 0.9
ADAM_B2 = 0.999
ADAM_EPS = 1e-08
ADAM_WD = 0.01
ADAM_STEP = 10


def _sds(shape, dtype):
    return jax.ShapeDtypeStruct(tuple(shape), dtype)


def _params(n_grid, vmem=VMEM_LIMIT):
    return pltpu.CompilerParams(dimension_semantics=("arbitrary",) * n_grid, vmem_limit_bytes=vmem)


def _nn(a, b):
    return jnp.dot(a, b, preferred_element_type=F32)


def _nt(a, b):
    return lax.dot_general(a, b, (((1,), (1,)), ((), ())), preferred_element_type=F32)


def _tn(a, b):
    return lax.dot_general(a, b, (((0,), (0,)), ((), ())), preferred_element_type=F32)


def _rowsum(v):
    return jnp.sum(v, axis=0, keepdims=True)


def _mean(v):
    return jnp.mean(v, axis=-1, keepdims=True)


def _add_row(ref, idx, val):
    ref[idx] = ref[idx] + val


def _ln_stats(v):
    mu = _mean(v)
    xc = v - mu
    rs = lax.rsqrt(_mean(xc * xc) + EPS)
    return xc * rs, rs


def _ln_bwd(dout, g, vhat, rs):
    dvh = dout * g
    return rs * (dvh - _mean(dvh) - vhat * _mean(dvh * vhat))


def _rms_bwd(dn, g, x, r):
    gd = dn * g
    return r * gd - x * (r * r * r) * _mean(x * gd)


class _Pack:
    def __init__(self, d, small_slot):
        self.n_in = 6 * d // N_DEV
        self.n_ff = 4 * d // N_DEV
        self.n_p = d // N_DEV
        self.off_ff1 = 0
        self.off_ff2 = self.off_ff1 + self.n_ff
        self.off_pa = self.off_ff2 + self.n_ff
        self.off_pb = self.off_pa + self.n_p
        self.off_out = self.off_pb + self.n_p
        self.off_small = self.off_out + self.n_p
        self.off_in = self.off_small + small_slot
        self.rows = self.off_in + self.n_in
        if small_slot:
            assert self.off_in % self.n_in == 0 and self.off_ff2 % self.n_ff == 0 and self.off_small % small_slot == 0


def _load_rows(g_hbm, w_vm, sems, sem0, off, rows):
    cps = [
        pltpu.make_async_copy(g_hbm.at[k, pl.ds(off, rows), :], w_vm.at[pl.ds(k * rows, rows), :], sems.at[sem0 + k])
        for k in range(N_DEV)
    ]
    for cp in cps:
        cp.start()
    for cp in cps:
        cp.wait()


def _row_spec(tm, cols, colblk=0):
    return pl.BlockSpec((tm, cols), lambda i: (i, colblk))


def _const_spec(shape):
    nd = len(shape)
    return pl.BlockSpec(tuple(shape), lambda i: (0,) * nd)


def _mod_spec(tps, cols):
    return pl.BlockSpec((1, 1, cols), lambda i: (i // tps, 0, 0))


def _mstat_spec(tps, d):
    return pl.BlockSpec((1, 8, d), lambda i: (i // tps, 0, 0))


def _position():
    return lax.axis_index("x"), lax.axis_index("y"), lax.axis_index("c")


def _other_chips(x, y):
    return [(1 - x, y), (x, 1 - y), (1 - x, 1 - y)]


def _remote(src, dst, send_sems, recv_sems, k, to):
    return pltpu.make_async_remote_copy(src_ref=src, dst_ref=dst, send_sem=send_sems.at[k], recv_sem=recv_sems.at[k],
                                        device_id=to, device_id_type=MESH)


def _slot(ref, p):
    return ref.at[4 * p[0] + 2 * p[1] + p[2]]


def _ag_stage1(x_ref, out_ref, send_sems, recv_sems, local_sem):
    x, y, c = _position()
    me = (x, y, c)
    peers = [(x, y, 1 - c)] + [(*chip, c) for chip in _other_chips(x, y)]
    sends = [_remote(x_ref, _slot(out_ref, me), send_sems, recv_sems, k, p) for k, p in enumerate(peers)]
    recvs = [_remote(x_ref, _slot(out_ref, p), send_sems, recv_sems, k, p) for k, p in enumerate(peers)]
    return pltpu.make_async_copy(x_ref, _slot(out_ref, me), local_sem), sends, recvs


def _ag_stage2(in_ref, out_ref, send_sems, recv_sems):
    x, y, c = _position()
    sibling = (x, y, 1 - c)
    chips = _other_chips(x, y)
    sends = [_remote(_slot(in_ref, (*ch, c)), _slot(out_ref, (*ch, c)), send_sems, recv_sems, j, sibling)
             for j, ch in enumerate(chips)]
    recvs = [_remote(_slot(in_ref, (*ch, c)), _slot(out_ref, (*ch, 1 - c)), send_sems, recv_sems, j, sibling)
             for j, ch in enumerate(chips)]
    return sends, recvs


def _rs_chip_copies(x_ref, out_ref, send_sems, recv_sems, local_sem):
    x, y, c = _position()
    q_me = 2 * x + y
    chips = _other_chips(x, y)
    sends = [_remote(x_ref.at[2 * px + py], out_ref.at[q_me], send_sems, recv_sems, j, (px, py, c))
             for j, (px, py) in enumerate(chips)]
    recvs = [_remote(x_ref.at[q_me], out_ref.at[2 * px + py], send_sems, recv_sems, j, (px, py, c))
             for j, (px, py) in enumerate(chips)]
    return pltpu.make_async_copy(x_ref.at[q_me], out_ref.at[q_me], local_sem), sends, recvs


def _start_all(local, sends):
    if local is not None:
        local.start()
    for cp in sends:
        cp.start()


def _finish_all(local, sends, recvs):
    for cp in recvs:
        cp.wait_recv()
    for cp in sends:
        cp.wait_send()
    if local is not None:
        local.wait()


def _sem_scratch(n, local):
    out = [pltpu.SemaphoreType.DMA((n,)), pltpu.SemaphoreType.DMA((n,))]
    return out + ([pltpu.SemaphoreType.DMA(())] if local else [])


def _fwd_in(x2d, mod3, g1, cwb, cb, gw, pk, seq, tm):
    t, d = x2d.shape
    nc = 6 * d
    tps = seq // tm
    lo = HALO - (CONV_TAPS - 1)
    sub = min(CONV_SUB, tm)

    def body(x_ref, mod_ref, g_ref, cw_ref, cb_ref, gw_hbm, proj_ref, zc_ref, w_vm, sems, zext, zsh, zc_buf, ztail):
        i = pl.program_id(0)

        @pl.when(i == 0)
        def _():
            _load_rows(gw_hbm, w_vm, sems, 0, pk.off_in, pk.n_in)
            ztail[...] = jnp.zeros(ztail.shape, F32)

        x = x_ref[...]
        m = mod_ref[0]
        r = lax.rsqrt(_mean(x * x) + EPS)
        h = (x * r * g_ref[...] * (1.0 + m[:, d:2 * d]) + m[:, 0:d]).astype(BF16)

        def chunk(j):
            proj_ref[:, j * 512:(j + 1) * 512] = _nt(h, w_vm[j * 512:(j + 1) * 512, :]).astype(BF16)

        glu = range(2 * d // 512, 4 * d // 512)
        for j in glu:
            chunk(j)
        z = proj_ref[:, 2 * d:3 * d].astype(F32) * jax.nn.sigmoid(proj_ref[:, 3 * d:4 * d].astype(F32))
        zext[pl.ds(0, HALO), :] = jnp.where((i % tps) == 0, 0.0, ztail[...])
        zext[pl.ds(HALO, tm), :] = z
        ztail[...] = zext[pl.ds(tm, HALO), :]
        for s in range(tm // sub):
            _shift_copies(zext, zsh, s * sub, sub + HALO - 8)
            _conv_taps(zext, zsh, s * sub, cw_ref, [lo + k for k in range(CONV_TAPS)], cb_ref[...], zc_buf, sub, d)
        zc_ref[...] = zc_buf[...].astype(BF16)
        for j in range(nc // 512):
            if j not in glu:
                chunk(j)

    return pl.pallas_call(
        body, grid=(t // tm,),
        in_specs=[_row_spec(tm, d), _mod_spec(tps, nc), _const_spec((1, d)), _const_spec((8 * HALO, d)),
                  _const_spec((1, d)), ANY],
        out_specs=[_row_spec(tm, nc), _row_spec(tm, d)], out_shape=[_sds((t, nc), BF16), _sds((t, d), BF16)],
        scratch_shapes=[pltpu.VMEM((nc, d), BF16), pltpu.SemaphoreType.DMA((N_DEV,)),
                        pltpu.VMEM((tm + HALO, d), F32), pltpu.VMEM((7, sub + HALO, d), F32),
                        pltpu.VMEM((tm, d), F32), pltpu.VMEM((HALO, d), F32)],
        compiler_params=_params(1), name="fwd_in",
    )(x2d, mod3, g1, cwb, cb, gw)


def _shift_copies(src, sh, base, rows):
    for r in range(1, 8):
        sh[r - 1, pl.ds(0, rows), :] = src[pl.ds(base + r, rows), :]


def _window(src, sh, base, offset, start, size):
    r, q = offset % 8, offset // 8
    if r == 0:
        return src[pl.ds(base + start + 8 * q, size), :]
    return sh[r - 1, pl.ds(start + 8 * q, size), :]


def _conv_taps(src, sh, base, cwb_ref, offsets, bias, out_ref, rows, d):
    nsub = CONV_ROWS // 8
    for rb in range(rows // CONV_ROWS):
        accs = [jnp.broadcast_to(bias, (8, d))] * nsub
        for k in range(CONV_TAPS):
            w8 = cwb_ref[pl.ds(8 * k, 8), :]
            accs = [a + w8 * _window(src, sh, base, offsets[k], rb * CONV_ROWS + 8 * j, 8) for j, a in enumerate(accs)]
        for j, a in enumerate(accs):
            out_ref[pl.ds(base + rb * CONV_ROWS + 8 * j, 8), :] = a


def _conv_wgrad(dsrc, zsrc, zsh, base, acc_ref, lo, rows, d):
    for r in range(8):
        taps = [k for k in range(CONV_TAPS) if (lo + k) % 8 == r]
        accs = [jnp.zeros((8, d), F32)] * len(taps)
        for rb in range(rows // 8):
            dblk = dsrc[pl.ds(base + rb * 8, 8), :]
            accs = [a + dblk * _window(zsrc, zsh, base, lo + k, rb * 8, 8) for a, k in zip(accs, taps)]
        for a, k in zip(accs, taps):
            acc_ref[pl.ds(8 * k, 8), :] = acc_ref[pl.ds(8 * k, 8), :] + a


def _fill_zext(zext, z, halo_ref, first, tm, d):
    hz = halo_ref[:, 0:d].astype(F32) * jax.nn.sigmoid(halo_ref[:, d:2 * d].astype(F32))
    zext[pl.ds(0, HALO), :] = jnp.where(first, 0.0, hz)
    zext[pl.ds(HALO, tm), :] = z


def _fwd_mix(proj, zc, x2d, mod3, vecs, wm, bsx, gw, pk, seq, tm, nxt=None):
    t, d = x2d.shape
    tps = seq // tm
    n = t // tm
    comm = nxt is not None

    def body(*refs):
        proj_ref, zc_ref, x_ref, mod_ref, vec_ref, wm_ref, bs_ref, gw_hbm = refs[:8]
        refs = refs[8:]
        if comm:
            nx_ref, refs = refs[0], refs[1:]
        x1_ref, ya_ref, yb_ref, o1_ref = refs[:4]
        refs = refs[4:]
        if comm:
            gwn_ref, refs = refs[0], refs[1:]
        wpa, wpb, wout, sems, vn_buf, a_buf = refs[:6]
        i = pl.program_id(0)

        if comm:
            ag = functools.partial(_ag_stage1, nx_ref, gwn_ref, *refs[6:9])

            @pl.when(i == 0)
            def _():
                local, sends, _ = ag()
                _start_all(local, sends)

        @pl.when(i == 0)
        def _():
            _load_rows(gw_hbm, wpa, sems, 0, pk.off_pa, pk.n_p)
            _load_rows(gw_hbm, wpb, sems, N_DEV, pk.off_pb, pk.n_p)
            _load_rows(gw_hbm, wout, sems, 2 * N_DEV, pk.off_out, pk.n_p)

        m = mod_ref[0]
        vhat, _ = _ln_stats(proj_ref[:, d:2 * d].astype(F32))
        vn_buf[...] = (vhat * vec_ref[0:1, :] + vec_ref[1:2, :]).astype(BF16)
        for c in range(tm // CHUNK):
            rs_ = slice(c * CHUNK, (c + 1) * CHUNK)
            for h in range(HEADS):
                cs_ = slice(h * CHUNK, (h + 1) * CHUNK)
                s_b = _nn(wm_ref[h], vn_buf[rs_, cs_]) + bs_ref[:, cs_]
                a_buf[rs_, cs_] = (proj_ref[rs_, cs_].astype(F32) * s_b).astype(BF16)
        y_a = _nn(a_buf[...], wpa[...])
        ya_ref[...] = y_a.astype(BF16)
        zhat, _ = _ln_stats(zc_ref[...].astype(F32))
        zn = zhat * vec_ref[3:4, :] + vec_ref[4:5, :]
        b_act = (zn * jax.nn.sigmoid(zn)).astype(BF16)
        y_b = _nn(b_act, wpb[...])
        yb_ref[...] = y_b.astype(BF16)
        merged = (jax.nn.sigmoid(proj_ref[:, 4 * d:5 * d].astype(F32)) * y_a
                  + jax.nn.sigmoid(proj_ref[:, 5 * d:6 * d].astype(F32)) * y_b).astype(BF16)
        o1 = _nn(merged, wout[...])
        o1_ref[...] = o1.astype(BF16)
        x1_ref[...] = x_ref[...] + m[:, 2 * d:3 * d] * o1

        if comm:
            @pl.when(i == n - 1)
            def _():
                _finish_all(*ag())

    act = _sds((t, d), BF16)
    return pl.pallas_call(
        body, grid=(n,),
        in_specs=[_row_spec(tm, 6 * d), _row_spec(tm, d), _row_spec(tm, d), _mod_spec(tps, 6 * d), _const_spec((8, d)),
                  _const_spec((HEADS, CHUNK, CHUNK)), _const_spec((CHUNK, d)), ANY]
        + ([ANY] if comm else []),
        out_specs=[_row_spec(tm, d)] * 4 + ([ANY] if comm else []),
        out_shape=[_sds((t, d), F32), act, act, act]
        + ([_sds((N_DEV,) + nxt.shape, nxt.dtype)] if comm else []),
        scratch_shapes=[pltpu.VMEM((d, d), BF16), pltpu.VMEM((d, d), BF16), pltpu.VMEM((d, d), BF16),
                        pltpu.SemaphoreType.DMA((3 * N_DEV,)),
                        pltpu.VMEM((tm, d), BF16), pltpu.VMEM((tm, d), BF16)]
        + (_sem_scratch(4, True) if comm else []),
        compiler_params=_params(1), name="fwd_mix_ag" if comm else "fwd_mix",
    )(proj, zc, x2d, mod3, vecs, wm, bsx, gw, *([nxt] if comm else []))


def _fwd_ffn(x1, mod3, g2, gw, pk, seq, tm, gw_next=None):
    t, d = x1.shape
    nf = 4 * d
    tps = seq // tm
    n = t // tm
    comm = gw_next is not None

    def body(*refs):
        x_ref, mod_ref, g_ref, gw_hbm = refs[:4]
        refs = refs[4:]
        if comm:
            gwn_in, refs = refs[0], refs[1:]
        x2_ref, f_ref, o2_ref = refs[:3]
        refs = refs[3:]
        if comm:
            gwn_out, refs = refs[0], refs[1:]
        w1, w2, sems = refs[:3]
        i = pl.program_id(0)

        if comm:
            ag = functools.partial(_ag_stage2, gwn_in, gwn_out, *refs[3:5])

            @pl.when(i == 0)
            def _():
                _start_all(None, ag()[0])

        @pl.when(i == 0)
        def _():
            _load_rows(gw_hbm, w1, sems, 0, pk.off_ff1, pk.n_ff)
            _load_rows(gw_hbm, w2, sems, N_DEV, pk.off_ff2, pk.n_ff)

        x = x_ref[...]
        m = mod_ref[0]
        r = lax.rsqrt(_mean(x * x) + EPS)
        h2 = (x * r * g_ref[...] * (1.0 + m[:, 4 * d:5 * d]) + m[:, 3 * d:4 * d]).astype(BF16)
        acc = jnp.zeros(x.shape, F32)
        for j in range(nf // 512):
            js = slice(j * 512, (j + 1) * 512)
            f = _nt(h2, w1[js, :])
            f_ref[:, js] = f.astype(BF16)
            acc = acc + _nn(jnp.square(jnp.maximum(f, 0.0)).astype(BF16), w2[js, :])
        o2_ref[...] = acc.astype(BF16)
        x2_ref[...] = x + m[:, 5 * d:6 * d] * acc

        if comm:
            @pl.when(i == n - 1)
            def _():
                _finish_all(None, *ag())

    return pl.pallas_call(
        body, grid=(n,),
        in_specs=[_row_spec(tm, d), _mod_spec(tps, 6 * d), _const_spec((1, d)), ANY] + ([ANY] if comm else []),
        out_specs=[_row_spec(tm, d), _row_spec(tm, nf), _row_spec(tm, d)] + ([ANY] if comm else []),
        out_shape=[_sds((t, d), F32), _sds((t, nf), BF16), _sds((t, d), BF16)]
        + ([_sds(gw_next.shape, gw_next.dtype)] if comm else []),
        scratch_shapes=[pltpu.VMEM((nf, d), BF16), pltpu.VMEM((nf, d), BF16), pltpu.SemaphoreType.DMA((2 * N_DEV,))]
        + (_sem_scratch(3, False) if comm else []),
        input_output_aliases={4: 3} if comm else {},
        compiler_params=_params(1), name="fwd_ffn_ag" if comm else "fwd_ffn",
    )(x1, mod3, g2, gw, *([gw_next] if comm else []))


def _loss_head(x, tgt, fg, tm):
    t, d = x.shape
    n = t // tm

    def body(x_ref, t_ref, g_ref, dx_ref, loss_ref, dg_ref, lacc):
        i = pl.program_id(0)

        @pl.when(i == 0)
        def _():
            lacc[...] = jnp.zeros(lacc.shape, F32)
            dg_ref[...] = jnp.zeros(dg_ref.shape, F32)

        xv = x_ref[...]
        g = g_ref[...]
        r = lax.rsqrt(_mean(xv * xv) + EPS)
        err = xv * r * g - t_ref[...]
        lacc[...] = lacc[...] + _rowsum(err * err)
        dy = err * (1.0 / d)
        _add_row(dg_ref, (slice(0, 1), slice(None)), _rowsum(dy * xv * r))
        dx_ref[...] = _rms_bwd(dy, g, xv, r)

        @pl.when(i == n - 1)
        def _():
            loss_ref[...] = jnp.broadcast_to(jnp.sum(lacc[...], keepdims=True) * (0.5 / d), loss_ref.shape)

    return pl.pallas_call(
        body, grid=(n,),
        in_specs=[_row_spec(tm, d), _row_spec(tm, d), _const_spec((1, d))],
        out_specs=[_row_spec(tm, d), _const_spec((8, 128)), _const_spec((8, d))],
        out_shape=[_sds((t, d), F32), _sds((8, 128), F32), _sds((8, d), F32)],
        scratch_shapes=[pltpu.VMEM((1, d), F32)],
        compiler_params=_params(1), name="loss_head",
    )(x, tgt, fg)


def _bwd_ffn(dx2, x1, f, o2, mod3, g2, gw, pk, seq, tm):
    t, d = x1.shape
    nf = 4 * d
    tps = seq // tm
    nb = t // seq

    def body(dx2_ref, x_ref, f_ref, o2_ref, mod_ref, g_ref, gw_hbm,
             dx1_ref, df_ref, do2_ref, h2_ref, ms_ref, ps_ref, w1, w2, sems):
        i = pl.program_id(0)

        @pl.when(i == 0)
        def _():
            _load_rows(gw_hbm, w1, sems, 0, pk.off_ff1, pk.n_ff)
            _load_rows(gw_hbm, w2, sems, N_DEV, pk.off_ff2, pk.n_ff)
            ps_ref[...] = jnp.zeros(ps_ref.shape, F32)

        @pl.when((i % tps) == 0)
        def _():
            ms_ref[...] = jnp.zeros(ms_ref.shape, F32)

        dx2 = dx2_ref[...]
        x = x_ref[...]
        m = mod_ref[0]
        g = g_ref[...]
        sh2, sc2, gt2 = m[:, 3 * d:4 * d], m[:, 4 * d:5 * d], m[:, 5 * d:6 * d]
        _add_row(ms_ref, (0, slice(2, 3), slice(None)), _rowsum(dx2 * o2_ref[...].astype(F32)))
        do2 = (gt2 * dx2).astype(BF16)
        do2_ref[...] = do2
        r = lax.rsqrt(_mean(x * x) + EPS)
        n = x * r * g
        h2_ref[...] = (n * (1.0 + sc2) + sh2).astype(BF16)
        dh = jnp.zeros(x.shape, F32)
        for j in range(nf // 512):
            js = slice(j * 512, (j + 1) * 512)
            dr = _nt(do2, w2[js, :])
            df = (dr * (2.0 * jnp.maximum(f_ref[:, js].astype(F32), 0.0))).astype(BF16)
            df_ref[:, js] = df
            dh = dh + _nn(df, w1[js, :])
        _add_row(ms_ref, (0, slice(0, 1), slice(None)), _rowsum(dh))
        _add_row(ms_ref, (0, slice(1, 2), slice(None)), _rowsum(dh * n))
        dn = dh * (1.0 + sc2)
        _add_row(ps_ref, (slice(0, 1), slice(None)), _rowsum(dn * x * r))
        dx1_ref[...] = dx2 + _rms_bwd(dn, g, x, r)

    act = _sds((t, d), BF16)
    return pl.pallas_call(
        body, grid=(t // tm,),
        in_specs=[_row_spec(tm, d), _row_spec(tm, d), _row_spec(tm, nf), _row_spec(tm, d), _mod_spec(tps, 6 * d),
                  _const_spec((1, d)), ANY],
        out_specs=[_row_spec(tm, d), _row_spec(tm, nf), _row_spec(tm, d), _row_spec(tm, d), _mstat_spec(tps, d),
                   _const_spec((8, d))],
        out_shape=[_sds((t, d), F32), _sds((t, nf), BF16), act, act, _sds((nb, 8, d), F32), _sds((8, d), F32)],
        scratch_shapes=[pltpu.VMEM((nf, d), BF16), pltpu.VMEM((nf, d), BF16), pltpu.SemaphoreType.DMA((2 * N_DEV,))],
        compiler_params=_params(1), name="bwd_ffn",
    )(dx2, x1, f, o2, mod3, g2, gw)


def _bwd_mix(dx1, proj, ya, yb, o1, zc, mod3, vecs, wm, wmt, bsx, gw, pk, seq, tm):
    t, d = dx1.shape
    tps = seq // tm
    nb = t // seq
    n = t // tm

    def body(dx1_ref, proj_ref, ya_ref, yb_ref, o1_ref, zc_ref, mod_ref, vec_ref, wm_ref, wmt_ref, bs_ref, gw_hbm,
             dp_ref, dzc_ref, do1_ref, dya_ref, dyb_ref, mg_ref, aa_ref, ba_ref, ms_ref, ps_ref, dws_ref, dbs_ref,
             wpa, wpb, wout, sems, vn_buf, da_buf, dvn_buf):
        i = pl.program_id(0)

        @pl.when(i == 0)
        def _():
            _load_rows(gw_hbm, wpa, sems, 0, pk.off_pa, pk.n_p)
            _load_rows(gw_hbm, wpb, sems, N_DEV, pk.off_pb, pk.n_p)
            _load_rows(gw_hbm, wout, sems, 2 * N_DEV, pk.off_out, pk.n_p)
            ps_ref[...] = jnp.zeros(ps_ref.shape, F32)
            dws_ref[...] = jnp.zeros(dws_ref.shape, F32)
            dbs_ref[...] = jnp.zeros(dbs_ref.shape, F32)

        @pl.when((i % tps) == 0)
        def _():
            ms_ref[...] = jnp.zeros(ms_ref.shape, F32)

        m = mod_ref[0]
        dx1v = dx1_ref[...]
        _add_row(ms_ref, (0, slice(0, 1), slice(None)), _rowsum(dx1v * o1_ref[...].astype(F32)))
        do1 = (m[:, 2 * d:3 * d] * dx1v).astype(BF16)
        do1_ref[...] = do1
        dmg = _nt(do1, wout[...])
        sa = jax.nn.sigmoid(proj_ref[:, 4 * d:5 * d].astype(F32))
        sb = jax.nn.sigmoid(proj_ref[:, 5 * d:6 * d].astype(F32))
        y_a = ya_ref[...].astype(F32)
        y_b = yb_ref[...].astype(F32)
        dya = (dmg * sa).astype(BF16)
        dyb = (dmg * sb).astype(BF16)
        dya_ref[...] = dya
        dyb_ref[...] = dyb
        dp_ref[:, 4 * d:5 * d] = (dmg * y_a * sa * (1.0 - sa)).astype(BF16)
        dp_ref[:, 5 * d:6 * d] = (dmg * y_b * sb * (1.0 - sb)).astype(BF16)
        mg_ref[...] = (sa * y_a + sb * y_b).astype(BF16)
        da_buf[...] = _nt(dya, wpa[...])
        db = _nt(dyb, wpb[...])
        vhat, rs = _ln_stats(proj_ref[:, d:2 * d].astype(F32))
        alg = vec_ref[0:1, :]
        vn_buf[...] = (vhat * alg + vec_ref[1:2, :]).astype(BF16)
        for c in range(tm // CHUNK):
            rs_ = slice(c * CHUNK, (c + 1) * CHUNK)
            for h in range(HEADS):
                cs_ = slice(h * CHUNK, (h + 1) * CHUNK)
                vn_b = vn_buf[rs_, cs_]
                s_b = _nn(wm_ref[h], vn_b) + bs_ref[:, cs_]
                u_b = proj_ref[rs_, cs_].astype(F32)
                da_b = da_buf[rs_, cs_]
                aa_ref[rs_, cs_] = (u_b * s_b).astype(BF16)
                dp_ref[rs_, cs_] = (da_b * s_b).astype(BF16)
                ds_b = da_b * u_b
                dbs_ref[:, cs_] = dbs_ref[:, cs_] + ds_b
                ds_bf = ds_b.astype(BF16)
                dvn_buf[rs_, cs_] = _nn(wmt_ref[h], ds_bf)
                dws_ref[:, cs_] = dws_ref[:, cs_] + _nt(ds_bf, vn_b)
        dvn = dvn_buf[...]
        _add_row(ps_ref, (slice(0, 1), slice(None)), _rowsum(dvn * vhat))
        _add_row(ps_ref, (slice(1, 2), slice(None)), _rowsum(dvn))
        dp_ref[:, d:2 * d] = _ln_bwd(dvn, alg, vhat, rs).astype(BF16)
        dp_ref[:, 2 * d:4 * d] = jnp.zeros((tm, 2 * d), BF16)
        zhat, rsb = _ln_stats(zc_ref[...].astype(F32))
        blg = vec_ref[3:4, :]
        zn = zhat * blg + vec_ref[4:5, :]
        sg = jax.nn.sigmoid(zn)
        ba_ref[...] = (zn * sg).astype(BF16)
        dzn = db * (sg * (1.0 + zn * (1.0 - sg)))
        _add_row(ps_ref, (slice(2, 3), slice(None)), _rowsum(dzn * zhat))
        _add_row(ps_ref, (slice(3, 4), slice(None)), _rowsum(dzn))
        dzc = _ln_bwd(dzn, blg, zhat, rsb)
        _add_row(ps_ref, (slice(4, 5), slice(None)), _rowsum(dzc))
        dzc_ref[...] = dzc.astype(BF16)

        @pl.when(i == n - 1)
        def _():
            causal = (lax.broadcasted_iota(jnp.int32, (CHUNK, CHUNK), 0)
                      >= lax.broadcasted_iota(jnp.int32, (CHUNK, CHUNK), 1))
            for h in range(HEADS):
                cs_ = slice(h * CHUNK, (h + 1) * CHUNK)
                dws_ref[:, cs_] = jnp.where(causal, dws_ref[:, cs_], 0.0)
                dbs_ref[:, cs_] = jnp.broadcast_to(jnp.sum(dbs_ref[:, cs_], axis=1, keepdims=True), (CHUNK, CHUNK))

    act = _sds((t, d), BF16)
    return pl.pallas_call(
        body, grid=(n,),
        in_specs=[_row_spec(tm, d), _row_spec(tm, 6 * d), _row_spec(tm, d), _row_spec(tm, d), _row_spec(tm, d),
                  _row_spec(tm, d), _mod_spec(tps, 6 * d), _const_spec((8, d)), _const_spec((HEADS, CHUNK, CHUNK)),
                  _const_spec((HEADS, CHUNK, CHUNK)), _const_spec((CHUNK, d)), ANY],
        out_specs=[_row_spec(tm, 6 * d)] + [_row_spec(tm, d)] * 7
        + [_mstat_spec(tps, d), _const_spec((8, d)), _const_spec((CHUNK, d)), _const_spec((CHUNK, d))],
        out_shape=[_sds((t, 6 * d), BF16)] + [act] * 7
        + [_sds((nb, 8, d), F32), _sds((8, d), F32), _sds((CHUNK, d), F32), _sds((CHUNK, d), F32)],
        scratch_shapes=[pltpu.VMEM((d, d), BF16), pltpu.VMEM((d, d), BF16), pltpu.VMEM((d, d), BF16),
                        pltpu.SemaphoreType.DMA((3 * N_DEV,)),
                        pltpu.VMEM((tm, d), BF16), pltpu.VMEM((tm, d), F32), pltpu.VMEM((tm, d), F32)],
        compiler_params=_params(1), name="bwd_mix",
    )(dx1, proj, ya, yb, o1, zc, mod3, vecs, wm, wmt, bsx, gw)


def _bwd_in(dproj, x2d, dx1, mod3, g1, gw, pk, seq, tm, part=None):
    t, d = x2d.shape
    tps = seq // tm
    nb = t // seq
    n = t // tm
    comm = part is not None

    def body(*refs):
        dp_ref, x_ref, dx1_ref, mod_ref, g_ref, gw_hbm = refs[:6]
        refs = refs[6:]
        if comm:
            part_ref, refs = refs[0], refs[1:]
        dx_ref, h_ref, ms_ref, ps_ref = refs[:4]
        refs = refs[4:]
        if comm:
            got_ref, refs = refs[0], refs[1:]
        w_vm, sems = refs[:2]
        i = pl.program_id(0)

        if comm:
            rs = functools.partial(_rs_chip_copies, part_ref, got_ref, *refs[2:5])

            @pl.when(i == 0)
            def _():
                local, sends, _ = rs()
                _start_all(local, sends)

        @pl.when(i == 0)
        def _():
            _load_rows(gw_hbm, w_vm, sems, 0, pk.off_in, pk.n_in)
            ps_ref[...] = jnp.zeros(ps_ref.shape, F32)

        @pl.when((i % tps) == 0)
        def _():
            ms_ref[...] = jnp.zeros(ms_ref.shape, F32)

        dh = _nn(dp_ref[...], w_vm[...])
        x = x_ref[...]
        m = mod_ref[0]
        g = g_ref[...]
        sh1, sc1 = m[:, 0:d], m[:, d:2 * d]
        r = lax.rsqrt(_mean(x * x) + EPS)
        nrm = x * r * g
        h_ref[...] = (nrm * (1.0 + sc1) + sh1).astype(BF16)
        _add_row(ms_ref, (0, slice(0, 1), slice(None)), _rowsum(dh))
        _add_row(ms_ref, (0, slice(1, 2), slice(None)), _rowsum(dh * nrm))
        dn = dh * (1.0 + sc1)
        _add_row(ps_ref, (slice(0, 1), slice(None)), _rowsum(dn * x * r))
        dx_ref[...] = dx1_ref[...] + _rms_bwd(dn, g, x, r)

        if comm:
            @pl.when(i == n - 1)
            def _():
                _finish_all(*rs())

    return pl.pallas_call(
        body, grid=(n,),
        in_specs=[_row_spec(tm, 6 * d), _row_spec(tm, d), _row_spec(tm, d), _mod_spec(tps, 6 * d),
                  _const_spec((1, d)), ANY] + ([ANY] if comm else []),
        out_specs=[_row_spec(tm, d), _row_spec(tm, d), _mstat_spec(tps, d), _const_spec((8, d))]
        + ([ANY] if comm else []),
        out_shape=[_sds((t, d), F32), _sds((t, d), BF16), _sds((nb, 8, d), F32), _sds((8, d), F32)]
        + ([_sds(part.shape, part.dtype)] if comm else []),
        scratch_shapes=[pltpu.VMEM((6 * d, d), BF16), pltpu.SemaphoreType.DMA((N_DEV,))]
        + (_sem_scratch(3, True) if comm else []),
        compiler_params=_params(1), name="bwd_in_rs" if comm else "bwd_in",
    )(dproj, x2d, dx1, mod3, g1, gw, *([part] if comm else []))


def _wgrad(a, b, pack, kb, off, relu2, name):
    t, mo = a.shape
    nn_ = b.shape[1]
    rows = mo // N_DEV
    tk = min(TK_WGRAD, t)
    nk = t // tk
    assert off % rows == 0 and N_DEV % kb == 0

    def body(a_ref, b_ref, pack_hbm, o_ref, acc):
        k = pl.program_id(1)

        @pl.when(k == 0)
        def _():
            acc[...] = jnp.zeros(acc.shape, F32)

        av = a_ref[...]
        if relu2:
            av = jnp.square(jnp.maximum(av, 0.0))
        acc[...] = acc[...] + _tn(av, b_ref[...])

        @pl.when(k == nk - 1)
        def _():
            for j in range(kb):
                o_ref[j] = acc[pl.ds(j * rows, rows), :].astype(BF16)

    return pl.pallas_call(
        body, grid=(N_DEV // kb, nk),
        in_specs=[pl.BlockSpec((tk, kb * rows), lambda i, k: (k, i)), pl.BlockSpec((tk, nn_), lambda i, k: (k, 0)), ANY],
        out_specs=pl.BlockSpec((kb, rows, nn_), lambda i, k: (i, off // rows, 0)),
        out_shape=_sds(pack.shape, BF16),
        scratch_shapes=[pltpu.VMEM((kb * rows, nn_), F32)],
        input_output_aliases={2: 0},
        compiler_params=_params(2), name=name,
    )(a, b, pack)


def _wgrad_conv(a, b, pack, kb, off, relu2, dproj, dzc, proj, cwb, dcw_in, half, seq, name):
    t, mo = a.shape
    nn_ = b.shape[1]
    d = dzc.shape[1]
    rows = mo // N_DEV
    tk = min(TK_WGRAD_CONV, t)
    nk = t // tk
    steps = (N_DEV // kb) * nk
    tc = t // (2 * steps)
    assert off % rows == 0 and tc % HALO == 0 and seq % tc == 0 and 2 * steps * tc == t
    tps = seq // tc
    hb = tc // HALO
    lo = HALO - (CONV_TAPS - 1)

    def tile(i, k):
        return half * steps + i * nk + k

    def body(a_ref, b_ref, pack_hbm, dp_hbm, dzc_ref, dzn_ref, pp_ref, halo_ref, cw_ref, dcwi_ref,
             o_ref, dpo_ref, dcw_ref, acc, zext, dzext, zsh, dz_buf, dcw_acc):
        i = pl.program_id(0)
        k = pl.program_id(1)
        s = i * nk + k
        g = half * steps + s

        @pl.when(k == 0)
        def _():
            acc[...] = jnp.zeros(acc.shape, F32)

        @pl.when(s == 0)
        def _():
            dcw_acc[...] = jnp.zeros(dcw_acc.shape, F32)

        av = a_ref[...]
        if relu2:
            av = jnp.square(jnp.maximum(av, 0.0))
        acc[...] = acc[...] + _tn(av, b_ref[...])

        first = (g % tps) == 0
        last = (g % tps) == tps - 1
        pa = pp_ref[:, 0:d].astype(F32)
        sgp = jax.nn.sigmoid(pp_ref[:, d:2 * d].astype(F32))
        _fill_zext(zext, pa * sgp, halo_ref, first, tc, d)
        dzext[pl.ds(0, tc), :] = dzc_ref[...].astype(F32)
        dzext[pl.ds(tc, HALO), :] = jnp.where(last, 0.0, dzn_ref[...].astype(F32))
        _shift_copies(zext, zsh, 0, tc + HALO - 8)
        _conv_wgrad(dzext, zext, zsh, 0, dcw_acc, lo, tc, d)
        _shift_copies(dzext, zsh, 0, tc + HALO - 8)
        _conv_taps(dzext, zsh, 0, cw_ref, [CONV_TAPS - 1 - j for j in range(CONV_TAPS)], jnp.zeros((1, d), F32),
                   dz_buf, tc, d)
        dz = dz_buf[...]
        dpo_ref[:, 0:d] = (dz * sgp).astype(BF16)
        dpo_ref[:, d:2 * d] = (dz * pa * sgp * (1.0 - sgp)).astype(BF16)

        @pl.when(k == nk - 1)
        def _():
            for j in range(kb):
                o_ref[j] = acc[pl.ds(j * rows, rows), :].astype(BF16)

        @pl.when(s == steps - 1)
        def _():
            for j in range(CONV_TAPS):
                dcw_ref[j:j + 1, :] = dcwi_ref[j:j + 1, :] + _rowsum(dcw_acc[pl.ds(j * 8, 8), :])
            dcw_ref[CONV_TAPS:HALO, :] = jnp.zeros((HALO - CONV_TAPS, d), F32)

    return pl.pallas_call(
        body, grid=(N_DEV // kb, nk),
        in_specs=[pl.BlockSpec((tk, kb * rows), lambda i, k: (k, i)), pl.BlockSpec((tk, nn_), lambda i, k: (k, 0)),
                  ANY, ANY,
                  pl.BlockSpec((tc, d), lambda i, k: (tile(i, k), 0)),
                  pl.BlockSpec((HALO, d), lambda i, k: (jnp.minimum((tile(i, k) + 1) * hb, t // HALO - 1), 0)),
                  pl.BlockSpec((tc, 2 * d), lambda i, k: (tile(i, k), 1)),
                  pl.BlockSpec((HALO, 2 * d), lambda i, k: (jnp.maximum(tile(i, k) * hb - 1, 0), 1)),
                  pl.BlockSpec((8 * HALO, d), lambda i, k: (0, 0)), pl.BlockSpec((HALO, d), lambda i, k: (0, 0))],
        out_specs=[pl.BlockSpec((kb, rows, nn_), lambda i, k: (i, off // rows, 0)),
                   pl.BlockSpec((tc, 2 * d), lambda i, k: (tile(i, k), 1)),
                   pl.BlockSpec((HALO, d), lambda i, k: (0, 0))],
        out_shape=[_sds(pack.shape, BF16), _sds(dproj.shape, dproj.dtype), _sds((HALO, d), F32)],
        scratch_shapes=[pltpu.VMEM((kb * rows, nn_), F32),
                        pltpu.VMEM((tc + HALO, d), F32), pltpu.VMEM((tc + HALO, d), F32),
                        pltpu.VMEM((7, tc + HALO, d), F32),
                        pltpu.VMEM((tc, d), F32), pltpu.VMEM((CONV_TAPS * 8, d), F32)],
        input_output_aliases={2: 0, 3: 1},
        compiler_params=_params(2), name=name,
    )(a, b, pack, dproj, dzc, dzc, proj, proj, cwb, dcw_in)


def _place_rows(pack, rows_blk, off):
    nblk, r, nn_ = rows_blk.shape
    assert off % r == 0

    def body(pack_hbm, s_ref, o_ref):
        o_ref[...] = s_ref[...]

    return pl.pallas_call(
        body, grid=(1,),
        in_specs=[ANY, pl.BlockSpec((nblk, r, nn_), lambda i: (0, 0, 0))],
        out_specs=pl.BlockSpec((nblk, r, nn_), lambda i: (0, off // r, 0)),
        out_shape=_sds(pack.shape, pack.dtype),
        input_output_aliases={0: 0},
        compiler_params=_params(1), name="place_small",
    )(pack, rows_blk)


def _mod_fwd(c_all, w_ada, b_cols):
    nl, d, cols = w_ada.shape
    bsz = c_all.shape[0]

    def body(c_ref, w_ref, b_ref, o_ref):
        cv = c_ref[...]
        ca = cv * jax.nn.sigmoid(cv)
        o_ref[0] = jnp.dot(ca, w_ref[0], preferred_element_type=F32, precision=lax.Precision.HIGHEST) + b_ref[0]

    return pl.pallas_call(
        body, grid=(nl,),
        in_specs=[_const_spec((bsz, d)), pl.BlockSpec((1, d, cols), lambda l: (l, 0, 0)),
                  pl.BlockSpec((1, 1, cols), lambda l: (l, 0, 0))],
        out_specs=pl.BlockSpec((1, bsz, cols), lambda l: (l, 0, 0)),
        out_shape=_sds((nl, bsz, cols), F32),
        compiler_params=_params(1), name="mod_fwd",
    )(c_all, w_ada, b_cols)


def _mod_bwd(c_all, dmod_cols, dmod_all):
    nl, bsz, cols = dmod_cols.shape
    d = c_all.shape[1]
    ncol = dmod_all.shape[2]

    def body(c_ref, dc_ref, da_ref, dw_ref, db_ref):
        cv = c_ref[...]
        ca = cv * jax.nn.sigmoid(cv)
        dw_ref[0] = lax.dot_general(ca, dc_ref[0], (((0,), (0,)), ((), ())), preferred_element_type=F32,
                                    precision=lax.Precision.HIGHEST)
        db_ref[0] = _rowsum(da_ref[0])

    return pl.pallas_call(
        body, grid=(nl,),
        in_specs=[_const_spec((bsz, d)), pl.BlockSpec((1, bsz, cols), lambda l: (l, 0, 0)),
                  pl.BlockSpec((1, bsz, ncol), lambda l: (l, 0, 0))],
        out_specs=[pl.BlockSpec((1, d, cols), lambda l: (l, 0, 0)), pl.BlockSpec((1, 1, ncol), lambda l: (l, 0, 0))],
        out_shape=[_sds((nl, d, cols), F32), _sds((nl, 1, ncol), F32)],
        compiler_params=_params(1), name="mod_bwd",
    )(c_all, dmod_cols, dmod_all)


def _row_tile(rows, cols, nbuf, itemsize=4, budget=24 * 1024 * 1024):
    cap = max(16, budget // (2 * nbuf * cols * itemsize))
    if rows <= cap:
        return rows
    best = None
    for tr in range(16, cap + 1, 16):
        if rows % tr == 0:
            best = tr
    assert best is not None, (rows, cols)
    return best


def _sum_blocks(xs, name):
    nblk, rows, cols = xs.shape
    tr = _row_tile(rows, cols, nblk + 1)

    def body(x_ref, o_ref):
        acc = x_ref[0].astype(F32)
        for j in range(1, nblk):
            acc = acc + x_ref[j].astype(F32)
        o_ref[...] = acc

    return pl.pallas_call(
        body, grid=(rows // tr,),
        in_specs=[pl.BlockSpec((nblk, tr, cols), lambda i: (0, i, 0))],
        out_specs=pl.BlockSpec((tr, cols), lambda i: (i, 0)),
        out_shape=_sds((rows, cols), F32),
        compiler_params=_params(1), name=name,
    )(xs)


def _add_sibling(dp, recv, core):
    nq, _, rows, cols = dp.shape
    tr = _row_tile(rows, cols, 3, itemsize=2)

    def body(c_ref, a_ref, b_ref, o_ref):
        o_ref[...] = (a_ref[...].astype(F32) + b_ref[...].astype(F32)).astype(BF16)

    return pl.pallas_call(
        body,
        grid_spec=pltpu.PrefetchScalarGridSpec(
            num_scalar_prefetch=1, grid=(nq, rows // tr),
            in_specs=[pl.BlockSpec((1, 1, tr, cols), lambda q, i, c: (q, c[0], i, 0)),
                      pl.BlockSpec((1, 1, tr, cols), lambda q, i, c: (q, 0, i, 0))],
            out_specs=pl.BlockSpec((1, 1, tr, cols), lambda q, i, c: (q, 0, i, 0))),
        out_shape=_sds((nq, 1, rows, cols), BF16),
        compiler_params=_params(2), name="add_sibling",
    )(core, dp, recv.reshape(nq, 1, rows, cols)).reshape(nq, rows, cols)


def _adamw(w, g, m, v, name):
    rows, cols = w.shape
    tr = _row_tile(rows, cols, 7)
    c1 = 1.0 - ADAM_B1 ** ADAM_STEP
    c2 = 1.0 - ADAM_B2 ** ADAM_STEP

    def body(w_ref, g_ref, m_ref, v_ref, d_ref, nm_ref, nv_ref):
        gv = g_ref[...]
        nm = ADAM_B1 * m_ref[...] + (1.0 - ADAM_B1) * gv
        nv = ADAM_B2 * v_ref[...] + (1.0 - ADAM_B2) * (gv * gv)
        nm_ref[...] = nm
        nv_ref[...] = nv
        d_ref[...] = -ADAM_LR * ((nm / c1) / (jnp.sqrt(nv / c2) + ADAM_EPS) + ADAM_WD * w_ref[...])

    spec = pl.BlockSpec((tr, cols), lambda i: (i, 0))
    out = _sds((rows, cols), F32)
    return pl.pallas_call(
        body, grid=(rows // tr,), in_specs=[spec] * 4, out_specs=[spec] * 3, out_shape=[out] * 3,
        compiler_params=_params(1), name=name,
    )(w, g, m, v)


def _all_gather(xs, name):
    rows, cols = xs.shape

    def body(x_ref, out_ref, send1, recv1, local_sem, send2, recv2):
        local, first, arrivals = _ag_stage1(x_ref, out_ref, send1, recv1, local_sem)
        _start_all(local, first)
        passed, from_sibling = _ag_stage2(out_ref, out_ref, send2, recv2)
        for arrival, onward in zip(arrivals[1:], passed):
            arrival.wait_recv()
            onward.start()
        arrivals[0].wait_recv()
        _finish_all(local, first + passed, from_sibling)

    return pl.pallas_call(
        body, out_shape=_sds((N_DEV, rows, cols), xs.dtype), in_specs=[ANY], out_specs=ANY,
        scratch_shapes=_sem_scratch(4, True) + _sem_scratch(3, False), name=name,
    )(xs)


def _sibling_exchange(dp):
    nq, _, rows, cols = dp.shape

    def body(x_ref, out_ref, send_sem, recv_sem):
        x, y, c = _position()
        cp = pltpu.make_async_remote_copy(
            src_ref=x_ref.at[pl.ds(0, nq), 1 - c], dst_ref=out_ref, send_sem=send_sem, recv_sem=recv_sem,
            device_id=(x, y, 1 - c), device_id_type=MESH)
        cp.start()
        cp.wait()

    return pl.pallas_call(
        body, out_shape=_sds((nq, rows, cols), dp.dtype), in_specs=[ANY], out_specs=ANY,
        scratch_shapes=[pltpu.SemaphoreType.DMA(()), pltpu.SemaphoreType.DMA(())],
        name="rs_sibling",
    )(dp)


def _chip_all_to_all(xs):
    nq, rows, cols = xs.shape

    def body(x_ref, out_ref, send_sems, recv_sems, local_sem):
        local, sends, recvs = _rs_chip_copies(x_ref, out_ref, send_sems, recv_sems, local_sem)
        _start_all(local, sends)
        _finish_all(local, sends, recvs)

    return pl.pallas_call(
        body, out_shape=_sds((nq, rows, cols), xs.dtype), in_specs=[ANY], out_specs=ANY,
        scratch_shapes=_sem_scratch(3, True), name="rs_chips",
    )(xs)


def _pad_rows(a, rows):
    return jnp.pad(a, ((0, rows - a.shape[0]), (0, 0)))


def kernel(x, c, w_ada, b_ada, norm1_g, w_in, a_ln_g, a_ln_b, a_ws, a_bs, w_pa, b_conv_w, b_conv_b, b_ln_g, b_ln_b, w_pb, w_out, norm2_g, w_ff1, w_ff2, final_g, loss_target, m_w_ada, m_b_ada, m_norm1_g, m_w_in, m_a_ln_g, m_a_ln_b, m_a_ws, m_a_bs, m_w_pa, m_b_conv_w, m_b_conv_b, m_b_ln_g, m_b_ln_b, m_w_pb, m_w_out, m_norm2_g, m_w_ff1, m_w_ff2, m_final_g, v_w_ada, v_b_ada, v_norm1_g, v_w_in, v_a_ln_g, v_a_ln_b, v_a_ws, v_a_bs, v_w_pa, v_b_conv_w, v_b_conv_b, v_b_ln_g, v_b_ln_b, v_w_pb, v_w_out, v_norm2_g, v_w_ff1, v_w_ff2, v_final_g):
    nb, seq, d = x.shape
    nl = w_in.shape[0]
    t = nb * seq
    pk = _Pack(d, 0)
    gk = _Pack(d, SMALL_SLOT)
    assert d % (N_DEV * CHUNK) == 0 and d // HEADS == CHUNK and seq % CHUNK == 0
    tm_big = min(TM_BIG, seq)
    tm_mix = min(TM_MIX, seq)
    ax, ay, ac = _position()
    dev = 4 * ax + 2 * ay + ac
    ncol = 6 * d
    cols = ncol // N_DEV
    cpd = d // N_DEV
    bsz = nb * N_DEV

    cw_rows = nl * HALO
    small = jnp.concatenate([
        c.reshape(nb * d // CHUNK, CHUNK),
        jnp.pad(b_conv_w.reshape(nl, CONV_TAPS, cpd), ((0, 0), (0, HALO - CONV_TAPS), (0, 0))).reshape(cw_rows, cpd),
    ], axis=0)
    c_rows = nb * d // CHUNK
    small_all = _all_gather(small, "ag_small")
    c_all = small_all[:, :c_rows].reshape(bsz, d)
    cw_all = small_all[:, c_rows:].reshape(N_DEV, nl, HALO, cpd).transpose(1, 2, 0, 3).reshape(nl, HALO, d)
    cwb_all = jnp.repeat(cw_all, 8, axis=1)
    b_cols = lax.dynamic_slice_in_dim(b_ada, dev * cols, cols, axis=1).reshape(nl, 1, cols)
    mod_cols = _mod_fwd(c_all, w_ada, b_cols)
    mod_all = _all_gather(mod_cols.reshape(nl * bsz, cols), "ag_mod")
    mod_all = mod_all.reshape(N_DEV, nl, bsz, cols).transpose(1, 2, 0, 3).reshape(nl, bsz, ncol)
    mod_mine = lax.dynamic_slice_in_dim(mod_all, dev * nb, nb, axis=1)

    causal = jnp.tril(jnp.ones((CHUNK, CHUNK), bool))
    wm_all = jnp.where(causal[None, None], a_ws, 0.0)
    wm_bf = wm_all.astype(BF16)
    wmt_bf = jnp.swapaxes(wm_all, 2, 3).astype(BF16)
    bsx_all = jnp.broadcast_to(jnp.swapaxes(a_bs, 1, 2)[:, :, :, None], (nl, CHUNK, HEADS, CHUNK)).reshape(nl, CHUNK, d)

    def vec_rows(l):
        return _pad_rows(jnp.stack([a_ln_g[l], a_ln_b[l], b_conv_b[l], b_ln_g[l], b_ln_b[l]]), 8)

    def weight_block(l):
        return jnp.concatenate([
            w_ff1[l].T, w_ff2[l], w_pa[l], w_pb[l], w_out[l], w_in[l].T], axis=0).astype(BF16)

    xs = x.reshape(t, d)
    saved = []
    gw = _all_gather(weight_block(0), "ag_weights")
    for l in range(nl):
        nxt = weight_block(l + 1) if l + 1 < nl else None
        mod3 = mod_mine[l].reshape(nb, 1, ncol)
        vecs = vec_rows(l)
        proj, zc = _fwd_in(xs, mod3, norm1_g[l].reshape(1, d), cwb_all[l], b_conv_b[l].reshape(1, d), gw, pk, seq,
                           tm_big)
        x1, ya, yb, o1, *gw_next = _fwd_mix(
            proj, zc, xs, mod3, vecs, wm_bf[l], bsx_all[l], gw, pk, seq, tm_big, nxt)
        x2, f, o2, *gw_next = _fwd_ffn(x1, mod3, norm2_g[l].reshape(1, d), gw, pk, seq, tm_big, *gw_next)
        saved.append((xs, x1, proj, ya, yb, o1, zc, f, o2, gw, mod3, vecs))
        xs = x2
        if gw_next:
            gw = gw_next[0]

    dx, loss_blk, dfg = _loss_head(xs, loss_target.reshape(t, d), final_g.reshape(1, d), tm_big)
    loss = lax.psum(loss_blk[0, 0], ("x", "y", "c"))

    core = ac.reshape(1).astype(jnp.int32)
    wg = {k: [None] * nl for k in ("w_in", "w_ff1", "w_ff2", "w_pa", "w_pb", "w_out")}
    small_red = [None] * nl
    dmod_rows = [None] * nl
    per_layer = 8 + 2 * CHUNK + HALO
    assert per_layer <= N_DEV * SMALL_ROWS <= N_DEV * SMALL_SLOT

    def reduced(l, red):
        wg["w_in"][l] = red[gk.off_in:gk.off_in + gk.n_in].T
        wg["w_ff1"][l] = red[gk.off_ff1:gk.off_ff1 + gk.n_ff].T
        wg["w_ff2"][l] = red[gk.off_ff2:gk.off_ff2 + gk.n_ff]
        wg["w_pa"][l] = red[gk.off_pa:gk.off_pa + gk.n_p]
        wg["w_pb"][l] = red[gk.off_pb:gk.off_pb + gk.n_p]
        wg["w_out"][l] = red[gk.off_out:gk.off_out + gk.n_p]
        small_red[l] = red[gk.off_small:gk.off_small + SMALL_ROWS]

    pending = None
    for l in reversed(range(nl)):
        x0, x1, proj, ya, yb, o1, zc, f, o2, gw, mod3, vecs = saved[l]
        dx1, df, do2, h2, ms2, ps2 = _bwd_ffn(dx, x1, f, o2, mod3, norm2_g[l].reshape(1, d), gw, pk, seq, tm_big)
        (dproj, dzc, do1, dya, dyb, mg, aa, ba, ms1, ps1, dws, dbs) = _bwd_mix(
            dx1, proj, ya, yb, o1, zc, mod3, vecs, wm_bf[l], wmt_bf[l], bsx_all[l], gw, pk, seq, tm_mix)
        grads = lax.empty((N_DEV, gk.rows, d), BF16)
        grads, dproj, dcw = _wgrad_conv(df, h2, grads, 2, gk.off_ff1, False, dproj, dzc, proj, cwb_all[l],
                                        jnp.zeros((HALO, d), F32), 0, seq, "wgrad_ff1_conv")
        grads, dproj, dcw = _wgrad_conv(f, do2, grads, 2, gk.off_ff2, True, dproj, dzc, proj, cwb_all[l],
                                        dcw, 1, seq, "wgrad_ff2_conv")
        dx, h, ms0, ps0, *got = _bwd_in(
            dproj, x0, dx1, mod3, norm1_g[l].reshape(1, d), gw, pk, seq, tm_big,
            None if pending is None else pending[1])
        if pending is not None:
            reduced(pending[0], _sum_blocks(got[0], "sum_chips"))
        vec_g = jnp.concatenate([ps0[0:1], ps1[0:5], ps2[0:1], jnp.zeros((1, d), F32)], axis=0)
        small = _pad_rows(jnp.concatenate([vec_g, dws, dbs, dcw], axis=0), N_DEV * SMALL_ROWS)
        small = jnp.pad(small.reshape(N_DEV, SMALL_ROWS, d).astype(BF16), ((0, 0), (0, SMALL_SLOT - SMALL_ROWS), (0, 0)))
        grads = _place_rows(grads, small, gk.off_small)
        grads = _wgrad(aa, dya, grads, N_DEV, gk.off_pa, False, "wgrad_pa")
        grads = _wgrad(ba, dyb, grads, N_DEV, gk.off_pb, False, "wgrad_pb")
        grads = _wgrad(mg, do1, grads, N_DEV, gk.off_out, False, "wgrad_out")
        grads = _wgrad(dproj, h, grads, 1, gk.off_in, False, "wgrad_in")
        dp = grads.reshape(N_CHIP, 2, gk.rows, d)
        pending = (l, _add_sibling(dp, _sibling_exchange(dp), core))
        dmod_rows[l] = jnp.concatenate([ms0[:, 0], ms0[:, 1], ms1[:, 0], ms2[:, 0], ms2[:, 1], ms2[:, 2]], axis=1)
    reduced(pending[0], _sum_blocks(_chip_all_to_all(pending[1]), "sum_chips"))

    small_all = _all_gather(jnp.concatenate(small_red, axis=0), "ag_small_grads")
    lay = small_all.reshape(N_DEV, nl, SMALL_ROWS, d).transpose(1, 0, 2, 3).reshape(nl, N_DEV * SMALL_ROWS, d)
    n_dm = nl * nb * 6
    tail = jnp.concatenate([jnp.stack(dmod_rows).reshape(n_dm, d), dfg], axis=0)
    tail_all = _all_gather(tail, "ag_dmod")
    dmod_all = tail_all[:, :n_dm].reshape(N_DEV, nl, nb, ncol).transpose(1, 0, 2, 3).reshape(nl, bsz, ncol)
    dmod_cols = lax.dynamic_slice_in_dim(dmod_all, dev * cols, cols, axis=2)
    g_w_ada, g_b_ada = _mod_bwd(c_all, dmod_cols, dmod_all)
    g_final = _sum_blocks(tail_all[:, n_dm:], "sum_final_g")[0]

    g_small = {
        "norm1_g": lay[:, 0], "a_ln_g": lay[:, 1], "a_ln_b": lay[:, 2], "b_ln_g": lay[:, 3], "b_ln_b": lay[:, 4],
        "b_conv_b": lay[:, 5], "norm2_g": lay[:, 6],
        "a_ws": lay[:, 8:8 + CHUNK].reshape(nl, CHUNK, HEADS, CHUNK).transpose(0, 2, 1, 3),
        "a_bs": jnp.swapaxes(lay[:, 8 + CHUNK:8 + 2 * CHUNK, ::CHUNK], 1, 2),
        "b_conv_w": lax.dynamic_slice_in_dim(
            lay[:, 8 + 2 * CHUNK:8 + 2 * CHUNK + CONV_TAPS], dev * cpd, cpd, axis=2).reshape(nl, CONV_TAPS, 1, cpd),
        "final_g": g_final,
    }
    grads = dict(g_small)
    grads["w_ada"] = g_w_ada
    grads["b_ada"] = g_b_ada.reshape(nl, ncol)
    for k, v in wg.items():
        grads[k] = jnp.stack(v)

    names = ["w_ada", "b_ada", "norm1_g", "w_in", "a_ln_g", "a_ln_b", "a_ws", "a_bs", "w_pa", "b_conv_w", "b_conv_b",
             "b_ln_g", "b_ln_b", "w_pb", "w_out", "norm2_g", "w_ff1", "w_ff2", "final_g"]
    weights = dict(w_ada=w_ada, b_ada=b_ada, norm1_g=norm1_g, w_in=w_in, a_ln_g=a_ln_g, a_ln_b=a_ln_b, a_ws=a_ws,
                   a_bs=a_bs, w_pa=w_pa, b_conv_w=b_conv_w, b_conv_b=b_conv_b, b_ln_g=b_ln_g, b_ln_b=b_ln_b,
                   w_pb=w_pb, w_out=w_out, norm2_g=norm2_g, w_ff1=w_ff1, w_ff2=w_ff2, final_g=final_g)
    m_in = dict(w_ada=m_w_ada, b_ada=m_b_ada, norm1_g=m_norm1_g, w_in=m_w_in, a_ln_g=m_a_ln_g, a_ln_b=m_a_ln_b,
                a_ws=m_a_ws, a_bs=m_a_bs, w_pa=m_w_pa, b_conv_w=m_b_conv_w, b_conv_b=m_b_conv_b, b_ln_g=m_b_ln_g,
                b_ln_b=m_b_ln_b, w_pb=m_w_pb, w_out=m_w_out, norm2_g=m_norm2_g, w_ff1=m_w_ff1, w_ff2=m_w_ff2,
                final_g=m_final_g)
    v_in = dict(w_ada=v_w_ada, b_ada=v_b_ada, norm1_g=v_norm1_g, w_in=v_w_in, a_ln_g=v_a_ln_g, a_ln_b=v_a_ln_b,
                a_ws=v_a_ws, a_bs=v_a_bs, w_pa=v_w_pa, b_conv_w=v_b_conv_w, b_conv_b=v_b_conv_b, b_ln_g=v_b_ln_g,
                b_ln_b=v_b_ln_b, w_pb=v_w_pb, w_out=v_w_out, norm2_g=v_norm2_g, w_ff1=v_w_ff1, w_ff2=v_w_ff2,
                final_g=v_final_g)

    deltas, new_m, new_v = {}, {}, {}
    for k in names:
        shape = weights[k].shape
        two_d = (-1, shape[-1])
        g2d = grads[k].reshape(shape).reshape(two_d)
        grads[k] = grads[k].reshape(shape)
        dl, nm, nv = _adamw(weights[k].reshape(two_d), g2d, m_in[k].reshape(two_d), v_in[k].reshape(two_d),
                            "adamw_" + k)
        deltas[k], new_m[k], new_v[k] = dl.reshape(shape), nm.reshape(shape), nv.reshape(shape)

    return (loss, dx.reshape(nb, seq, d), *[grads[k] for k in names], *[deltas[k] for k in names],
            *[new_m[k] for k in names], *[new_v[k] for k in names])
```

```python
import functools

import jax
import jax.numpy as jnp
from jax import lax
from jax.experimental import pallas as pl
from jax.experimental.pallas import tpu as pltpu

F32 = jnp.float32
BF16 = jnp.bfloat16
MESH = pl.DeviceIdType.MESH
ANY = pl.BlockSpec(memory_space=pl.ANY)

N_DEV = 8
N_CHIP = 4
EPS = 1e-6
CHUNK = 128
HEADS = 8
CONV_TAPS = 31
HALO = 32
SMALL_ROWS = 40
SMALL_SLOT = 128
CONV_ROWS = 32
CONV_WGRAD_TAPS = 4
TM_BIG = 512
TM_MIX = 256
CONV_SUB = 256
TK_WGRAD = 4096
TK_WGRAD_CONV = 2048
VMEM_LIMIT = 56 * 1024 * 1024

ADAM_LR = 0.001
ADAM_B1 = 0.9
ADAM_B2 = 0.999
ADAM_EPS = 1e-08
ADAM_WD = 0.01
ADAM_STEP = 10


def _sds(shape, dtype):
    return jax.ShapeDtypeStruct(tuple(shape), dtype)


def _params(n_grid, vmem=VMEM_LIMIT):
    return pltpu.CompilerParams(dimension_semantics=("arbitrary",) * n_grid, vmem_limit_bytes=vmem)


def _nn(a, b):
    return jnp.dot(a, b, preferred_element_type=F32)


def _nt(a, b):
    return lax.dot_general(a, b, (((1,), (1,)), ((), ())), preferred_element_type=F32)


def _tn(a, b):
    return lax.dot_general(a, b, (((0,), (0,)), ((), ())), preferred_element_type=F32)


def _rowsum(v):
    return jnp.sum(v, axis=0, keepdims=True)


def _mean(v):
    return jnp.mean(v, axis=-1, keepdims=True)


def _add_row(ref, idx, val):
    ref[idx] = ref[idx] + val


def _ln_stats(v):
    mu = _mean(v)
    xc = v - mu
    rs = lax.rsqrt(_mean(xc * xc) + EPS)
    return xc * rs, rs


def _ln_bwd(dout, g, vhat, rs):
    dvh = dout * g
    return rs * (dvh - _mean(dvh) - vhat * _mean(dvh * vhat))


def _rms_bwd(dn, g, x, r):
    gd = dn * g
    return r * gd - x * (r * r * r) * _mean(x * gd)


class _Pack:
    def __init__(self, d, small_slot):
        self.n_in = 6 * d // N_DEV
        self.n_ff = 4 * d // N_DEV
        self.n_p = d // N_DEV
        self.off_ff1 = 0
        self.off_ff2 = self.off_ff1 + self.n_ff
        self.off_pa = self.off_ff2 + self.n_ff
        self.off_pb = self.off_pa + self.n_p
        self.off_out = self.off_pb + self.n_p
        self.off_small = self.off_out + self.n_p
        self.off_in = self.off_small + small_slot
        self.rows = self.off_in + self.n_in
        if small_slot:
            assert self.off_in % self.n_in == 0 and self.off_ff2 % self.n_ff == 0 and self.off_small % small_slot == 0


def _load_rows(g_hbm, w_vm, sems, sem0, off, rows):
    cps = [
        pltpu.make_async_copy(g_hbm.at[k, pl.ds(off, rows), :], w_vm.at[pl.ds(k * rows, rows), :], sems.at[sem0 + k])
        for k in range(N_DEV)
    ]
    for cp in cps:
        cp.start()
    for cp in cps:
        cp.wait()


def _row_spec(tm, cols, colblk=0):
    return pl.BlockSpec((tm, cols), lambda i: (i, colblk))


def _const_spec(shape):
    nd = len(shape)
    return pl.BlockSpec(tuple(shape), lambda i: (0,) * nd)


def _mod_spec(tps, cols):
    return pl.BlockSpec((1, 1, cols), lambda i: (i // tps, 0, 0))


def _mstat_spec(tps, d):
    return pl.BlockSpec((1, 8, d), lambda i: (i // tps, 0, 0))


def _position():
    return lax.axis_index("x"), lax.axis_index("y"), lax.axis_index("c")


def _other_chips(x, y):
    return [(1 - x, y), (x, 1 - y), (1 - x, 1 - y)]


def _remote(src, dst, send_sems, recv_sems, k, to):
    return pltpu.make_async_remote_copy(src_ref=src, dst_ref=dst, send_sem=send_sems.at[k], recv_sem=recv_sems.at[k],
                                        device_id=to, device_id_type=MESH)


def _slot(ref, p):
    return ref.at[4 * p[0] + 2 * p[1] + p[2]]


def _ag_stage1(x_ref, out_ref, send_sems, recv_sems, local_sem):
    x, y, c = _position()
    me = (x, y, c)
    peers = [(x, y, 1 - c)] + [(*chip, c) for chip in _other_chips(x, y)]
    sends = [_remote(x_ref, _slot(out_ref, me), send_sems, recv_sems, k, p) for k, p in enumerate(peers)]
    recvs = [_remote(x_ref, _slot(out_ref, p), send_sems, recv_sems, k, p) for k, p in enumerate(peers)]
    return pltpu.make_async_copy(x_ref, _slot(out_ref, me), local_sem), sends, recvs


def _ag_stage2(in_ref, out_ref, send_sems, recv_sems):
    x, y, c = _position()
    sibling = (x, y, 1 - c)
    chips = _other_chips(x, y)
    sends = [_remote(_slot(in_ref, (*ch, c)), _slot(out_ref, (*ch, c)), send_sems, recv_sems, j, sibling)
             for j, ch in enumerate(chips)]
    recvs = [_remote(_slot(in_ref, (*ch, c)), _slot(out_ref, (*ch, 1 - c)), send_sems, recv_sems, j, sibling)
             for j, ch in enumerate(chips)]
    return sends, recvs


def _rs_chip_copies(x_ref, out_ref, send_sems, recv_sems, local_sem):
    x, y, c = _position()
    q_me = 2 * x + y
    chips = _other_chips(x, y)
    sends = [_remote(x_ref.at[2 * px + py], out_ref.at[q_me], send_sems, recv_sems, j, (px, py, c))
             for j, (px, py) in enumerate(chips)]
    recvs = [_remote(x_ref.at[q_me], out_ref.at[2 * px + py], send_sems, recv_sems, j, (px, py, c))
             for j, (px, py) in enumerate(chips)]
    return pltpu.make_async_copy(x_ref.at[q_me], out_ref.at[q_me], local_sem), sends, recvs


def _start_all(local, sends):
    if local is not None:
        local.start()
    for cp in sends:
        cp.start()


def _finish_all(local, sends, recvs):
    for cp in recvs:
        cp.wait_recv()
    for cp in sends:
        cp.wait_send()
    if local is not None:
        local.wait()


def _sem_scratch(n, local):
    out = [pltpu.SemaphoreType.DMA((n,)), pltpu.SemaphoreType.DMA((n,))]
    return out + ([pltpu.SemaphoreType.DMA(())] if local else [])


def _fwd_in(x2d, mod3, g1, cwb, cb, gw, pk, seq, tm, nxt=None):
    t, d = x2d.shape
    nc = 6 * d
    tps = seq // tm
    n = t // tm
    lo = HALO - (CONV_TAPS - 1)
    sub = min(CONV_SUB, tm)
    comm = nxt is not None

    def body(*refs):
        x_ref, mod_ref, g_ref, cw_ref, cb_ref, gw_hbm = refs[:6]
        refs = refs[6:]
        if comm:
            nx_ref, refs = refs[0], refs[1:]
        proj_ref, zc_ref = refs[:2]
        refs = refs[2:]
        if comm:
            gwn_ref, refs = refs[0], refs[1:]
        w_vm, sems, zext, zsh, zc_buf, ztail = refs[:6]
        i = pl.program_id(0)

        if comm:
            ag = functools.partial(_ag_stage1, nx_ref, gwn_ref, *refs[6:9])

            @pl.when(i == 0)
            def _():
                local, sends, _ = ag()
                _start_all(local, sends)

        @pl.when(i == 0)
        def _():
            _load_rows(gw_hbm, w_vm, sems, 0, pk.off_in, pk.n_in)
            ztail[...] = jnp.zeros(ztail.shape, F32)

        x = x_ref[...]
        m = mod_ref[0]
        r = lax.rsqrt(_mean(x * x) + EPS)
        h = (x * r * g_ref[...] * (1.0 + m[:, d:2 * d]) + m[:, 0:d]).astype(BF16)

        def chunk(j):
            proj_ref[:, j * 512:(j + 1) * 512] = _nt(h, w_vm[j * 512:(j + 1) * 512, :]).astype(BF16)

        glu = range(2 * d // 512, 4 * d // 512)
        for j in glu:
            chunk(j)
        z = proj_ref[:, 2 * d:3 * d].astype(F32) * jax.nn.sigmoid(proj_ref[:, 3 * d:4 * d].astype(F32))
        zext[pl.ds(0, HALO), :] = jnp.where((i % tps) == 0, 0.0, ztail[...])
        zext[pl.ds(HALO, tm), :] = z
        ztail[...] = zext[pl.ds(tm, HALO), :]
        for s in range(tm // sub):
            _shift_copies(zext, zsh, s * sub, sub + HALO - 8)
            _conv_taps(zext, zsh, s * sub, cw_ref, [lo + k for k in range(CONV_TAPS)], cb_ref[...], zc_buf, sub, d)
        zc_ref[...] = zc_buf[...].astype(BF16)
        for j in range(nc // 512):
            if j not in glu:
                chunk(j)

        if comm:
            @pl.when(i == n - 1)
            def _():
                _finish_all(*ag())

    return pl.pallas_call(
        body, grid=(n,),
        in_specs=[_row_spec(tm, d), _mod_spec(tps, nc), _const_spec((1, d)), _const_spec((8 * HALO, d)),
                  _const_spec((1, d)), ANY] + ([ANY] if comm else []),
        out_specs=[_row_spec(tm, nc), _row_spec(tm, d)] + ([ANY] if comm else []),
        out_shape=[_sds((t, nc), BF16), _sds((t, d), BF16)] + ([_sds((N_DEV,) + nxt.shape, nxt.dtype)] if comm else []),
        scratch_shapes=[pltpu.VMEM((nc, d), BF16), pltpu.SemaphoreType.DMA((N_DEV,)),
                        pltpu.VMEM((tm + HALO, d), F32), pltpu.VMEM((7, sub + HALO, d), F32),
                        pltpu.VMEM((tm, d), F32), pltpu.VMEM((HALO, d), F32)]
        + (_sem_scratch(4, True) if comm else []),
        compiler_params=_params(1), name="fwd_in_ag" if comm else "fwd_in",
    )(x2d, mod3, g1, cwb, cb, gw, *([nxt] if comm else []))


def _shift_copies(src, sh, base, rows):
    for r in range(1, 8):
        sh[r - 1, pl.ds(0, rows), :] = src[pl.ds(base + r, rows), :]


def _window(src, sh, base, offset, start, size):
    r, q = offset % 8, offset // 8
    if r == 0:
        return src[pl.ds(base + start + 8 * q, size), :]
    return sh[r - 1, pl.ds(start + 8 * q, size), :]


def _conv_taps(src, sh, base, cwb_ref, offsets, bias, out_ref, rows, d):
    nsub = CONV_ROWS // 8
    for rb in range(rows // CONV_ROWS):
        accs = [jnp.broadcast_to(bias, (8, d))] * nsub
        for k in range(CONV_TAPS):
            w8 = cwb_ref[pl.ds(8 * k, 8), :]
            accs = [a + w8 * _window(src, sh, base, offsets[k], rb * CONV_ROWS + 8 * j, 8) for j, a in enumerate(accs)]
        for j, a in enumerate(accs):
            out_ref[pl.ds(base + rb * CONV_ROWS + 8 * j, 8), :] = a


def _conv_wgrad(dsrc, zsrc, zsh, base, acc_ref, lo, rows, d):
    for k0 in range(0, CONV_TAPS, CONV_WGRAD_TAPS):
        taps = list(range(k0, min(k0 + CONV_WGRAD_TAPS, CONV_TAPS)))
        accs = [jnp.zeros((8, d), F32)] * len(taps)
        for rb in range(rows // 8):
            dblk = dsrc[pl.ds(base + rb * 8, 8), :]
            accs = [a + dblk * _window(zsrc, zsh, base, lo + k, rb * 8, 8) for a, k in zip(accs, taps)]
        for a, k in zip(accs, taps):
            acc_ref[pl.ds(8 * k, 8), :] = acc_ref[pl.ds(8 * k, 8), :] + a


def _fill_zext(zext, z, halo_ref, first, tm, d):
    hz = halo_ref[:, 0:d].astype(F32) * jax.nn.sigmoid(halo_ref[:, d:2 * d].astype(F32))
    zext[pl.ds(0, HALO), :] = jnp.where(first, 0.0, hz)
    zext[pl.ds(HALO, tm), :] = z


def _fwd_mix(proj, zc, x2d, mod3, vecs, wm, bsx, gw, pk, seq, tm, nxt=None):
    t, d = x2d.shape
    tps = seq // tm
    n = t // tm
    comm = nxt is not None

    def body(*refs):
        proj_ref, zc_ref, x_ref, mod_ref, vec_ref, wm_ref, bs_ref, gw_hbm = refs[:8]
        refs = refs[8:]
        if comm:
            nx_ref, refs = refs[0], refs[1:]
        x1_ref, ya_ref, yb_ref, o1_ref = refs[:4]
        refs = refs[4:]
        if comm:
            gwn_ref, refs = refs[0], refs[1:]
        wpa, wpb, wout, sems, vn_buf, a_buf = refs[:6]
        i = pl.program_id(0)

        if comm:
            ag = functools.partial(_ag_stage1, nx_ref, gwn_ref, *refs[6:9])

            @pl.when(i == 0)
            def _():
                local, sends, _ = ag()
                _start_all(local, sends)

        @pl.when(i == 0)
        def _():
            _load_rows(gw_hbm, wpa, sems, 0, pk.off_pa, pk.n_p)
            _load_rows(gw_hbm, wpb, sems, N_DEV, pk.off_pb, pk.n_p)
            _load_rows(gw_hbm, wout, sems, 2 * N_DEV, pk.off_out, pk.n_p)

        m = mod_ref[0]
        vhat, _ = _ln_stats(proj_ref[:, d:2 * d].astype(F32))
        vn_buf[...] = (vhat * vec_ref[0:1, :] + vec_ref[1:2, :]).astype(BF16)
        for c in range(tm // CHUNK):
            rs_ = slice(c * CHUNK, (c + 1) * CHUNK)
            for h in range(HEADS):
                cs_ = slice(h * CHUNK, (h + 1) * CHUNK)
                s_b = _nn(wm_ref[h], vn_buf[rs_, cs_]) + bs_ref[:, cs_]
                a_buf[rs_, cs_] = (proj_ref[rs_, cs_].astype(F32) * s_b).astype(BF16)
        y_a = _nn(a_buf[...], wpa[...])
        ya_ref[...] = y_a.astype(BF16)
        zhat, _ = _ln_stats(zc_ref[...].astype(F32))
        zn = zhat * vec_ref[3:4, :] + vec_ref[4:5, :]
        b_act = (zn * jax.nn.sigmoid(zn)).astype(BF16)
        y_b = _nn(b_act, wpb[...])
        yb_ref[...] = y_b.astype(BF16)
        merged = (jax.nn.sigmoid(proj_ref[:, 4 * d:5 * d].astype(F32)) * y_a
                  + jax.nn.sigmoid(proj_ref[:, 5 * d:6 * d].astype(F32)) * y_b).astype(BF16)
        o1 = _nn(merged, wout[...])
        o1_ref[...] = o1.astype(BF16)
        x1_ref[...] = x_ref[...] + m[:, 2 * d:3 * d] * o1

        if comm:
            @pl.when(i == n - 1)
            def _():
                _finish_all(*ag())

    act = _sds((t, d), BF16)
    return pl.pallas_call(
        body, grid=(n,),
        in_specs=[_row_spec(tm, 6 * d), _row_spec(tm, d), _row_spec(tm, d), _mod_spec(tps, 6 * d), _const_spec((8, d)),
                  _const_spec((HEADS, CHUNK, CHUNK)), _const_spec((CHUNK, d)), ANY]
        + ([ANY] if comm else []),
        out_specs=[_row_spec(tm, d)] * 4 + ([ANY] if comm else []),
        out_shape=[_sds((t, d), F32), act, act, act]
        + ([_sds((N_DEV,) + nxt.shape, nxt.dtype)] if comm else []),
        scratch_shapes=[pltpu.VMEM((d, d), BF16), pltpu.VMEM((d, d), BF16), pltpu.VMEM((d, d), BF16),
                        pltpu.SemaphoreType.DMA((3 * N_DEV,)),
                        pltpu.VMEM((tm, d), BF16), pltpu.VMEM((tm, d), BF16)]
        + (_sem_scratch(4, True) if comm else []),
        compiler_params=_params(1), name="fwd_mix_ag" if comm else "fwd_mix",
    )(proj, zc, x2d, mod3, vecs, wm, bsx, gw, *([nxt] if comm else []))


def _fwd_ffn(x1, mod3, g2, gw, pk, seq, tm, gw_next=None):
    t, d = x1.shape
    nf = 4 * d
    tps = seq // tm
    n = t // tm
    comm = gw_next is not None

    def body(*refs):
        x_ref, mod_ref, g_ref, gw_hbm = refs[:4]
        refs = refs[4:]
        if comm:
            gwn_in, refs = refs[0], refs[1:]
        x2_ref, f_ref, o2_ref = refs[:3]
        refs = refs[3:]
        if comm:
            gwn_out, refs = refs[0], refs[1:]
        w1, w2, sems = refs[:3]
        i = pl.program_id(0)

        if comm:
            ag = functools.partial(_ag_stage2, gwn_in, gwn_out, *refs[3:5])

            @pl.when(i == 0)
            def _():
                _start_all(None, ag()[0])

        @pl.when(i == 0)
        def _():
            _load_rows(gw_hbm, w1, sems, 0, pk.off_ff1, pk.n_ff)
            _load_rows(gw_hbm, w2, sems, N_DEV, pk.off_ff2, pk.n_ff)

        x = x_ref[...]
        m = mod_ref[0]
        r = lax.rsqrt(_mean(x * x) + EPS)
        h2 = (x * r * g_ref[...] * (1.0 + m[:, 4 * d:5 * d]) + m[:, 3 * d:4 * d]).astype(BF16)
        acc = jnp.zeros(x.shape, F32)
        for j in range(nf // 512):
            js = slice(j * 512, (j + 1) * 512)
            f = _nt(h2, w1[js, :])
            f_ref[:, js] = f.astype(BF16)
            acc = acc + _nn(jnp.square(jnp.maximum(f, 0.0)).astype(BF16), w2[js, :])
        o2_ref[...] = acc.astype(BF16)
        x2_ref[...] = x + m[:, 5 * d:6 * d] * acc

        if comm:
            @pl.when(i == n - 1)
            def _():
                _finish_all(None, *ag())

    return pl.pallas_call(
        body, grid=(n,),
        in_specs=[_row_spec(tm, d), _mod_spec(tps, 6 * d), _const_spec((1, d)), ANY] + ([ANY] if comm else []),
        out_specs=[_row_spec(tm, d), _row_spec(tm, nf), _row_spec(tm, d)] + ([ANY] if comm else []),
        out_shape=[_sds((t, d), F32), _sds((t, nf), BF16), _sds((t, d), BF16)]
        + ([_sds(gw_next.shape, gw_next.dtype)] if comm else []),
        scratch_shapes=[pltpu.VMEM((nf, d), BF16), pltpu.VMEM((nf, d), BF16), pltpu.SemaphoreType.DMA((2 * N_DEV,))]
        + (_sem_scratch(3, False) if comm else []),
        input_output_aliases={4: 3} if comm else {},
        compiler_params=_params(1), name="fwd_ffn_ag" if comm else "fwd_ffn",
    )(x1, mod3, g2, gw, *([gw_next] if comm else []))


def _loss_head(x, tgt, fg, tm):
    t, d = x.shape
    n = t // tm

    def body(x_ref, t_ref, g_ref, dx_ref, loss_ref, dg_ref, lacc):
        i = pl.program_id(0)

        @pl.when(i == 0)
        def _():
            lacc[...] = jnp.zeros(lacc.shape, F32)
            dg_ref[...] = jnp.zeros(dg_ref.shape, F32)

        xv = x_ref[...]
        g = g_ref[...]
        r = lax.rsqrt(_mean(xv * xv) + EPS)
        err = xv * r * g - t_ref[...]
        lacc[...] = lacc[...] + _rowsum(err * err)
        dy = err * (1.0 / d)
        _add_row(dg_ref, (slice(0, 1), slice(None)), _rowsum(dy * xv * r))
        dx_ref[...] = _rms_bwd(dy, g, xv, r)

        @pl.when(i == n - 1)
        def _():
            loss_ref[...] = jnp.broadcast_to(jnp.sum(lacc[...], keepdims=True) * (0.5 / d), loss_ref.shape)

    return pl.pallas_call(
        body, grid=(n,),
        in_specs=[_row_spec(tm, d), _row_spec(tm, d), _const_spec((1, d))],
        out_specs=[_row_spec(tm, d), _const_spec((8, 128)), _const_spec((8, d))],
        out_shape=[_sds((t, d), F32), _sds((8, 128), F32), _sds((8, d), F32)],
        scratch_shapes=[pltpu.VMEM((1, d), F32)],
        compiler_params=_params(1), name="loss_head",
    )(x, tgt, fg)


def _sibling_copy(x_ref, out_ref, send_sem, recv_sem):
    x, y, c = _position()
    return pltpu.make_async_remote_copy(
        src_ref=x_ref.at[pl.ds(0, x_ref.shape[0]), 1 - c], dst_ref=out_ref, send_sem=send_sem, recv_sem=recv_sem,
        device_id=(x, y, 1 - c), device_id_type=MESH)


def _bwd_ffn(dx2, x1, f, o2, mod3, g2, gw, pk, seq, tm, dp=None):
    t, d = x1.shape
    nf = 4 * d
    tps = seq // tm
    nb = t // seq
    steps = t // tm
    comm = dp is not None

    def body(*refs):
        dx2_ref, x_ref, f_ref, o2_ref, mod_ref, g_ref, gw_hbm = refs[:7]
        refs = refs[7:]
        if comm:
            dp_ref, refs = refs[0], refs[1:]
        dx1_ref, df_ref, do2_ref, h2_ref, ms_ref, ps_ref = refs[:6]
        refs = refs[6:]
        if comm:
            got_ref, refs = refs[0], refs[1:]
        w1, w2, sems = refs[:3]
        i = pl.program_id(0)

        if comm:
            swap = functools.partial(_sibling_copy, dp_ref, got_ref, *refs[3:5])

            @pl.when(i == 0)
            def _():
                swap().start()

        @pl.when(i == 0)
        def _():
            _load_rows(gw_hbm, w1, sems, 0, pk.off_ff1, pk.n_ff)
            _load_rows(gw_hbm, w2, sems, N_DEV, pk.off_ff2, pk.n_ff)
            ps_ref[...] = jnp.zeros(ps_ref.shape, F32)

        @pl.when((i % tps) == 0)
        def _():
            ms_ref[...] = jnp.zeros(ms_ref.shape, F32)

        dx2 = dx2_ref[...]
        x = x_ref[...]
        m = mod_ref[0]
        g = g_ref[...]
        sh2, sc2, gt2 = m[:, 3 * d:4 * d], m[:, 4 * d:5 * d], m[:, 5 * d:6 * d]
        _add_row(ms_ref, (0, slice(2, 3), slice(None)), _rowsum(dx2 * o2_ref[...].astype(F32)))
        do2 = (gt2 * dx2).astype(BF16)
        do2_ref[...] = do2
        r = lax.rsqrt(_mean(x * x) + EPS)
        n = x * r * g
        h2_ref[...] = (n * (1.0 + sc2) + sh2).astype(BF16)
        dh = jnp.zeros(x.shape, F32)
        for j in range(nf // 512):
            js = slice(j * 512, (j + 1) * 512)
            dr = _nt(do2, w2[js, :])
            df = (dr * (2.0 * jnp.maximum(f_ref[:, js].astype(F32), 0.0))).astype(BF16)
            df_ref[:, js] = df
            dh = dh + _nn(df, w1[js, :])
        _add_row(ms_ref, (0, slice(0, 1), slice(None)), _rowsum(dh))
        _add_row(ms_ref, (0, slice(1, 2), slice(None)), _rowsum(dh * n))
        dn = dh * (1.0 + sc2)
        _add_row(ps_ref, (slice(0, 1), slice(None)), _rowsum(dn * x * r))
        dx1_ref[...] = dx2 + _rms_bwd(dn, g, x, r)

        if comm:
            @pl.when(i == steps - 1)
            def _():
                swap().wait()

    act = _sds((t, d), BF16)
    return pl.pallas_call(
        body, grid=(steps,),
        in_specs=[_row_spec(tm, d), _row_spec(tm, d), _row_spec(tm, nf), _row_spec(tm, d), _mod_spec(tps, 6 * d),
                  _const_spec((1, d)), ANY] + ([ANY] if comm else []),
        out_specs=[_row_spec(tm, d), _row_spec(tm, nf), _row_spec(tm, d), _row_spec(tm, d), _mstat_spec(tps, d),
                   _const_spec((8, d))] + ([ANY] if comm else []),
        out_shape=[_sds((t, d), F32), _sds((t, nf), BF16), act, act, _sds((nb, 8, d), F32), _sds((8, d), F32)]
        + ([_sds((dp.shape[0],) + dp.shape[2:], dp.dtype)] if comm else []),
        scratch_shapes=[pltpu.VMEM((nf, d), BF16), pltpu.VMEM((nf, d), BF16), pltpu.SemaphoreType.DMA((2 * N_DEV,))]
        + ([pltpu.SemaphoreType.DMA(()), pltpu.SemaphoreType.DMA(())] if comm else []),
        compiler_params=_params(1), name="bwd_ffn_rs" if comm else "bwd_ffn",
    )(dx2, x1, f, o2, mod3, g2, gw, *([dp] if comm else []))


def _bwd_mix(dx1, proj, ya, yb, o1, zc, mod3, vecs, wm, wmt, bsx, gw, pk, seq, tm):
    t, d = dx1.shape
    tps = seq // tm
    nb = t // seq
    n = t // tm

    def body(dx1_ref, proj_ref, ya_ref, yb_ref, o1_ref, zc_ref, mod_ref, vec_ref, wm_ref, wmt_ref, bs_ref, gw_hbm,
             dp_ref, dzc_ref, do1_ref, dya_ref, dyb_ref, mg_ref, aa_ref, ba_ref, ms_ref, ps_ref, dws_ref, dbs_ref,
             wpa, wpb, wout, sems, vn_buf, da_buf, dvn_buf):
        i = pl.program_id(0)

        @pl.when(i == 0)
        def _():
            _load_rows(gw_hbm, wpa, sems, 0, pk.off_pa, pk.n_p)
            _load_rows(gw_hbm, wpb, sems, N_DEV, pk.off_pb, pk.n_p)
            _load_rows(gw_hbm, wout, sems, 2 * N_DEV, pk.off_out, pk.n_p)
            ps_ref[...] = jnp.zeros(ps_ref.shape, F32)
            dws_ref[...] = jnp.zeros(dws_ref.shape, F32)
            dbs_ref[...] = jnp.zeros(dbs_ref.shape, F32)

        @pl.when((i % tps) == 0)
        def _():
            ms_ref[...] = jnp.zeros(ms_ref.shape, F32)

        m = mod_ref[0]
        dx1v = dx1_ref[...]
        _add_row(ms_ref, (0, slice(0, 1), slice(None)), _rowsum(dx1v * o1_ref[...].astype(F32)))
        do1 = (m[:, 2 * d:3 * d] * dx1v).astype(BF16)
        do1_ref[...] = do1
        dmg = _nt(do1, wout[...])
        sa = jax.nn.sigmoid(proj_ref[:, 4 * d:5 * d].astype(F32))
        sb = jax.nn.sigmoid(proj_ref[:, 5 * d:6 * d].astype(F32))
        y_a = ya_ref[...].astype(F32)
        y_b = yb_ref[...].astype(F32)
        dya = (dmg * sa).astype(BF16)
        dyb = (dmg * sb).astype(BF16)
        dya_ref[...] = dya
        dyb_ref[...] = dyb
        dp_ref[:, 4 * d:5 * d] = (dmg * y_a * sa * (1.0 - sa)).astype(BF16)
        dp_ref[:, 5 * d:6 * d] = (dmg * y_b * sb * (1.0 - sb)).astype(BF16)
        mg_ref[...] = (sa * y_a + sb * y_b).astype(BF16)
        da_buf[...] = _nt(dya, wpa[...])
        db = _nt(dyb, wpb[...])
        vhat, rs = _ln_stats(proj_ref[:, d:2 * d].astype(F32))
        alg = vec_ref[0:1, :]
        vn_buf[...] = (vhat * alg + vec_ref[1:2, :]).astype(BF16)
        for c in range(tm // CHUNK):
            rs_ = slice(c * CHUNK, (c + 1) * CHUNK)
            for h in range(HEADS):
                cs_ = slice(h * CHUNK, (h + 1) * CHUNK)
                vn_b = vn_buf[rs_, cs_]
                s_b = _nn(wm_ref[h], vn_b) + bs_ref[:, cs_]
                u_b = proj_ref[rs_, cs_].astype(F32)
                da_b = da_buf[rs_, cs_]
                aa_ref[rs_, cs_] = (u_b * s_b).astype(BF16)
                dp_ref[rs_, cs_] = (da_b * s_b).astype(BF16)
                ds_b = da_b * u_b
                dbs_ref[:, cs_] = dbs_ref[:, cs_] + ds_b
                ds_bf = ds_b.astype(BF16)
                dvn_buf[rs_, cs_] = _nn(wmt_ref[h], ds_bf)
                dws_ref[:, cs_] = dws_ref[:, cs_] + _nt(ds_bf, vn_b)
        dvn = dvn_buf[...]
        _add_row(ps_ref, (slice(0, 1), slice(None)), _rowsum(dvn * vhat))
        _add_row(ps_ref, (slice(1, 2), slice(None)), _rowsum(dvn))
        dp_ref[:, d:2 * d] = _ln_bwd(dvn, alg, vhat, rs).astype(BF16)
        dp_ref[:, 2 * d:4 * d] = jnp.zeros((tm, 2 * d), BF16)
        zhat, rsb = _ln_stats(zc_ref[...].astype(F32))
        blg = vec_ref[3:4, :]
        zn = zhat * blg + vec_ref[4:5, :]
        sg = jax.nn.sigmoid(zn)
        ba_ref[...] = (zn * sg).astype(BF16)
        dzn = db * (sg * (1.0 + zn * (1.0 - sg)))
        _add_row(ps_ref, (slice(2, 3), slice(None)), _rowsum(dzn * zhat))
        _add_row(ps_ref, (slice(3, 4), slice(None)), _rowsum(dzn))
        dzc = _ln_bwd(dzn, blg, zhat, rsb)
        _add_row(ps_ref, (slice(4, 5), slice(None)), _rowsum(dzc))
        dzc_ref[...] = dzc.astype(BF16)

        @pl.when(i == n - 1)
        def _():
            causal = (lax.broadcasted_iota(jnp.int32, (CHUNK, CHUNK), 0)
                      >= lax.broadcasted_iota(jnp.int32, (CHUNK, CHUNK), 1))
            for h in range(HEADS):
                cs_ = slice(h * CHUNK, (h + 1) * CHUNK)
                dws_ref[:, cs_] = jnp.where(causal, dws_ref[:, cs_], 0.0)
                dbs_ref[:, cs_] = jnp.broadcast_to(jnp.sum(dbs_ref[:, cs_], axis=1, keepdims=True), (CHUNK, CHUNK))

    act = _sds((t, d), BF16)
    return pl.pallas_call(
        body, grid=(n,),
        in_specs=[_row_spec(tm, d), _row_spec(tm, 6 * d), _row_spec(tm, d), _row_spec(tm, d), _row_spec(tm, d),
                  _row_spec(tm, d), _mod_spec(tps, 6 * d), _const_spec((8, d)), _const_spec((HEADS, CHUNK, CHUNK)),
                  _const_spec((HEADS, CHUNK, CHUNK)), _const_spec((CHUNK, d)), ANY],
        out_specs=[_row_spec(tm, 6 * d)] + [_row_spec(tm, d)] * 7
        + [_mstat_spec(tps, d), _const_spec((8, d)), _const_spec((CHUNK, d)), _const_spec((CHUNK, d))],
        out_shape=[_sds((t, 6 * d), BF16)] + [act] * 7
        + [_sds((nb, 8, d), F32), _sds((8, d), F32), _sds((CHUNK, d), F32), _sds((CHUNK, d), F32)],
        scratch_shapes=[pltpu.VMEM((d, d), BF16), pltpu.VMEM((d, d), BF16), pltpu.VMEM((d, d), BF16),
                        pltpu.SemaphoreType.DMA((3 * N_DEV,)),
                        pltpu.VMEM((tm, d), BF16), pltpu.VMEM((tm, d), F32), pltpu.VMEM((tm, d), F32)],
        compiler_params=_params(1), name="bwd_mix",
    )(dx1, proj, ya, yb, o1, zc, mod3, vecs, wm, wmt, bsx, gw)


def _bwd_in(dproj, dzc, proj, x2d, dx1, mod3, g1, cw, gw, pk, seq, tm, part=None):
    t, d = x2d.shape
    tps = seq // tm
    nb = t // seq
    n = t // tm
    hb = tm // HALO
    lo = HALO - (CONV_TAPS - 1)
    comm = part is not None

    def body(*refs):
        dpi_ref, dzc_ref, dzn_ref, pp_ref, halo_ref, x_ref, dx1_ref, mod_ref, g_ref, cw_ref, gw_hbm = refs[:11]
        refs = refs[11:]
        if comm:
            part_ref, refs = refs[0], refs[1:]
        dpo_ref, dx_ref, h_ref, ms_ref, ps_ref, dcw_ref = refs[:6]
        refs = refs[6:]
        if comm:
            got_ref, refs = refs[0], refs[1:]
        w_vm, sems, zext, dzext, zsh, dz_buf, dcw_acc = refs[:7]
        i = pl.program_id(0)

        if comm:
            rs = functools.partial(_rs_chip_copies, part_ref, got_ref, *refs[7:10])

            @pl.when(i == 0)
            def _():
                local, sends, _ = rs()
                _start_all(local, sends)

        @pl.when(i == 0)
        def _():
            _load_rows(gw_hbm, w_vm, sems, 0, pk.off_in, pk.n_in)
            ps_ref[...] = jnp.zeros(ps_ref.shape, F32)
            dcw_acc[...] = jnp.zeros(dcw_acc.shape, F32)

        first = (i % tps) == 0
        last = (i % tps) == tps - 1

        @pl.when(first)
        def _():
            ms_ref[...] = jnp.zeros(ms_ref.shape, F32)

        pa = pp_ref[:, 0:d].astype(F32)
        sgp = jax.nn.sigmoid(pp_ref[:, d:2 * d].astype(F32))
        _fill_zext(zext, pa * sgp, halo_ref, first, tm, d)
        dzext[pl.ds(0, tm), :] = dzc_ref[...].astype(F32)
        dzext[pl.ds(tm, HALO), :] = jnp.where(last, 0.0, dzn_ref[...].astype(F32))
        _shift_copies(zext, zsh, 0, tm + HALO - 8)
        _conv_wgrad(dzext, zext, zsh, 0, dcw_acc, lo, tm, d)
        _shift_copies(dzext, zsh, 0, tm + HALO - 8)
        _conv_taps(dzext, zsh, 0, cw_ref, [CONV_TAPS - 1 - k for k in range(CONV_TAPS)], jnp.zeros((1, d), F32),
                   dz_buf, tm, d)
        dz = dz_buf[...]
        dpa = (dz * sgp).astype(BF16)
        dpg = (dz * pa * sgp * (1.0 - sgp)).astype(BF16)
        dpo_ref[:, 0:d] = dpa
        dpo_ref[:, d:2 * d] = dpg
        dh = (_nn(dpi_ref[:, 0:2 * d], w_vm[0:2 * d, :]) + _nn(dpa, w_vm[2 * d:3 * d, :])
              + _nn(dpg, w_vm[3 * d:4 * d, :]) + _nn(dpi_ref[:, 4 * d:6 * d], w_vm[4 * d:6 * d, :]))
        x = x_ref[...]
        m = mod_ref[0]
        g = g_ref[...]
        sh1, sc1 = m[:, 0:d], m[:, d:2 * d]
        r = lax.rsqrt(_mean(x * x) + EPS)
        nrm = x * r * g
        h_ref[...] = (nrm * (1.0 + sc1) + sh1).astype(BF16)
        _add_row(ms_ref, (0, slice(0, 1), slice(None)), _rowsum(dh))
        _add_row(ms_ref, (0, slice(1, 2), slice(None)), _rowsum(dh * nrm))
        dn = dh * (1.0 + sc1)
        _add_row(ps_ref, (slice(0, 1), slice(None)), _rowsum(dn * x * r))
        dx_ref[...] = dx1_ref[...] + _rms_bwd(dn, g, x, r)

        @pl.when(i == n - 1)
        def _():
            for k in range(CONV_TAPS):
                dcw_ref[k:k + 1, :] = _rowsum(dcw_acc[pl.ds(k * 8, 8), :])
            dcw_ref[CONV_TAPS:HALO, :] = jnp.zeros((HALO - CONV_TAPS, d), F32)

        if comm:
            @pl.when(i == n - 1)
            def _():
                _finish_all(*rs())

    halo_prev = pl.BlockSpec((HALO, 2 * d), lambda i: (jnp.maximum(i * hb - 1, 0), 1))
    halo_next = pl.BlockSpec((HALO, d), lambda i: (jnp.minimum((i + 1) * hb, t // HALO - 1), 0))
    return pl.pallas_call(
        body, grid=(n,),
        in_specs=[_row_spec(tm, 6 * d), _row_spec(tm, d), halo_next, _row_spec(tm, 2 * d, 1), halo_prev,
                  _row_spec(tm, d), _row_spec(tm, d), _mod_spec(tps, 6 * d), _const_spec((1, d)),
                  _const_spec((8 * HALO, d)), ANY] + ([ANY] if comm else []),
        out_specs=[_row_spec(tm, 2 * d, 1), _row_spec(tm, d), _row_spec(tm, d), _mstat_spec(tps, d),
                   _const_spec((8, d)), _const_spec((HALO, d))] + ([ANY] if comm else []),
        out_shape=[_sds((t, 6 * d), BF16), _sds((t, d), F32), _sds((t, d), BF16), _sds((nb, 8, d), F32),
                   _sds((8, d), F32), _sds((HALO, d), F32)] + ([_sds(part.shape, part.dtype)] if comm else []),
        scratch_shapes=[pltpu.VMEM((6 * d, d), BF16), pltpu.SemaphoreType.DMA((N_DEV,)),
                        pltpu.VMEM((tm + HALO, d), F32), pltpu.VMEM((tm + HALO, d), F32),
                        pltpu.VMEM((7, tm + HALO, d), F32),
                        pltpu.VMEM((tm, d), F32), pltpu.VMEM((CONV_TAPS * 8, d), F32)]
        + (_sem_scratch(3, True) if comm else []),
        input_output_aliases={0: 0},
        compiler_params=_params(1), name="bwd_in_rs" if comm else "bwd_in",
    )(dproj, dzc, dzc, proj, proj, x2d, dx1, mod3, g1, cw, gw, *([part] if comm else []))


def _wgrad(a, b, pack, kb, off, relu2, name):
    t, mo = a.shape
    nn_ = b.shape[1]
    rows = mo // N_DEV
    tk = min(TK_WGRAD, t)
    nk = t // tk
    assert off % rows == 0 and N_DEV % kb == 0

    def body(a_ref, b_ref, pack_hbm, o_ref, acc):
        k = pl.program_id(1)

        @pl.when(k == 0)
        def _():
            acc[...] = jnp.zeros(acc.shape, F32)

        av = a_ref[...]
        if relu2:
            av = jnp.square(jnp.maximum(av, 0.0))
        acc[...] = acc[...] + _tn(av, b_ref[...])

        @pl.when(k == nk - 1)
        def _():
            for j in range(kb):
                o_ref[j] = acc[pl.ds(j * rows, rows), :].astype(BF16)

    return pl.pallas_call(
        body, grid=(N_DEV // kb, nk),
        in_specs=[pl.BlockSpec((tk, kb * rows), lambda i, k: (k, i)), pl.BlockSpec((tk, nn_), lambda i, k: (k, 0)), ANY],
        out_specs=pl.BlockSpec((kb, rows, nn_), lambda i, k: (i, off // rows, 0)),
        out_shape=_sds(pack.shape, BF16),
        scratch_shapes=[pltpu.VMEM((kb * rows, nn_), F32)],
        input_output_aliases={2: 0},
        compiler_params=_params(2), name=name,
    )(a, b, pack)


def _wgrad_conv(a, b, pack, kb, off, relu2, dproj, dzc, proj, cwb, dcw_in, half, seq, name):
    t, mo = a.shape
    nn_ = b.shape[1]
    d = dzc.shape[1]
    rows = mo // N_DEV
    tk = min(TK_WGRAD_CONV, t)
    nk = t // tk
    steps = (N_DEV // kb) * nk
    tc = t // (2 * steps)
    assert off % rows == 0 and tc % HALO == 0 and seq % tc == 0 and 2 * steps * tc == t
    tps = seq // tc
    hb = tc // HALO
    lo = HALO - (CONV_TAPS - 1)

    def tile(i, k):
        return half * steps + i * nk + k

    def body(a_ref, b_ref, pack_hbm, dp_hbm, dzc_ref, dzn_ref, pp_ref, halo_ref, cw_ref, dcwi_ref,
             o_ref, dpo_ref, dcw_ref, acc, zext, dzext, zsh, dz_buf, dcw_acc):
        i = pl.program_id(0)
        k = pl.program_id(1)
        s = i * nk + k
        g = half * steps + s

        @pl.when(k == 0)
        def _():
            acc[...] = jnp.zeros(acc.shape, F32)

        @pl.when(s == 0)
        def _():
            dcw_acc[...] = jnp.zeros(dcw_acc.shape, F32)

        av = a_ref[...]
        if relu2:
            av = jnp.square(jnp.maximum(av, 0.0))
        acc[...] = acc[...] + _tn(av, b_ref[...])

        first = (g % tps) == 0
        last = (g % tps) == tps - 1
        pa = pp_ref[:, 0:d].astype(F32)
        sgp = jax.nn.sigmoid(pp_ref[:, d:2 * d].astype(F32))
        _fill_zext(zext, pa * sgp, halo_ref, first, tc, d)
        dzext[pl.ds(0, tc), :] = dzc_ref[...].astype(F32)
        dzext[pl.ds(tc, HALO), :] = jnp.where(last, 0.0, dzn_ref[...].astype(F32))
        _shift_copies(zext, zsh, 0, tc + HALO - 8)
        _conv_wgrad(dzext, zext, zsh, 0, dcw_acc, lo, tc, d)
        _shift_copies(dzext, zsh, 0, tc + HALO - 8)
        _conv_taps(dzext, zsh, 0, cw_ref, [CONV_TAPS - 1 - j for j in range(CONV_TAPS)], jnp.zeros((1, d), F32),
                   dz_buf, tc, d)
        dz = dz_buf[...]
        dpo_ref[:, 0:d] = (dz * sgp).astype(BF16)
        dpo_ref[:, d:2 * d] = (dz * pa * sgp * (1.0 - sgp)).astype(BF16)

        @pl.when(k == nk - 1)
        def _():
            for j in range(kb):
                o_ref[j] = acc[pl.ds(j * rows, rows), :].astype(BF16)

        @pl.when(s == steps - 1)
        def _():
            for j in range(CONV_TAPS):
                dcw_ref[j:j + 1, :] = dcwi_ref[j:j + 1, :] + _rowsum(dcw_acc[pl.ds(j * 8, 8), :])
            dcw_ref[CONV_TAPS:HALO, :] = jnp.zeros((HALO - CONV_TAPS, d), F32)

    return pl.pallas_call(
        body, grid=(N_DEV // kb, nk),
        in_specs=[pl.BlockSpec((tk, kb * rows), lambda i, k: (k, i)), pl.BlockSpec((tk, nn_), lambda i, k: (k, 0)),
                  ANY, ANY,
                  pl.BlockSpec((tc, d), lambda i, k: (tile(i, k), 0)),
                  pl.BlockSpec((HALO, d), lambda i, k: (jnp.minimum((tile(i, k) + 1) * hb, t // HALO - 1), 0)),
                  pl.BlockSpec((tc, 2 * d), lambda i, k: (tile(i, k), 1)),
                  pl.BlockSpec((HALO, 2 * d), lambda i, k: (jnp.maximum(tile(i, k) * hb - 1, 0), 1)),
                  pl.BlockSpec((8 * HALO, d), lambda i, k: (0, 0)), pl.BlockSpec((HALO, d), lambda i, k: (0, 0))],
        out_specs=[pl.BlockSpec((kb, rows, nn_), lambda i, k: (i, off // rows, 0)),
                   pl.BlockSpec((tc, 2 * d), lambda i, k: (tile(i, k), 1)),
                   pl.BlockSpec((HALO, d), lambda i, k: (0, 0))],
        out_shape=[_sds(pack.shape, BF16), _sds(dproj.shape, dproj.dtype), _sds((HALO, d), F32)],
        scratch_shapes=[pltpu.VMEM((kb * rows, nn_), F32),
                        pltpu.VMEM((tc + HALO, d), F32), pltpu.VMEM((tc + HALO, d), F32),
                        pltpu.VMEM((7, tc + HALO, d), F32),
                        pltpu.VMEM((tc, d), F32), pltpu.VMEM((CONV_TAPS * 8, d), F32)],
        input_output_aliases={2: 0, 3: 1},
        compiler_params=_params(2), name=name,
    )(a, b, pack, dproj, dzc, dzc, proj, proj, cwb, dcw_in)


def _place_rows(pack, rows_blk, off):
    nblk, r, nn_ = rows_blk.shape
    assert off % r == 0

    def body(pack_hbm, s_ref, o_ref):
        o_ref[...] = s_ref[...]

    return pl.pallas_call(
        body, grid=(1,),
        in_specs=[ANY, pl.BlockSpec((nblk, r, nn_), lambda i: (0, 0, 0))],
        out_specs=pl.BlockSpec((nblk, r, nn_), lambda i: (0, off // r, 0)),
        out_shape=_sds(pack.shape, pack.dtype),
        input_output_aliases={0: 0},
        compiler_params=_params(1), name="place_small",
    )(pack, rows_blk)


def _mod_fwd(c_all, w_ada, b_cols):
    nl, d, cols = w_ada.shape
    bsz = c_all.shape[0]

    def body(c_ref, w_ref, b_ref, o_ref):
        cv = c_ref[...]
        ca = cv * jax.nn.sigmoid(cv)
        o_ref[0] = jnp.dot(ca, w_ref[0], preferred_element_type=F32, precision=lax.Precision.HIGHEST) + b_ref[0]

    return pl.pallas_call(
        body, grid=(nl,),
        in_specs=[_const_spec((bsz, d)), pl.BlockSpec((1, d, cols), lambda l: (l, 0, 0)),
                  pl.BlockSpec((1, 1, cols), lambda l: (l, 0, 0))],
        out_specs=pl.BlockSpec((1, bsz, cols), lambda l: (l, 0, 0)),
        out_shape=_sds((nl, bsz, cols), F32),
        compiler_params=_params(1), name="mod_fwd",
    )(c_all, w_ada, b_cols)


def _mod_bwd(c_all, dmod_cols, dmod_all):
    nl, bsz, cols = dmod_cols.shape
    d = c_all.shape[1]
    ncol = dmod_all.shape[2]

    def body(c_ref, dc_ref, da_ref, dw_ref, db_ref):
        cv = c_ref[...]
        ca = cv * jax.nn.sigmoid(cv)
        dw_ref[0] = lax.dot_general(ca, dc_ref[0], (((0,), (0,)), ((), ())), preferred_element_type=F32,
                                    precision=lax.Precision.HIGHEST)
        db_ref[0] = _rowsum(da_ref[0])

    return pl.pallas_call(
        body, grid=(nl,),
        in_specs=[_const_spec((bsz, d)), pl.BlockSpec((1, bsz, cols), lambda l: (l, 0, 0)),
                  pl.BlockSpec((1, bsz, ncol), lambda l: (l, 0, 0))],
        out_specs=[pl.BlockSpec((1, d, cols), lambda l: (l, 0, 0)), pl.BlockSpec((1, 1, ncol), lambda l: (l, 0, 0))],
        out_shape=[_sds((nl, d, cols), F32), _sds((nl, 1, ncol), F32)],
        compiler_params=_params(1), name="mod_bwd",
    )(c_all, dmod_cols, dmod_all)


def _row_tile(rows, cols, nbuf, itemsize=4, budget=24 * 1024 * 1024):
    cap = max(16, budget // (2 * nbuf * cols * itemsize))
    if rows <= cap:
        return rows
    best = None
    for tr in range(16, cap + 1, 16):
        if rows % tr == 0:
            best = tr
    assert best is not None, (rows, cols)
    return best


def _sum_blocks(xs, name):
    nblk, rows, cols = xs.shape
    tr = _row_tile(rows, cols, nblk + 1)

    def body(x_ref, o_ref):
        acc = x_ref[0].astype(F32)
        for j in range(1, nblk):
            acc = acc + x_ref[j].astype(F32)
        o_ref[...] = acc

    return pl.pallas_call(
        body, grid=(rows // tr,),
        in_specs=[pl.BlockSpec((nblk, tr, cols), lambda i: (0, i, 0))],
        out_specs=pl.BlockSpec((tr, cols), lambda i: (i, 0)),
        out_shape=_sds((rows, cols), F32),
        compiler_params=_params(1), name=name,
    )(xs)


def _add_sibling(dp, recv, core):
    nq, _, rows, cols = dp.shape
    tr = _row_tile(rows, cols, 3, itemsize=2)

    def body(c_ref, a_ref, b_ref, o_ref):
        o_ref[...] = (a_ref[...].astype(F32) + b_ref[...].astype(F32)).astype(BF16)

    return pl.pallas_call(
        body,
        grid_spec=pltpu.PrefetchScalarGridSpec(
            num_scalar_prefetch=1, grid=(nq, rows // tr),
            in_specs=[pl.BlockSpec((1, 1, tr, cols), lambda q, i, c: (q, c[0], i, 0)),
                      pl.BlockSpec((1, 1, tr, cols), lambda q, i, c: (q, 0, i, 0))],
            out_specs=pl.BlockSpec((1, 1, tr, cols), lambda q, i, c: (q, 0, i, 0))),
        out_shape=_sds((nq, 1, rows, cols), BF16),
        compiler_params=_params(2), name="add_sibling",
    )(core, dp, recv.reshape(nq, 1, rows, cols)).reshape(nq, rows, cols)


def _adamw(w, g, m, v, name):
    rows, cols = w.shape
    tr = _row_tile(rows, cols, 7)
    c1 = 1.0 - ADAM_B1 ** ADAM_STEP
    c2 = 1.0 - ADAM_B2 ** ADAM_STEP

    def body(w_ref, g_ref, m_ref, v_ref, d_ref, nm_ref, nv_ref):
        gv = g_ref[...]
        nm = ADAM_B1 * m_ref[...] + (1.0 - ADAM_B1) * gv
        nv = ADAM_B2 * v_ref[...] + (1.0 - ADAM_B2) * (gv * gv)
        nm_ref[...] = nm
        nv_ref[...] = nv
        d_ref[...] = -ADAM_LR * ((nm / c1) / (jnp.sqrt(nv / c2) + ADAM_EPS) + ADAM_WD * w_ref[...])

    spec = pl.BlockSpec((tr, cols), lambda i: (i, 0))
    out = _sds((rows, cols), F32)
    return pl.pallas_call(
        body, grid=(rows // tr,), in_specs=[spec] * 4, out_specs=[spec] * 3, out_shape=[out] * 3,
        compiler_params=_params(1), name=name,
    )(w, g, m, v)


def _all_gather(xs, name):
    rows, cols = xs.shape

    def body(x_ref, out_ref, send1, recv1, local_sem, send2, recv2):
        local, first, arrivals = _ag_stage1(x_ref, out_ref, send1, recv1, local_sem)
        _start_all(local, first)
        passed, from_sibling = _ag_stage2(out_ref, out_ref, send2, recv2)
        for arrival, onward in zip(arrivals[1:], passed):
            arrival.wait_recv()
            onward.start()
        arrivals[0].wait_recv()
        _finish_all(local, first + passed, from_sibling)

    return pl.pallas_call(
        body, out_shape=_sds((N_DEV, rows, cols), xs.dtype), in_specs=[ANY], out_specs=ANY,
        scratch_shapes=_sem_scratch(4, True) + _sem_scratch(3, False), name=name,
    )(xs)


def _sibling_exchange(dp):
    nq, _, rows, cols = dp.shape

    def body(x_ref, out_ref, send_sem, recv_sem):
        cp = _sibling_copy(x_ref, out_ref, send_sem, recv_sem)
        cp.start()
        cp.wait()

    return pl.pallas_call(
        body, out_shape=_sds((nq, rows, cols), dp.dtype), in_specs=[ANY], out_specs=ANY,
        scratch_shapes=[pltpu.SemaphoreType.DMA(()), pltpu.SemaphoreType.DMA(())],
        name="rs_sibling",
    )(dp)


def _chip_all_to_all(xs):
    nq, rows, cols = xs.shape

    def body(x_ref, out_ref, send_sems, recv_sems, local_sem):
        local, sends, recvs = _rs_chip_copies(x_ref, out_ref, send_sems, recv_sems, local_sem)
        _start_all(local, sends)
        _finish_all(local, sends, recvs)

    return pl.pallas_call(
        body, out_shape=_sds((nq, rows, cols), xs.dtype), in_specs=[ANY], out_specs=ANY,
        scratch_shapes=_sem_scratch(3, True), name="rs_chips",
    )(xs)


def _pad_rows(a, rows):
    return jnp.pad(a, ((0, rows - a.shape[0]), (0, 0)))


def kernel(x, c, w_ada, b_ada, norm1_g, w_in, a_ln_g, a_ln_b, a_ws, a_bs, w_pa, b_conv_w, b_conv_b, b_ln_g, b_ln_b, w_pb, w_out, norm2_g, w_ff1, w_ff2, final_g, loss_target, m_w_ada, m_b_ada, m_norm1_g, m_w_in, m_a_ln_g, m_a_ln_b, m_a_ws, m_a_bs, m_w_pa, m_b_conv_w, m_b_conv_b, m_b_ln_g, m_b_ln_b, m_w_pb, m_w_out, m_norm2_g, m_w_ff1, m_w_ff2, m_final_g, v_w_ada, v_b_ada, v_norm1_g, v_w_in, v_a_ln_g, v_a_ln_b, v_a_ws, v_a_bs, v_w_pa, v_b_conv_w, v_b_conv_b, v_b_ln_g, v_b_ln_b, v_w_pb, v_w_out, v_norm2_g, v_w_ff1, v_w_ff2, v_final_g):
    nb, seq, d = x.shape
    nl = w_in.shape[0]
    t = nb * seq
    pk = _Pack(d, 0)
    gk = _Pack(d, SMALL_SLOT)
    assert d % (N_DEV * CHUNK) == 0 and d // HEADS == CHUNK and seq % CHUNK == 0
    tm_big = min(TM_BIG, seq)
    tm_mix = min(TM_MIX, seq)
    ax, ay, ac = _position()
    dev = 4 * ax + 2 * ay + ac
    ncol = 6 * d
    cols = ncol // N_DEV
    cpd = d // N_DEV
    bsz = nb * N_DEV

    cw_rows = nl * HALO
    small = jnp.concatenate([
        c.reshape(nb * d // CHUNK, CHUNK),
        jnp.pad(b_conv_w.reshape(nl, CONV_TAPS, cpd), ((0, 0), (0, HALO - CONV_TAPS), (0, 0))).reshape(cw_rows, cpd),
    ], axis=0)
    c_rows = nb * d // CHUNK
    small_all = _all_gather(small, "ag_small")
    c_all = small_all[:, :c_rows].reshape(bsz, d)
    cw_all = small_all[:, c_rows:].reshape(N_DEV, nl, HALO, cpd).transpose(1, 2, 0, 3).reshape(nl, HALO, d)
    cwb_all = jnp.repeat(cw_all, 8, axis=1)
    b_cols = lax.dynamic_slice_in_dim(b_ada, dev * cols, cols, axis=1).reshape(nl, 1, cols)
    mod_cols = _mod_fwd(c_all, w_ada, b_cols)
    mod_all = _all_gather(mod_cols.reshape(nl * bsz, cols), "ag_mod")
    mod_all = mod_all.reshape(N_DEV, nl, bsz, cols).transpose(1, 2, 0, 3).reshape(nl, bsz, ncol)
    mod_mine = lax.dynamic_slice_in_dim(mod_all, dev * nb, nb, axis=1)

    causal = jnp.tril(jnp.ones((CHUNK, CHUNK), bool))
    wm_all = jnp.where(causal[None, None], a_ws, 0.0)
    wm_bf = wm_all.astype(BF16)
    wmt_bf = jnp.swapaxes(wm_all, 2, 3).astype(BF16)
    bsx_all = jnp.broadcast_to(jnp.swapaxes(a_bs, 1, 2)[:, :, :, None], (nl, CHUNK, HEADS, CHUNK)).reshape(nl, CHUNK, d)

    def vec_rows(l):
        return _pad_rows(jnp.stack([a_ln_g[l], a_ln_b[l], b_conv_b[l], b_ln_g[l], b_ln_b[l]]), 8)

    def weight_block(l):
        return jnp.concatenate([
            w_ff1[l].T, w_ff2[l], w_pa[l], w_pb[l], w_out[l], w_in[l].T], axis=0).astype(BF16)

    xs = x.reshape(t, d)
    saved = []
    gw = _all_gather(weight_block(0), "ag_weights")
    for l in range(nl):
        nxt = weight_block(l + 1) if l + 1 < nl else None
        mod3 = mod_mine[l].reshape(nb, 1, ncol)
        vecs = vec_rows(l)
        proj, zc, *gw_next = _fwd_in(xs, mod3, norm1_g[l].reshape(1, d), cwb_all[l], b_conv_b[l].reshape(1, d), gw, pk,
                                     seq, tm_big, nxt)
        x1, ya, yb, o1 = _fwd_mix(proj, zc, xs, mod3, vecs, wm_bf[l], bsx_all[l], gw, pk, seq, tm_big)
        x2, f, o2, *gw_next = _fwd_ffn(x1, mod3, norm2_g[l].reshape(1, d), gw, pk, seq, tm_big, *gw_next)
        saved.append((xs, x1, proj, ya, yb, o1, zc, f, o2, gw, mod3, vecs))
        xs = x2
        if gw_next:
            gw = gw_next[0]

    dx, loss_blk, dfg = _loss_head(xs, loss_target.reshape(t, d), final_g.reshape(1, d), tm_big)
    loss = lax.psum(loss_blk[0, 0], ("x", "y", "c"))

    core = ac.reshape(1).astype(jnp.int32)
    wg = {k: [None] * nl for k in ("w_in", "w_ff1", "w_ff2", "w_pa", "w_pb", "w_out")}
    small_red = [None] * nl
    dmod_rows = [None] * nl
    per_layer = 8 + 2 * CHUNK + HALO
    assert per_layer <= N_DEV * SMALL_ROWS <= N_DEV * SMALL_SLOT

    def reduced(l, red):
        wg["w_in"][l] = red[gk.off_in:gk.off_in + gk.n_in].T
        wg["w_ff1"][l] = red[gk.off_ff1:gk.off_ff1 + gk.n_ff].T
        wg["w_ff2"][l] = red[gk.off_ff2:gk.off_ff2 + gk.n_ff]
        wg["w_pa"][l] = red[gk.off_pa:gk.off_pa + gk.n_p]
        wg["w_pb"][l] = red[gk.off_pb:gk.off_pb + gk.n_p]
        wg["w_out"][l] = red[gk.off_out:gk.off_out + gk.n_p]
        small_red[l] = red[gk.off_small:gk.off_small + SMALL_ROWS]

    waiting = None
    pending = None
    for l in reversed(range(nl)):
        x0, x1, proj, ya, yb, o1, zc, f, o2, gw, mod3, vecs = saved[l]
        dx1, df, do2, h2, ms2, ps2, *got = _bwd_ffn(dx, x1, f, o2, mod3, norm2_g[l].reshape(1, d), gw, pk, seq, tm_big,
                                                    None if waiting is None else waiting[1])
        if waiting is not None:
            pending = (waiting[0], _add_sibling(waiting[1], got[0], core))
        (dproj, dzc, do1, dya, dyb, mg, aa, ba, ms1, ps1, dws, dbs) = _bwd_mix(
            dx1, proj, ya, yb, o1, zc, mod3, vecs, wm_bf[l], wmt_bf[l], bsx_all[l], gw, pk, seq, tm_mix)
        grads = lax.empty((N_DEV, gk.rows, d), BF16)
        grads = _wgrad(df, h2, grads, 2, gk.off_ff1, False, "wgrad_ff1")
        grads = _wgrad(f, do2, grads, 2, gk.off_ff2, True, "wgrad_ff2")
        dproj, dx, h, ms0, ps0, dcw, *got = _bwd_in(
            dproj, dzc, proj, x0, dx1, mod3, norm1_g[l].reshape(1, d), cwb_all[l], gw, pk, seq, tm_mix,
            None if pending is None else pending[1])
        if pending is not None:
            reduced(pending[0], _sum_blocks(got[0], "sum_chips"))
        vec_g = jnp.concatenate([ps0[0:1], ps1[0:5], ps2[0:1], jnp.zeros((1, d), F32)], axis=0)
        small = _pad_rows(jnp.concatenate([vec_g, dws, dbs, dcw], axis=0), N_DEV * SMALL_ROWS)
        small = jnp.pad(small.reshape(N_DEV, SMALL_ROWS, d).astype(BF16), ((0, 0), (0, SMALL_SLOT - SMALL_ROWS), (0, 0)))
        grads = _place_rows(grads, small, gk.off_small)
        grads = _wgrad(aa, dya, grads, N_DEV, gk.off_pa, False, "wgrad_pa")
        grads = _wgrad(ba, dyb, grads, N_DEV, gk.off_pb, False, "wgrad_pb")
        grads = _wgrad(mg, do1, grads, N_DEV, gk.off_out, False, "wgrad_out")
        grads = _wgrad(dproj, h, grads, 1, gk.off_in, False, "wgrad_in")
        waiting = (l, grads.reshape(N_CHIP, 2, gk.rows, d))
        pending = None
        dmod_rows[l] = jnp.concatenate([ms0[:, 0], ms0[:, 1], ms1[:, 0], ms2[:, 0], ms2[:, 1], ms2[:, 2]], axis=1)
    part = _add_sibling(waiting[1], _sibling_exchange(waiting[1]), core)
    reduced(waiting[0], _sum_blocks(_chip_all_to_all(part), "sum_chips"))

    small_all = _all_gather(jnp.concatenate(small_red, axis=0), "ag_small_grads")
    lay = small_all.reshape(N_DEV, nl, SMALL_ROWS, d).transpose(1, 0, 2, 3).reshape(nl, N_DEV * SMALL_ROWS, d)
    n_dm = nl * nb * 6
    tail = jnp.concatenate([jnp.stack(dmod_rows).reshape(n_dm, d), dfg], axis=0)
    tail_all = _all_gather(tail, "ag_dmod")
    dmod_all = tail_all[:, :n_dm].reshape(N_DEV, nl, nb, ncol).transpose(1, 0, 2, 3).reshape(nl, bsz, ncol)
    dmod_cols = lax.dynamic_slice_in_dim(dmod_all, dev * cols, cols, axis=2)
    g_w_ada, g_b_ada = _mod_bwd(c_all, dmod_cols, dmod_all)
    g_final = _sum_blocks(tail_all[:, n_dm:], "sum_final_g")[0]

    g_small = {
        "norm1_g": lay[:, 0], "a_ln_g": lay[:, 1], "a_ln_b": lay[:, 2], "b_ln_g": lay[:, 3], "b_ln_b": lay[:, 4],
        "b_conv_b": lay[:, 5], "norm2_g": lay[:, 6],
        "a_ws": lay[:, 8:8 + CHUNK].reshape(nl, CHUNK, HEADS, CHUNK).transpose(0, 2, 1, 3),
        "a_bs": jnp.swapaxes(lay[:, 8 + CHUNK:8 + 2 * CHUNK, ::CHUNK], 1, 2),
        "b_conv_w": lax.dynamic_slice_in_dim(
            lay[:, 8 + 2 * CHUNK:8 + 2 * CHUNK + CONV_TAPS], dev * cpd, cpd, axis=2).reshape(nl, CONV_TAPS, 1, cpd),
        "final_g": g_final,
    }
    grads = dict(g_small)
    grads["w_ada"] = g_w_ada
    grads["b_ada"] = g_b_ada.reshape(nl, ncol)
    for k, v in wg.items():
        grads[k] = jnp.stack(v)

    names = ["w_ada", "b_ada", "norm1_g", "w_in", "a_ln_g", "a_ln_b", "a_ws", "a_bs", "w_pa", "b_conv_w", "b_conv_b",
             "b_ln_g", "b_ln_b", "w_pb", "w_out", "norm2_g", "w_ff1", "w_ff2", "final_g"]
    weights = dict(w_ada=w_ada, b_ada=b_ada, norm1_g=norm1_g, w_in=w_in, a_ln_g=a_ln_g, a_ln_b=a_ln_b, a_ws=a_ws,
                   a_bs=a_bs, w_pa=w_pa, b_conv_w=b_conv_w, b_conv_b=b_conv_b, b_ln_g=b_ln_g, b_ln_b=b_ln_b,
                   w_pb=w_pb, w_out=w_out, norm2_g=norm2_g, w_ff1=w_ff1, w_ff2=w_ff2, final_g=final_g)
    m_in = dict(w_ada=m_w_ada, b_ada=m_b_ada, norm1_g=m_norm1_g, w_in=m_w_in, a_ln_g=m_a_ln_g, a_ln_b=m_a_ln_b,
                a_ws=m_a_ws, a_bs=m_a_bs, w_pa=m_w_pa, b_conv_w=m_b_conv_w, b_conv_b=m_b_conv_b, b_ln_g=m_b_ln_g,
                b_ln_b=m_b_ln_b, w_pb=m_w_pb, w_out=m_w_out, norm2_g=m_norm2_g, w_ff1=m_w_ff1, w_ff2=m_w_ff2,
                final_g=m_final_g)
    v_in = dict(w_ada=v_w_ada, b_ada=v_b_ada, norm1_g=v_norm1_g, w_in=v_w_in, a_ln_g=v_a_ln_g, a_ln_b=v_a_ln_b,
                a_ws=v_a_ws, a_bs=v_a_bs, w_pa=v_w_pa, b_conv_w=v_b_conv_w, b_conv_b=v_b_conv_b, b_ln_g=v_b_ln_g,
                b_ln_b=v_b_ln_b, w_pb=v_w_pb, w_out=v_w_out, norm2_g=v_norm2_g, w_ff1=v_w_ff1, w_ff2=v_w_ff2,
                final_g=v_final_g)

    deltas, new_m, new_v = {}, {}, {}
    for k in names:
        shape = weights[k].shape
        two_d = (-1, shape[-1])
        g2d = grads[k].reshape(shape).reshape(two_d)
        grads[k] = grads[k].reshape(shape)
        dl, nm, nv = _adamw(weights[k].reshape(two_d), g2d, m_in[k].reshape(two_d), v_in[k].reshape(two_d),
                            "adamw_" + k)
        deltas[k], new_m[k], new_v[k] = dl.reshape(shape), nm.reshape(shape), nv.reshape(shape)

    return (loss, dx.reshape(nb, seq, d), *[grads[k] for k in names], *[deltas[k] for k in names],
            *[new_m[k] for k in names], *[new_v[k] for k in names])
```

```python
import functools

import jax
import jax.numpy as jnp
from jax import lax
from jax.experimental import pallas as pl
from jax.experimental.pallas import tpu as pltpu

F32 = jnp.float32
BF16 = jnp.bfloat16
MESH = pl.DeviceIdType.MESH
ANY = pl.BlockSpec(memory_space=pl.ANY)

N_DEV = 8
N_CHIP = 4
EPS = 1e-6
CHUNK = 128
HEADS = 8
CONV_TAPS = 31
HALO = 32
SMALL_ROWS = 40
SMALL_SLOT = 128
CONV_ROWS = 32
CONV_WGRAD_TAPS = 4
TM_BIG = 512
TM_MIX = 256
CONV_SUB = 256
TK_WGRAD = 4096
TK_WGRAD_ONE = 1024
VMEM_LIMIT = 56 * 1024 * 1024

ADAM_LR = 0.001
ADAM_B1 = 0.9
ADAM_B2 = 0.999
ADAM_EPS = 1e-08
ADAM_WD = 0.01
ADAM_STEP = 10


def _sds(shape, dtype):
    return jax.ShapeDtypeStruct(tuple(shape), dtype)


def _params(n_grid, vmem=VMEM_LIMIT):
    return pltpu.CompilerParams(dimension_semantics=("arbitrary",) * n_grid, vmem_limit_bytes=vmem)


def _nn(a, b):
    return jnp.dot(a, b, preferred_element_type=F32)


def _nt(a, b):
    return lax.dot_general(a, b, (((1,), (1,)), ((), ())), preferred_element_type=F32)


def _tn(a, b):
    return lax.dot_general(a, b, (((0,), (0,)), ((), ())), preferred_element_type=F32)


def _rowsum(v):
    return jnp.sum(v, axis=0, keepdims=True)


def _mean(v):
    return jnp.mean(v, axis=-1, keepdims=True)


def _add_row(ref, idx, val):
    ref[idx] = ref[idx] + val


def _ln_stats(v):
    mu = _mean(v)
    xc = v - mu
    rs = lax.rsqrt(_mean(xc * xc) + EPS)
    return xc * rs, rs


def _ln_bwd(dout, g, vhat, rs):
    dvh = dout * g
    return rs * (dvh - _mean(dvh) - vhat * _mean(dvh * vhat))


def _rms_bwd(dn, g, x, r):
    gd = dn * g
    return r * gd - x * (r * r * r) * _mean(x * gd)


class _Pack:
    def __init__(self, d, small_slot):
        self.n_in = 6 * d // N_DEV
        self.n_ff = 4 * d // N_DEV
        self.n_p = d // N_DEV
        self.off_ff1 = 0
        self.off_ff2 = self.off_ff1 + self.n_ff
        self.off_pa = self.off_ff2 + self.n_ff
        self.off_pb = self.off_pa + self.n_p
        self.off_out = self.off_pb + self.n_p
        self.off_small = self.off_out + self.n_p
        self.off_in = self.off_small + small_slot
        self.rows = self.off_in + self.n_in
        if small_slot:
            assert self.off_in % self.n_in == 0 and self.off_ff2 % self.n_ff == 0 and self.off_small % small_slot == 0


def _load_rows(g_hbm, w_vm, sems, sem0, off, rows):
    cps = [
        pltpu.make_async_copy(g_hbm.at[k, pl.ds(off, rows), :], w_vm.at[pl.ds(k * rows, rows), :], sems.at[sem0 + k])
        for k in range(N_DEV)
    ]
    for cp in cps:
        cp.start()
    for cp in cps:
        cp.wait()


def _row_spec(tm, cols, colblk=0):
    return pl.BlockSpec((tm, cols), lambda i: (i, colblk))


def _const_spec(shape):
    nd = len(shape)
    return pl.BlockSpec(tuple(shape), lambda i: (0,) * nd)


def _mod_spec(tps, cols):
    return pl.BlockSpec((1, 1, cols), lambda i: (i // tps, 0, 0))


def _mstat_spec(tps, d):
    return pl.BlockSpec((1, 8, d), lambda i: (i // tps, 0, 0))


def _position():
    return lax.axis_index("x"), lax.axis_index("y"), lax.axis_index("c")


def _other_chips(x, y):
    return [(1 - x, y), (x, 1 - y), (1 - x, 1 - y)]


def _remote(src, dst, send_sems, recv_sems, k, to):
    return pltpu.make_async_remote_copy(src_ref=src, dst_ref=dst, send_sem=send_sems.at[k], recv_sem=recv_sems.at[k],
                                        device_id=to, device_id_type=MESH)


def _slot(ref, p):
    return ref.at[4 * p[0] + 2 * p[1] + p[2]]


def _ag_stage1(x_ref, out_ref, send_sems, recv_sems, local_sem):
    x, y, c = _position()
    me = (x, y, c)
    peers = [(x, y, 1 - c)] + [(*chip, c) for chip in _other_chips(x, y)]
    sends = [_remote(x_ref, _slot(out_ref, me), send_sems, recv_sems, k, p) for k, p in enumerate(peers)]
    recvs = [_remote(x_ref, _slot(out_ref, p), send_sems, recv_sems, k, p) for k, p in enumerate(peers)]
    return pltpu.make_async_copy(x_ref, _slot(out_ref, me), local_sem), sends, recvs


def _ag_stage2(in_ref, out_ref, send_sems, recv_sems):
    x, y, c = _position()
    sibling = (x, y, 1 - c)
    chips = _other_chips(x, y)
    sends = [_remote(_slot(in_ref, (*ch, c)), _slot(out_ref, (*ch, c)), send_sems, recv_sems, j, sibling)
             for j, ch in enumerate(chips)]
    recvs = [_remote(_slot(in_ref, (*ch, c)), _slot(out_ref, (*ch, 1 - c)), send_sems, recv_sems, j, sibling)
             for j, ch in enumerate(chips)]
    return sends, recvs


def _rs_chip_copies(x_ref, out_ref, send_sems, recv_sems, local_sem):
    x, y, c = _position()
    q_me = 2 * x + y
    chips = _other_chips(x, y)
    sends = [_remote(x_ref.at[2 * px + py], out_ref.at[q_me], send_sems, recv_sems, j, (px, py, c))
             for j, (px, py) in enumerate(chips)]
    recvs = [_remote(x_ref.at[q_me], out_ref.at[2 * px + py], send_sems, recv_sems, j, (px, py, c))
             for j, (px, py) in enumerate(chips)]
    return pltpu.make_async_copy(x_ref.at[q_me], out_ref.at[q_me], local_sem), sends, recvs


def _start_all(local, sends):
    if local is not None:
        local.start()
    for cp in sends:
        cp.start()


def _finish_all(local, sends, recvs):
    for cp in recvs:
        cp.wait_recv()
    for cp in sends:
        cp.wait_send()
    if local is not None:
        local.wait()


def _sem_scratch(n, local):
    out = [pltpu.SemaphoreType.DMA((n,)), pltpu.SemaphoreType.DMA((n,))]
    return out + ([pltpu.SemaphoreType.DMA(())] if local else [])


def _fwd_in(x2d, mod3, g1, cwb, cb, gw, pk, seq, tm, nxt=None):
    t, d = x2d.shape
    nc = 6 * d
    tps = seq // tm
    n = t // tm
    lo = HALO - (CONV_TAPS - 1)
    sub = min(CONV_SUB, tm)
    comm = nxt is not None

    def body(*refs):
        x_ref, mod_ref, g_ref, cw_ref, cb_ref, gw_hbm = refs[:6]
        refs = refs[6:]
        if comm:
            nx_ref, refs = refs[0], refs[1:]
        proj_ref, zc_ref = refs[:2]
        refs = refs[2:]
        if comm:
            gwn_ref, refs = refs[0], refs[1:]
        w_vm, sems, zext, zsh, zc_buf, ztail = refs[:6]
        i = pl.program_id(0)

        if comm:
            ag = functools.partial(_ag_stage1, nx_ref, gwn_ref, *refs[6:9])

            @pl.when(i == 0)
            def _():
                local, sends, _ = ag()
                _start_all(local, sends)

        @pl.when(i == 0)
        def _():
            _load_rows(gw_hbm, w_vm, sems, 0, pk.off_in, pk.n_in)
            ztail[...] = jnp.zeros(ztail.shape, F32)

        x = x_ref[...]
        m = mod_ref[0]
        r = lax.rsqrt(_mean(x * x) + EPS)
        h = (x * r * g_ref[...] * (1.0 + m[:, d:2 * d]) + m[:, 0:d]).astype(BF16)

        def chunk(j):
            proj_ref[:, j * 512:(j + 1) * 512] = _nt(h, w_vm[j * 512:(j + 1) * 512, :]).astype(BF16)

        glu = range(2 * d // 512, 4 * d // 512)
        for j in glu:
            chunk(j)
        z = proj_ref[:, 2 * d:3 * d].astype(F32) * jax.nn.sigmoid(proj_ref[:, 3 * d:4 * d].astype(F32))
        zext[pl.ds(0, HALO), :] = jnp.where((i % tps) == 0, 0.0, ztail[...])
        zext[pl.ds(HALO, tm), :] = z
        ztail[...] = zext[pl.ds(tm, HALO), :]
        for s in range(tm // sub):
            _shift_copies(zext, zsh, s * sub, sub + HALO - 8)
            _conv_taps(zext, zsh, s * sub, cw_ref, [lo + k for k in range(CONV_TAPS)], cb_ref[...], zc_buf, sub, d)
        zc_ref[...] = zc_buf[...].astype(BF16)
        for j in range(nc // 512):
            if j not in glu:
                chunk(j)

        if comm:
            @pl.when(i == n - 1)
            def _():
                _finish_all(*ag())

    return pl.pallas_call(
        body, grid=(n,),
        in_specs=[_row_spec(tm, d), _mod_spec(tps, nc), _const_spec((1, d)), _const_spec((8 * HALO, d)),
                  _const_spec((1, d)), ANY] + ([ANY] if comm else []),
        out_specs=[_row_spec(tm, nc), _row_spec(tm, d)] + ([ANY] if comm else []),
        out_shape=[_sds((t, nc), BF16), _sds((t, d), BF16)] + ([_sds((N_DEV,) + nxt.shape, nxt.dtype)] if comm else []),
        scratch_shapes=[pltpu.VMEM((nc, d), BF16), pltpu.SemaphoreType.DMA((N_DEV,)),
                        pltpu.VMEM((tm + HALO, d), F32), pltpu.VMEM((7, sub + HALO, d), F32),
                        pltpu.VMEM((tm, d), F32), pltpu.VMEM((HALO, d), F32)]
        + (_sem_scratch(4, True) if comm else []),
        compiler_params=_params(1), name="fwd_in_ag" if comm else "fwd_in",
    )(x2d, mod3, g1, cwb, cb, gw, *([nxt] if comm else []))


def _shift_copies(src, sh, base, rows):
    for r in range(1, 8):
        sh[r - 1, pl.ds(0, rows), :] = src[pl.ds(base + r, rows), :]


def _window(src, sh, base, offset, start, size):
    r, q = offset % 8, offset // 8
    if r == 0:
        return src[pl.ds(base + start + 8 * q, size), :]
    return sh[r - 1, pl.ds(start + 8 * q, size), :]


def _conv_taps(src, sh, base, cwb_ref, offsets, bias, out_ref, rows, d):
    nsub = CONV_ROWS // 8
    for rb in range(rows // CONV_ROWS):
        accs = [jnp.broadcast_to(bias, (8, d))] * nsub
        for k in range(CONV_TAPS):
            w8 = cwb_ref[pl.ds(8 * k, 8), :]
            accs = [a + w8 * _window(src, sh, base, offsets[k], rb * CONV_ROWS + 8 * j, 8) for j, a in enumerate(accs)]
        for j, a in enumerate(accs):
            out_ref[pl.ds(base + rb * CONV_ROWS + 8 * j, 8), :] = a


def _conv_wgrad(z_ref, dsrc, dsh, acc_ref, rows, d):
    for k0 in range(0, CONV_TAPS, CONV_WGRAD_TAPS):
        taps = list(range(k0, min(k0 + CONV_WGRAD_TAPS, CONV_TAPS)))
        accs = [jnp.zeros((8, d), F32)] * len(taps)
        for rb in range(rows // 8):
            zblk = z_ref[pl.ds(rb * 8, 8), :]
            accs = [a + zblk * _window(dsrc, dsh, 0, CONV_TAPS - 1 - k, rb * 8, 8) for a, k in zip(accs, taps)]
        for a, k in zip(accs, taps):
            acc_ref[pl.ds(8 * k, 8), :] = acc_ref[pl.ds(8 * k, 8), :] + a


def _fwd_mix(proj, zc, x2d, mod3, vecs, wm, bsx, gw, pk, seq, tm, nxt=None):
    t, d = x2d.shape
    tps = seq // tm
    n = t // tm
    comm = nxt is not None

    def body(*refs):
        proj_ref, zc_ref, x_ref, mod_ref, vec_ref, wm_ref, bs_ref, gw_hbm = refs[:8]
        refs = refs[8:]
        if comm:
            nx_ref, refs = refs[0], refs[1:]
        x1_ref, ya_ref, yb_ref, o1_ref = refs[:4]
        refs = refs[4:]
        if comm:
            gwn_ref, refs = refs[0], refs[1:]
        wpa, wpb, wout, sems, vn_buf, a_buf = refs[:6]
        i = pl.program_id(0)

        if comm:
            ag = functools.partial(_ag_stage1, nx_ref, gwn_ref, *refs[6:9])

            @pl.when(i == 0)
            def _():
                local, sends, _ = ag()
                _start_all(local, sends)

        @pl.when(i == 0)
        def _():
            _load_rows(gw_hbm, wpa, sems, 0, pk.off_pa, pk.n_p)
            _load_rows(gw_hbm, wpb, sems, N_DEV, pk.off_pb, pk.n_p)
            _load_rows(gw_hbm, wout, sems, 2 * N_DEV, pk.off_out, pk.n_p)

        m = mod_ref[0]
        vhat, _ = _ln_stats(proj_ref[:, d:2 * d].astype(F32))
        vn_buf[...] = (vhat * vec_ref[0:1, :] + vec_ref[1:2, :]).astype(BF16)
        for c in range(tm // CHUNK):
            rs_ = slice(c * CHUNK, (c + 1) * CHUNK)
            for h in range(HEADS):
                cs_ = slice(h * CHUNK, (h + 1) * CHUNK)
                s_b = _nn(wm_ref[h], vn_buf[rs_, cs_]) + bs_ref[:, cs_]
                a_buf[rs_, cs_] = (proj_ref[rs_, cs_].astype(F32) * s_b).astype(BF16)
        y_a = _nn(a_buf[...], wpa[...])
        ya_ref[...] = y_a.astype(BF16)
        zhat, _ = _ln_stats(zc_ref[...].astype(F32))
        zn = zhat * vec_ref[3:4, :] + vec_ref[4:5, :]
        b_act = (zn * jax.nn.sigmoid(zn)).astype(BF16)
        y_b = _nn(b_act, wpb[...])
        yb_ref[...] = y_b.astype(BF16)
        merged = (jax.nn.sigmoid(proj_ref[:, 4 * d:5 * d].astype(F32)) * y_a
                  + jax.nn.sigmoid(proj_ref[:, 5 * d:6 * d].astype(F32)) * y_b).astype(BF16)
        o1 = _nn(merged, wout[...])
        o1_ref[...] = o1.astype(BF16)
        x1_ref[...] = x_ref[...] + m[:, 2 * d:3 * d] * o1

        if comm:
            @pl.when(i == n - 1)
            def _():
                _finish_all(*ag())

    act = _sds((t, d), BF16)
    return pl.pallas_call(
        body, grid=(n,),
        in_specs=[_row_spec(tm, 6 * d), _row_spec(tm, d), _row_spec(tm, d), _mod_spec(tps, 6 * d), _const_spec((8, d)),
                  _const_spec((HEADS, CHUNK, CHUNK)), _const_spec((CHUNK, d)), ANY]
        + ([ANY] if comm else []),
        out_specs=[_row_spec(tm, d)] * 4 + ([ANY] if comm else []),
        out_shape=[_sds((t, d), F32), act, act, act]
        + ([_sds((N_DEV,) + nxt.shape, nxt.dtype)] if comm else []),
        scratch_shapes=[pltpu.VMEM((d, d), BF16), pltpu.VMEM((d, d), BF16), pltpu.VMEM((d, d), BF16),
                        pltpu.SemaphoreType.DMA((3 * N_DEV,)),
                        pltpu.VMEM((tm, d), BF16), pltpu.VMEM((tm, d), BF16)]
        + (_sem_scratch(4, True) if comm else []),
        compiler_params=_params(1), name="fwd_mix_ag" if comm else "fwd_mix",
    )(proj, zc, x2d, mod3, vecs, wm, bsx, gw, *([nxt] if comm else []))


def _fwd_ffn(x1, mod3, g2, gw, pk, seq, tm, gw_next=None):
    t, d = x1.shape
    nf = 4 * d
    tps = seq // tm
    n = t // tm
    comm = gw_next is not None

    def body(*refs):
        x_ref, mod_ref, g_ref, gw_hbm = refs[:4]
        refs = refs[4:]
        if comm:
            gwn_in, refs = refs[0], refs[1:]
        x2_ref, f_ref, o2_ref = refs[:3]
        refs = refs[3:]
        if comm:
            gwn_out, refs = refs[0], refs[1:]
        w1, w2, sems = refs[:3]
        i = pl.program_id(0)

        if comm:
            ag = functools.partial(_ag_stage2, gwn_in, gwn_out, *refs[3:5])

            @pl.when(i == 0)
            def _():
                _start_all(None, ag()[0])

        @pl.when(i == 0)
        def _():
            _load_rows(gw_hbm, w1, sems, 0, pk.off_ff1, pk.n_ff)
            _load_rows(gw_hbm, w2, sems, N_DEV, pk.off_ff2, pk.n_ff)

        x = x_ref[...]
        m = mod_ref[0]
        r = lax.rsqrt(_mean(x * x) + EPS)
        h2 = (x * r * g_ref[...] * (1.0 + m[:, 4 * d:5 * d]) + m[:, 3 * d:4 * d]).astype(BF16)
        acc = jnp.zeros(x.shape, F32)
        for j in range(nf // 512):
            js = slice(j * 512, (j + 1) * 512)
            f = _nt(h2, w1[js, :])
            f_ref[:, js] = f.astype(BF16)
            acc = acc + _nn(jnp.square(jnp.maximum(f, 0.0)).astype(BF16), w2[js, :])
        o2_ref[...] = acc.astype(BF16)
        x2_ref[...] = x + m[:, 5 * d:6 * d] * acc

        if comm:
            @pl.when(i == n - 1)
            def _():
                _finish_all(None, *ag())

    return pl.pallas_call(
        body, grid=(n,),
        in_specs=[_row_spec(tm, d), _mod_spec(tps, 6 * d), _const_spec((1, d)), ANY] + ([ANY] if comm else []),
        out_specs=[_row_spec(tm, d), _row_spec(tm, nf), _row_spec(tm, d)] + ([ANY] if comm else []),
        out_shape=[_sds((t, d), F32), _sds((t, nf), BF16), _sds((t, d), BF16)]
        + ([_sds(gw_next.shape, gw_next.dtype)] if comm else []),
        scratch_shapes=[pltpu.VMEM((nf, d), BF16), pltpu.VMEM((nf, d), BF16), pltpu.SemaphoreType.DMA((2 * N_DEV,))]
        + (_sem_scratch(3, False) if comm else []),
        input_output_aliases={4: 3} if comm else {},
        compiler_params=_params(1), name="fwd_ffn_ag" if comm else "fwd_ffn",
    )(x1, mod3, g2, gw, *([gw_next] if comm else []))


def _loss_head(x, tgt, fg, tm):
    t, d = x.shape
    n = t // tm

    def body(x_ref, t_ref, g_ref, dx_ref, loss_ref, dg_ref, lacc):
        i = pl.program_id(0)

        @pl.when(i == 0)
        def _():
            lacc[...] = jnp.zeros(lacc.shape, F32)
            dg_ref[...] = jnp.zeros(dg_ref.shape, F32)

        xv = x_ref[...]
        g = g_ref[...]
        r = lax.rsqrt(_mean(xv * xv) + EPS)
        err = xv * r * g - t_ref[...]
        lacc[...] = lacc[...] + _rowsum(err * err)
        dy = err * (1.0 / d)
        _add_row(dg_ref, (slice(0, 1), slice(None)), _rowsum(dy * xv * r))
        dx_ref[...] = _rms_bwd(dy, g, xv, r)

        @pl.when(i == n - 1)
        def _():
            loss_ref[...] = jnp.broadcast_to(jnp.sum(lacc[...], keepdims=True) * (0.5 / d), loss_ref.shape)

    return pl.pallas_call(
        body, grid=(n,),
        in_specs=[_row_spec(tm, d), _row_spec(tm, d), _const_spec((1, d))],
        out_specs=[_row_spec(tm, d), _const_spec((8, 128)), _const_spec((8, d))],
        out_shape=[_sds((t, d), F32), _sds((8, 128), F32), _sds((8, d), F32)],
        scratch_shapes=[pltpu.VMEM((1, d), F32)],
        compiler_params=_params(1), name="loss_head",
    )(x, tgt, fg)


def _sibling_copy(x_ref, out_ref, send_sem, recv_sem):
    x, y, c = _position()
    return pltpu.make_async_remote_copy(
        src_ref=x_ref.at[pl.ds(0, x_ref.shape[0]), 1 - c], dst_ref=out_ref, send_sem=send_sem, recv_sem=recv_sem,
        device_id=(x, y, 1 - c), device_id_type=MESH)


def _bwd_ffn(dx2, x1, f, o2, mod3, g2, gw, pk, seq, tm, dp=None):
    t, d = x1.shape
    nf = 4 * d
    tps = seq // tm
    nb = t // seq
    steps = t // tm
    comm = dp is not None

    def body(*refs):
        dx2_ref, x_ref, f_ref, o2_ref, mod_ref, g_ref, gw_hbm = refs[:7]
        refs = refs[7:]
        if comm:
            dp_ref, refs = refs[0], refs[1:]
        dx1_ref, df_ref, do2_ref, h2_ref, ms_ref, ps_ref = refs[:6]
        refs = refs[6:]
        if comm:
            got_ref, refs = refs[0], refs[1:]
        w1, w2, sems = refs[:3]
        i = pl.program_id(0)

        if comm:
            swap = functools.partial(_sibling_copy, dp_ref, got_ref, *refs[3:5])

            @pl.when(i == 0)
            def _():
                swap().start()

        @pl.when(i == 0)
        def _():
            _load_rows(gw_hbm, w1, sems, 0, pk.off_ff1, pk.n_ff)
            _load_rows(gw_hbm, w2, sems, N_DEV, pk.off_ff2, pk.n_ff)
            ps_ref[...] = jnp.zeros(ps_ref.shape, F32)

        @pl.when((i % tps) == 0)
        def _():
            ms_ref[...] = jnp.zeros(ms_ref.shape, F32)

        dx2 = dx2_ref[...]
        x = x_ref[...]
        m = mod_ref[0]
        g = g_ref[...]
        sh2, sc2, gt2 = m[:, 3 * d:4 * d], m[:, 4 * d:5 * d], m[:, 5 * d:6 * d]
        _add_row(ms_ref, (0, slice(2, 3), slice(None)), _rowsum(dx2 * o2_ref[...].astype(F32)))
        do2 = (gt2 * dx2).astype(BF16)
        do2_ref[...] = do2
        r = lax.rsqrt(_mean(x * x) + EPS)
        n = x * r * g
        h2_ref[...] = (n * (1.0 + sc2) + sh2).astype(BF16)
        dh = jnp.zeros(x.shape, F32)
        for j in range(nf // 512):
            js = slice(j * 512, (j + 1) * 512)
            dr = _nt(do2, w2[js, :])
            df = (dr * (2.0 * jnp.maximum(f_ref[:, js].astype(F32), 0.0))).astype(BF16)
            df_ref[:, js] = df
            dh = dh + _nn(df, w1[js, :])
        _add_row(ms_ref, (0, slice(0, 1), slice(None)), _rowsum(dh))
        _add_row(ms_ref, (0, slice(1, 2), slice(None)), _rowsum(dh * n))
        dn = dh * (1.0 + sc2)
        _add_row(ps_ref, (slice(0, 1), slice(None)), _rowsum(dn * x * r))
        dx1_ref[...] = dx2 + _rms_bwd(dn, g, x, r)

        if comm:
            @pl.when(i == steps - 1)
            def _():
                swap().wait()

    act = _sds((t, d), BF16)
    return pl.pallas_call(
        body, grid=(steps,),
        in_specs=[_row_spec(tm, d), _row_spec(tm, d), _row_spec(tm, nf), _row_spec(tm, d), _mod_spec(tps, 6 * d),
                  _const_spec((1, d)), ANY] + ([ANY] if comm else []),
        out_specs=[_row_spec(tm, d), _row_spec(tm, nf), _row_spec(tm, d), _row_spec(tm, d), _mstat_spec(tps, d),
                   _const_spec((8, d))] + ([ANY] if comm else []),
        out_shape=[_sds((t, d), F32), _sds((t, nf), BF16), act, act, _sds((nb, 8, d), F32), _sds((8, d), F32)]
        + ([_sds((dp.shape[0],) + dp.shape[2:], dp.dtype)] if comm else []),
        scratch_shapes=[pltpu.VMEM((nf, d), BF16), pltpu.VMEM((nf, d), BF16), pltpu.SemaphoreType.DMA((2 * N_DEV,))]
        + ([pltpu.SemaphoreType.DMA(()), pltpu.SemaphoreType.DMA(())] if comm else []),
        compiler_params=_params(1), name="bwd_ffn_rs" if comm else "bwd_ffn",
    )(dx2, x1, f, o2, mod3, g2, gw, *([dp] if comm else []))


def _bwd_mix(dx1, proj, ya, yb, o1, zc, mod3, vecs, wm, wmt, bsx, gw, pk, seq, tm):
    t, d = dx1.shape
    tps = seq // tm
    nb = t // seq
    n = t // tm

    def body(dx1_ref, proj_ref, ya_ref, yb_ref, o1_ref, zc_ref, mod_ref, vec_ref, wm_ref, wmt_ref, bs_ref, gw_hbm,
             dp_ref, dzc_ref, do1_ref, dya_ref, dyb_ref, mg_ref, aa_ref, ba_ref, ms_ref, ps_ref, dws_ref, dbs_ref,
             wpa, wpb, wout, sems, vn_buf, da_buf, dvn_buf):
        i = pl.program_id(0)

        @pl.when(i == 0)
        def _():
            _load_rows(gw_hbm, wpa, sems, 0, pk.off_pa, pk.n_p)
            _load_rows(gw_hbm, wpb, sems, N_DEV, pk.off_pb, pk.n_p)
            _load_rows(gw_hbm, wout, sems, 2 * N_DEV, pk.off_out, pk.n_p)
            ps_ref[...] = jnp.zeros(ps_ref.shape, F32)
            dws_ref[...] = jnp.zeros(dws_ref.shape, F32)
            dbs_ref[...] = jnp.zeros(dbs_ref.shape, F32)

        @pl.when((i % tps) == 0)
        def _():
            ms_ref[...] = jnp.zeros(ms_ref.shape, F32)

        m = mod_ref[0]
        dx1v = dx1_ref[...]
        _add_row(ms_ref, (0, slice(0, 1), slice(None)), _rowsum(dx1v * o1_ref[...].astype(F32)))
        do1 = (m[:, 2 * d:3 * d] * dx1v).astype(BF16)
        do1_ref[...] = do1
        dmg = _nt(do1, wout[...])
        sa = jax.nn.sigmoid(proj_ref[:, 4 * d:5 * d].astype(F32))
        sb = jax.nn.sigmoid(proj_ref[:, 5 * d:6 * d].astype(F32))
        y_a = ya_ref[...].astype(F32)
        y_b = yb_ref[...].astype(F32)
        dya = (dmg * sa).astype(BF16)
        dyb = (dmg * sb).astype(BF16)
        dya_ref[...] = dya
        dyb_ref[...] = dyb
        dp_ref[:, 4 * d:5 * d] = (dmg * y_a * sa * (1.0 - sa)).astype(BF16)
        dp_ref[:, 5 * d:6 * d] = (dmg * y_b * sb * (1.0 - sb)).astype(BF16)
        mg_ref[...] = (sa * y_a + sb * y_b).astype(BF16)
        da_buf[...] = _nt(dya, wpa[...])
        db = _nt(dyb, wpb[...])
        vhat, rs = _ln_stats(proj_ref[:, d:2 * d].astype(F32))
        alg = vec_ref[0:1, :]
        vn_buf[...] = (vhat * alg + vec_ref[1:2, :]).astype(BF16)
        for c in range(tm // CHUNK):
            rs_ = slice(c * CHUNK, (c + 1) * CHUNK)
            for h in range(HEADS):
                cs_ = slice(h * CHUNK, (h + 1) * CHUNK)
                vn_b = vn_buf[rs_, cs_]
                s_b = _nn(wm_ref[h], vn_b) + bs_ref[:, cs_]
                u_b = proj_ref[rs_, cs_].astype(F32)
                da_b = da_buf[rs_, cs_]
                aa_ref[rs_, cs_] = (u_b * s_b).astype(BF16)
                dp_ref[rs_, cs_] = (da_b * s_b).astype(BF16)
                ds_b = da_b * u_b
                dbs_ref[:, cs_] = dbs_ref[:, cs_] + ds_b
                ds_bf = ds_b.astype(BF16)
                dvn_buf[rs_, cs_] = _nn(wmt_ref[h], ds_bf)
                dws_ref[:, cs_] = dws_ref[:, cs_] + _nt(ds_bf, vn_b)
        dvn = dvn_buf[...]
        _add_row(ps_ref, (slice(0, 1), slice(None)), _rowsum(dvn * vhat))
        _add_row(ps_ref, (slice(1, 2), slice(None)), _rowsum(dvn))
        dp_ref[:, d:2 * d] = _ln_bwd(dvn, alg, vhat, rs).astype(BF16)
        dp_ref[:, 2 * d:4 * d] = jnp.zeros((tm, 2 * d), BF16)
        zhat, rsb = _ln_stats(zc_ref[...].astype(F32))
        blg = vec_ref[3:4, :]
        zn = zhat * blg + vec_ref[4:5, :]
        sg = jax.nn.sigmoid(zn)
        ba_ref[...] = (zn * sg).astype(BF16)
        dzn = db * (sg * (1.0 + zn * (1.0 - sg)))
        _add_row(ps_ref, (slice(2, 3), slice(None)), _rowsum(dzn * zhat))
        _add_row(ps_ref, (slice(3, 4), slice(None)), _rowsum(dzn))
        dzc = _ln_bwd(dzn, blg, zhat, rsb)
        _add_row(ps_ref, (slice(4, 5), slice(None)), _rowsum(dzc))
        dzc_ref[...] = dzc.astype(BF16)

        @pl.when(i == n - 1)
        def _():
            causal = (lax.broadcasted_iota(jnp.int32, (CHUNK, CHUNK), 0)
                      >= lax.broadcasted_iota(jnp.int32, (CHUNK, CHUNK), 1))
            for h in range(HEADS):
                cs_ = slice(h * CHUNK, (h + 1) * CHUNK)
                dws_ref[:, cs_] = jnp.where(causal, dws_ref[:, cs_], 0.0)
                dbs_ref[:, cs_] = jnp.broadcast_to(jnp.sum(dbs_ref[:, cs_], axis=1, keepdims=True), (CHUNK, CHUNK))

    act = _sds((t, d), BF16)
    return pl.pallas_call(
        body, grid=(n,),
        in_specs=[_row_spec(tm, d), _row_spec(tm, 6 * d), _row_spec(tm, d), _row_spec(tm, d), _row_spec(tm, d),
                  _row_spec(tm, d), _mod_spec(tps, 6 * d), _const_spec((8, d)), _const_spec((HEADS, CHUNK, CHUNK)),
                  _const_spec((HEADS, CHUNK, CHUNK)), _const_spec((CHUNK, d)), ANY],
        out_specs=[_row_spec(tm, 6 * d)] + [_row_spec(tm, d)] * 7
        + [_mstat_spec(tps, d), _const_spec((8, d)), _const_spec((CHUNK, d)), _const_spec((CHUNK, d))],
        out_shape=[_sds((t, 6 * d), BF16)] + [act] * 7
        + [_sds((nb, 8, d), F32), _sds((8, d), F32), _sds((CHUNK, d), F32), _sds((CHUNK, d), F32)],
        scratch_shapes=[pltpu.VMEM((d, d), BF16), pltpu.VMEM((d, d), BF16), pltpu.VMEM((d, d), BF16),
                        pltpu.SemaphoreType.DMA((3 * N_DEV,)),
                        pltpu.VMEM((tm, d), BF16), pltpu.VMEM((tm, d), F32), pltpu.VMEM((tm, d), F32)],
        compiler_params=_params(1), name="bwd_mix",
    )(dx1, proj, ya, yb, o1, zc, mod3, vecs, wm, wmt, bsx, gw)


def _bwd_in(dproj, dzc, proj, x2d, dx1, mod3, g1, cw, gw, pk, seq, tm, part=None):
    t, d = x2d.shape
    tps = seq // tm
    nb = t // seq
    n = t // tm
    hb = tm // HALO
    comm = part is not None

    def body(*refs):
        dpi_ref, dzc_ref, dzn_ref, pp_ref, x_ref, dx1_ref, mod_ref, g_ref, cw_ref, gw_hbm = refs[:10]
        refs = refs[10:]
        if comm:
            part_ref, refs = refs[0], refs[1:]
        dpo_ref, dx_ref, h_ref, ms_ref, ps_ref, dcw_ref = refs[:6]
        refs = refs[6:]
        if comm:
            got_ref, refs = refs[0], refs[1:]
        w_vm, sems, z_buf, dzext, dsh, dz_buf, dcw_acc = refs[:7]
        i = pl.program_id(0)

        if comm:
            rs = functools.partial(_rs_chip_copies, part_ref, got_ref, *refs[7:10])

            @pl.when(i == 0)
            def _():
                local, sends, _ = rs()
                _start_all(local, sends)

        @pl.when(i == 0)
        def _():
            _load_rows(gw_hbm, w_vm, sems, 0, pk.off_in, pk.n_in)
            ps_ref[...] = jnp.zeros(ps_ref.shape, F32)
            dcw_acc[...] = jnp.zeros(dcw_acc.shape, F32)

        last = (i % tps) == tps - 1

        @pl.when((i % tps) == 0)
        def _():
            ms_ref[...] = jnp.zeros(ms_ref.shape, F32)

        pa = pp_ref[:, 0:d].astype(F32)
        sgp = jax.nn.sigmoid(pp_ref[:, d:2 * d].astype(F32))
        z_buf[...] = pa * sgp
        dzext[pl.ds(0, tm), :] = dzc_ref[...].astype(F32)
        dzext[pl.ds(tm, HALO), :] = jnp.where(last, 0.0, dzn_ref[...].astype(F32))
        _shift_copies(dzext, dsh, 0, tm + HALO - 8)
        _conv_wgrad(z_buf, dzext, dsh, dcw_acc, tm, d)
        _conv_taps(dzext, dsh, 0, cw_ref, [CONV_TAPS - 1 - k for k in range(CONV_TAPS)], jnp.zeros((1, d), F32),
                   dz_buf, tm, d)
        dz = dz_buf[...]
        dpa = (dz * sgp).astype(BF16)
        dpg = (dz * pa * sgp * (1.0 - sgp)).astype(BF16)
        dpo_ref[:, 0:d] = dpa
        dpo_ref[:, d:2 * d] = dpg
        dh = (_nn(dpi_ref[:, 0:2 * d], w_vm[0:2 * d, :]) + _nn(dpa, w_vm[2 * d:3 * d, :])
              + _nn(dpg, w_vm[3 * d:4 * d, :]) + _nn(dpi_ref[:, 4 * d:6 * d], w_vm[4 * d:6 * d, :]))
        x = x_ref[...]
        m = mod_ref[0]
        g = g_ref[...]
        sh1, sc1 = m[:, 0:d], m[:, d:2 * d]
        r = lax.rsqrt(_mean(x * x) + EPS)
        nrm = x * r * g
        h_ref[...] = (nrm * (1.0 + sc1) + sh1).astype(BF16)
        _add_row(ms_ref, (0, slice(0, 1), slice(None)), _rowsum(dh))
        _add_row(ms_ref, (0, slice(1, 2), slice(None)), _rowsum(dh * nrm))
        dn = dh * (1.0 + sc1)
        _add_row(ps_ref, (slice(0, 1), slice(None)), _rowsum(dn * x * r))
        dx_ref[...] = dx1_ref[...] + _rms_bwd(dn, g, x, r)

        @pl.when(i == n - 1)
        def _():
            for k in range(CONV_TAPS):
                dcw_ref[k:k + 1, :] = _rowsum(dcw_acc[pl.ds(k * 8, 8), :])
            dcw_ref[CONV_TAPS:HALO, :] = jnp.zeros((HALO - CONV_TAPS, d), F32)

        if comm:
            @pl.when(i == n - 1)
            def _():
                _finish_all(*rs())

    halo_next = pl.BlockSpec((HALO, d), lambda i: (jnp.minimum((i + 1) * hb, t // HALO - 1), 0))
    return pl.pallas_call(
        body, grid=(n,),
        in_specs=[_row_spec(tm, 6 * d), _row_spec(tm, d), halo_next, _row_spec(tm, 2 * d, 1),
                  _row_spec(tm, d), _row_spec(tm, d), _mod_spec(tps, 6 * d), _const_spec((1, d)),
                  _const_spec((8 * HALO, d)), ANY] + ([ANY] if comm else []),
        out_specs=[_row_spec(tm, 2 * d, 1), _row_spec(tm, d), _row_spec(tm, d), _mstat_spec(tps, d),
                   _const_spec((8, d)), _const_spec((HALO, d))] + ([ANY] if comm else []),
        out_shape=[_sds((t, 6 * d), BF16), _sds((t, d), F32), _sds((t, d), BF16), _sds((nb, 8, d), F32),
                   _sds((8, d), F32), _sds((HALO, d), F32)] + ([_sds(part.shape, part.dtype)] if comm else []),
        scratch_shapes=[pltpu.VMEM((6 * d, d), BF16), pltpu.SemaphoreType.DMA((N_DEV,)),
                        pltpu.VMEM((tm, d), F32), pltpu.VMEM((tm + HALO, d), F32),
                        pltpu.VMEM((7, tm + HALO, d), F32),
                        pltpu.VMEM((tm, d), F32), pltpu.VMEM((CONV_TAPS * 8, d), F32)]
        + (_sem_scratch(3, True) if comm else []),
        input_output_aliases={0: 0},
        compiler_params=_params(1), name="bwd_in_rs" if comm else "bwd_in",
    )(dproj, dzc, dzc, proj, x2d, dx1, mod3, g1, cw, gw, *([part] if comm else []))


def _wgrad(a, b, pack, kb, off, relu2, name):
    t, mo = a.shape
    nn_ = b.shape[1]
    rows = mo // N_DEV
    tk = min(TK_WGRAD_ONE if kb == N_DEV else TK_WGRAD, t)
    nk = t // tk
    assert off % rows == 0 and N_DEV % kb == 0

    def body(a_ref, b_ref, pack_hbm, o_ref, acc):
        k = pl.program_id(1)

        @pl.when(k == 0)
        def _():
            acc[...] = jnp.zeros(acc.shape, F32)

        av = a_ref[...]
        if relu2:
            av = jnp.square(jnp.maximum(av, 0.0))
        acc[...] = acc[...] + _tn(av, b_ref[...])

        @pl.when(k == nk - 1)
        def _():
            for j in range(kb):
                o_ref[j] = acc[pl.ds(j * rows, rows), :].astype(BF16)

    return pl.pallas_call(
        body, grid=(N_DEV // kb, nk),
        in_specs=[pl.BlockSpec((tk, kb * rows), lambda i, k: (k, i)), pl.BlockSpec((tk, nn_), lambda i, k: (k, 0)), ANY],
        out_specs=pl.BlockSpec((kb, rows, nn_), lambda i, k: (i, off // rows, 0)),
        out_shape=_sds(pack.shape, BF16),
        scratch_shapes=[pltpu.VMEM((kb * rows, nn_), F32)],
        input_output_aliases={2: 0},
        compiler_params=_params(2), name=name,
    )(a, b, pack)


def _place_rows(pack, rows_blk, off):
    nblk, r, nn_ = rows_blk.shape
    assert off % r == 0

    def body(pack_hbm, s_ref, o_ref):
        o_ref[...] = s_ref[...]

    return pl.pallas_call(
        body, grid=(1,),
        in_specs=[ANY, pl.BlockSpec((nblk, r, nn_), lambda i: (0, 0, 0))],
        out_specs=pl.BlockSpec((nblk, r, nn_), lambda i: (0, off // r, 0)),
        out_shape=_sds(pack.shape, pack.dtype),
        input_output_aliases={0: 0},
        compiler_params=_params(1), name="place_small",
    )(pack, rows_blk)


def _mod_fwd(c_all, w_ada, b_cols):
    nl, d, cols = w_ada.shape
    bsz = c_all.shape[0]

    def body(c_ref, w_ref, b_ref, o_ref):
        cv = c_ref[...]
        ca = cv * jax.nn.sigmoid(cv)
        o_ref[0] = jnp.dot(ca, w_ref[0], preferred_element_type=F32, precision=lax.Precision.HIGHEST) + b_ref[0]

    return pl.pallas_call(
        body, grid=(nl,),
        in_specs=[_const_spec((bsz, d)), pl.BlockSpec((1, d, cols), lambda l: (l, 0, 0)),
                  pl.BlockSpec((1, 1, cols), lambda l: (l, 0, 0))],
        out_specs=pl.BlockSpec((1, bsz, cols), lambda l: (l, 0, 0)),
        out_shape=_sds((nl, bsz, cols), F32),
        compiler_params=_params(1), name="mod_fwd",
    )(c_all, w_ada, b_cols)


def _mod_bwd(c_all, dmod_cols, dmod_all):
    nl, bsz, cols = dmod_cols.shape
    d = c_all.shape[1]
    ncol = dmod_all.shape[2]

    def body(c_ref, dc_ref, da_ref, dw_ref, db_ref):
        cv = c_ref[...]
        ca = cv * jax.nn.sigmoid(cv)
        dw_ref[0] = lax.dot_general(ca, dc_ref[0], (((0,), (0,)), ((), ())), preferred_element_type=F32,
                                    precision=lax.Precision.HIGHEST)
        db_ref[0] = _rowsum(da_ref[0])

    return pl.pallas_call(
        body, grid=(nl,),
        in_specs=[_const_spec((bsz, d)), pl.BlockSpec((1, bsz, cols), lambda l: (l, 0, 0)),
                  pl.BlockSpec((1, bsz, ncol), lambda l: (l, 0, 0))],
        out_specs=[pl.BlockSpec((1, d, cols), lambda l: (l, 0, 0)), pl.BlockSpec((1, 1, ncol), lambda l: (l, 0, 0))],
        out_shape=[_sds((nl, d, cols), F32), _sds((nl, 1, ncol), F32)],
        compiler_params=_params(1), name="mod_bwd",
    )(c_all, dmod_cols, dmod_all)


def _row_tile(rows, cols, nbuf, itemsize=4, budget=24 * 1024 * 1024):
    cap = max(16, budget // (2 * nbuf * cols * itemsize))
    if rows <= cap:
        return rows
    best = None
    for tr in range(16, cap + 1, 16):
        if rows % tr == 0:
            best = tr
    assert best is not None, (rows, cols)
    return best


def _sum_blocks(xs, name):
    nblk, rows, cols = xs.shape
    tr = _row_tile(rows, cols, nblk + 1)

    def body(x_ref, o_ref):
        acc = x_ref[0].astype(F32)
        for j in range(1, nblk):
            acc = acc + x_ref[j].astype(F32)
        o_ref[...] = acc

    return pl.pallas_call(
        body, grid=(rows // tr,),
        in_specs=[pl.BlockSpec((nblk, tr, cols), lambda i: (0, i, 0))],
        out_specs=pl.BlockSpec((tr, cols), lambda i: (i, 0)),
        out_shape=_sds((rows, cols), F32),
        compiler_params=_params(1), name=name,
    )(xs)


def _add_sibling(dp, recv, core):
    nq, _, rows, cols = dp.shape
    tr = _row_tile(rows, cols, 3, itemsize=2)

    def body(c_ref, a_ref, b_ref, o_ref):
        o_ref[...] = (a_ref[...].astype(F32) + b_ref[...].astype(F32)).astype(BF16)

    return pl.pallas_call(
        body,
        grid_spec=pltpu.PrefetchScalarGridSpec(
            num_scalar_prefetch=1, grid=(nq, rows // tr),
            in_specs=[pl.BlockSpec((1, 1, tr, cols), lambda q, i, c: (q, c[0], i, 0)),
                      pl.BlockSpec((1, 1, tr, cols), lambda q, i, c: (q, 0, i, 0))],
            out_specs=pl.BlockSpec((1, 1, tr, cols), lambda q, i, c: (q, 0, i, 0))),
        out_shape=_sds((nq, 1, rows, cols), BF16),
        compiler_params=_params(2), name="add_sibling",
    )(core, dp, recv.reshape(nq, 1, rows, cols)).reshape(nq, rows, cols)


def _adamw(w, g, m, v, name):
    rows, cols = w.shape
    tr = _row_tile(rows, cols, 7)
    c1 = 1.0 - ADAM_B1 ** ADAM_STEP
    c2 = 1.0 - ADAM_B2 ** ADAM_STEP

    def body(w_ref, g_ref, m_ref, v_ref, d_ref, nm_ref, nv_ref):
        gv = g_ref[...]
        nm = ADAM_B1 * m_ref[...] + (1.0 - ADAM_B1) * gv
        nv = ADAM_B2 * v_ref[...] + (1.0 - ADAM_B2) * (gv * gv)
        nm_ref[...] = nm
        nv_ref[...] = nv
        d_ref[...] = -ADAM_LR * ((nm / c1) / (jnp.sqrt(nv / c2) + ADAM_EPS) + ADAM_WD * w_ref[...])

    spec = pl.BlockSpec((tr, cols), lambda i: (i, 0))
    out = _sds((rows, cols), F32)
    return pl.pallas_call(
        body, grid=(rows // tr,), in_specs=[spec] * 4, out_specs=[spec] * 3, out_shape=[out] * 3,
        compiler_params=_params(1), name=name,
    )(w, g, m, v)


def _all_gather(xs, name):
    rows, cols = xs.shape

    def body(x_ref, out_ref, send1, recv1, local_sem, send2, recv2):
        local, first, arrivals = _ag_stage1(x_ref, out_ref, send1, recv1, local_sem)
        _start_all(local, first)
        passed, from_sibling = _ag_stage2(out_ref, out_ref, send2, recv2)
        for arrival, onward in zip(arrivals[1:], passed):
            arrival.wait_recv()
            onward.start()
        arrivals[0].wait_recv()
        _finish_all(local, first + passed, from_sibling)

    return pl.pallas_call(
        body, out_shape=_sds((N_DEV, rows, cols), xs.dtype), in_specs=[ANY], out_specs=ANY,
        scratch_shapes=_sem_scratch(4, True) + _sem_scratch(3, False), name=name,
    )(xs)


def _sibling_exchange(dp):
    nq, _, rows, cols = dp.shape

    def body(x_ref, out_ref, send_sem, recv_sem):
        cp = _sibling_copy(x_ref, out_ref, send_sem, recv_sem)
        cp.start()
        cp.wait()

    return pl.pallas_call(
        body, out_shape=_sds((nq, rows, cols), dp.dtype), in_specs=[ANY], out_specs=ANY,
        scratch_shapes=[pltpu.SemaphoreType.DMA(()), pltpu.SemaphoreType.DMA(())],
        name="rs_sibling",
    )(dp)


def _chip_all_to_all(xs):
    nq, rows, cols = xs.shape

    def body(x_ref, out_ref, send_sems, recv_sems, local_sem):
        local, sends, recvs = _rs_chip_copies(x_ref, out_ref, send_sems, recv_sems, local_sem)
        _start_all(local, sends)
        _finish_all(local, sends, recvs)

    return pl.pallas_call(
        body, out_shape=_sds((nq, rows, cols), xs.dtype), in_specs=[ANY], out_specs=ANY,
        scratch_shapes=_sem_scratch(3, True), name="rs_chips",
    )(xs)


def _pad_rows(a, rows):
    return jnp.pad(a, ((0, rows - a.shape[0]), (0, 0)))


def kernel(x, c, w_ada, b_ada, norm1_g, w_in, a_ln_g, a_ln_b, a_ws, a_bs, w_pa, b_conv_w, b_conv_b, b_ln_g, b_ln_b, w_pb, w_out, norm2_g, w_ff1, w_ff2, final_g, loss_target, m_w_ada, m_b_ada, m_norm1_g, m_w_in, m_a_ln_g, m_a_ln_b, m_a_ws, m_a_bs, m_w_pa, m_b_conv_w, m_b_conv_b, m_b_ln_g, m_b_ln_b, m_w_pb, m_w_out, m_norm2_g, m_w_ff1, m_w_ff2, m_final_g, v_w_ada, v_b_ada, v_norm1_g, v_w_in, v_a_ln_g, v_a_ln_b, v_a_ws, v_a_bs, v_w_pa, v_b_conv_w, v_b_conv_b, v_b_ln_g, v_b_ln_b, v_w_pb, v_w_out, v_norm2_g, v_w_ff1, v_w_ff2, v_final_g):
    nb, seq, d = x.shape
    nl = w_in.shape[0]
    t = nb * seq
    pk = _Pack(d, 0)
    gk = _Pack(d, SMALL_SLOT)
    assert d % (N_DEV * CHUNK) == 0 and d // HEADS == CHUNK and seq % CHUNK == 0
    tm_big = min(TM_BIG, seq)
    tm_mix = min(TM_MIX, seq)
    ax, ay, ac = _position()
    dev = 4 * ax + 2 * ay + ac
    ncol = 6 * d
    cols = ncol // N_DEV
    cpd = d // N_DEV
    bsz = nb * N_DEV

    cw_rows = nl * HALO
    small = jnp.concatenate([
        c.reshape(nb * d // CHUNK, CHUNK),
        jnp.pad(b_conv_w.reshape(nl, CONV_TAPS, cpd), ((0, 0), (0, HALO - CONV_TAPS), (0, 0))).reshape(cw_rows, cpd),
    ], axis=0)
    c_rows = nb * d // CHUNK
    small_all = _all_gather(small, "ag_small")
    c_all = small_all[:, :c_rows].reshape(bsz, d)
    cw_all = small_all[:, c_rows:].reshape(N_DEV, nl, HALO, cpd).transpose(1, 2, 0, 3).reshape(nl, HALO, d)
    cwb_all = jnp.repeat(cw_all, 8, axis=1)
    b_cols = lax.dynamic_slice_in_dim(b_ada, dev * cols, cols, axis=1).reshape(nl, 1, cols)
    mod_cols = _mod_fwd(c_all, w_ada, b_cols)
    mod_all = _all_gather(mod_cols.reshape(nl * bsz, cols), "ag_mod")
    mod_all = mod_all.reshape(N_DEV, nl, bsz, cols).transpose(1, 2, 0, 3).reshape(nl, bsz, ncol)
    mod_mine = lax.dynamic_slice_in_dim(mod_all, dev * nb, nb, axis=1)

    causal = jnp.tril(jnp.ones((CHUNK, CHUNK), bool))
    wm_all = jnp.where(causal[None, None], a_ws, 0.0)
    wm_bf = wm_all.astype(BF16)
    wmt_bf = jnp.swapaxes(wm_all, 2, 3).astype(BF16)
    bsx_all = jnp.broadcast_to(jnp.swapaxes(a_bs, 1, 2)[:, :, :, None], (nl, CHUNK, HEADS, CHUNK)).reshape(nl, CHUNK, d)

    def vec_rows(l):
        return _pad_rows(jnp.stack([a_ln_g[l], a_ln_b[l], b_conv_b[l], b_ln_g[l], b_ln_b[l]]), 8)

    def weight_block(l):
        return jnp.concatenate([
            w_ff1[l].T, w_ff2[l], w_pa[l], w_pb[l], w_out[l], w_in[l].T], axis=0).astype(BF16)

    xs = x.reshape(t, d)
    saved = []
    gw = _all_gather(weight_block(0), "ag_weights")
    for l in range(nl):
        nxt = weight_block(l + 1) if l + 1 < nl else None
        mod3 = mod_mine[l].reshape(nb, 1, ncol)
        vecs = vec_rows(l)
        proj, zc, *gw_next = _fwd_in(xs, mod3, norm1_g[l].reshape(1, d), cwb_all[l], b_conv_b[l].reshape(1, d), gw, pk,
                                     seq, tm_big, nxt)
        x1, ya, yb, o1 = _fwd_mix(proj, zc, xs, mod3, vecs, wm_bf[l], bsx_all[l], gw, pk, seq, tm_big)
        x2, f, o2, *gw_next = _fwd_ffn(x1, mod3, norm2_g[l].reshape(1, d), gw, pk, seq, tm_big, *gw_next)
        saved.append((xs, x1, proj, ya, yb, o1, zc, f, o2, gw, mod3, vecs))
        xs = x2
        if gw_next:
            gw = gw_next[0]

    dx, loss_blk, dfg = _loss_head(xs, loss_target.reshape(t, d), final_g.reshape(1, d), tm_big)
    loss = lax.psum(loss_blk[0, 0], ("x", "y", "c"))

    core = ac.reshape(1).astype(jnp.int32)
    wg = {k: [None] * nl for k in ("w_in", "w_ff1", "w_ff2", "w_pa", "w_pb", "w_out")}
    small_red = [None] * nl
    dmod_rows = [None] * nl
    per_layer = 8 + 2 * CHUNK + HALO
    assert per_layer <= N_DEV * SMALL_ROWS <= N_DEV * SMALL_SLOT

    def reduced(l, red):
        wg["w_in"][l] = red[gk.off_in:gk.off_in + gk.n_in].T
        wg["w_ff1"][l] = red[gk.off_ff1:gk.off_ff1 + gk.n_ff].T
        wg["w_ff2"][l] = red[gk.off_ff2:gk.off_ff2 + gk.n_ff]
        wg["w_pa"][l] = red[gk.off_pa:gk.off_pa + gk.n_p]
        wg["w_pb"][l] = red[gk.off_pb:gk.off_pb + gk.n_p]
        wg["w_out"][l] = red[gk.off_out:gk.off_out + gk.n_p]
        small_red[l] = red[gk.off_small:gk.off_small + SMALL_ROWS]

    waiting = None
    pending = None
    for l in reversed(range(nl)):
        x0, x1, proj, ya, yb, o1, zc, f, o2, gw, mod3, vecs = saved[l]
        dx1, df, do2, h2, ms2, ps2, *got = _bwd_ffn(dx, x1, f, o2, mod3, norm2_g[l].reshape(1, d), gw, pk, seq, tm_big,
                                                    None if waiting is None else waiting[1])
        if waiting is not None:
            pending = (waiting[0], _add_sibling(waiting[1], got[0], core))
        (dproj, dzc, do1, dya, dyb, mg, aa, ba, ms1, ps1, dws, dbs) = _bwd_mix(
            dx1, proj, ya, yb, o1, zc, mod3, vecs, wm_bf[l], wmt_bf[l], bsx_all[l], gw, pk, seq, tm_mix)
        grads = lax.empty((N_DEV, gk.rows, d), BF16)
        grads = _wgrad(df, h2, grads, 2, gk.off_ff1, False, "wgrad_ff1")
        grads = _wgrad(f, do2, grads, 2, gk.off_ff2, True, "wgrad_ff2")
        dproj, dx, h, ms0, ps0, dcw, *got = _bwd_in(
            dproj, dzc, proj, x0, dx1, mod3, norm1_g[l].reshape(1, d), cwb_all[l], gw, pk, seq, tm_mix,
            None if pending is None else pending[1])
        if pending is not None:
            reduced(pending[0], _sum_blocks(got[0], "sum_chips"))
        vec_g = jnp.concatenate([ps0[0:1], ps1[0:5], ps2[0:1], jnp.zeros((1, d), F32)], axis=0)
        small = _pad_rows(jnp.concatenate([vec_g, dws, dbs, dcw], axis=0), N_DEV * SMALL_ROWS)
        small = jnp.pad(small.reshape(N_DEV, SMALL_ROWS, d).astype(BF16), ((0, 0), (0, SMALL_SLOT - SMALL_ROWS), (0, 0)))
        grads = _place_rows(grads, small, gk.off_small)
        grads = _wgrad(aa, dya, grads, N_DEV, gk.off_pa, False, "wgrad_pa")
        grads = _wgrad(ba, dyb, grads, N_DEV, gk.off_pb, False, "wgrad_pb")
        grads = _wgrad(mg, do1, grads, N_DEV, gk.off_out, False, "wgrad_out")
        grads = _wgrad(dproj, h, grads, 1, gk.off_in, False, "wgrad_in")
        waiting = (l, grads.reshape(N_CHIP, 2, gk.rows, d))
        pending = None
        dmod_rows[l] = jnp.concatenate([ms0[:, 0], ms0[:, 1], ms1[:, 0], ms2[:, 0], ms2[:, 1], ms2[:, 2]], axis=1)
    part = _add_sibling(waiting[1], _sibling_exchange(waiting[1]), core)
    reduced(waiting[0], _sum_blocks(_chip_all_to_all(part), "sum_chips"))

    small_all = _all_gather(jnp.concatenate(small_red, axis=0), "ag_small_grads")
    lay = small_all.reshape(N_DEV, nl, SMALL_ROWS, d).transpose(1, 0, 2, 3).reshape(nl, N_DEV * SMALL_ROWS, d)
    n_dm = nl * nb * 6
    tail = jnp.concatenate([jnp.stack(dmod_rows).reshape(n_dm, d), dfg], axis=0)
    tail_all = _all_gather(tail, "ag_dmod")
    dmod_all = tail_all[:, :n_dm].reshape(N_DEV, nl, nb, ncol).transpose(1, 0, 2, 3).reshape(nl, bsz, ncol)
    dmod_cols = lax.dynamic_slice_in_dim(dmod_all, dev * cols, cols, axis=2)
    g_w_ada, g_b_ada = _mod_bwd(c_all, dmod_cols, dmod_all)
    g_final = _sum_blocks(tail_all[:, n_dm:], "sum_final_g")[0]

    g_small = {
        "norm1_g": lay[:, 0], "a_ln_g": lay[:, 1], "a_ln_b": lay[:, 2], "b_ln_g": lay[:, 3], "b_ln_b": lay[:, 4],
        "b_conv_b": lay[:, 5], "norm2_g": lay[:, 6],
        "a_ws": lay[:, 8:8 + CHUNK].reshape(nl, CHUNK, HEADS, CHUNK).transpose(0, 2, 1, 3),
        "a_bs": jnp.swapaxes(lay[:, 8 + CHUNK:8 + 2 * CHUNK, ::CHUNK], 1, 2),
        "b_conv_w": lax.dynamic_slice_in_dim(
            lay[:, 8 + 2 * CHUNK:8 + 2 * CHUNK + CONV_TAPS], dev * cpd, cpd, axis=2).reshape(nl, CONV_TAPS, 1, cpd),
        "final_g": g_final,
    }
    grads = dict(g_small)
    grads["w_ada"] = g_w_ada
    grads["b_ada"] = g_b_ada.reshape(nl, ncol)
    for k, v in wg.items():
        grads[k] = jnp.stack(v)

    names = ["w_ada", "b_ada", "norm1_g", "w_in", "a_ln_g", "a_ln_b", "a_ws", "a_bs", "w_pa", "b_conv_w", "b_conv_b",
             "b_ln_g", "b_ln_b", "w_pb", "w_out", "norm2_g", "w_ff1", "w_ff2", "final_g"]
    weights = dict(w_ada=w_ada, b_ada=b_ada, norm1_g=norm1_g, w_in=w_in, a_ln_g=a_ln_g, a_ln_b=a_ln_b, a_ws=a_ws,
                   a_bs=a_bs, w_pa=w_pa, b_conv_w=b_conv_w, b_conv_b=b_conv_b, b_ln_g=b_ln_g, b_ln_b=b_ln_b,
                   w_pb=w_pb, w_out=w_out, norm2_g=norm2_g, w_ff1=w_ff1, w_ff2=w_ff2, final_g=final_g)
    m_in = dict(w_ada=m_w_ada, b_ada=m_b_ada, norm1_g=m_norm1_g, w_in=m_w_in, a_ln_g=m_a_ln_g, a_ln_b=m_a_ln_b,
                a_ws=m_a_ws, a_bs=m_a_bs, w_pa=m_w_pa, b_conv_w=m_b_conv_w, b_conv_b=m_b_conv_b, b_ln_g=m_b_ln_g,
                b_ln_b=m_b_ln_b, w_pb=m_w_pb, w_out=m_w_out, norm2_g=m_norm2_g, w_ff1=m_w_ff1, w_ff2=m_w_ff2,
                final_g=m_final_g)
    v_in = dict(w_ada=v_w_ada, b_ada=v_b_ada, norm1_g=v_norm1_g, w_in=v_w_in, a_ln_g=v_a_ln_g, a_ln_b=v_a_ln_b,
                a_ws=v_a_ws, a_bs=v_a_bs, w_pa=v_w_pa, b_conv_w=v_b_conv_w, b_conv_b=v_b_conv_b, b_ln_g=v_b_ln_g,
                b_ln_b=v_b_ln_b, w_pb=v_w_pb, w_out=v_w_out, norm2_g=v_norm2_g, w_ff1=v_w_ff1, w_ff2=v_w_ff2,
                final_g=v_final_g)

    deltas, new_m, new_v = {}, {}, {}
    for k in names:
        shape = weights[k].shape
        two_d = (-1, shape[-1])
        g2d = grads[k].reshape(shape).reshape(two_d)
        grads[k] = grads[k].reshape(shape)
        dl, nm, nv = _adamw(weights[k].reshape(two_d), g2d, m_in[k].reshape(two_d), v_in[k].reshape(two_d),
                            "adamw_" + k)
        deltas[k], new_m[k], new_v[k] = dl.reshape(shape), nm.reshape(shape), nv.reshape(shape)

    return (loss, dx.reshape(nb, seq, d), *[grads[k] for k in names], *[deltas[k] for k in names],
            *[new_m[k] for k in names], *[new_v[k] for k in names])
```

```python
import functools

import jax
import jax.numpy as jnp
from jax import lax
from jax.experimental import pallas as pl
from jax.experimental.pallas import tpu as pltpu

F32 = jnp.float32
BF16 = jnp.bfloat16
MESH = pl.DeviceIdType.MESH
ANY = pl.BlockSpec(memory_space=pl.ANY)

N_DEV = 8
N_CHIP = 4
EPS = 1e-6
CHUNK = 128
HEADS = 8
CONV_TAPS = 31
HALO = 32
SMALL_ROWS = 40
SMALL_SLOT = 128
CONV_ROWS = 32
CONV_WGRAD_TAPS = 4
TM_BIG = 512
TM_MIX = 256
CONV_SUB = 256
TK_WGRAD = 4096
TK_WGRAD_ONE = 1024
VMEM_LIMIT = 56 * 1024 * 1024

ADAM_LR = 0.001
ADAM_B1 = 0.9
ADAM_B2 = 0.999
ADAM_EPS = 1e-08
ADAM_WD = 0.01
ADAM_STEP = 10


def _sds(shape, dtype):
    return jax.ShapeDtypeStruct(tuple(shape), dtype)


def _params(n_grid, vmem=VMEM_LIMIT):
    return pltpu.CompilerParams(dimension_semantics=("arbitrary",) * n_grid, vmem_limit_bytes=vmem)


def _nn(a, b):
    return jnp.dot(a, b, preferred_element_type=F32)


def _nt(a, b):
    return lax.dot_general(a, b, (((1,), (1,)), ((), ())), preferred_element_type=F32)


def _tn(a, b):
    return lax.dot_general(a, b, (((0,), (0,)), ((), ())), preferred_element_type=F32)


def _rowsum(v):
    return jnp.sum(v, axis=0, keepdims=True)


def _mean(v):
    return jnp.mean(v, axis=-1, keepdims=True)


def _add_row(ref, idx, val):
    ref[idx] = ref[idx] + val


def _ln_stats(v):
    mu = _mean(v)
    xc = v - mu
    rs = lax.rsqrt(_mean(xc * xc) + EPS)
    return xc * rs, rs


def _ln_bwd(dout, g, vhat, rs):
    dvh = dout * g
    return rs * (dvh - _mean(dvh) - vhat * _mean(dvh * vhat))


def _rms_bwd(dn, g, x, r):
    gd = dn * g
    return r * gd - x * (r * r * r) * _mean(x * gd)


class _Pack:
    def __init__(self, d, small_slot):
        self.n_in = 6 * d // N_DEV
        self.n_ff = 4 * d // N_DEV
        self.n_p = d // N_DEV
        self.off_ff1 = 0
        self.off_ff2 = self.off_ff1 + self.n_ff
        self.off_pa = self.off_ff2 + self.n_ff
        self.off_pb = self.off_pa + self.n_p
        self.off_out = self.off_pb + self.n_p
        self.off_small = self.off_out + self.n_p
        self.off_in = self.off_small + small_slot
        self.rows = self.off_in + self.n_in
        if small_slot:
            assert self.off_in % self.n_in == 0 and self.off_ff2 % self.n_ff == 0 and self.off_small % small_slot == 0


def _load_rows(g_hbm, w_vm, sems, sem0, off, rows):
    cps = [
        pltpu.make_async_copy(g_hbm.at[k, pl.ds(off, rows), :], w_vm.at[pl.ds(k * rows, rows), :], sems.at[sem0 + k])
        for k in range(N_DEV)
    ]
    for cp in cps:
        cp.start()
    for cp in cps:
        cp.wait()


def _row_spec(tm, cols, colblk=0):
    return pl.BlockSpec((tm, cols), lambda i: (i, colblk))


def _const_spec(shape):
    nd = len(shape)
    return pl.BlockSpec(tuple(shape), lambda i: (0,) * nd)


def _mod_spec(tps, cols):
    return pl.BlockSpec((1, 1, cols), lambda i: (i // tps, 0, 0))


def _mstat_spec(tps, d):
    return pl.BlockSpec((1, 8, d), lambda i: (i // tps, 0, 0))


def _position():
    return lax.axis_index("x"), lax.axis_index("y"), lax.axis_index("c")


def _other_chips(x, y):
    return [(1 - x, y), (x, 1 - y), (1 - x, 1 - y)]


def _remote(src, dst, send_sems, recv_sems, k, to):
    return pltpu.make_async_remote_copy(src_ref=src, dst_ref=dst, send_sem=send_sems.at[k], recv_sem=recv_sems.at[k],
                                        device_id=to, device_id_type=MESH)


def _slot(ref, p):
    return ref.at[4 * p[0] + 2 * p[1] + p[2]]


def _ag_stage1(x_ref, out_ref, send_sems, recv_sems, local_sem):
    x, y, c = _position()
    me = (x, y, c)
    peers = [(x, y, 1 - c)] + [(*chip, c) for chip in _other_chips(x, y)]
    sends = [_remote(x_ref, _slot(out_ref, me), send_sems, recv_sems, k, p) for k, p in enumerate(peers)]
    recvs = [_remote(x_ref, _slot(out_ref, p), send_sems, recv_sems, k, p) for k, p in enumerate(peers)]
    return pltpu.make_async_copy(x_ref, _slot(out_ref, me), local_sem), sends, recvs


def _ag_stage2(in_ref, out_ref, send_sems, recv_sems):
    x, y, c = _position()
    sibling = (x, y, 1 - c)
    chips = _other_chips(x, y)
    sends = [_remote(_slot(in_ref, (*ch, c)), _slot(out_ref, (*ch, c)), send_sems, recv_sems, j, sibling)
             for j, ch in enumerate(chips)]
    recvs = [_remote(_slot(in_ref, (*ch, c)), _slot(out_ref, (*ch, 1 - c)), send_sems, recv_sems, j, sibling)
             for j, ch in enumerate(chips)]
    return sends, recvs


def _rs_chip_copies(x_ref, out_ref, send_sems, recv_sems, local_sem):
    x, y, c = _position()
    q_me = 2 * x + y
    chips = _other_chips(x, y)
    sends = [_remote(x_ref.at[2 * px + py], out_ref.at[q_me], send_sems, recv_sems, j, (px, py, c))
             for j, (px, py) in enumerate(chips)]
    recvs = [_remote(x_ref.at[q_me], out_ref.at[2 * px + py], send_sems, recv_sems, j, (px, py, c))
             for j, (px, py) in enumerate(chips)]
    return pltpu.make_async_copy(x_ref.at[q_me], out_ref.at[q_me], local_sem), sends, recvs


def _start_all(local, sends):
    if local is not None:
        local.start()
    for cp in sends:
        cp.start()


def _finish_all(local, sends, recvs):
    for cp in recvs:
        cp.wait_recv()
    for cp in sends:
        cp.wait_send()
    if local is not None:
        local.wait()


def _sem_scratch(n, local):
    out = [pltpu.SemaphoreType.DMA((n,)), pltpu.SemaphoreType.DMA((n,))]
    return out + ([pltpu.SemaphoreType.DMA(())] if local else [])


def _fwd_in(x2d, mod3, g1, cwb, cb, gw, pk, seq, tm, nxt=None):
    t, d = x2d.shape
    nc = 6 * d
    tps = seq // tm
    n = t // tm
    lo = HALO - (CONV_TAPS - 1)
    sub = min(CONV_SUB, tm)
    comm = nxt is not None

    def body(*refs):
        x_ref, mod_ref, g_ref, cw_ref, cb_ref, gw_hbm = refs[:6]
        refs = refs[6:]
        if comm:
            nx_ref, refs = refs[0], refs[1:]
        proj_ref, zc_ref = refs[:2]
        refs = refs[2:]
        if comm:
            gwn_ref, refs = refs[0], refs[1:]
        w_vm, sems, zext, zsh, zc_buf, ztail = refs[:6]
        i = pl.program_id(0)

        if comm:
            ag = functools.partial(_ag_stage1, nx_ref, gwn_ref, *refs[6:9])

            @pl.when(i == 0)
            def _():
                local, sends, _ = ag()
                _start_all(local, sends)

        @pl.when(i == 0)
        def _():
            _load_rows(gw_hbm, w_vm, sems, 0, pk.off_in, pk.n_in)
            ztail[...] = jnp.zeros(ztail.shape, F32)

        x = x_ref[...]
        m = mod_ref[0]
        r = lax.rsqrt(_mean(x * x) + EPS)
        h = (x * r * g_ref[...] * (1.0 + m[:, d:2 * d]) + m[:, 0:d]).astype(BF16)

        def chunk(j):
            proj_ref[:, j * 512:(j + 1) * 512] = _nt(h, w_vm[j * 512:(j + 1) * 512, :]).astype(BF16)

        glu = range(2 * d // 512, 4 * d // 512)
        for j in glu:
            chunk(j)
        z = proj_ref[:, 2 * d:3 * d].astype(F32) * jax.nn.sigmoid(proj_ref[:, 3 * d:4 * d].astype(F32))
        zext[pl.ds(0, HALO), :] = jnp.where((i % tps) == 0, 0.0, ztail[...])
        zext[pl.ds(HALO, tm), :] = z
        ztail[...] = zext[pl.ds(tm, HALO), :]
        for s in range(tm // sub):
            _shift_copies(zext, zsh, s * sub, sub + HALO - 8)
            _conv_taps(zext, zsh, s * sub, cw_ref, [lo + k for k in range(CONV_TAPS)], cb_ref[...], zc_buf, sub, d)
        zc_ref[...] = zc_buf[...].astype(BF16)
        for j in range(nc // 512):
            if j not in glu:
                chunk(j)

        if comm:
            @pl.when(i == n - 1)
            def _():
                _finish_all(*ag())

    return pl.pallas_call(
        body, grid=(n,),
        in_specs=[_row_spec(tm, d), _mod_spec(tps, nc), _const_spec((1, d)), _const_spec((8 * HALO, d)),
                  _const_spec((1, d)), ANY] + ([ANY] if comm else []),
        out_specs=[_row_spec(tm, nc), _row_spec(tm, d)] + ([ANY] if comm else []),
        out_shape=[_sds((t, nc), BF16), _sds((t, d), BF16)] + ([_sds((N_DEV,) + nxt.shape, nxt.dtype)] if comm else []),
        scratch_shapes=[pltpu.VMEM((nc, d), BF16), pltpu.SemaphoreType.DMA((N_DEV,)),
                        pltpu.VMEM((tm + HALO, d), F32), pltpu.VMEM((7, sub + HALO, d), F32),
                        pltpu.VMEM((tm, d), F32), pltpu.VMEM((HALO, d), F32)]
        + (_sem_scratch(4, True) if comm else []),
        compiler_params=_params(1), name="fwd_in_ag" if comm else "fwd_in",
    )(x2d, mod3, g1, cwb, cb, gw, *([nxt] if comm else []))


def _shift_copies(src, sh, base, rows):
    for r in range(1, 8):
        sh[r - 1, pl.ds(0, rows), :] = src[pl.ds(base + r, rows), :]


def _window(src, sh, base, offset, start, size):
    r, q = offset % 8, offset // 8
    if r == 0:
        return src[pl.ds(base + start + 8 * q, size), :]
    return sh[r - 1, pl.ds(start + 8 * q, size), :]


def _conv_taps(src, sh, base, cwb_ref, offsets, bias, out_ref, rows, d):
    nsub = CONV_ROWS // 8
    for rb in range(rows // CONV_ROWS):
        accs = [jnp.broadcast_to(bias, (8, d))] * nsub
        for k in range(CONV_TAPS):
            w8 = cwb_ref[pl.ds(8 * k, 8), :]
            accs = [a + w8 * _window(src, sh, base, offsets[k], rb * CONV_ROWS + 8 * j, 8) for j, a in enumerate(accs)]
        for j, a in enumerate(accs):
            out_ref[pl.ds(base + rb * CONV_ROWS + 8 * j, 8), :] = a


def _conv_wgrad(z_ref, dsrc, dsh, acc_ref, rows, d):
    for k0 in range(0, CONV_TAPS, CONV_WGRAD_TAPS):
        taps = list(range(k0, min(k0 + CONV_WGRAD_TAPS, CONV_TAPS)))
        accs = [jnp.zeros((8, d), F32)] * len(taps)
        for rb in range(rows // 8):
            zblk = z_ref[pl.ds(rb * 8, 8), :]
            accs = [a + zblk * _window(dsrc, dsh, 0, CONV_TAPS - 1 - k, rb * 8, 8) for a, k in zip(accs, taps)]
        for a, k in zip(accs, taps):
            acc_ref[pl.ds(8 * k, 8), :] = acc_ref[pl.ds(8 * k, 8), :] + a


def _fwd_mix(proj, zc, x2d, mod3, vecs, wm, bsx, gw, pk, seq, tm, nxt=None):
    t, d = x2d.shape
    tps = seq // tm
    n = t // tm
    comm = nxt is not None

    def body(*refs):
        proj_ref, zc_ref, x_ref, mod_ref, vec_ref, wm_ref, bs_ref, gw_hbm = refs[:8]
        refs = refs[8:]
        if comm:
            nx_ref, refs = refs[0], refs[1:]
        x1_ref, ya_ref, yb_ref, o1_ref = refs[:4]
        refs = refs[4:]
        if comm:
            gwn_ref, refs = refs[0], refs[1:]
        wpa, wpb, wout, sems, vn_buf, a_buf = refs[:6]
        i = pl.program_id(0)

        if comm:
            ag = functools.partial(_ag_stage1, nx_ref, gwn_ref, *refs[6:9])

            @pl.when(i == 0)
            def _():
                local, sends, _ = ag()
                _start_all(local, sends)

        @pl.when(i == 0)
        def _():
            _load_rows(gw_hbm, wpa, sems, 0, pk.off_pa, pk.n_p)
            _load_rows(gw_hbm, wpb, sems, N_DEV, pk.off_pb, pk.n_p)
            _load_rows(gw_hbm, wout, sems, 2 * N_DEV, pk.off_out, pk.n_p)

        m = mod_ref[0]
        vhat, _ = _ln_stats(proj_ref[:, d:2 * d].astype(F32))
        vn_buf[...] = (vhat * vec_ref[0:1, :] + vec_ref[1:2, :]).astype(BF16)
        for c in range(tm // CHUNK):
            rs_ = slice(c * CHUNK, (c + 1) * CHUNK)
            for h in range(HEADS):
                cs_ = slice(h * CHUNK, (h + 1) * CHUNK)
                s_b = _nn(wm_ref[h], vn_buf[rs_, cs_]) + bs_ref[:, cs_]
                a_buf[rs_, cs_] = (proj_ref[rs_, cs_].astype(F32) * s_b).astype(BF16)
        y_a = _nn(a_buf[...], wpa[...])
        ya_ref[...] = y_a.astype(BF16)
        zhat, _ = _ln_stats(zc_ref[...].astype(F32))
        zn = zhat * vec_ref[3:4, :] + vec_ref[4:5, :]
        b_act = (zn * jax.nn.sigmoid(zn)).astype(BF16)
        y_b = _nn(b_act, wpb[...])
        yb_ref[...] = y_b.astype(BF16)
        merged = (jax.nn.sigmoid(proj_ref[:, 4 * d:5 * d].astype(F32)) * y_a
                  + jax.nn.sigmoid(proj_ref[:, 5 * d:6 * d].astype(F32)) * y_b).astype(BF16)
        o1 = _nn(merged, wout[...])
        o1_ref[...] = o1.astype(BF16)
        x1_ref[...] = x_ref[...] + m[:, 2 * d:3 * d] * o1

        if comm:
            @pl.when(i == n - 1)
            def _():
                _finish_all(*ag())

    act = _sds((t, d), BF16)
    return pl.pallas_call(
        body, grid=(n,),
        in_specs=[_row_spec(tm, 6 * d), _row_spec(tm, d), _row_spec(tm, d), _mod_spec(tps, 6 * d), _const_spec((8, d)),
                  _const_spec((HEADS, CHUNK, CHUNK)), _const_spec((CHUNK, d)), ANY]
        + ([ANY] if comm else []),
        out_specs=[_row_spec(tm, d)] * 4 + ([ANY] if comm else []),
        out_shape=[_sds((t, d), F32), act, act, act]
        + ([_sds((N_DEV,) + nxt.shape, nxt.dtype)] if comm else []),
        scratch_shapes=[pltpu.VMEM((d, d), BF16), pltpu.VMEM((d, d), BF16), pltpu.VMEM((d, d), BF16),
                        pltpu.SemaphoreType.DMA((3 * N_DEV,)),
                        pltpu.VMEM((tm, d), BF16), pltpu.VMEM((tm, d), BF16)]
        + (_sem_scratch(4, True) if comm else []),
        compiler_params=_params(1), name="fwd_mix_ag" if comm else "fwd_mix",
    )(proj, zc, x2d, mod3, vecs, wm, bsx, gw, *([nxt] if comm else []))


def _fwd_ffn(x1, mod3, g2, gw, pk, seq, tm, gw_next=None):
    t, d = x1.shape
    nf = 4 * d
    tps = seq // tm
    n = t // tm
    comm = gw_next is not None

    def body(*refs):
        x_ref, mod_ref, g_ref, gw_hbm = refs[:4]
        refs = refs[4:]
        if comm:
            gwn_in, refs = refs[0], refs[1:]
        x2_ref, f_ref, o2_ref = refs[:3]
        refs = refs[3:]
        if comm:
            gwn_out, refs = refs[0], refs[1:]
        w1, w2, sems = refs[:3]
        i = pl.program_id(0)

        if comm:
            ag = functools.partial(_ag_stage2, gwn_in, gwn_out, *refs[3:5])

            @pl.when(i == 0)
            def _():
                _start_all(None, ag()[0])

        @pl.when(i == 0)
        def _():
            _load_rows(gw_hbm, w1, sems, 0, pk.off_ff1, pk.n_ff)
            _load_rows(gw_hbm, w2, sems, N_DEV, pk.off_ff2, pk.n_ff)

        x = x_ref[...]
        m = mod_ref[0]
        r = lax.rsqrt(_mean(x * x) + EPS)
        h2 = (x * r * g_ref[...] * (1.0 + m[:, 4 * d:5 * d]) + m[:, 3 * d:4 * d]).astype(BF16)
        acc = jnp.zeros(x.shape, F32)
        for j in range(nf // 512):
            js = slice(j * 512, (j + 1) * 512)
            f = _nt(h2, w1[js, :])
            f_ref[:, js] = f.astype(BF16)
            acc = acc + _nn(jnp.square(jnp.maximum(f, 0.0)).astype(BF16), w2[js, :])
        o2_ref[...] = acc.astype(BF16)
        x2_ref[...] = x + m[:, 5 * d:6 * d] * acc

        if comm:
            @pl.when(i == n - 1)
            def _():
                _finish_all(None, *ag())

    return pl.pallas_call(
        body, grid=(n,),
        in_specs=[_row_spec(tm, d), _mod_spec(tps, 6 * d), _const_spec((1, d)), ANY] + ([ANY] if comm else []),
        out_specs=[_row_spec(tm, d), _row_spec(tm, nf), _row_spec(tm, d)] + ([ANY] if comm else []),
        out_shape=[_sds((t, d), F32), _sds((t, nf), BF16), _sds((t, d), BF16)]
        + ([_sds(gw_next.shape, gw_next.dtype)] if comm else []),
        scratch_shapes=[pltpu.VMEM((nf, d), BF16), pltpu.VMEM((nf, d), BF16), pltpu.SemaphoreType.DMA((2 * N_DEV,))]
        + (_sem_scratch(3, False) if comm else []),
        input_output_aliases={4: 3} if comm else {},
        compiler_params=_params(1), name="fwd_ffn_ag" if comm else "fwd_ffn",
    )(x1, mod3, g2, gw, *([gw_next] if comm else []))


def _loss_head(x, tgt, fg, tm):
    t, d = x.shape
    n = t // tm

    def body(x_ref, t_ref, g_ref, dx_ref, loss_ref, dg_ref, lacc):
        i = pl.program_id(0)

        @pl.when(i == 0)
        def _():
            lacc[...] = jnp.zeros(lacc.shape, F32)
            dg_ref[...] = jnp.zeros(dg_ref.shape, F32)

        xv = x_ref[...]
        g = g_ref[...]
        r = lax.rsqrt(_mean(xv * xv) + EPS)
        err = xv * r * g - t_ref[...]
        lacc[...] = lacc[...] + _rowsum(err * err)
        dy = err * (1.0 / d)
        _add_row(dg_ref, (slice(0, 1), slice(None)), _rowsum(dy * xv * r))
        dx_ref[...] = _rms_bwd(dy, g, xv, r)

        @pl.when(i == n - 1)
        def _():
            loss_ref[...] = jnp.broadcast_to(jnp.sum(lacc[...], keepdims=True) * (0.5 / d), loss_ref.shape)

    return pl.pallas_call(
        body, grid=(n,),
        in_specs=[_row_spec(tm, d), _row_spec(tm, d), _const_spec((1, d))],
        out_specs=[_row_spec(tm, d), _const_spec((8, 128)), _const_spec((8, d))],
        out_shape=[_sds((t, d), F32), _sds((8, 128), F32), _sds((8, d), F32)],
        scratch_shapes=[pltpu.VMEM((1, d), F32)],
        compiler_params=_params(1), name="loss_head",
    )(x, tgt, fg)


def _sibling_copy(x_ref, out_ref, send_sem, recv_sem):
    x, y, c = _position()
    return pltpu.make_async_remote_copy(
        src_ref=x_ref.at[pl.ds(0, x_ref.shape[0]), 1 - c], dst_ref=out_ref, send_sem=send_sem, recv_sem=recv_sem,
        device_id=(x, y, 1 - c), device_id_type=MESH)


def _bwd_ffn(dx2, x1, f, o2, mod3, g2, gw, pk, seq, tm, dp=None):
    t, d = x1.shape
    nf = 4 * d
    tps = seq // tm
    nb = t // seq
    steps = t // tm
    comm = dp is not None

    def body(*refs):
        dx2_ref, x_ref, f_ref, o2_ref, mod_ref, g_ref, gw_hbm = refs[:7]
        refs = refs[7:]
        if comm:
            dp_ref, refs = refs[0], refs[1:]
        dx1_ref, df_ref, do2_ref, h2_ref, ms_ref, ps_ref = refs[:6]
        refs = refs[6:]
        if comm:
            got_ref, refs = refs[0], refs[1:]
        w1, w2, sems = refs[:3]
        i = pl.program_id(0)

        if comm:
            swap = functools.partial(_sibling_copy, dp_ref, got_ref, *refs[3:5])

            @pl.when(i == 0)
            def _():
                swap().start()

        @pl.when(i == 0)
        def _():
            _load_rows(gw_hbm, w1, sems, 0, pk.off_ff1, pk.n_ff)
            _load_rows(gw_hbm, w2, sems, N_DEV, pk.off_ff2, pk.n_ff)
            ps_ref[...] = jnp.zeros(ps_ref.shape, F32)

        @pl.when((i % tps) == 0)
        def _():
            ms_ref[...] = jnp.zeros(ms_ref.shape, F32)

        dx2 = dx2_ref[...]
        x = x_ref[...]
        m = mod_ref[0]
        g = g_ref[...]
        sh2, sc2, gt2 = m[:, 3 * d:4 * d], m[:, 4 * d:5 * d], m[:, 5 * d:6 * d]
        _add_row(ms_ref, (0, slice(2, 3), slice(None)), _rowsum(dx2 * o2_ref[...].astype(F32)))
        do2 = (gt2 * dx2).astype(BF16)
        do2_ref[...] = do2
        r = lax.rsqrt(_mean(x * x) + EPS)
        n = x * r * g
        h2_ref[...] = (n * (1.0 + sc2) + sh2).astype(BF16)
        dh = jnp.zeros(x.shape, F32)
        for j in range(nf // 512):
            js = slice(j * 512, (j + 1) * 512)
            dr = _nt(do2, w2[js, :])
            df = (dr * (2.0 * jnp.maximum(f_ref[:, js].astype(F32), 0.0))).astype(BF16)
            df_ref[:, js] = df
            dh = dh + _nn(df, w1[js, :])
        _add_row(ms_ref, (0, slice(0, 1), slice(None)), _rowsum(dh))
        _add_row(ms_ref, (0, slice(1, 2), slice(None)), _rowsum(dh * n))
        dn = dh * (1.0 + sc2)
        _add_row(ps_ref, (slice(0, 1), slice(None)), _rowsum(dn * x * r))
        dx1_ref[...] = dx2 + _rms_bwd(dn, g, x, r)

        if comm:
            @pl.when(i == steps - 1)
            def _():
                swap().wait()

    act = _sds((t, d), BF16)
    return pl.pallas_call(
        body, grid=(steps,),
        in_specs=[_row_spec(tm, d), _row_spec(tm, d), _row_spec(tm, nf), _row_spec(tm, d), _mod_spec(tps, 6 * d),
                  _const_spec((1, d)), ANY] + ([ANY] if comm else []),
        out_specs=[_row_spec(tm, d), _row_spec(tm, nf), _row_spec(tm, d), _row_spec(tm, d), _mstat_spec(tps, d),
                   _const_spec((8, d))] + ([ANY] if comm else []),
        out_shape=[_sds((t, d), F32), _sds((t, nf), BF16), act, act, _sds((nb, 8, d), F32), _sds((8, d), F32)]
        + ([_sds((dp.shape[0],) + dp.shape[2:], dp.dtype)] if comm else []),
        scratch_shapes=[pltpu.VMEM((nf, d), BF16), pltpu.VMEM((nf, d), BF16), pltpu.SemaphoreType.DMA((2 * N_DEV,))]
        + ([pltpu.SemaphoreType.DMA(()), pltpu.SemaphoreType.DMA(())] if comm else []),
        compiler_params=_params(1), name="bwd_ffn_rs" if comm else "bwd_ffn",
    )(dx2, x1, f, o2, mod3, g2, gw, *([dp] if comm else []))


def _bwd_mix(dx1, proj, ya, yb, o1, zc, mod3, vecs, wm, wmt, bsx, gw, pk, seq, tm):
    t, d = dx1.shape
    tps = seq // tm
    nb = t // seq
    n = t // tm

    def body(dx1_ref, proj_ref, ya_ref, yb_ref, o1_ref, zc_ref, mod_ref, vec_ref, wm_ref, wmt_ref, bs_ref, gw_hbm,
             dp_ref, dzc_ref, do1_ref, dya_ref, dyb_ref, mg_ref, aa_ref, ba_ref, ms_ref, ps_ref, dws_ref, dbs_ref,
             wpa, wpb, wout, sems, vn_buf, da_buf, dvn_buf):
        i = pl.program_id(0)

        @pl.when(i == 0)
        def _():
            _load_rows(gw_hbm, wpa, sems, 0, pk.off_pa, pk.n_p)
            _load_rows(gw_hbm, wpb, sems, N_DEV, pk.off_pb, pk.n_p)
            _load_rows(gw_hbm, wout, sems, 2 * N_DEV, pk.off_out, pk.n_p)
            ps_ref[...] = jnp.zeros(ps_ref.shape, F32)
            dws_ref[...] = jnp.zeros(dws_ref.shape, F32)
            dbs_ref[...] = jnp.zeros(dbs_ref.shape, F32)

        @pl.when((i % tps) == 0)
        def _():
            ms_ref[...] = jnp.zeros(ms_ref.shape, F32)

        m = mod_ref[0]
        dx1v = dx1_ref[...]
        _add_row(ms_ref, (0, slice(0, 1), slice(None)), _rowsum(dx1v * o1_ref[...].astype(F32)))
        do1 = (m[:, 2 * d:3 * d] * dx1v).astype(BF16)
        do1_ref[...] = do1
        dmg = _nt(do1, wout[...])
        sa = jax.nn.sigmoid(proj_ref[:, 4 * d:5 * d].astype(F32))
        sb = jax.nn.sigmoid(proj_ref[:, 5 * d:6 * d].astype(F32))
        y_a = ya_ref[...].astype(F32)
        y_b = yb_ref[...].astype(F32)
        dya = (dmg * sa).astype(BF16)
        dyb = (dmg * sb).astype(BF16)
        dya_ref[...] = dya
        dyb_ref[...] = dyb
        dp_ref[:, 4 * d:5 * d] = (dmg * y_a * sa * (1.0 - sa)).astype(BF16)
        dp_ref[:, 5 * d:6 * d] = (dmg * y_b * sb * (1.0 - sb)).astype(BF16)
        mg_ref[...] = (sa * y_a + sb * y_b).astype(BF16)
        da_buf[...] = _nt(dya, wpa[...])
        db = _nt(dyb, wpb[...])
        vhat, rs = _ln_stats(proj_ref[:, d:2 * d].astype(F32))
        alg = vec_ref[0:1, :]
        vn_buf[...] = (vhat * alg + vec_ref[1:2, :]).astype(BF16)
        for c in range(tm // CHUNK):
            rs_ = slice(c * CHUNK, (c + 1) * CHUNK)
            for h in range(HEADS):
                cs_ = slice(h * CHUNK, (h + 1) * CHUNK)
                vn_b = vn_buf[rs_, cs_]
                s_b = _nn(wm_ref[h], vn_b) + bs_ref[:, cs_]
                u_b = proj_ref[rs_, cs_].astype(F32)
                da_b = da_buf[rs_, cs_]
                aa_ref[rs_, cs_] = (u_b * s_b).astype(BF16)
                dp_ref[rs_, cs_] = (da_b * s_b).astype(BF16)
                ds_b = da_b * u_b
                dbs_ref[:, cs_] = dbs_ref[:, cs_] + ds_b
                ds_bf = ds_b.astype(BF16)
                dvn_buf[rs_, cs_] = _nn(wmt_ref[h], ds_bf)
                dws_ref[:, cs_] = dws_ref[:, cs_] + _nt(ds_bf, vn_b)
        dvn = dvn_buf[...]
        _add_row(ps_ref, (slice(0, 1), slice(None)), _rowsum(dvn * vhat))
        _add_row(ps_ref, (slice(1, 2), slice(None)), _rowsum(dvn))
        dp_ref[:, d:2 * d] = _ln_bwd(dvn, alg, vhat, rs).astype(BF16)
        dp_ref[:, 2 * d:4 * d] = jnp.zeros((tm, 2 * d), BF16)
        zhat, rsb = _ln_stats(zc_ref[...].astype(F32))
        blg = vec_ref[3:4, :]
        zn = zhat * blg + vec_ref[4:5, :]
        sg = jax.nn.sigmoid(zn)
        ba_ref[...] = (zn * sg).astype(BF16)
        dzn = db * (sg * (1.0 + zn * (1.0 - sg)))
        _add_row(ps_ref, (slice(2, 3), slice(None)), _rowsum(dzn * zhat))
        _add_row(ps_ref, (slice(3, 4), slice(None)), _rowsum(dzn))
        dzc = _ln_bwd(dzn, blg, zhat, rsb)
        _add_row(ps_ref, (slice(4, 5), slice(None)), _rowsum(dzc))
        dzc_ref[...] = dzc.astype(BF16)

        @pl.when(i == n - 1)
        def _():
            causal = (lax.broadcasted_iota(jnp.int32, (CHUNK, CHUNK), 0)
                      >= lax.broadcasted_iota(jnp.int32, (CHUNK, CHUNK), 1))
            for h in range(HEADS):
                cs_ = slice(h * CHUNK, (h + 1) * CHUNK)
                dws_ref[:, cs_] = jnp.where(causal, dws_ref[:, cs_], 0.0)
                dbs_ref[:, cs_] = jnp.broadcast_to(jnp.sum(dbs_ref[:, cs_], axis=1, keepdims=True), (CHUNK, CHUNK))

    act = _sds((t, d), BF16)
    return pl.pallas_call(
        body, grid=(n,),
        in_specs=[_row_spec(tm, d), _row_spec(tm, 6 * d), _row_spec(tm, d), _row_spec(tm, d), _row_spec(tm, d),
                  _row_spec(tm, d), _mod_spec(tps, 6 * d), _const_spec((8, d)), _const_spec((HEADS, CHUNK, CHUNK)),
                  _const_spec((HEADS, CHUNK, CHUNK)), _const_spec((CHUNK, d)), ANY],
        out_specs=[_row_spec(tm, 6 * d)] + [_row_spec(tm, d)] * 7
        + [_mstat_spec(tps, d), _const_spec((8, d)), _const_spec((CHUNK, d)), _const_spec((CHUNK, d))],
        out_shape=[_sds((t, 6 * d), BF16)] + [act] * 7
        + [_sds((nb, 8, d), F32), _sds((8, d), F32), _sds((CHUNK, d), F32), _sds((CHUNK, d), F32)],
        scratch_shapes=[pltpu.VMEM((d, d), BF16), pltpu.VMEM((d, d), BF16), pltpu.VMEM((d, d), BF16),
                        pltpu.SemaphoreType.DMA((3 * N_DEV,)),
                        pltpu.VMEM((tm, d), BF16), pltpu.VMEM((tm, d), F32), pltpu.VMEM((tm, d), F32)],
        compiler_params=_params(1), name="bwd_mix",
    )(dx1, proj, ya, yb, o1, zc, mod3, vecs, wm, wmt, bsx, gw)


def _bwd_in(dproj, dzc, proj, x2d, dx1, mod3, g1, cw, gw, pk, seq, tm, part=None):
    t, d = x2d.shape
    tps = seq // tm
    nb = t // seq
    n = t // tm
    hb = tm // HALO
    comm = part is not None

    def body(*refs):
        dpi_ref, dzc_ref, dzn_ref, pp_ref, x_ref, dx1_ref, mod_ref, g_ref, cw_ref, gw_hbm = refs[:10]
        refs = refs[10:]
        if comm:
            part_ref, refs = refs[0], refs[1:]
        dpo_ref, dx_ref, h_ref, ms_ref, ps_ref, dcw_ref = refs[:6]
        refs = refs[6:]
        if comm:
            got_ref, refs = refs[0], refs[1:]
        w_vm, sems, z_buf, dzext, dsh, dz_buf, dcw_acc = refs[:7]
        i = pl.program_id(0)

        if comm:
            rs = functools.partial(_rs_chip_copies, part_ref, got_ref, *refs[7:10])

            @pl.when(i == 0)
            def _():
                local, sends, _ = rs()
                _start_all(local, sends)

        @pl.when(i == 0)
        def _():
            _load_rows(gw_hbm, w_vm, sems, 0, pk.off_in, pk.n_in)
            ps_ref[...] = jnp.zeros(ps_ref.shape, F32)
            dcw_acc[...] = jnp.zeros(dcw_acc.shape, F32)

        last = (i % tps) == tps - 1

        @pl.when((i % tps) == 0)
        def _():
            ms_ref[...] = jnp.zeros(ms_ref.shape, F32)

        pa = pp_ref[:, 0:d].astype(F32)
        sgp = jax.nn.sigmoid(pp_ref[:, d:2 * d].astype(F32))
        z_buf[...] = pa * sgp
        dzext[pl.ds(0, tm), :] = dzc_ref[...].astype(F32)
        dzext[pl.ds(tm, HALO), :] = jnp.where(last, 0.0, dzn_ref[...].astype(F32))
        _shift_copies(dzext, dsh, 0, tm + HALO - 8)
        _conv_wgrad(z_buf, dzext, dsh, dcw_acc, tm, d)
        _conv_taps(dzext, dsh, 0, cw_ref, [CONV_TAPS - 1 - k for k in range(CONV_TAPS)], jnp.zeros((1, d), F32),
                   dz_buf, tm, d)
        dz = dz_buf[...]
        dpa = (dz * sgp).astype(BF16)
        dpg = (dz * pa * sgp * (1.0 - sgp)).astype(BF16)
        dpo_ref[:, 0:d] = dpa
        dpo_ref[:, d:2 * d] = dpg
        dh = (_nn(dpi_ref[:, 0:2 * d], w_vm[0:2 * d, :]) + _nn(dpa, w_vm[2 * d:3 * d, :])
              + _nn(dpg, w_vm[3 * d:4 * d, :]) + _nn(dpi_ref[:, 4 * d:6 * d], w_vm[4 * d:6 * d, :]))
        x = x_ref[...]
        m = mod_ref[0]
        g = g_ref[...]
        sh1, sc1 = m[:, 0:d], m[:, d:2 * d]
        r = lax.rsqrt(_mean(x * x) + EPS)
        nrm = x * r * g
        h_ref[...] = (nrm * (1.0 + sc1) + sh1).astype(BF16)
        _add_row(ms_ref, (0, slice(0, 1), slice(None)), _rowsum(dh))
        _add_row(ms_ref, (0, slice(1, 2), slice(None)), _rowsum(dh * nrm))
        dn = dh * (1.0 + sc1)
        _add_row(ps_ref, (slice(0, 1), slice(None)), _rowsum(dn * x * r))
        dx_ref[...] = dx1_ref[...] + _rms_bwd(dn, g, x, r)

        @pl.when(i == n - 1)
        def _():
            for k in range(CONV_TAPS):
                dcw_ref[k:k + 1, :] = _rowsum(dcw_acc[pl.ds(k * 8, 8), :])
            dcw_ref[CONV_TAPS:HALO, :] = jnp.zeros((HALO - CONV_TAPS, d), F32)

        if comm:
            @pl.when(i == n - 1)
            def _():
                _finish_all(*rs())

    halo_next = pl.BlockSpec((HALO, d), lambda i: (jnp.minimum((i + 1) * hb, t // HALO - 1), 0))
    return pl.pallas_call(
        body, grid=(n,),
        in_specs=[_row_spec(tm, 6 * d), _row_spec(tm, d), halo_next, _row_spec(tm, 2 * d, 1),
                  _row_spec(tm, d), _row_spec(tm, d), _mod_spec(tps, 6 * d), _const_spec((1, d)),
                  _const_spec((8 * HALO, d)), ANY] + ([ANY] if comm else []),
        out_specs=[_row_spec(tm, 2 * d, 1), _row_spec(tm, d), _row_spec(tm, d), _mstat_spec(tps, d),
                   _const_spec((8, d)), _const_spec((HALO, d))] + ([ANY] if comm else []),
        out_shape=[_sds((t, 6 * d), BF16), _sds((t, d), F32), _sds((t, d), BF16), _sds((nb, 8, d), F32),
                   _sds((8, d), F32), _sds((HALO, d), F32)] + ([_sds(part.shape, part.dtype)] if comm else []),
        scratch_shapes=[pltpu.VMEM((6 * d, d), BF16), pltpu.SemaphoreType.DMA((N_DEV,)),
                        pltpu.VMEM((tm, d), F32), pltpu.VMEM((tm + HALO, d), F32),
                        pltpu.VMEM((7, tm + HALO, d), F32),
                        pltpu.VMEM((tm, d), F32), pltpu.VMEM((CONV_TAPS * 8, d), F32)]
        + (_sem_scratch(3, True) if comm else []),
        input_output_aliases={0: 0},
        compiler_params=_params(1), name="bwd_in_rs" if comm else "bwd_in",
    )(dproj, dzc, dzc, proj, x2d, dx1, mod3, g1, cw, gw, *([part] if comm else []))


def _wgrad(a, b, pack, kb, off, relu2, name, part=None):
    t, mo = a.shape
    nn_ = b.shape[1]
    rows = mo // N_DEV
    tk = min(TK_WGRAD_ONE if kb == N_DEV else TK_WGRAD, t)
    nk = t // tk
    ni = N_DEV // kb
    assert off % rows == 0 and N_DEV % kb == 0
    comm = part is not None

    def body(*refs):
        a_ref, b_ref, pack_hbm = refs[:3]
        refs = refs[3:]
        if comm:
            part_ref, refs = refs[0], refs[1:]
        o_ref, refs = refs[0], refs[1:]
        if comm:
            got_ref, refs = refs[0], refs[1:]
        acc = refs[0]
        i = pl.program_id(0)
        k = pl.program_id(1)

        if comm:
            rs = functools.partial(_rs_chip_copies, part_ref, got_ref, *refs[1:4])

            @pl.when((i == 0) & (k == 0))
            def _():
                local, sends, _ = rs()
                _start_all(local, sends)

        @pl.when(k == 0)
        def _():
            acc[...] = jnp.zeros(acc.shape, F32)

        av = a_ref[...]
        if relu2:
            av = jnp.square(jnp.maximum(av, 0.0))
        acc[...] = acc[...] + _tn(av, b_ref[...])

        @pl.when(k == nk - 1)
        def _():
            for j in range(kb):
                o_ref[j] = acc[pl.ds(j * rows, rows), :].astype(BF16)

        if comm:
            @pl.when((i == ni - 1) & (k == nk - 1))
            def _():
                _finish_all(*rs())

    return pl.pallas_call(
        body, grid=(ni, nk),
        in_specs=[pl.BlockSpec((tk, kb * rows), lambda i, k: (k, i)), pl.BlockSpec((tk, nn_), lambda i, k: (k, 0)), ANY]
        + ([ANY] if comm else []),
        out_specs=[pl.BlockSpec((kb, rows, nn_), lambda i, k: (i, off // rows, 0))] + ([ANY] if comm else []),
        out_shape=[_sds(pack.shape, BF16)] + ([_sds(part.shape, part.dtype)] if comm else []),
        scratch_shapes=[pltpu.VMEM((kb * rows, nn_), F32)] + (_sem_scratch(3, True) if comm else []),
        input_output_aliases={2: 0},
        compiler_params=_params(2), name=name,
    )(a, b, pack, *([part] if comm else []))


def _place_rows(pack, rows_blk, off):
    nblk, r, nn_ = rows_blk.shape
    assert off % r == 0

    def body(pack_hbm, s_ref, o_ref):
        o_ref[...] = s_ref[...]

    return pl.pallas_call(
        body, grid=(1,),
        in_specs=[ANY, pl.BlockSpec((nblk, r, nn_), lambda i: (0, 0, 0))],
        out_specs=pl.BlockSpec((nblk, r, nn_), lambda i: (0, off // r, 0)),
        out_shape=_sds(pack.shape, pack.dtype),
        input_output_aliases={0: 0},
        compiler_params=_params(1), name="place_small",
    )(pack, rows_blk)


def _mod_fwd(c_all, w_ada, b_cols):
    nl, d, cols = w_ada.shape
    bsz = c_all.shape[0]

    def body(c_ref, w_ref, b_ref, o_ref):
        cv = c_ref[...]
        ca = cv * jax.nn.sigmoid(cv)
        o_ref[0] = jnp.dot(ca, w_ref[0], preferred_element_type=F32, precision=lax.Precision.HIGHEST) + b_ref[0]

    return pl.pallas_call(
        body, grid=(nl,),
        in_specs=[_const_spec((bsz, d)), pl.BlockSpec((1, d, cols), lambda l: (l, 0, 0)),
                  pl.BlockSpec((1, 1, cols), lambda l: (l, 0, 0))],
        out_specs=pl.BlockSpec((1, bsz, cols), lambda l: (l, 0, 0)),
        out_shape=_sds((nl, bsz, cols), F32),
        compiler_params=_params(1), name="mod_fwd",
    )(c_all, w_ada, b_cols)


def _mod_bwd(c_all, dmod_cols, dmod_all):
    nl, bsz, cols = dmod_cols.shape
    d = c_all.shape[1]
    ncol = dmod_all.shape[2]

    def body(c_ref, dc_ref, da_ref, dw_ref, db_ref):
        cv = c_ref[...]
        ca = cv * jax.nn.sigmoid(cv)
        dw_ref[0] = lax.dot_general(ca, dc_ref[0], (((0,), (0,)), ((), ())), preferred_element_type=F32,
                                    precision=lax.Precision.HIGHEST)
        db_ref[0] = _rowsum(da_ref[0])

    return pl.pallas_call(
        body, grid=(nl,),
        in_specs=[_const_spec((bsz, d)), pl.BlockSpec((1, bsz, cols), lambda l: (l, 0, 0)),
                  pl.BlockSpec((1, bsz, ncol), lambda l: (l, 0, 0))],
        out_specs=[pl.BlockSpec((1, d, cols), lambda l: (l, 0, 0)), pl.BlockSpec((1, 1, ncol), lambda l: (l, 0, 0))],
        out_shape=[_sds((nl, d, cols), F32), _sds((nl, 1, ncol), F32)],
        compiler_params=_params(1), name="mod_bwd",
    )(c_all, dmod_cols, dmod_all)


def _row_tile(rows, cols, nbuf, itemsize=4, budget=24 * 1024 * 1024):
    cap = max(16, budget // (2 * nbuf * cols * itemsize))
    if rows <= cap:
        return rows
    best = None
    for tr in range(16, cap + 1, 16):
        if rows % tr == 0:
            best = tr
    assert best is not None, (rows, cols)
    return best


def _sum_blocks(xs, name):
    nblk, rows, cols = xs.shape
    tr = _row_tile(rows, cols, nblk + 1)

    def body(x_ref, o_ref):
        acc = x_ref[0].astype(F32)
        for j in range(1, nblk):
            acc = acc + x_ref[j].astype(F32)
        o_ref[...] = acc

    return pl.pallas_call(
        body, grid=(rows // tr,),
        in_specs=[pl.BlockSpec((nblk, tr, cols), lambda i: (0, i, 0))],
        out_specs=pl.BlockSpec((tr, cols), lambda i: (i, 0)),
        out_shape=_sds((rows, cols), F32),
        compiler_params=_params(1), name=name,
    )(xs)


def _add_sibling(dp, recv, core):
    nq, _, rows, cols = dp.shape
    tr = _row_tile(rows, cols, 3, itemsize=2)

    def body(c_ref, a_ref, b_ref, o_ref):
        o_ref[...] = (a_ref[...].astype(F32) + b_ref[...].astype(F32)).astype(BF16)

    return pl.pallas_call(
        body,
        grid_spec=pltpu.PrefetchScalarGridSpec(
            num_scalar_prefetch=1, grid=(nq, rows // tr),
            in_specs=[pl.BlockSpec((1, 1, tr, cols), lambda q, i, c: (q, c[0], i, 0)),
                      pl.BlockSpec((1, 1, tr, cols), lambda q, i, c: (q, 0, i, 0))],
            out_specs=pl.BlockSpec((1, 1, tr, cols), lambda q, i, c: (q, 0, i, 0))),
        out_shape=_sds((nq, 1, rows, cols), BF16),
        compiler_params=_params(2), name="add_sibling",
    )(core, dp, recv.reshape(nq, 1, rows, cols)).reshape(nq, rows, cols)


def _adamw(w, g, m, v, name):
    rows, cols = w.shape
    tr = _row_tile(rows, cols, 7)
    c1 = 1.0 - ADAM_B1 ** ADAM_STEP
    c2 = 1.0 - ADAM_B2 ** ADAM_STEP

    def body(w_ref, g_ref, m_ref, v_ref, d_ref, nm_ref, nv_ref):
        gv = g_ref[...]
        nm = ADAM_B1 * m_ref[...] + (1.0 - ADAM_B1) * gv
        nv = ADAM_B2 * v_ref[...] + (1.0 - ADAM_B2) * (gv * gv)
        nm_ref[...] = nm
        nv_ref[...] = nv
        d_ref[...] = -ADAM_LR * ((nm / c1) / (jnp.sqrt(nv / c2) + ADAM_EPS) + ADAM_WD * w_ref[...])

    spec = pl.BlockSpec((tr, cols), lambda i: (i, 0))
    out = _sds((rows, cols), F32)
    return pl.pallas_call(
        body, grid=(rows // tr,), in_specs=[spec] * 4, out_specs=[spec] * 3, out_shape=[out] * 3,
        compiler_params=_params(1), name=name,
    )(w, g, m, v)


def _all_gather(xs, name):
    rows, cols = xs.shape

    def body(x_ref, out_ref, send1, recv1, local_sem, send2, recv2):
        local, first, arrivals = _ag_stage1(x_ref, out_ref, send1, recv1, local_sem)
        _start_all(local, first)
        passed, from_sibling = _ag_stage2(out_ref, out_ref, send2, recv2)
        for arrival, onward in zip(arrivals[1:], passed):
            arrival.wait_recv()
            onward.start()
        arrivals[0].wait_recv()
        _finish_all(local, first + passed, from_sibling)

    return pl.pallas_call(
        body, out_shape=_sds((N_DEV, rows, cols), xs.dtype), in_specs=[ANY], out_specs=ANY,
        scratch_shapes=_sem_scratch(4, True) + _sem_scratch(3, False), name=name,
    )(xs)


def _sibling_exchange(dp):
    nq, _, rows, cols = dp.shape

    def body(x_ref, out_ref, send_sem, recv_sem):
        cp = _sibling_copy(x_ref, out_ref, send_sem, recv_sem)
        cp.start()
        cp.wait()

    return pl.pallas_call(
        body, out_shape=_sds((nq, rows, cols), dp.dtype), in_specs=[ANY], out_specs=ANY,
        scratch_shapes=[pltpu.SemaphoreType.DMA(()), pltpu.SemaphoreType.DMA(())],
        name="rs_sibling",
    )(dp)


def _chip_all_to_all(xs):
    nq, rows, cols = xs.shape

    def body(x_ref, out_ref, send_sems, recv_sems, local_sem):
        local, sends, recvs = _rs_chip_copies(x_ref, out_ref, send_sems, recv_sems, local_sem)
        _start_all(local, sends)
        _finish_all(local, sends, recvs)

    return pl.pallas_call(
        body, out_shape=_sds((nq, rows, cols), xs.dtype), in_specs=[ANY], out_specs=ANY,
        scratch_shapes=_sem_scratch(3, True), name="rs_chips",
    )(xs)


def _pad_rows(a, rows):
    return jnp.pad(a, ((0, rows - a.shape[0]), (0, 0)))


def kernel(x, c, w_ada, b_ada, norm1_g, w_in, a_ln_g, a_ln_b, a_ws, a_bs, w_pa, b_conv_w, b_conv_b, b_ln_g, b_ln_b, w_pb, w_out, norm2_g, w_ff1, w_ff2, final_g, loss_target, m_w_ada, m_b_ada, m_norm1_g, m_w_in, m_a_ln_g, m_a_ln_b, m_a_ws, m_a_bs, m_w_pa, m_b_conv_w, m_b_conv_b, m_b_ln_g, m_b_ln_b, m_w_pb, m_w_out, m_norm2_g, m_w_ff1, m_w_ff2, m_final_g, v_w_ada, v_b_ada, v_norm1_g, v_w_in, v_a_ln_g, v_a_ln_b, v_a_ws, v_a_bs, v_w_pa, v_b_conv_w, v_b_conv_b, v_b_ln_g, v_b_ln_b, v_w_pb, v_w_out, v_norm2_g, v_w_ff1, v_w_ff2, v_final_g):
    nb, seq, d = x.shape
    nl = w_in.shape[0]
    t = nb * seq
    pk = _Pack(d, 0)
    gk = _Pack(d, SMALL_SLOT)
    assert d % (N_DEV * CHUNK) == 0 and d // HEADS == CHUNK and seq % CHUNK == 0
    tm_big = min(TM_BIG, seq)
    tm_mix = min(TM_MIX, seq)
    ax, ay, ac = _position()
    dev = 4 * ax + 2 * ay + ac
    ncol = 6 * d
    cols = ncol // N_DEV
    cpd = d // N_DEV
    bsz = nb * N_DEV

    cw_rows = nl * HALO
    small = jnp.concatenate([
        c.reshape(nb * d // CHUNK, CHUNK),
        jnp.pad(b_conv_w.reshape(nl, CONV_TAPS, cpd), ((0, 0), (0, HALO - CONV_TAPS), (0, 0))).reshape(cw_rows, cpd),
    ], axis=0)
    c_rows = nb * d // CHUNK
    small_all = _all_gather(small, "ag_small")
    c_all = small_all[:, :c_rows].reshape(bsz, d)
    cw_all = small_all[:, c_rows:].reshape(N_DEV, nl, HALO, cpd).transpose(1, 2, 0, 3).reshape(nl, HALO, d)
    cwb_all = jnp.repeat(cw_all, 8, axis=1)
    b_cols = lax.dynamic_slice_in_dim(b_ada, dev * cols, cols, axis=1).reshape(nl, 1, cols)
    mod_cols = _mod_fwd(c_all, w_ada, b_cols)
    mod_all = _all_gather(mod_cols.reshape(nl * bsz, cols), "ag_mod")
    mod_all = mod_all.reshape(N_DEV, nl, bsz, cols).transpose(1, 2, 0, 3).reshape(nl, bsz, ncol)
    mod_mine = lax.dynamic_slice_in_dim(mod_all, dev * nb, nb, axis=1)

    causal = jnp.tril(jnp.ones((CHUNK, CHUNK), bool))
    wm_all = jnp.where(causal[None, None], a_ws, 0.0)
    wm_bf = wm_all.astype(BF16)
    wmt_bf = jnp.swapaxes(wm_all, 2, 3).astype(BF16)
    bsx_all = jnp.broadcast_to(jnp.swapaxes(a_bs, 1, 2)[:, :, :, None], (nl, CHUNK, HEADS, CHUNK)).reshape(nl, CHUNK, d)

    def vec_rows(l):
        return _pad_rows(jnp.stack([a_ln_g[l], a_ln_b[l], b_conv_b[l], b_ln_g[l], b_ln_b[l]]), 8)

    def weight_block(l):
        return jnp.concatenate([
            w_ff1[l].T, w_ff2[l], w_pa[l], w_pb[l], w_out[l], w_in[l].T], axis=0).astype(BF16)

    xs = x.reshape(t, d)
    saved = []
    gw = _all_gather(weight_block(0), "ag_weights")
    for l in range(nl):
        nxt = weight_block(l + 1) if l + 1 < nl else None
        mod3 = mod_mine[l].reshape(nb, 1, ncol)
        vecs = vec_rows(l)
        proj, zc, *gw_next = _fwd_in(xs, mod3, norm1_g[l].reshape(1, d), cwb_all[l], b_conv_b[l].reshape(1, d), gw, pk,
                                     seq, tm_big, nxt)
        x1, ya, yb, o1 = _fwd_mix(proj, zc, xs, mod3, vecs, wm_bf[l], bsx_all[l], gw, pk, seq, tm_big)
        x2, f, o2, *gw_next = _fwd_ffn(x1, mod3, norm2_g[l].reshape(1, d), gw, pk, seq, tm_big, *gw_next)
        saved.append((xs, x1, proj, ya, yb, o1, zc, f, o2, gw, mod3, vecs))
        xs = x2
        if gw_next:
            gw = gw_next[0]

    dx, loss_blk, dfg = _loss_head(xs, loss_target.reshape(t, d), final_g.reshape(1, d), tm_big)
    loss = lax.psum(loss_blk[0, 0], ("x", "y", "c"))

    core = ac.reshape(1).astype(jnp.int32)
    wg = {k: [None] * nl for k in ("w_in", "w_ff1", "w_ff2", "w_pa", "w_pb", "w_out")}
    small_red = [None] * nl
    dmod_rows = [None] * nl
    per_layer = 8 + 2 * CHUNK + HALO
    assert per_layer <= N_DEV * SMALL_ROWS <= N_DEV * SMALL_SLOT

    rows_a = gk.off_in

    def reduced(l, red):
        reduced_a(l, red[:rows_a])
        wg["w_in"][l] = red[rows_a:].T

    def reduced_a(l, red):
        wg["w_ff1"][l] = red[gk.off_ff1:gk.off_ff1 + gk.n_ff].T
        wg["w_ff2"][l] = red[gk.off_ff2:gk.off_ff2 + gk.n_ff]
        wg["w_pa"][l] = red[gk.off_pa:gk.off_pa + gk.n_p]
        wg["w_pb"][l] = red[gk.off_pb:gk.off_pb + gk.n_p]
        wg["w_out"][l] = red[gk.off_out:gk.off_out + gk.n_p]
        small_red[l] = red[gk.off_small:gk.off_small + SMALL_ROWS]

    waiting = None
    pending = None
    for l in reversed(range(nl)):
        x0, x1, proj, ya, yb, o1, zc, f, o2, gw, mod3, vecs = saved[l]
        dx1, df, do2, h2, ms2, ps2, *got = _bwd_ffn(dx, x1, f, o2, mod3, norm2_g[l].reshape(1, d), gw, pk, seq, tm_big,
                                                    None if waiting is None else waiting[1])
        if waiting is not None:
            pending = (waiting[0], _add_sibling(waiting[1], got[0], core))
        (dproj, dzc, do1, dya, dyb, mg, aa, ba, ms1, ps1, dws, dbs) = _bwd_mix(
            dx1, proj, ya, yb, o1, zc, mod3, vecs, wm_bf[l], wmt_bf[l], bsx_all[l], gw, pk, seq, tm_mix)
        apart = l == 0
        grads = lax.empty((N_DEV, rows_a if apart else gk.rows, d), BF16)
        grads, = _wgrad(df, h2, grads, 2, gk.off_ff1, False, "wgrad_ff1")
        grads, = _wgrad(f, do2, grads, 2, gk.off_ff2, True, "wgrad_ff2")
        dproj, dx, h, ms0, ps0, dcw, *got = _bwd_in(
            dproj, dzc, proj, x0, dx1, mod3, norm1_g[l].reshape(1, d), cwb_all[l], gw, pk, seq, tm_mix,
            None if pending is None else pending[1])
        if pending is not None:
            reduced(pending[0], _sum_blocks(got[0], "sum_chips"))
        vec_g = jnp.concatenate([ps0[0:1], ps1[0:5], ps2[0:1], jnp.zeros((1, d), F32)], axis=0)
        small = _pad_rows(jnp.concatenate([vec_g, dws, dbs, dcw], axis=0), N_DEV * SMALL_ROWS)
        small = jnp.pad(small.reshape(N_DEV, SMALL_ROWS, d).astype(BF16), ((0, 0), (0, SMALL_SLOT - SMALL_ROWS), (0, 0)))
        grads = _place_rows(grads, small, gk.off_small)
        grads, = _wgrad(aa, dya, grads, N_DEV, gk.off_pa, False, "wgrad_pa")
        grads, = _wgrad(ba, dyb, grads, N_DEV, gk.off_pb, False, "wgrad_pb")
        grads, = _wgrad(mg, do1, grads, N_DEV, gk.off_out, False, "wgrad_out")
        if apart:
            dp_a = grads.reshape(N_CHIP, 2, rows_a, d)
            part_a = _add_sibling(dp_a, _sibling_exchange(dp_a), core)
            g_in, got_a = _wgrad(dproj, h, lax.empty((N_DEV, gk.n_in, d), BF16), 1, 0, False, "wgrad_in_rs", part_a)
            reduced_a(l, _sum_blocks(got_a, "sum_chips_a"))
            dp_b = g_in.reshape(N_CHIP, 2, gk.n_in, d)
            part_b = _add_sibling(dp_b, _sibling_exchange(dp_b), core)
            wg["w_in"][l] = _sum_blocks(_chip_all_to_all(part_b), "sum_chips_b").T
        else:
            grads, = _wgrad(dproj, h, grads, 1, gk.off_in, False, "wgrad_in")
            waiting = (l, grads.reshape(N_CHIP, 2, gk.rows, d))
        pending = None
        dmod_rows[l] = jnp.concatenate([ms0[:, 0], ms0[:, 1], ms1[:, 0], ms2[:, 0], ms2[:, 1], ms2[:, 2]], axis=1)

    small_all = _all_gather(jnp.concatenate(small_red, axis=0), "ag_small_grads")
    lay = small_all.reshape(N_DEV, nl, SMALL_ROWS, d).transpose(1, 0, 2, 3).reshape(nl, N_DEV * SMALL_ROWS, d)
    n_dm = nl * nb * 6
    tail = jnp.concatenate([jnp.stack(dmod_rows).reshape(n_dm, d), dfg], axis=0)
    tail_all = _all_gather(tail, "ag_dmod")
    dmod_all = tail_all[:, :n_dm].reshape(N_DEV, nl, nb, ncol).transpose(1, 0, 2, 3).reshape(nl, bsz, ncol)
    dmod_cols = lax.dynamic_slice_in_dim(dmod_all, dev * cols, cols, axis=2)
    g_w_ada, g_b_ada = _mod_bwd(c_all, dmod_cols, dmod_all)
    g_final = _sum_blocks(tail_all[:, n_dm:], "sum_final_g")[0]

    g_small = {
        "norm1_g": lay[:, 0], "a_ln_g": lay[:, 1], "a_ln_b": lay[:, 2], "b_ln_g": lay[:, 3], "b_ln_b": lay[:, 4],
        "b_conv_b": lay[:, 5], "norm2_g": lay[:, 6],
        "a_ws": lay[:, 8:8 + CHUNK].reshape(nl, CHUNK, HEADS, CHUNK).transpose(0, 2, 1, 3),
        "a_bs": jnp.swapaxes(lay[:, 8 + CHUNK:8 + 2 * CHUNK, ::CHUNK], 1, 2),
        "b_conv_w": lax.dynamic_slice_in_dim(
            lay[:, 8 + 2 * CHUNK:8 + 2 * CHUNK + CONV_TAPS], dev * cpd, cpd, axis=2).reshape(nl, CONV_TAPS, 1, cpd),
        "final_g": g_final,
    }
    grads = dict(g_small)
    grads["w_ada"] = g_w_ada
    grads["b_ada"] = g_b_ada.reshape(nl, ncol)
    for k, v in wg.items():
        grads[k] = jnp.stack(v)

    names = ["w_ada", "b_ada", "norm1_g", "w_in", "a_ln_g", "a_ln_b", "a_ws", "a_bs", "w_pa", "b_conv_w", "b_conv_b",
             "b_ln_g", "b_ln_b", "w_pb", "w_out", "norm2_g", "w_ff1", "w_ff2", "final_g"]
    weights = dict(w_ada=w_ada, b_ada=b_ada, norm1_g=norm1_g, w_in=w_in, a_ln_g=a_ln_g, a_ln_b=a_ln_b, a_ws=a_ws,
                   a_bs=a_bs, w_pa=w_pa, b_conv_w=b_conv_w, b_conv_b=b_conv_b, b_ln_g=b_ln_g, b_ln_b=b_ln_b,
                   w_pb=w_pb, w_out=w_out, norm2_g=norm2_g, w_ff1=w_ff1, w_ff2=w_ff2, final_g=final_g)
    m_in = dict(w_ada=m_w_ada, b_ada=m_b_ada, norm1_g=m_norm1_g, w_in=m_w_in, a_ln_g=m_a_ln_g, a_ln_b=m_a_ln_b,
                a_ws=m_a_ws, a_bs=m_a_bs, w_pa=m_w_pa, b_conv_w=m_b_conv_w, b_conv_b=m_b_conv_b, b_ln_g=m_b_ln_g,
                b_ln_b=m_b_ln_b, w_pb=m_w_pb, w_out=m_w_out, norm2_g=m_norm2_g, w_ff1=m_w_ff1, w_ff2=m_w_ff2,
                final_g=m_final_g)
    v_in = dict(w_ada=v_w_ada, b_ada=v_b_ada, norm1_g=v_norm1_g, w_in=v_w_in, a_ln_g=v_a_ln_g, a_ln_b=v_a_ln_b,
                a_ws=v_a_ws, a_bs=v_a_bs, w_pa=v_w_pa, b_conv_w=v_b_conv_w, b_conv_b=v_b_conv_b, b_ln_g=v_b_ln_g,
                b_ln_b=v_b_ln_b, w_pb=v_w_pb, w_out=v_w_out, norm2_g=v_norm2_g, w_ff1=v_w_ff1, w_ff2=v_w_ff2,
                final_g=v_final_g)

    deltas, new_m, new_v = {}, {}, {}
    for k in names:
        shape = weights[k].shape
        two_d = (-1, shape[-1])
        g2d = grads[k].reshape(shape).reshape(two_d)
        grads[k] = grads[k].reshape(shape)
        dl, nm, nv = _adamw(weights[k].reshape(two_d), g2d, m_in[k].reshape(two_d), v_in[k].reshape(two_d),
                            "adamw_" + k)
        deltas[k], new_m[k], new_v[k] = dl.reshape(shape), nm.reshape(shape), nv.reshape(shape)

    return (loss, dx.reshape(nb, seq, d), *[grads[k] for k in names], *[deltas[k] for k in names],
            *[new_m[k] for k in names], *[new_v[k] for k in names])
```

```python
import functools

import jax
import jax.numpy as jnp
from jax import lax
from jax.experimental import pallas as pl
from jax.experimental.pallas import tpu as pltpu

F32 = jnp.float32
BF16 = jnp.bfloat16
MESH = pl.DeviceIdType.MESH
ANY = pl.BlockSpec(memory_space=pl.ANY)

N_DEV = 8
N_CHIP = 4
EPS = 1e-6
CHUNK = 128
HEADS = 8
CONV_TAPS = 31
HALO = 32
SMALL_ROWS = 40
SMALL_SLOT = 128
CONV_ROWS = 32
CONV_WGRAD_TAPS = 4
TM_BIG = 512
MM_COLS = 512
TM_MIX = 256
CONV_SUB = 256
TK_WGRAD = 4096
TK_WGRAD_ONE = 1024
VMEM_LIMIT = 56 * 1024 * 1024

ADAM_LR = 0.001
ADAM_B1 = 0.9
ADAM_B2 = 0.999
ADAM_EPS = 1e-08
ADAM_WD = 0.01
ADAM_STEP = 10


def _sds(shape, dtype):
    return jax.ShapeDtypeStruct(tuple(shape), dtype)


def _params(n_grid, vmem=VMEM_LIMIT):
    return pltpu.CompilerParams(dimension_semantics=("arbitrary",) * n_grid, vmem_limit_bytes=vmem)


def _nn(a, b):
    return jnp.dot(a, b, preferred_element_type=F32)


def _nt(a, b):
    return lax.dot_general(a, b, (((1,), (1,)), ((), ())), preferred_element_type=F32)


def _tn(a, b):
    return lax.dot_general(a, b, (((0,), (0,)), ((), ())), preferred_element_type=F32)


def _rowsum(v):
    return jnp.sum(v, axis=0, keepdims=True)


def _mean(v):
    return jnp.mean(v, axis=-1, keepdims=True)


def _add_row(ref, idx, val):
    ref[idx] = ref[idx] + val


def _ln_stats(v):
    mu = _mean(v)
    xc = v - mu
    rs = lax.rsqrt(_mean(xc * xc) + EPS)
    return xc * rs, rs


def _ln_bwd(dout, g, vhat, rs):
    dvh = dout * g
    return rs * (dvh - _mean(dvh) - vhat * _mean(dvh * vhat))


def _rms_bwd(dn, g, x, r):
    gd = dn * g
    return r * gd - x * (r * r * r) * _mean(x * gd)


class _Pack:
    def __init__(self, d, small_slot):
        self.n_in = 6 * d // N_DEV
        self.n_ff = 4 * d // N_DEV
        self.n_p = d // N_DEV
        self.off_ff1 = 0
        self.off_ff2 = self.off_ff1 + self.n_ff
        self.off_pa = self.off_ff2 + self.n_ff
        self.off_pb = self.off_pa + self.n_p
        self.off_out = self.off_pb + self.n_p
        self.off_small = self.off_out + self.n_p
        self.off_in = self.off_small + small_slot
        self.rows = self.off_in + self.n_in
        if small_slot:
            assert self.off_in % self.n_in == 0 and self.off_ff2 % self.n_ff == 0 and self.off_small % small_slot == 0


def _load_rows(g_hbm, w_vm, sems, sem0, off, rows):
    cps = [
        pltpu.make_async_copy(g_hbm.at[k, pl.ds(off, rows), :], w_vm.at[pl.ds(k * rows, rows), :], sems.at[sem0 + k])
        for k in range(N_DEV)
    ]
    for cp in cps:
        cp.start()
    for cp in cps:
        cp.wait()


def _row_spec(tm, cols, colblk=0):
    return pl.BlockSpec((tm, cols), lambda i: (i, colblk))


def _const_spec(shape):
    nd = len(shape)
    return pl.BlockSpec(tuple(shape), lambda i: (0,) * nd)


def _mod_spec(tps, cols):
    return pl.BlockSpec((1, 1, cols), lambda i: (i // tps, 0, 0))


def _mstat_spec(tps, d):
    return pl.BlockSpec((1, 8, d), lambda i: (i // tps, 0, 0))


def _position():
    return lax.axis_index("x"), lax.axis_index("y"), lax.axis_index("c")


def _other_chips(x, y):
    return [(1 - x, y), (x, 1 - y), (1 - x, 1 - y)]


def _remote(src, dst, send_sems, recv_sems, k, to):
    return pltpu.make_async_remote_copy(src_ref=src, dst_ref=dst, send_sem=send_sems.at[k], recv_sem=recv_sems.at[k],
                                        device_id=to, device_id_type=MESH)


def _slot(ref, p):
    return ref.at[4 * p[0] + 2 * p[1] + p[2]]


def _ag_stage1(x_ref, out_ref, send_sems, recv_sems, local_sem):
    x, y, c = _position()
    me = (x, y, c)
    peers = [(x, y, 1 - c)] + [(*chip, c) for chip in _other_chips(x, y)]
    sends = [_remote(x_ref, _slot(out_ref, me), send_sems, recv_sems, k, p) for k, p in enumerate(peers)]
    recvs = [_remote(x_ref, _slot(out_ref, p), send_sems, recv_sems, k, p) for k, p in enumerate(peers)]
    return pltpu.make_async_copy(x_ref, _slot(out_ref, me), local_sem), sends, recvs


def _ag_stage2(in_ref, out_ref, send_sems, recv_sems):
    x, y, c = _position()
    sibling = (x, y, 1 - c)
    chips = _other_chips(x, y)
    sends = [_remote(_slot(in_ref, (*ch, c)), _slot(out_ref, (*ch, c)), send_sems, recv_sems, j, sibling)
             for j, ch in enumerate(chips)]
    recvs = [_remote(_slot(in_ref, (*ch, c)), _slot(out_ref, (*ch, 1 - c)), send_sems, recv_sems, j, sibling)
             for j, ch in enumerate(chips)]
    return sends, recvs


def _rs_chip_copies(x_ref, out_ref, send_sems, recv_sems, local_sem):
    x, y, c = _position()
    q_me = 2 * x + y
    chips = _other_chips(x, y)
    sends = [_remote(x_ref.at[2 * px + py], out_ref.at[q_me], send_sems, recv_sems, j, (px, py, c))
             for j, (px, py) in enumerate(chips)]
    recvs = [_remote(x_ref.at[q_me], out_ref.at[2 * px + py], send_sems, recv_sems, j, (px, py, c))
             for j, (px, py) in enumerate(chips)]
    return pltpu.make_async_copy(x_ref.at[q_me], out_ref.at[q_me], local_sem), sends, recvs


def _start_all(local, sends):
    if local is not None:
        local.start()
    for cp in sends:
        cp.start()


def _finish_all(local, sends, recvs):
    for cp in recvs:
        cp.wait_recv()
    for cp in sends:
        cp.wait_send()
    if local is not None:
        local.wait()


def _sem_scratch(n, local):
    out = [pltpu.SemaphoreType.DMA((n,)), pltpu.SemaphoreType.DMA((n,))]
    return out + ([pltpu.SemaphoreType.DMA(())] if local else [])


def _fwd_in(x2d, mod3, g1, cwb, cb, gw, pk, seq, tm, nxt=None):
    t, d = x2d.shape
    nc = 6 * d
    tps = seq // tm
    n = t // tm
    lo = HALO - (CONV_TAPS - 1)
    sub = min(CONV_SUB, tm)
    comm = nxt is not None

    def body(*refs):
        x_ref, mod_ref, g_ref, cw_ref, cb_ref, gw_hbm = refs[:6]
        refs = refs[6:]
        if comm:
            nx_ref, refs = refs[0], refs[1:]
        proj_ref, zc_ref = refs[:2]
        refs = refs[2:]
        if comm:
            gwn_ref, refs = refs[0], refs[1:]
        w_vm, sems, zext, zsh, zc_buf, ztail = refs[:6]
        i = pl.program_id(0)

        if comm:
            ag = functools.partial(_ag_stage1, nx_ref, gwn_ref, *refs[6:9])

            @pl.when(i == 0)
            def _():
                local, sends, _ = ag()
                _start_all(local, sends)

        @pl.when(i == 0)
        def _():
            _load_rows(gw_hbm, w_vm, sems, 0, pk.off_in, pk.n_in)
            ztail[...] = jnp.zeros(ztail.shape, F32)

        x = x_ref[...]
        m = mod_ref[0]
        r = lax.rsqrt(_mean(x * x) + EPS)
        h = (x * r * g_ref[...] * (1.0 + m[:, d:2 * d]) + m[:, 0:d]).astype(BF16)

        def chunk(j):
            js = slice(j * MM_COLS, (j + 1) * MM_COLS)
            proj_ref[:, js] = _nt(h, w_vm[js, :]).astype(BF16)

        glu = range(2 * d // MM_COLS, 4 * d // MM_COLS)
        for j in glu:
            chunk(j)
        z = proj_ref[:, 2 * d:3 * d].astype(F32) * jax.nn.sigmoid(proj_ref[:, 3 * d:4 * d].astype(F32))
        zext[pl.ds(0, HALO), :] = jnp.where((i % tps) == 0, 0.0, ztail[...])
        zext[pl.ds(HALO, tm), :] = z
        ztail[...] = zext[pl.ds(tm, HALO), :]
        for s in range(tm // sub):
            _shift_copies(zext, zsh, s * sub, sub + HALO - 8)
            _conv_taps(zext, zsh, s * sub, cw_ref, [lo + k for k in range(CONV_TAPS)], cb_ref[...], zc_buf, sub, d)
        zc_ref[...] = zc_buf[...].astype(BF16)
        for j in range(nc // MM_COLS):
            if j not in glu:
                chunk(j)

        if comm:
            @pl.when(i == n - 1)
            def _():
                _finish_all(*ag())

    return pl.pallas_call(
        body, grid=(n,),
        in_specs=[_row_spec(tm, d), _mod_spec(tps, nc), _const_spec((1, d)), _const_spec((8 * HALO, d)),
                  _const_spec((1, d)), ANY] + ([ANY] if comm else []),
        out_specs=[_row_spec(tm, nc), _row_spec(tm, d)] + ([ANY] if comm else []),
        out_shape=[_sds((t, nc), BF16), _sds((t, d), BF16)] + ([_sds((N_DEV,) + nxt.shape, nxt.dtype)] if comm else []),
        scratch_shapes=[pltpu.VMEM((nc, d), BF16), pltpu.SemaphoreType.DMA((N_DEV,)),
                        pltpu.VMEM((tm + HALO, d), F32), pltpu.VMEM((7, sub + HALO, d), F32),
                        pltpu.VMEM((tm, d), F32), pltpu.VMEM((HALO, d), F32)]
        + (_sem_scratch(4, True) if comm else []),
        compiler_params=_params(1), name="fwd_in_ag" if comm else "fwd_in",
    )(x2d, mod3, g1, cwb, cb, gw, *([nxt] if comm else []))


def _shift_copies(src, sh, base, rows):
    for r in range(1, 8):
        sh[r - 1, pl.ds(0, rows), :] = src[pl.ds(base + r, rows), :]


def _window(src, sh, base, offset, start, size):
    r, q = offset % 8, offset // 8
    if r == 0:
        return src[pl.ds(base + start + 8 * q, size), :]
    return sh[r - 1, pl.ds(start + 8 * q, size), :]


def _conv_taps(src, sh, base, cwb_ref, offsets, bias, out_ref, rows, d):
    nsub = CONV_ROWS // 8
    for rb in range(rows // CONV_ROWS):
        accs = [jnp.broadcast_to(bias, (8, d))] * nsub
        for k in range(CONV_TAPS):
            w8 = cwb_ref[pl.ds(8 * k, 8), :]
            accs = [a + w8 * _window(src, sh, base, offsets[k], rb * CONV_ROWS + 8 * j, 8) for j, a in enumerate(accs)]
        for j, a in enumerate(accs):
            out_ref[pl.ds(base + rb * CONV_ROWS + 8 * j, 8), :] = a


def _conv_wgrad(z_ref, dsrc, dsh, acc_ref, rows, d):
    for k0 in range(0, CONV_TAPS, CONV_WGRAD_TAPS):
        taps = list(range(k0, min(k0 + CONV_WGRAD_TAPS, CONV_TAPS)))
        accs = [jnp.zeros((8, d), F32)] * len(taps)
        for rb in range(rows // 8):
            zblk = z_ref[pl.ds(rb * 8, 8), :]
            accs = [a + zblk * _window(dsrc, dsh, 0, CONV_TAPS - 1 - k, rb * 8, 8) for a, k in zip(accs, taps)]
        for a, k in zip(accs, taps):
            acc_ref[pl.ds(8 * k, 8), :] = acc_ref[pl.ds(8 * k, 8), :] + a


def _fwd_mix(proj, zc, x2d, mod3, vecs, wm, bsx, gw, pk, seq, tm):
    t, d = x2d.shape
    tps = seq // tm

    def body(proj_ref, zc_ref, x_ref, mod_ref, vec_ref, wm_ref, bs_ref, gw_hbm,
             x1_ref, ya_ref, yb_ref, o1_ref, aa_ref, ba_ref, mg_ref, wpa, wpb, wout, sems, vn_buf):
        @pl.when(pl.program_id(0) == 0)
        def _():
            _load_rows(gw_hbm, wpa, sems, 0, pk.off_pa, pk.n_p)
            _load_rows(gw_hbm, wpb, sems, N_DEV, pk.off_pb, pk.n_p)
            _load_rows(gw_hbm, wout, sems, 2 * N_DEV, pk.off_out, pk.n_p)

        m = mod_ref[0]
        vhat, _ = _ln_stats(proj_ref[:, d:2 * d].astype(F32))
        vn_buf[...] = (vhat * vec_ref[0:1, :] + vec_ref[1:2, :]).astype(BF16)
        for c in range(tm // CHUNK):
            rs_ = slice(c * CHUNK, (c + 1) * CHUNK)
            for h in range(HEADS):
                cs_ = slice(h * CHUNK, (h + 1) * CHUNK)
                s_b = _nn(wm_ref[h], vn_buf[rs_, cs_]) + bs_ref[:, cs_]
                aa_ref[rs_, cs_] = (proj_ref[rs_, cs_].astype(F32) * s_b).astype(BF16)
        y_a = _nn(aa_ref[...], wpa[...])
        ya_ref[...] = y_a.astype(BF16)
        zhat, _ = _ln_stats(zc_ref[...].astype(F32))
        zn = zhat * vec_ref[3:4, :] + vec_ref[4:5, :]
        b_act = (zn * jax.nn.sigmoid(zn)).astype(BF16)
        ba_ref[...] = b_act
        y_b = _nn(b_act, wpb[...])
        yb_ref[...] = y_b.astype(BF16)
        merged = (jax.nn.sigmoid(proj_ref[:, 4 * d:5 * d].astype(F32)) * y_a
                  + jax.nn.sigmoid(proj_ref[:, 5 * d:6 * d].astype(F32)) * y_b).astype(BF16)
        mg_ref[...] = merged
        o1 = _nn(merged, wout[...])
        o1_ref[...] = o1.astype(BF16)
        x1_ref[...] = x_ref[...] + m[:, 2 * d:3 * d] * o1

    act = _sds((t, d), BF16)
    return pl.pallas_call(
        body, grid=(t // tm,),
        in_specs=[_row_spec(tm, 6 * d), _row_spec(tm, d), _row_spec(tm, d), _mod_spec(tps, 6 * d), _const_spec((8, d)),
                  _const_spec((HEADS, CHUNK, CHUNK)), _const_spec((CHUNK, d)), ANY],
        out_specs=[_row_spec(tm, d)] * 7,
        out_shape=[_sds((t, d), F32), act, act, act, act, act, act],
        scratch_shapes=[pltpu.VMEM((d, d), BF16), pltpu.VMEM((d, d), BF16), pltpu.VMEM((d, d), BF16),
                        pltpu.SemaphoreType.DMA((3 * N_DEV,)), pltpu.VMEM((tm, d), BF16)],
        compiler_params=_params(1), name="fwd_mix",
    )(proj, zc, x2d, mod3, vecs, wm, bsx, gw)


def _fwd_ffn(x1, mod3, g2, gw, pk, seq, tm, gw_next=None):
    t, d = x1.shape
    nf = 4 * d
    tps = seq // tm
    n = t // tm
    comm = gw_next is not None

    def body(*refs):
        x_ref, mod_ref, g_ref, gw_hbm = refs[:4]
        refs = refs[4:]
        if comm:
            gwn_in, refs = refs[0], refs[1:]
        x2_ref, f_ref, o2_ref = refs[:3]
        refs = refs[3:]
        if comm:
            gwn_out, refs = refs[0], refs[1:]
        w1, w2, sems = refs[:3]
        i = pl.program_id(0)

        if comm:
            ag = functools.partial(_ag_stage2, gwn_in, gwn_out, *refs[3:5])

            @pl.when(i == 0)
            def _():
                _start_all(None, ag()[0])

        @pl.when(i == 0)
        def _():
            _load_rows(gw_hbm, w1, sems, 0, pk.off_ff1, pk.n_ff)
            _load_rows(gw_hbm, w2, sems, N_DEV, pk.off_ff2, pk.n_ff)

        x = x_ref[...]
        m = mod_ref[0]
        r = lax.rsqrt(_mean(x * x) + EPS)
        h2 = (x * r * g_ref[...] * (1.0 + m[:, 4 * d:5 * d]) + m[:, 3 * d:4 * d]).astype(BF16)
        acc = jnp.zeros(x.shape, F32)
        for j in range(nf // MM_COLS):
            js = slice(j * MM_COLS, (j + 1) * MM_COLS)
            f = _nt(h2, w1[js, :])
            f_ref[:, js] = f.astype(BF16)
            acc = acc + _nn(jnp.square(jnp.maximum(f, 0.0)).astype(BF16), w2[js, :])
        o2_ref[...] = acc.astype(BF16)
        x2_ref[...] = x + m[:, 5 * d:6 * d] * acc

        if comm:
            @pl.when(i == n - 1)
            def _():
                _finish_all(None, *ag())

    return pl.pallas_call(
        body, grid=(n,),
        in_specs=[_row_spec(tm, d), _mod_spec(tps, 6 * d), _const_spec((1, d)), ANY] + ([ANY] if comm else []),
        out_specs=[_row_spec(tm, d), _row_spec(tm, nf), _row_spec(tm, d)] + ([ANY] if comm else []),
        out_shape=[_sds((t, d), F32), _sds((t, nf), BF16), _sds((t, d), BF16)]
        + ([_sds(gw_next.shape, gw_next.dtype)] if comm else []),
        scratch_shapes=[pltpu.VMEM((nf, d), BF16), pltpu.VMEM((nf, d), BF16), pltpu.SemaphoreType.DMA((2 * N_DEV,))]
        + (_sem_scratch(3, False) if comm else []),
        input_output_aliases={4: 3} if comm else {},
        compiler_params=_params(1), name="fwd_ffn_ag" if comm else "fwd_ffn",
    )(x1, mod3, g2, gw, *([gw_next] if comm else []))


def _loss_head(x, tgt, fg, tm):
    t, d = x.shape
    n = t // tm

    def body(x_ref, t_ref, g_ref, dx_ref, loss_ref, dg_ref, lacc):
        i = pl.program_id(0)

        @pl.when(i == 0)
        def _():
            lacc[...] = jnp.zeros(lacc.shape, F32)
            dg_ref[...] = jnp.zeros(dg_ref.shape, F32)

        xv = x_ref[...]
        g = g_ref[...]
        r = lax.rsqrt(_mean(xv * xv) + EPS)
        err = xv * r * g - t_ref[...]
        lacc[...] = lacc[...] + _rowsum(err * err)
        dy = err * (1.0 / d)
        _add_row(dg_ref, (slice(0, 1), slice(None)), _rowsum(dy * xv * r))
        dx_ref[...] = _rms_bwd(dy, g, xv, r)

        @pl.when(i == n - 1)
        def _():
            loss_ref[...] = jnp.broadcast_to(jnp.sum(lacc[...], keepdims=True) * (0.5 / d), loss_ref.shape)

    return pl.pallas_call(
        body, grid=(n,),
        in_specs=[_row_spec(tm, d), _row_spec(tm, d), _const_spec((1, d))],
        out_specs=[_row_spec(tm, d), _const_spec((8, 128)), _const_spec((8, d))],
        out_shape=[_sds((t, d), F32), _sds((8, 128), F32), _sds((8, d), F32)],
        scratch_shapes=[pltpu.VMEM((1, d), F32)],
        compiler_params=_params(1), name="loss_head",
    )(x, tgt, fg)


def _sibling_copy(x_ref, out_ref, send_sem, recv_sem):
    x, y, c = _position()
    return pltpu.make_async_remote_copy(
        src_ref=x_ref.at[pl.ds(0, x_ref.shape[0]), 1 - c], dst_ref=out_ref, send_sem=send_sem, recv_sem=recv_sem,
        device_id=(x, y, 1 - c), device_id_type=MESH)


def _bwd_ffn(dx2, x1, f, o2, mod3, g2, gw, pk, seq, tm, dp=None):
    t, d = x1.shape
    nf = 4 * d
    tps = seq // tm
    nb = t // seq
    steps = t // tm
    comm = dp is not None

    def body(*refs):
        dx2_ref, x_ref, f_ref, o2_ref, mod_ref, g_ref, gw_hbm = refs[:7]
        refs = refs[7:]
        if comm:
            dp_ref, refs = refs[0], refs[1:]
        dx1_ref, df_ref, do2_ref, h2_ref, ms_ref, ps_ref = refs[:6]
        refs = refs[6:]
        if comm:
            got_ref, refs = refs[0], refs[1:]
        w1, w2, sems = refs[:3]
        i = pl.program_id(0)

        if comm:
            swap = functools.partial(_sibling_copy, dp_ref, got_ref, *refs[3:5])

            @pl.when(i == 0)
            def _():
                swap().start()

        @pl.when(i == 0)
        def _():
            _load_rows(gw_hbm, w1, sems, 0, pk.off_ff1, pk.n_ff)
            _load_rows(gw_hbm, w2, sems, N_DEV, pk.off_ff2, pk.n_ff)
            ps_ref[...] = jnp.zeros(ps_ref.shape, F32)

        @pl.when((i % tps) == 0)
        def _():
            ms_ref[...] = jnp.zeros(ms_ref.shape, F32)

        dx2 = dx2_ref[...]
        x = x_ref[...]
        m = mod_ref[0]
        g = g_ref[...]
        sh2, sc2, gt2 = m[:, 3 * d:4 * d], m[:, 4 * d:5 * d], m[:, 5 * d:6 * d]
        _add_row(ms_ref, (0, slice(2, 3), slice(None)), _rowsum(dx2 * o2_ref[...].astype(F32)))
        do2 = (gt2 * dx2).astype(BF16)
        do2_ref[...] = do2
        r = lax.rsqrt(_mean(x * x) + EPS)
        n = x * r * g
        h2_ref[...] = (n * (1.0 + sc2) + sh2).astype(BF16)
        dh = jnp.zeros(x.shape, F32)
        for j in range(nf // MM_COLS):
            js = slice(j * MM_COLS, (j + 1) * MM_COLS)
            dr = _nt(do2, w2[js, :])
            df = (dr * (2.0 * jnp.maximum(f_ref[:, js].astype(F32), 0.0))).astype(BF16)
            df_ref[:, js] = df
            dh = dh + _nn(df, w1[js, :])
        _add_row(ms_ref, (0, slice(0, 1), slice(None)), _rowsum(dh))
        _add_row(ms_ref, (0, slice(1, 2), slice(None)), _rowsum(dh * n))
        dn = dh * (1.0 + sc2)
        _add_row(ps_ref, (slice(0, 1), slice(None)), _rowsum(dn * x * r))
        dx1_ref[...] = dx2 + _rms_bwd(dn, g, x, r)

        if comm:
            @pl.when(i == steps - 1)
            def _():
                swap().wait()

    act = _sds((t, d), BF16)
    return pl.pallas_call(
        body, grid=(steps,),
        in_specs=[_row_spec(tm, d), _row_spec(tm, d), _row_spec(tm, nf), _row_spec(tm, d), _mod_spec(tps, 6 * d),
                  _const_spec((1, d)), ANY] + ([ANY] if comm else []),
        out_specs=[_row_spec(tm, d), _row_spec(tm, nf), _row_spec(tm, d), _row_spec(tm, d), _mstat_spec(tps, d),
                   _const_spec((8, d))] + ([ANY] if comm else []),
        out_shape=[_sds((t, d), F32), _sds((t, nf), BF16), act, act, _sds((nb, 8, d), F32), _sds((8, d), F32)]
        + ([_sds((dp.shape[0],) + dp.shape[2:], dp.dtype)] if comm else []),
        scratch_shapes=[pltpu.VMEM((nf, d), BF16), pltpu.VMEM((nf, d), BF16), pltpu.SemaphoreType.DMA((2 * N_DEV,))]
        + ([pltpu.SemaphoreType.DMA(()), pltpu.SemaphoreType.DMA(())] if comm else []),
        compiler_params=_params(1), name="bwd_ffn_rs" if comm else "bwd_ffn",
    )(dx2, x1, f, o2, mod3, g2, gw, *([dp] if comm else []))


def _bwd_mix(dx1, proj, ya, yb, o1, zc, mod3, vecs, wm, wmt, bsx, gw, pk, seq, tm):
    t, d = dx1.shape
    tps = seq // tm
    nb = t // seq
    n = t // tm

    def body(dx1_ref, proj_ref, ya_ref, yb_ref, o1_ref, zc_ref, mod_ref, vec_ref, wm_ref, wmt_ref, bs_ref, gw_hbm,
             dp_ref, dzc_ref, do1_ref, dya_ref, dyb_ref, ms_ref, ps_ref, dws_ref, dbs_ref,
             wpa, wpb, wout, sems, vn_buf, da_buf, dvn_buf):
        i = pl.program_id(0)

        @pl.when(i == 0)
        def _():
            _load_rows(gw_hbm, wpa, sems, 0, pk.off_pa, pk.n_p)
            _load_rows(gw_hbm, wpb, sems, N_DEV, pk.off_pb, pk.n_p)
            _load_rows(gw_hbm, wout, sems, 2 * N_DEV, pk.off_out, pk.n_p)
            ps_ref[...] = jnp.zeros(ps_ref.shape, F32)
            dws_ref[...] = jnp.zeros(dws_ref.shape, F32)
            dbs_ref[...] = jnp.zeros(dbs_ref.shape, F32)

        @pl.when((i % tps) == 0)
        def _():
            ms_ref[...] = jnp.zeros(ms_ref.shape, F32)

        m = mod_ref[0]
        dx1v = dx1_ref[...]
        _add_row(ms_ref, (0, slice(0, 1), slice(None)), _rowsum(dx1v * o1_ref[...].astype(F32)))
        do1 = (m[:, 2 * d:3 * d] * dx1v).astype(BF16)
        do1_ref[...] = do1
        dmg = _nt(do1, wout[...])
        sa = jax.nn.sigmoid(proj_ref[:, 4 * d:5 * d].astype(F32))
        sb = jax.nn.sigmoid(proj_ref[:, 5 * d:6 * d].astype(F32))
        y_a = ya_ref[...].astype(F32)
        y_b = yb_ref[...].astype(F32)
        dya = (dmg * sa).astype(BF16)
        dyb = (dmg * sb).astype(BF16)
        dya_ref[...] = dya
        dyb_ref[...] = dyb
        dp_ref[:, 4 * d:5 * d] = (dmg * y_a * sa * (1.0 - sa)).astype(BF16)
        dp_ref[:, 5 * d:6 * d] = (dmg * y_b * sb * (1.0 - sb)).astype(BF16)
        da_buf[...] = _nt(dya, wpa[...])
        db = _nt(dyb, wpb[...])
        vhat, rs = _ln_stats(proj_ref[:, d:2 * d].astype(F32))
        alg = vec_ref[0:1, :]
        vn_buf[...] = (vhat * alg + vec_ref[1:2, :]).astype(BF16)
        for c in range(tm // CHUNK):
            rs_ = slice(c * CHUNK, (c + 1) * CHUNK)
            for h in range(HEADS):
                cs_ = slice(h * CHUNK, (h + 1) * CHUNK)
                vn_b = vn_buf[rs_, cs_]
                s_b = _nn(wm_ref[h], vn_b) + bs_ref[:, cs_]
                u_b = proj_ref[rs_, cs_].astype(F32)
                da_b = da_buf[rs_, cs_]
                dp_ref[rs_, cs_] = (da_b * s_b).astype(BF16)
                ds_b = da_b * u_b
                dbs_ref[:, cs_] = dbs_ref[:, cs_] + ds_b
                ds_bf = ds_b.astype(BF16)
                dvn_buf[rs_, cs_] = _nn(wmt_ref[h], ds_bf)
                dws_ref[:, cs_] = dws_ref[:, cs_] + _nt(ds_bf, vn_b)
        dvn = dvn_buf[...]
        _add_row(ps_ref, (slice(0, 1), slice(None)), _rowsum(dvn * vhat))
        _add_row(ps_ref, (slice(1, 2), slice(None)), _rowsum(dvn))
        dp_ref[:, d:2 * d] = _ln_bwd(dvn, alg, vhat, rs).astype(BF16)
        dp_ref[:, 2 * d:4 * d] = jnp.zeros((tm, 2 * d), BF16)
        zhat, rsb = _ln_stats(zc_ref[...].astype(F32))
        blg = vec_ref[3:4, :]
        zn = zhat * blg + vec_ref[4:5, :]
        sg = jax.nn.sigmoid(zn)
        dzn = db * (sg * (1.0 + zn * (1.0 - sg)))
        _add_row(ps_ref, (slice(2, 3), slice(None)), _rowsum(dzn * zhat))
        _add_row(ps_ref, (slice(3, 4), slice(None)), _rowsum(dzn))
        dzc = _ln_bwd(dzn, blg, zhat, rsb)
        _add_row(ps_ref, (slice(4, 5), slice(None)), _rowsum(dzc))
        dzc_ref[...] = dzc.astype(BF16)

        @pl.when(i == n - 1)
        def _():
            causal = (lax.broadcasted_iota(jnp.int32, (CHUNK, CHUNK), 0)
                      >= lax.broadcasted_iota(jnp.int32, (CHUNK, CHUNK), 1))
            for h in range(HEADS):
                cs_ = slice(h * CHUNK, (h + 1) * CHUNK)
                dws_ref[:, cs_] = jnp.where(causal, dws_ref[:, cs_], 0.0)
                dbs_ref[:, cs_] = jnp.broadcast_to(jnp.sum(dbs_ref[:, cs_], axis=1, keepdims=True), (CHUNK, CHUNK))

    act = _sds((t, d), BF16)
    return pl.pallas_call(
        body, grid=(n,),
        in_specs=[_row_spec(tm, d), _row_spec(tm, 6 * d), _row_spec(tm, d), _row_spec(tm, d), _row_spec(tm, d),
                  _row_spec(tm, d), _mod_spec(tps, 6 * d), _const_spec((8, d)), _const_spec((HEADS, CHUNK, CHUNK)),
                  _const_spec((HEADS, CHUNK, CHUNK)), _const_spec((CHUNK, d)), ANY],
        out_specs=[_row_spec(tm, 6 * d)] + [_row_spec(tm, d)] * 4
        + [_mstat_spec(tps, d), _const_spec((8, d)), _const_spec((CHUNK, d)), _const_spec((CHUNK, d))],
        out_shape=[_sds((t, 6 * d), BF16)] + [act] * 4
        + [_sds((nb, 8, d), F32), _sds((8, d), F32), _sds((CHUNK, d), F32), _sds((CHUNK, d), F32)],
        scratch_shapes=[pltpu.VMEM((d, d), BF16), pltpu.VMEM((d, d), BF16), pltpu.VMEM((d, d), BF16),
                        pltpu.SemaphoreType.DMA((3 * N_DEV,)),
                        pltpu.VMEM((tm, d), BF16), pltpu.VMEM((tm, d), F32), pltpu.VMEM((tm, d), F32)],
        compiler_params=_params(1), name="bwd_mix",
    )(dx1, proj, ya, yb, o1, zc, mod3, vecs, wm, wmt, bsx, gw)


def _bwd_in(dproj, dzc, proj, x2d, dx1, mod3, g1, cw, gw, pk, seq, tm, part=None):
    t, d = x2d.shape
    tps = seq // tm
    nb = t // seq
    n = t // tm
    hb = tm // HALO
    comm = part is not None

    def body(*refs):
        dpi_ref, dzc_ref, dzn_ref, pp_ref, x_ref, dx1_ref, mod_ref, g_ref, cw_ref, gw_hbm = refs[:10]
        refs = refs[10:]
        if comm:
            part_ref, refs = refs[0], refs[1:]
        dpo_ref, dx_ref, h_ref, ms_ref, ps_ref, dcw_ref = refs[:6]
        refs = refs[6:]
        if comm:
            got_ref, refs = refs[0], refs[1:]
        w_vm, sems, z_buf, dzext, dsh, dz_buf, dcw_acc = refs[:7]
        i = pl.program_id(0)

        if comm:
            rs = functools.partial(_rs_chip_copies, part_ref, got_ref, *refs[7:10])

            @pl.when(i == 0)
            def _():
                local, sends, _ = rs()
                _start_all(local, sends)

        @pl.when(i == 0)
        def _():
            _load_rows(gw_hbm, w_vm, sems, 0, pk.off_in, pk.n_in)
            ps_ref[...] = jnp.zeros(ps_ref.shape, F32)
            dcw_acc[...] = jnp.zeros(dcw_acc.shape, F32)

        last = (i % tps) == tps - 1

        @pl.when((i % tps) == 0)
        def _():
            ms_ref[...] = jnp.zeros(ms_ref.shape, F32)

        pa = pp_ref[:, 0:d].astype(F32)
        sgp = jax.nn.sigmoid(pp_ref[:, d:2 * d].astype(F32))
        z_buf[...] = pa * sgp
        dzext[pl.ds(0, tm), :] = dzc_ref[...].astype(F32)
        dzext[pl.ds(tm, HALO), :] = jnp.where(last, 0.0, dzn_ref[...].astype(F32))
        _shift_copies(dzext, dsh, 0, tm + HALO - 8)
        _conv_wgrad(z_buf, dzext, dsh, dcw_acc, tm, d)
        _conv_taps(dzext, dsh, 0, cw_ref, [CONV_TAPS - 1 - k for k in range(CONV_TAPS)], jnp.zeros((1, d), F32),
                   dz_buf, tm, d)
        dz = dz_buf[...]
        dpa = (dz * sgp).astype(BF16)
        dpg = (dz * pa * sgp * (1.0 - sgp)).astype(BF16)
        dpo_ref[:, 0:d] = dpa
        dpo_ref[:, d:2 * d] = dpg
        dh = (_nn(dpi_ref[:, 0:2 * d], w_vm[0:2 * d, :]) + _nn(dpa, w_vm[2 * d:3 * d, :])
              + _nn(dpg, w_vm[3 * d:4 * d, :]) + _nn(dpi_ref[:, 4 * d:6 * d], w_vm[4 * d:6 * d, :]))
        x = x_ref[...]
        m = mod_ref[0]
        g = g_ref[...]
        sh1, sc1 = m[:, 0:d], m[:, d:2 * d]
        r = lax.rsqrt(_mean(x * x) + EPS)
        nrm = x * r * g
        h_ref[...] = (nrm * (1.0 + sc1) + sh1).astype(BF16)
        _add_row(ms_ref, (0, slice(0, 1), slice(None)), _rowsum(dh))
        _add_row(ms_ref, (0, slice(1, 2), slice(None)), _rowsum(dh * nrm))
        dn = dh * (1.0 + sc1)
        _add_row(ps_ref, (slice(0, 1), slice(None)), _rowsum(dn * x * r))
        dx_ref[...] = dx1_ref[...] + _rms_bwd(dn, g, x, r)

        @pl.when(i == n - 1)
        def _():
            for k in range(CONV_TAPS):
                dcw_ref[k:k + 1, :] = _rowsum(dcw_acc[pl.ds(k * 8, 8), :])
            dcw_ref[CONV_TAPS:HALO, :] = jnp.zeros((HALO - CONV_TAPS, d), F32)

        if comm:
            @pl.when(i == n - 1)
            def _():
                _finish_all(*rs())

    halo_next = pl.BlockSpec((HALO, d), lambda i: (jnp.minimum((i + 1) * hb, t // HALO - 1), 0))
    return pl.pallas_call(
        body, grid=(n,),
        in_specs=[_row_spec(tm, 6 * d), _row_spec(tm, d), halo_next, _row_spec(tm, 2 * d, 1),
                  _row_spec(tm, d), _row_spec(tm, d), _mod_spec(tps, 6 * d), _const_spec((1, d)),
                  _const_spec((8 * HALO, d)), ANY] + ([ANY] if comm else []),
        out_specs=[_row_spec(tm, 2 * d, 1), _row_spec(tm, d), _row_spec(tm, d), _mstat_spec(tps, d),
                   _const_spec((8, d)), _const_spec((HALO, d))] + ([ANY] if comm else []),
        out_shape=[_sds((t, 6 * d), BF16), _sds((t, d), F32), _sds((t, d), BF16), _sds((nb, 8, d), F32),
                   _sds((8, d), F32), _sds((HALO, d), F32)] + ([_sds(part.shape, part.dtype)] if comm else []),
        scratch_shapes=[pltpu.VMEM((6 * d, d), BF16), pltpu.SemaphoreType.DMA((N_DEV,)),
                        pltpu.VMEM((tm, d), F32), pltpu.VMEM((tm + HALO, d), F32),
                        pltpu.VMEM((7, tm + HALO, d), F32),
                        pltpu.VMEM((tm, d), F32), pltpu.VMEM((CONV_TAPS * 8, d), F32)]
        + (_sem_scratch(3, True) if comm else []),
        input_output_aliases={0: 0},
        compiler_params=_params(1), name="bwd_in_rs" if comm else "bwd_in",
    )(dproj, dzc, dzc, proj, x2d, dx1, mod3, g1, cw, gw, *([part] if comm else []))


def _wgrad(a, b, pack, kb, off, relu2, name, part=None):
    t, mo = a.shape
    nn_ = b.shape[1]
    rows = mo // N_DEV
    tk = min(TK_WGRAD_ONE if kb == N_DEV else TK_WGRAD, t)
    nk = t // tk
    ni = N_DEV // kb
    assert off % rows == 0 and N_DEV % kb == 0
    comm = part is not None

    def body(*refs):
        a_ref, b_ref, pack_hbm = refs[:3]
        refs = refs[3:]
        if comm:
            part_ref, refs = refs[0], refs[1:]
        o_ref, refs = refs[0], refs[1:]
        if comm:
            got_ref, refs = refs[0], refs[1:]
        acc = refs[0]
        i = pl.program_id(0)
        k = pl.program_id(1)

        if comm:
            rs = functools.partial(_rs_chip_copies, part_ref, got_ref, *refs[1:4])

            @pl.when((i == 0) & (k == 0))
            def _():
                local, sends, _ = rs()
                _start_all(local, sends)

        @pl.when(k == 0)
        def _():
            acc[...] = jnp.zeros(acc.shape, F32)

        av = a_ref[...]
        if relu2:
            av = jnp.square(jnp.maximum(av, 0.0))
        acc[...] = acc[...] + _tn(av, b_ref[...])

        @pl.when(k == nk - 1)
        def _():
            for j in range(kb):
                o_ref[j] = acc[pl.ds(j * rows, rows), :].astype(BF16)

        if comm:
            @pl.when((i == ni - 1) & (k == nk - 1))
            def _():
                _finish_all(*rs())

    return pl.pallas_call(
        body, grid=(ni, nk),
        in_specs=[pl.BlockSpec((tk, kb * rows), lambda i, k: (k, i)), pl.BlockSpec((tk, nn_), lambda i, k: (k, 0)), ANY]
        + ([ANY] if comm else []),
        out_specs=[pl.BlockSpec((kb, rows, nn_), lambda i, k: (i, off // rows, 0))] + ([ANY] if comm else []),
        out_shape=[_sds(pack.shape, BF16)] + ([_sds(part.shape, part.dtype)] if comm else []),
        scratch_shapes=[pltpu.VMEM((kb * rows, nn_), F32)] + (_sem_scratch(3, True) if comm else []),
        input_output_aliases={2: 0},
        compiler_params=_params(2), name=name,
    )(a, b, pack, *([part] if comm else []))


def _place_rows(pack, rows_blk, off):
    nblk, r, nn_ = rows_blk.shape
    assert off % r == 0

    def body(pack_hbm, s_ref, o_ref):
        o_ref[...] = s_ref[...]

    return pl.pallas_call(
        body, grid=(1,),
        in_specs=[ANY, pl.BlockSpec((nblk, r, nn_), lambda i: (0, 0, 0))],
        out_specs=pl.BlockSpec((nblk, r, nn_), lambda i: (0, off // r, 0)),
        out_shape=_sds(pack.shape, pack.dtype),
        input_output_aliases={0: 0},
        compiler_params=_params(1), name="place_small",
    )(pack, rows_blk)


def _mod_fwd(c_all, w_ada, b_cols):
    nl, d, cols = w_ada.shape
    bsz = c_all.shape[0]

    def body(c_ref, w_ref, b_ref, o_ref):
        cv = c_ref[...]
        ca = cv * jax.nn.sigmoid(cv)
        o_ref[0] = jnp.dot(ca, w_ref[0], preferred_element_type=F32, precision=lax.Precision.HIGHEST) + b_ref[0]

    return pl.pallas_call(
        body, grid=(nl,),
        in_specs=[_const_spec((bsz, d)), pl.BlockSpec((1, d, cols), lambda l: (l, 0, 0)),
                  pl.BlockSpec((1, 1, cols), lambda l: (l, 0, 0))],
        out_specs=pl.BlockSpec((1, bsz, cols), lambda l: (l, 0, 0)),
        out_shape=_sds((nl, bsz, cols), F32),
        compiler_params=_params(1), name="mod_fwd",
    )(c_all, w_ada, b_cols)


def _mod_bwd(c_all, dmod_cols, dmod_all):
    nl, bsz, cols = dmod_cols.shape
    d = c_all.shape[1]
    ncol = dmod_all.shape[2]

    def body(c_ref, dc_ref, da_ref, dw_ref, db_ref):
        cv = c_ref[...]
        ca = cv * jax.nn.sigmoid(cv)
        dw_ref[0] = lax.dot_general(ca, dc_ref[0], (((0,), (0,)), ((), ())), preferred_element_type=F32,
                                    precision=lax.Precision.HIGHEST)
        db_ref[0] = _rowsum(da_ref[0])

    return pl.pallas_call(
        body, grid=(nl,),
        in_specs=[_const_spec((bsz, d)), pl.BlockSpec((1, bsz, cols), lambda l: (l, 0, 0)),
                  pl.BlockSpec((1, bsz, ncol), lambda l: (l, 0, 0))],
        out_specs=[pl.BlockSpec((1, d, cols), lambda l: (l, 0, 0)), pl.BlockSpec((1, 1, ncol), lambda l: (l, 0, 0))],
        out_shape=[_sds((nl, d, cols), F32), _sds((nl, 1, ncol), F32)],
        compiler_params=_params(1), name="mod_bwd",
    )(c_all, dmod_cols, dmod_all)


def _row_tile(rows, cols, nbuf, itemsize=4, budget=24 * 1024 * 1024):
    cap = max(16, budget // (2 * nbuf * cols * itemsize))
    if rows <= cap:
        return rows
    best = None
    for tr in range(16, cap + 1, 16):
        if rows % tr == 0:
            best = tr
    assert best is not None, (rows, cols)
    return best


def _sum_blocks(xs, name):
    nblk, rows, cols = xs.shape
    tr = _row_tile(rows, cols, nblk + 1)

    def body(x_ref, o_ref):
        acc = x_ref[0].astype(F32)
        for j in range(1, nblk):
            acc = acc + x_ref[j].astype(F32)
        o_ref[...] = acc

    return pl.pallas_call(
        body, grid=(rows // tr,),
        in_specs=[pl.BlockSpec((nblk, tr, cols), lambda i: (0, i, 0))],
        out_specs=pl.BlockSpec((tr, cols), lambda i: (i, 0)),
        out_shape=_sds((rows, cols), F32),
        compiler_params=_params(1), name=name,
    )(xs)


def _add_sibling(dp, recv, core):
    nq, _, rows, cols = dp.shape
    tr = _row_tile(rows, cols, 3, itemsize=2)

    def body(c_ref, a_ref, b_ref, o_ref):
        o_ref[...] = (a_ref[...].astype(F32) + b_ref[...].astype(F32)).astype(BF16)

    return pl.pallas_call(
        body,
        grid_spec=pltpu.PrefetchScalarGridSpec(
            num_scalar_prefetch=1, grid=(nq, rows // tr),
            in_specs=[pl.BlockSpec((1, 1, tr, cols), lambda q, i, c: (q, c[0], i, 0)),
                      pl.BlockSpec((1, 1, tr, cols), lambda q, i, c: (q, 0, i, 0))],
            out_specs=pl.BlockSpec((1, 1, tr, cols), lambda q, i, c: (q, 0, i, 0))),
        out_shape=_sds((nq, 1, rows, cols), BF16),
        compiler_params=_params(2), name="add_sibling",
    )(core, dp, recv.reshape(nq, 1, rows, cols)).reshape(nq, rows, cols)


def _adamw(w, g, m, v, name):
    rows, cols = w.shape
    tr = _row_tile(rows, cols, 7)
    c1 = 1.0 - ADAM_B1 ** ADAM_STEP
    c2 = 1.0 - ADAM_B2 ** ADAM_STEP

    def body(w_ref, g_ref, m_ref, v_ref, d_ref, nm_ref, nv_ref):
        gv = g_ref[...]
        nm = ADAM_B1 * m_ref[...] + (1.0 - ADAM_B1) * gv
        nv = ADAM_B2 * v_ref[...] + (1.0 - ADAM_B2) * (gv * gv)
        nm_ref[...] = nm
        nv_ref[...] = nv
        d_ref[...] = -ADAM_LR * ((nm / c1) / (jnp.sqrt(nv / c2) + ADAM_EPS) + ADAM_WD * w_ref[...])

    spec = pl.BlockSpec((tr, cols), lambda i: (i, 0))
    out = _sds((rows, cols), F32)
    return pl.pallas_call(
        body, grid=(rows // tr,), in_specs=[spec] * 4, out_specs=[spec] * 3, out_shape=[out] * 3,
        compiler_params=_params(1), name=name,
    )(w, g, m, v)


def _all_gather(xs, name):
    rows, cols = xs.shape

    def body(x_ref, out_ref, send1, recv1, local_sem, send2, recv2):
        local, first, arrivals = _ag_stage1(x_ref, out_ref, send1, recv1, local_sem)
        _start_all(local, first)
        passed, from_sibling = _ag_stage2(out_ref, out_ref, send2, recv2)
        for arrival, onward in zip(arrivals[1:], passed):
            arrival.wait_recv()
            onward.start()
        arrivals[0].wait_recv()
        _finish_all(local, first + passed, from_sibling)

    return pl.pallas_call(
        body, out_shape=_sds((N_DEV, rows, cols), xs.dtype), in_specs=[ANY], out_specs=ANY,
        scratch_shapes=_sem_scratch(4, True) + _sem_scratch(3, False), name=name,
    )(xs)


def _sibling_exchange(dp):
    nq, _, rows, cols = dp.shape

    def body(x_ref, out_ref, send_sem, recv_sem):
        cp = _sibling_copy(x_ref, out_ref, send_sem, recv_sem)
        cp.start()
        cp.wait()

    return pl.pallas_call(
        body, out_shape=_sds((nq, rows, cols), dp.dtype), in_specs=[ANY], out_specs=ANY,
        scratch_shapes=[pltpu.SemaphoreType.DMA(()), pltpu.SemaphoreType.DMA(())],
        name="rs_sibling",
    )(dp)


def _chip_all_to_all(xs):
    nq, rows, cols = xs.shape

    def body(x_ref, out_ref, send_sems, recv_sems, local_sem):
        local, sends, recvs = _rs_chip_copies(x_ref, out_ref, send_sems, recv_sems, local_sem)
        _start_all(local, sends)
        _finish_all(local, sends, recvs)

    return pl.pallas_call(
        body, out_shape=_sds((nq, rows, cols), xs.dtype), in_specs=[ANY], out_specs=ANY,
        scratch_shapes=_sem_scratch(3, True), name="rs_chips",
    )(xs)


def _pad_rows(a, rows):
    return jnp.pad(a, ((0, rows - a.shape[0]), (0, 0)))


def kernel(x, c, w_ada, b_ada, norm1_g, w_in, a_ln_g, a_ln_b, a_ws, a_bs, w_pa, b_conv_w, b_conv_b, b_ln_g, b_ln_b, w_pb, w_out, norm2_g, w_ff1, w_ff2, final_g, loss_target, m_w_ada, m_b_ada, m_norm1_g, m_w_in, m_a_ln_g, m_a_ln_b, m_a_ws, m_a_bs, m_w_pa, m_b_conv_w, m_b_conv_b, m_b_ln_g, m_b_ln_b, m_w_pb, m_w_out, m_norm2_g, m_w_ff1, m_w_ff2, m_final_g, v_w_ada, v_b_ada, v_norm1_g, v_w_in, v_a_ln_g, v_a_ln_b, v_a_ws, v_a_bs, v_w_pa, v_b_conv_w, v_b_conv_b, v_b_ln_g, v_b_ln_b, v_w_pb, v_w_out, v_norm2_g, v_w_ff1, v_w_ff2, v_final_g):
    nb, seq, d = x.shape
    nl = w_in.shape[0]
    t = nb * seq
    pk = _Pack(d, 0)
    gk = _Pack(d, SMALL_SLOT)
    assert d % (N_DEV * CHUNK) == 0 and d // HEADS == CHUNK and seq % CHUNK == 0
    tm_big = min(TM_BIG, seq)
    tm_mix = min(TM_MIX, seq)
    ax, ay, ac = _position()
    dev = 4 * ax + 2 * ay + ac
    ncol = 6 * d
    cols = ncol // N_DEV
    cpd = d // N_DEV
    bsz = nb * N_DEV

    cw_rows = nl * HALO
    small = jnp.concatenate([
        c.reshape(nb * d // CHUNK, CHUNK),
        jnp.pad(b_conv_w.reshape(nl, CONV_TAPS, cpd), ((0, 0), (0, HALO - CONV_TAPS), (0, 0))).reshape(cw_rows, cpd),
    ], axis=0)
    c_rows = nb * d // CHUNK
    small_all = _all_gather(small, "ag_small")
    c_all = small_all[:, :c_rows].reshape(bsz, d)
    cw_all = small_all[:, c_rows:].reshape(N_DEV, nl, HALO, cpd).transpose(1, 2, 0, 3).reshape(nl, HALO, d)
    cwb_all = jnp.repeat(cw_all, 8, axis=1)
    b_cols = lax.dynamic_slice_in_dim(b_ada, dev * cols, cols, axis=1).reshape(nl, 1, cols)
    mod_cols = _mod_fwd(c_all, w_ada, b_cols)
    mod_all = _all_gather(mod_cols.reshape(nl * bsz, cols), "ag_mod")
    mod_all = mod_all.reshape(N_DEV, nl, bsz, cols).transpose(1, 2, 0, 3).reshape(nl, bsz, ncol)
    mod_mine = lax.dynamic_slice_in_dim(mod_all, dev * nb, nb, axis=1)

    causal = jnp.tril(jnp.ones((CHUNK, CHUNK), bool))
    wm_all = jnp.where(causal[None, None], a_ws, 0.0)
    wm_bf = wm_all.astype(BF16)
    wmt_bf = jnp.swapaxes(wm_all, 2, 3).astype(BF16)
    bsx_all = jnp.broadcast_to(jnp.swapaxes(a_bs, 1, 2)[:, :, :, None], (nl, CHUNK, HEADS, CHUNK)).reshape(nl, CHUNK, d)

    def vec_rows(l):
        return _pad_rows(jnp.stack([a_ln_g[l], a_ln_b[l], b_conv_b[l], b_ln_g[l], b_ln_b[l]]), 8)

    def weight_block(l):
        return jnp.concatenate([
            w_ff1[l].T, w_ff2[l], w_pa[l], w_pb[l], w_out[l], w_in[l].T], axis=0).astype(BF16)

    xs = x.reshape(t, d)
    saved = []
    gw = _all_gather(weight_block(0), "ag_weights")
    for l in range(nl):
        nxt = weight_block(l + 1) if l + 1 < nl else None
        mod3 = mod_mine[l].reshape(nb, 1, ncol)
        vecs = vec_rows(l)
        proj, zc, *gw_next = _fwd_in(xs, mod3, norm1_g[l].reshape(1, d), cwb_all[l], b_conv_b[l].reshape(1, d), gw, pk,
                                     seq, tm_big, nxt)
        x1, ya, yb, o1, aa, ba, mg = _fwd_mix(proj, zc, xs, mod3, vecs, wm_bf[l], bsx_all[l], gw, pk, seq, tm_big)
        x2, f, o2, *gw_next = _fwd_ffn(x1, mod3, norm2_g[l].reshape(1, d), gw, pk, seq, tm_big, *gw_next)
        saved.append((xs, x1, proj, ya, yb, o1, zc, f, o2, gw, mod3, vecs, aa, ba, mg))
        xs = x2
        if gw_next:
            gw = gw_next[0]

    dx, loss_blk, dfg = _loss_head(xs, loss_target.reshape(t, d), final_g.reshape(1, d), tm_big)
    loss = lax.psum(loss_blk[0, 0], ("x", "y", "c"))

    core = ac.reshape(1).astype(jnp.int32)
    wg = {k: [None] * nl for k in ("w_in", "w_ff1", "w_ff2", "w_pa", "w_pb", "w_out")}
    small_red = [None] * nl
    dmod_rows = [None] * nl
    per_layer = 8 + 2 * CHUNK + HALO
    assert per_layer <= N_DEV * SMALL_ROWS <= N_DEV * SMALL_SLOT

    rows_a = gk.off_in

    def reduced(l, red):
        reduced_a(l, red[:rows_a])
        wg["w_in"][l] = red[rows_a:].T

    def reduced_a(l, red):
        wg["w_ff1"][l] = red[gk.off_ff1:gk.off_ff1 + gk.n_ff].T
        wg["w_ff2"][l] = red[gk.off_ff2:gk.off_ff2 + gk.n_ff]
        wg["w_pa"][l] = red[gk.off_pa:gk.off_pa + gk.n_p]
        wg["w_pb"][l] = red[gk.off_pb:gk.off_pb + gk.n_p]
        wg["w_out"][l] = red[gk.off_out:gk.off_out + gk.n_p]
        small_red[l] = red[gk.off_small:gk.off_small + SMALL_ROWS]

    waiting = None
    pending = None
    for l in reversed(range(nl)):
        x0, x1, proj, ya, yb, o1, zc, f, o2, gw, mod3, vecs, aa, ba, mg = saved[l]
        dx1, df, do2, h2, ms2, ps2, *got = _bwd_ffn(dx, x1, f, o2, mod3, norm2_g[l].reshape(1, d), gw, pk, seq, tm_big,
                                                    None if waiting is None else waiting[1])
        if waiting is not None:
            pending = (waiting[0], _add_sibling(waiting[1], got[0], core))
        (dproj, dzc, do1, dya, dyb, ms1, ps1, dws, dbs) = _bwd_mix(
            dx1, proj, ya, yb, o1, zc, mod3, vecs, wm_bf[l], wmt_bf[l], bsx_all[l], gw, pk, seq, tm_mix)
        apart = l == 0
        grads = lax.empty((N_DEV, rows_a if apart else gk.rows, d), BF16)
        grads, = _wgrad(df, h2, grads, 2, gk.off_ff1, False, "wgrad_ff1")
        grads, = _wgrad(f, do2, grads, 2, gk.off_ff2, True, "wgrad_ff2")
        dproj, dx, h, ms0, ps0, dcw, *got = _bwd_in(
            dproj, dzc, proj, x0, dx1, mod3, norm1_g[l].reshape(1, d), cwb_all[l], gw, pk, seq, tm_mix,
            None if pending is None else pending[1])
        if pending is not None:
            reduced(pending[0], _sum_blocks(got[0], "sum_chips"))
        vec_g = jnp.concatenate([ps0[0:1], ps1[0:5], ps2[0:1], jnp.zeros((1, d), F32)], axis=0)
        small = _pad_rows(jnp.concatenate([vec_g, dws, dbs, dcw], axis=0), N_DEV * SMALL_ROWS)
        small = jnp.pad(small.reshape(N_DEV, SMALL_ROWS, d).astype(BF16), ((0, 0), (0, SMALL_SLOT - SMALL_ROWS), (0, 0)))
        grads = _place_rows(grads, small, gk.off_small)
        grads, = _wgrad(aa, dya, grads, N_DEV, gk.off_pa, False, "wgrad_pa")
        grads, = _wgrad(ba, dyb, grads, N_DEV, gk.off_pb, False, "wgrad_pb")
        grads, = _wgrad(mg, do1, grads, N_DEV, gk.off_out, False, "wgrad_out")
        if apart:
            dp_a = grads.reshape(N_CHIP, 2, rows_a, d)
            part_a = _add_sibling(dp_a, _sibling_exchange(dp_a), core)
            g_in, got_a = _wgrad(dproj, h, lax.empty((N_DEV, gk.n_in, d), BF16), 1, 0, False, "wgrad_in_rs", part_a)
            reduced_a(l, _sum_blocks(got_a, "sum_chips_a"))
            dp_b = g_in.reshape(N_CHIP, 2, gk.n_in, d)
            part_b = _add_sibling(dp_b, _sibling_exchange(dp_b), core)
            wg["w_in"][l] = _sum_blocks(_chip_all_to_all(part_b), "sum_chips_b").T
        else:
            grads, = _wgrad(dproj, h, grads, 1, gk.off_in, False, "wgrad_in")
            waiting = (l, grads.reshape(N_CHIP, 2, gk.rows, d))
        pending = None
        dmod_rows[l] = jnp.concatenate([ms0[:, 0], ms0[:, 1], ms1[:, 0], ms2[:, 0], ms2[:, 1], ms2[:, 2]], axis=1)

    small_all = _all_gather(jnp.concatenate(small_red, axis=0), "ag_small_grads")
    lay = small_all.reshape(N_DEV, nl, SMALL_ROWS, d).transpose(1, 0, 2, 3).reshape(nl, N_DEV * SMALL_ROWS, d)
    n_dm = nl * nb * 6
    tail = jnp.concatenate([jnp.stack(dmod_rows).reshape(n_dm, d), dfg], axis=0)
    tail_all = _all_gather(tail, "ag_dmod")
    dmod_all = tail_all[:, :n_dm].reshape(N_DEV, nl, nb, ncol).transpose(1, 0, 2, 3).reshape(nl, bsz, ncol)
    dmod_cols = lax.dynamic_slice_in_dim(dmod_all, dev * cols, cols, axis=2)
    g_w_ada, g_b_ada = _mod_bwd(c_all, dmod_cols, dmod_all)
    g_final = _sum_blocks(tail_all[:, n_dm:], "sum_final_g")[0]

    g_small = {
        "norm1_g": lay[:, 0], "a_ln_g": lay[:, 1], "a_ln_b": lay[:, 2], "b_ln_g": lay[:, 3], "b_ln_b": lay[:, 4],
        "b_conv_b": lay[:, 5], "norm2_g": lay[:, 6],
        "a_ws": lay[:, 8:8 + CHUNK].reshape(nl, CHUNK, HEADS, CHUNK).transpose(0, 2, 1, 3),
        "a_bs": jnp.swapaxes(lay[:, 8 + CHUNK:8 + 2 * CHUNK, ::CHUNK], 1, 2),
        "b_conv_w": lax.dynamic_slice_in_dim(
            lay[:, 8 + 2 * CHUNK:8 + 2 * CHUNK + CONV_TAPS], dev * cpd, cpd, axis=2).reshape(nl, CONV_TAPS, 1, cpd),
        "final_g": g_final,
    }
    grads = dict(g_small)
    grads["w_ada"] = g_w_ada
    grads["b_ada"] = g_b_ada.reshape(nl, ncol)
    for k, v in wg.items():
        grads[k] = jnp.stack(v)

    names = ["w_ada", "b_ada", "norm1_g", "w_in", "a_ln_g", "a_ln_b", "a_ws", "a_bs", "w_pa", "b_conv_w", "b_conv_b",
             "b_ln_g", "b_ln_b", "w_pb", "w_out", "norm2_g", "w_ff1", "w_ff2", "final_g"]
    weights = dict(w_ada=w_ada, b_ada=b_ada, norm1_g=norm1_g, w_in=w_in, a_ln_g=a_ln_g, a_ln_b=a_ln_b, a_ws=a_ws,
                   a_bs=a_bs, w_pa=w_pa, b_conv_w=b_conv_w, b_conv_b=b_conv_b, b_ln_g=b_ln_g, b_ln_b=b_ln_b,
                   w_pb=w_pb, w_out=w_out, norm2_g=norm2_g, w_ff1=w_ff1, w_ff2=w_ff2, final_g=final_g)
    m_in = dict(w_ada=m_w_ada, b_ada=m_b_ada, norm1_g=m_norm1_g, w_in=m_w_in, a_ln_g=m_a_ln_g, a_ln_b=m_a_ln_b,
                a_ws=m_a_ws, a_bs=m_a_bs, w_pa=m_w_pa, b_conv_w=m_b_conv_w, b_conv_b=m_b_conv_b, b_ln_g=m_b_ln_g,
                b_ln_b=m_b_ln_b, w_pb=m_w_pb, w_out=m_w_out, norm2_g=m_norm2_g, w_ff1=m_w_ff1, w_ff2=m_w_ff2,
                final_g=m_final_g)
    v_in = dict(w_ada=v_w_ada, b_ada=v_b_ada, norm1_g=v_norm1_g, w_in=v_w_in, a_ln_g=v_a_ln_g, a_ln_b=v_a_ln_b,
                a_ws=v_a_ws, a_bs=v_a_bs, w_pa=v_w_pa, b_conv_w=v_b_conv_w, b_conv_b=v_b_conv_b, b_ln_g=v_b_ln_g,
                b_ln_b=v_b_ln_b, w_pb=v_w_pb, w_out=v_w_out, norm2_g=v_norm2_g, w_ff1=v_w_ff1, w_ff2=v_w_ff2,
                final_g=v_final_g)

    deltas, new_m, new_v = {}, {}, {}
    for k in names:
        shape = weights[k].shape
        two_d = (-1, shape[-1])
        g2d = grads[k].reshape(shape).reshape(two_d)
        grads[k] = grads[k].reshape(shape)
        dl, nm, nv = _adamw(weights[k].reshape(two_d), g2d, m_in[k].reshape(two_d), v_in[k].reshape(two_d),
                            "adamw_" + k)
        deltas[k], new_m[k], new_v[k] = dl.reshape(shape), nm.reshape(shape), nv.reshape(shape)

    return (loss, dx.reshape(nb, seq, d), *[grads[k] for k in names], *[deltas[k] for k in names],
            *[new_m[k] for k in names], *[new_v[k] for k in names])
```

```python
import functools

import jax
import jax.numpy as jnp
from jax import lax
from jax.experimental import pallas as pl
from jax.experimental.pallas import tpu as pltpu

F32 = jnp.float32
BF16 = jnp.bfloat16
MESH = pl.DeviceIdType.MESH
ANY = pl.BlockSpec(memory_space=pl.ANY)

N_DEV = 8
N_CHIP = 4
EPS = 1e-6
CHUNK = 128
HEADS = 8
CONV_TAPS = 31
HALO = 32
SMALL_ROWS = 40
SMALL_SLOT = 128
CONV_ROWS = 32
CONV_WGRAD_TAPS = 4
TM_BIG = 512
MM_COLS = 512
TM_MIX = 256
CONV_SUB = 256
TK_WGRAD = 4096
TK_WGRAD_ONE = 1024
VMEM_LIMIT = 56 * 1024 * 1024

ADAM_LR = 0.001
ADAM_B1 = 0.9
ADAM_B2 = 0.999
ADAM_EPS = 1e-08
ADAM_WD = 0.01
ADAM_STEP = 10


def _sds(shape, dtype):
    return jax.ShapeDtypeStruct(tuple(shape), dtype)


def _params(n_grid, vmem=VMEM_LIMIT):
    return pltpu.CompilerParams(dimension_semantics=("arbitrary",) * n_grid, vmem_limit_bytes=vmem)


def _nn(a, b):
    return jnp.dot(a, b, preferred_element_type=F32)


def _nt(a, b):
    return lax.dot_general(a, b, (((1,), (1,)), ((), ())), preferred_element_type=F32)


def _tn(a, b):
    return lax.dot_general(a, b, (((0,), (0,)), ((), ())), preferred_element_type=F32)


def _rowsum(v):
    return jnp.sum(v, axis=0, keepdims=True)


def _mean(v):
    return jnp.mean(v, axis=-1, keepdims=True)


def _add_row(ref, idx, val):
    ref[idx] = ref[idx] + val


def _ln_stats(v):
    mu = _mean(v)
    xc = v - mu
    rs = lax.rsqrt(_mean(xc * xc) + EPS)
    return xc * rs, rs


def _ln_bwd(dout, g, vhat, rs):
    dvh = dout * g
    return rs * (dvh - _mean(dvh) - vhat * _mean(dvh * vhat))


def _rms_bwd(dn, g, x, r):
    gd = dn * g
    return r * gd - x * (r * r * r) * _mean(x * gd)


class _Pack:
    def __init__(self, d, small_slot):
        self.n_in = 6 * d // N_DEV
        self.n_ff = 4 * d // N_DEV
        self.n_p = d // N_DEV
        self.off_ff1 = 0
        self.off_ff2 = self.off_ff1 + self.n_ff
        self.off_pa = self.off_ff2 + self.n_ff
        self.off_pb = self.off_pa + self.n_p
        self.off_out = self.off_pb + self.n_p
        self.off_small = self.off_out + self.n_p
        self.off_in = self.off_small + small_slot
        self.rows = self.off_in + self.n_in
        if small_slot:
            assert self.off_in % self.n_in == 0 and self.off_ff2 % self.n_ff == 0 and self.off_small % small_slot == 0


def _load_rows(g_hbm, w_vm, sems, sem0, off, rows):
    cps = [
        pltpu.make_async_copy(g_hbm.at[k, pl.ds(off, rows), :], w_vm.at[pl.ds(k * rows, rows), :], sems.at[sem0 + k])
        for k in range(N_DEV)
    ]
    for cp in cps:
        cp.start()
    for cp in cps:
        cp.wait()


def _row_spec(tm, cols, colblk=0):
    return pl.BlockSpec((tm, cols), lambda i: (i, colblk))


def _const_spec(shape):
    nd = len(shape)
    return pl.BlockSpec(tuple(shape), lambda i: (0,) * nd)


def _mod_spec(tps, cols):
    return pl.BlockSpec((1, 1, cols), lambda i: (i // tps, 0, 0))


def _mstat_spec(tps, d):
    return pl.BlockSpec((1, 8, d), lambda i: (i // tps, 0, 0))


def _position():
    return lax.axis_index("x"), lax.axis_index("y"), lax.axis_index("c")


def _other_chips(x, y):
    return [(1 - x, y), (x, 1 - y), (1 - x, 1 - y)]


def _remote(src, dst, send_sems, recv_sems, k, to):
    return pltpu.make_async_remote_copy(src_ref=src, dst_ref=dst, send_sem=send_sems.at[k], recv_sem=recv_sems.at[k],
                                        device_id=to, device_id_type=MESH)


def _slot(ref, p):
    return ref.at[4 * p[0] + 2 * p[1] + p[2]]


def _ag_stage1(x_ref, out_ref, send_sems, recv_sems, local_sem):
    x, y, c = _position()
    me = (x, y, c)
    peers = [(x, y, 1 - c)] + [(*chip, c) for chip in _other_chips(x, y)]
    sends = [_remote(x_ref, _slot(out_ref, me), send_sems, recv_sems, k, p) for k, p in enumerate(peers)]
    recvs = [_remote(x_ref, _slot(out_ref, p), send_sems, recv_sems, k, p) for k, p in enumerate(peers)]
    return pltpu.make_async_copy(x_ref, _slot(out_ref, me), local_sem), sends, recvs


def _ag_stage2(in_ref, out_ref, send_sems, recv_sems):
    x, y, c = _position()
    sibling = (x, y, 1 - c)
    chips = _other_chips(x, y)
    sends = [_remote(_slot(in_ref, (*ch, c)), _slot(out_ref, (*ch, c)), send_sems, recv_sems, j, sibling)
             for j, ch in enumerate(chips)]
    recvs = [_remote(_slot(in_ref, (*ch, c)), _slot(out_ref, (*ch, 1 - c)), send_sems, recv_sems, j, sibling)
             for j, ch in enumerate(chips)]
    return sends, recvs


def _rs_chip_copies(x_ref, out_ref, send_sems, recv_sems, local_sem):
    x, y, c = _position()
    q_me = 2 * x + y
    chips = _other_chips(x, y)
    sends = [_remote(x_ref.at[2 * px + py], out_ref.at[q_me], send_sems, recv_sems, j, (px, py, c))
             for j, (px, py) in enumerate(chips)]
    recvs = [_remote(x_ref.at[q_me], out_ref.at[2 * px + py], send_sems, recv_sems, j, (px, py, c))
             for j, (px, py) in enumerate(chips)]
    return pltpu.make_async_copy(x_ref.at[q_me], out_ref.at[q_me], local_sem), sends, recvs


def _start_all(local, sends):
    if local is not None:
        local.start()
    for cp in sends:
        cp.start()


def _finish_all(local, sends, recvs):
    for cp in recvs:
        cp.wait_recv()
    for cp in sends:
        cp.wait_send()
    if local is not None:
        local.wait()


def _sem_scratch(n, local):
    out = [pltpu.SemaphoreType.DMA((n,)), pltpu.SemaphoreType.DMA((n,))]
    return out + ([pltpu.SemaphoreType.DMA(())] if local else [])


def _fwd_in(x2d, mod3, g1, cwb, cb, gw, pk, seq, tm, nxt=None):
    t, d = x2d.shape
    nc = 6 * d
    tps = seq // tm
    n = t // tm
    lo = HALO - (CONV_TAPS - 1)
    sub = min(CONV_SUB, tm)
    comm = nxt is not None

    def body(*refs):
        x_ref, mod_ref, g_ref, cw_ref, cb_ref, gw_hbm = refs[:6]
        refs = refs[6:]
        if comm:
            nx_ref, refs = refs[0], refs[1:]
        proj_ref, zc_ref = refs[:2]
        refs = refs[2:]
        if comm:
            gwn_ref, refs = refs[0], refs[1:]
        w_vm, sems, zext, zsh, zc_buf, ztail = refs[:6]
        i = pl.program_id(0)

        if comm:
            ag = functools.partial(_ag_stage1, nx_ref, gwn_ref, *refs[6:9])

            @pl.when(i == 0)
            def _():
                local, sends, _ = ag()
                _start_all(local, sends)

        @pl.when(i == 0)
        def _():
            _load_rows(gw_hbm, w_vm, sems, 0, pk.off_in, pk.n_in)
            ztail[...] = jnp.zeros(ztail.shape, F32)

        x = x_ref[...]
        m = mod_ref[0]
        r = lax.rsqrt(_mean(x * x) + EPS)
        h = (x * r * g_ref[...] * (1.0 + m[:, d:2 * d]) + m[:, 0:d]).astype(BF16)

        def chunk(j):
            js = slice(j * MM_COLS, (j + 1) * MM_COLS)
            proj_ref[:, js] = _nt(h, w_vm[js, :]).astype(BF16)

        glu = range(2 * d // MM_COLS, 4 * d // MM_COLS)
        for j in glu:
            chunk(j)
        z = proj_ref[:, 2 * d:3 * d].astype(F32) * jax.nn.sigmoid(proj_ref[:, 3 * d:4 * d].astype(F32))
        zext[pl.ds(0, HALO), :] = jnp.where((i % tps) == 0, 0.0, ztail[...])
        zext[pl.ds(HALO, tm), :] = z
        ztail[...] = zext[pl.ds(tm, HALO), :]
        for s in range(tm // sub):
            _shift_copies(zext, zsh, s * sub, sub + HALO - 8)
            _conv_taps(zext, zsh, s * sub, cw_ref, [lo + k for k in range(CONV_TAPS)], cb_ref[...], zc_buf, sub, d)
        zc_ref[...] = zc_buf[...].astype(BF16)
        for j in range(nc // MM_COLS):
            if j not in glu:
                chunk(j)

        if comm:
            @pl.when(i == n - 1)
            def _():
                _finish_all(*ag())

    return pl.pallas_call(
        body, grid=(n,),
        in_specs=[_row_spec(tm, d), _mod_spec(tps, nc), _const_spec((1, d)), _const_spec((8 * HALO, d)),
                  _const_spec((1, d)), ANY] + ([ANY] if comm else []),
        out_specs=[_row_spec(tm, nc), _row_spec(tm, d)] + ([ANY] if comm else []),
        out_shape=[_sds((t, nc), BF16), _sds((t, d), BF16)] + ([_sds((N_DEV,) + nxt.shape, nxt.dtype)] if comm else []),
        scratch_shapes=[pltpu.VMEM((nc, d), BF16), pltpu.SemaphoreType.DMA((N_DEV,)),
                        pltpu.VMEM((tm + HALO, d), F32), pltpu.VMEM((7, sub + HALO, d), F32),
                        pltpu.VMEM((tm, d), F32), pltpu.VMEM((HALO, d), F32)]
        + (_sem_scratch(4, True) if comm else []),
        compiler_params=_params(1), name="fwd_in_ag" if comm else "fwd_in",
    )(x2d, mod3, g1, cwb, cb, gw, *([nxt] if comm else []))


def _shift_copies(src, sh, base, rows):
    for r in range(1, 8):
        sh[r - 1, pl.ds(0, rows), :] = src[pl.ds(base + r, rows), :]


def _window(src, sh, base, offset, start, size):
    r, q = offset % 8, offset // 8
    if r == 0:
        return src[pl.ds(base + start + 8 * q, size), :]
    return sh[r - 1, pl.ds(start + 8 * q, size), :]


def _conv_taps(src, sh, base, cwb_ref, offsets, bias, out_ref, rows, d):
    nsub = CONV_ROWS // 8
    for rb in range(rows // CONV_ROWS):
        accs = [jnp.broadcast_to(bias, (8, d))] * nsub
        for k in range(CONV_TAPS):
            w8 = cwb_ref[pl.ds(8 * k, 8), :]
            accs = [a + w8 * _window(src, sh, base, offsets[k], rb * CONV_ROWS + 8 * j, 8) for j, a in enumerate(accs)]
        for j, a in enumerate(accs):
            out_ref[pl.ds(base + rb * CONV_ROWS + 8 * j, 8), :] = a


def _conv_wgrad(z_ref, dsrc, dsh, acc_ref, rows, d):
    for k0 in range(0, CONV_TAPS, CONV_WGRAD_TAPS):
        taps = list(range(k0, min(k0 + CONV_WGRAD_TAPS, CONV_TAPS)))
        accs = [jnp.zeros((8, d), F32)] * len(taps)
        for rb in range(rows // 8):
            zblk = z_ref[pl.ds(rb * 8, 8), :]
            accs = [a + zblk * _window(dsrc, dsh, 0, CONV_TAPS - 1 - k, rb * 8, 8) for a, k in zip(accs, taps)]
        for a, k in zip(accs, taps):
            acc_ref[pl.ds(8 * k, 8), :] = acc_ref[pl.ds(8 * k, 8), :] + a


def _fwd_mix(proj, zc, x2d, mod3, vecs, wm, bsx, gw, pk, seq, tm):
    t, d = x2d.shape
    tps = seq // tm

    def body(proj_ref, zc_ref, x_ref, mod_ref, vec_ref, wm_ref, bs_ref, gw_hbm,
             x1_ref, ya_ref, yb_ref, o1_ref, aa_ref, ba_ref, mg_ref, wpa, wpb, wout, sems, vn_buf):
        @pl.when(pl.program_id(0) == 0)
        def _():
            _load_rows(gw_hbm, wpa, sems, 0, pk.off_pa, pk.n_p)
            _load_rows(gw_hbm, wpb, sems, N_DEV, pk.off_pb, pk.n_p)
            _load_rows(gw_hbm, wout, sems, 2 * N_DEV, pk.off_out, pk.n_p)

        m = mod_ref[0]
        vhat, _ = _ln_stats(proj_ref[:, d:2 * d].astype(F32))
        vn_buf[...] = (vhat * vec_ref[0:1, :] + vec_ref[1:2, :]).astype(BF16)
        for c in range(tm // CHUNK):
            rs_ = slice(c * CHUNK, (c + 1) * CHUNK)
            for h in range(HEADS):
                cs_ = slice(h * CHUNK, (h + 1) * CHUNK)
                s_b = _nn(wm_ref[h], vn_buf[rs_, cs_]) + bs_ref[:, cs_]
                aa_ref[rs_, cs_] = (proj_ref[rs_, cs_].astype(F32) * s_b).astype(BF16)
        y_a = _nn(aa_ref[...], wpa[...])
        ya_ref[...] = y_a.astype(BF16)
        zhat, _ = _ln_stats(zc_ref[...].astype(F32))
        zn = zhat * vec_ref[3:4, :] + vec_ref[4:5, :]
        b_act = (zn * jax.nn.sigmoid(zn)).astype(BF16)
        ba_ref[...] = b_act
        y_b = _nn(b_act, wpb[...])
        yb_ref[...] = y_b.astype(BF16)
        merged = (jax.nn.sigmoid(proj_ref[:, 4 * d:5 * d].astype(F32)) * y_a
                  + jax.nn.sigmoid(proj_ref[:, 5 * d:6 * d].astype(F32)) * y_b).astype(BF16)
        mg_ref[...] = merged
        o1 = _nn(merged, wout[...])
        o1_ref[...] = o1.astype(BF16)
        x1_ref[...] = x_ref[...] + m[:, 2 * d:3 * d] * o1

    act = _sds((t, d), BF16)
    return pl.pallas_call(
        body, grid=(t // tm,),
        in_specs=[_row_spec(tm, 6 * d), _row_spec(tm, d), _row_spec(tm, d), _mod_spec(tps, 6 * d), _const_spec((8, d)),
                  _const_spec((HEADS, CHUNK, CHUNK)), _const_spec((CHUNK, d)), ANY],
        out_specs=[_row_spec(tm, d)] * 7,
        out_shape=[_sds((t, d), F32), act, act, act, act, act, act],
        scratch_shapes=[pltpu.VMEM((d, d), BF16), pltpu.VMEM((d, d), BF16), pltpu.VMEM((d, d), BF16),
                        pltpu.SemaphoreType.DMA((3 * N_DEV,)), pltpu.VMEM((tm, d), BF16)],
        compiler_params=_params(1), name="fwd_mix",
    )(proj, zc, x2d, mod3, vecs, wm, bsx, gw)


def _fwd_ffn(x1, mod3, g2, gw, pk, seq, tm, gw_next=None):
    t, d = x1.shape
    nf = 4 * d
    tps = seq // tm
    n = t // tm
    comm = gw_next is not None

    def body(*refs):
        x_ref, mod_ref, g_ref, gw_hbm = refs[:4]
        refs = refs[4:]
        if comm:
            gwn_in, refs = refs[0], refs[1:]
        x2_ref, f_ref, o2_ref = refs[:3]
        refs = refs[3:]
        if comm:
            gwn_out, refs = refs[0], refs[1:]
        w1, w2, sems = refs[:3]
        i = pl.program_id(0)

        if comm:
            ag = functools.partial(_ag_stage2, gwn_in, gwn_out, *refs[3:5])

            @pl.when(i == 0)
            def _():
                _start_all(None, ag()[0])

        @pl.when(i == 0)
        def _():
            _load_rows(gw_hbm, w1, sems, 0, pk.off_ff1, pk.n_ff)
            _load_rows(gw_hbm, w2, sems, N_DEV, pk.off_ff2, pk.n_ff)

        x = x_ref[...]
        m = mod_ref[0]
        r = lax.rsqrt(_mean(x * x) + EPS)
        h2 = (x * r * g_ref[...] * (1.0 + m[:, 4 * d:5 * d]) + m[:, 3 * d:4 * d]).astype(BF16)
        acc = jnp.zeros(x.shape, F32)
        for j in range(nf // MM_COLS):
            js = slice(j * MM_COLS, (j + 1) * MM_COLS)
            f = _nt(h2, w1[js, :])
            f_ref[:, js] = f.astype(BF16)
            acc = acc + _nn(jnp.square(jnp.maximum(f, 0.0)).astype(BF16), w2[js, :])
        o2_ref[...] = acc.astype(BF16)
        x2_ref[...] = x + m[:, 5 * d:6 * d] * acc

        if comm:
            @pl.when(i == n - 1)
            def _():
                _finish_all(None, *ag())

    return pl.pallas_call(
        body, grid=(n,),
        in_specs=[_row_spec(tm, d), _mod_spec(tps, 6 * d), _const_spec((1, d)), ANY] + ([ANY] if comm else []),
        out_specs=[_row_spec(tm, d), _row_spec(tm, nf), _row_spec(tm, d)] + ([ANY] if comm else []),
        out_shape=[_sds((t, d), F32), _sds((t, nf), BF16), _sds((t, d), BF16)]
        + ([_sds(gw_next.shape, gw_next.dtype)] if comm else []),
        scratch_shapes=[pltpu.VMEM((nf, d), BF16), pltpu.VMEM((nf, d), BF16), pltpu.SemaphoreType.DMA((2 * N_DEV,))]
        + (_sem_scratch(3, False) if comm else []),
        input_output_aliases={4: 3} if comm else {},
        compiler_params=_params(1), name="fwd_ffn_ag" if comm else "fwd_ffn",
    )(x1, mod3, g2, gw, *([gw_next] if comm else []))


def _loss_head(x, tgt, fg, tm):
    t, d = x.shape
    n = t // tm

    def body(x_ref, t_ref, g_ref, dx_ref, loss_ref, dg_ref, lacc):
        i = pl.program_id(0)

        @pl.when(i == 0)
        def _():
            lacc[...] = jnp.zeros(lacc.shape, F32)
            dg_ref[...] = jnp.zeros(dg_ref.shape, F32)

        xv = x_ref[...]
        g = g_ref[...]
        r = lax.rsqrt(_mean(xv * xv) + EPS)
        err = xv * r * g - t_ref[...]
        lacc[...] = lacc[...] + _rowsum(err * err)
        dy = err * (1.0 / d)
        _add_row(dg_ref, (slice(0, 1), slice(None)), _rowsum(dy * xv * r))
        dx_ref[...] = _rms_bwd(dy, g, xv, r)

        @pl.when(i == n - 1)
        def _():
            loss_ref[...] = jnp.broadcast_to(jnp.sum(lacc[...], keepdims=True) * (0.5 / d), loss_ref.shape)

    return pl.pallas_call(
        body, grid=(n,),
        in_specs=[_row_spec(tm, d), _row_spec(tm, d), _const_spec((1, d))],
        out_specs=[_row_spec(tm, d), _const_spec((8, 128)), _const_spec((8, d))],
        out_shape=[_sds((t, d), F32), _sds((8, 128), F32), _sds((8, d), F32)],
        scratch_shapes=[pltpu.VMEM((1, d), F32)],
        compiler_params=_params(1), name="loss_head",
    )(x, tgt, fg)


def _sibling_copy(x_ref, out_ref, send_sem, recv_sem):
    x, y, c = _position()
    return pltpu.make_async_remote_copy(
        src_ref=x_ref.at[pl.ds(0, x_ref.shape[0]), 1 - c], dst_ref=out_ref, send_sem=send_sem, recv_sem=recv_sem,
        device_id=(x, y, 1 - c), device_id_type=MESH)


def _bwd_ffn(dx2, x1, f, o2, mod3, g2, gw, pk, seq, tm, dp=None):
    t, d = x1.shape
    nf = 4 * d
    tps = seq // tm
    nb = t // seq
    steps = t // tm
    comm = dp is not None

    def body(*refs):
        dx2_ref, x_ref, f_ref, o2_ref, mod_ref, g_ref, gw_hbm = refs[:7]
        refs = refs[7:]
        if comm:
            dp_ref, refs = refs[0], refs[1:]
        dx1_ref, df_ref, do2_ref, h2_ref, ms_ref, ps_ref = refs[:6]
        refs = refs[6:]
        if comm:
            got_ref, refs = refs[0], refs[1:]
        w1, w2, sems = refs[:3]
        i = pl.program_id(0)

        if comm:
            swap = functools.partial(_sibling_copy, dp_ref, got_ref, *refs[3:5])

            @pl.when(i == 0)
            def _():
                swap().start()

        @pl.when(i == 0)
        def _():
            _load_rows(gw_hbm, w1, sems, 0, pk.off_ff1, pk.n_ff)
            _load_rows(gw_hbm, w2, sems, N_DEV, pk.off_ff2, pk.n_ff)
            ps_ref[...] = jnp.zeros(ps_ref.shape, F32)

        @pl.when((i % tps) == 0)
        def _():
            ms_ref[...] = jnp.zeros(ms_ref.shape, F32)

        dx2 = dx2_ref[...]
        x = x_ref[...]
        m = mod_ref[0]
        g = g_ref[...]
        sh2, sc2, gt2 = m[:, 3 * d:4 * d], m[:, 4 * d:5 * d], m[:, 5 * d:6 * d]
        _add_row(ms_ref, (0, slice(2, 3), slice(None)), _rowsum(dx2 * o2_ref[...].astype(F32)))
        do2 = (gt2 * dx2).astype(BF16)
        do2_ref[...] = do2
        r = lax.rsqrt(_mean(x * x) + EPS)
        n = x * r * g
        h2_ref[...] = (n * (1.0 + sc2) + sh2).astype(BF16)
        dh = jnp.zeros(x.shape, F32)
        for j in range(nf // MM_COLS):
            js = slice(j * MM_COLS, (j + 1) * MM_COLS)
            dr = _nt(do2, w2[js, :])
            df = (dr * (2.0 * jnp.maximum(f_ref[:, js].astype(F32), 0.0))).astype(BF16)
            df_ref[:, js] = df
            dh = dh + _nn(df, w1[js, :])
        _add_row(ms_ref, (0, slice(0, 1), slice(None)), _rowsum(dh))
        _add_row(ms_ref, (0, slice(1, 2), slice(None)), _rowsum(dh * n))
        dn = dh * (1.0 + sc2)
        _add_row(ps_ref, (slice(0, 1), slice(None)), _rowsum(dn * x * r))
        dx1_ref[...] = dx2 + _rms_bwd(dn, g, x, r)

        if comm:
            @pl.when(i == steps - 1)
            def _():
                swap().wait()

    act = _sds((t, d), BF16)
    return pl.pallas_call(
        body, grid=(steps,),
        in_specs=[_row_spec(tm, d), _row_spec(tm, d), _row_spec(tm, nf), _row_spec(tm, d), _mod_spec(tps, 6 * d),
                  _const_spec((1, d)), ANY] + ([ANY] if comm else []),
        out_specs=[_row_spec(tm, d), _row_spec(tm, nf), _row_spec(tm, d), _row_spec(tm, d), _mstat_spec(tps, d),
                   _const_spec((8, d))] + ([ANY] if comm else []),
        out_shape=[_sds((t, d), F32), _sds((t, nf), BF16), act, act, _sds((nb, 8, d), F32), _sds((8, d), F32)]
        + ([_sds((dp.shape[0],) + dp.shape[2:], dp.dtype)] if comm else []),
        scratch_shapes=[pltpu.VMEM((nf, d), BF16), pltpu.VMEM((nf, d), BF16), pltpu.SemaphoreType.DMA((2 * N_DEV,))]
        + ([pltpu.SemaphoreType.DMA(()), pltpu.SemaphoreType.DMA(())] if comm else []),
        compiler_params=_params(1), name="bwd_ffn_rs" if comm else "bwd_ffn",
    )(dx2, x1, f, o2, mod3, g2, gw, *([dp] if comm else []))


def _bwd_mix(dx1, proj, ya, yb, o1, zc, mod3, vecs, wm, wmt, bsx, gw, pk, seq, tm):
    t, d = dx1.shape
    tps = seq // tm
    nb = t // seq
    n = t // tm

    def body(dx1_ref, proj_ref, ya_ref, yb_ref, o1_ref, zc_ref, mod_ref, vec_ref, wm_ref, wmt_ref, bs_ref, gw_hbm,
             dp_ref, dzc_ref, do1_ref, dya_ref, dyb_ref, ms_ref, ps_ref, dws_ref, dbs_ref,
             wpa, wpb, wout, sems, vn_buf, da_buf, dvn_buf):
        i = pl.program_id(0)

        @pl.when(i == 0)
        def _():
            _load_rows(gw_hbm, wpa, sems, 0, pk.off_pa, pk.n_p)
            _load_rows(gw_hbm, wpb, sems, N_DEV, pk.off_pb, pk.n_p)
            _load_rows(gw_hbm, wout, sems, 2 * N_DEV, pk.off_out, pk.n_p)
            ps_ref[...] = jnp.zeros(ps_ref.shape, F32)
            dws_ref[...] = jnp.zeros(dws_ref.shape, F32)
            dbs_ref[...] = jnp.zeros(dbs_ref.shape, F32)

        @pl.when((i % tps) == 0)
        def _():
            ms_ref[...] = jnp.zeros(ms_ref.shape, F32)

        m = mod_ref[0]
        dx1v = dx1_ref[...]
        _add_row(ms_ref, (0, slice(0, 1), slice(None)), _rowsum(dx1v * o1_ref[...].astype(F32)))
        do1 = (m[:, 2 * d:3 * d] * dx1v).astype(BF16)
        do1_ref[...] = do1
        dmg = _nt(do1, wout[...])
        sa = jax.nn.sigmoid(proj_ref[:, 4 * d:5 * d].astype(F32))
        sb = jax.nn.sigmoid(proj_ref[:, 5 * d:6 * d].astype(F32))
        y_a = ya_ref[...].astype(F32)
        y_b = yb_ref[...].astype(F32)
        dya32 = dmg * sa
        dyb32 = dmg * sb
        dya = dya32.astype(BF16)
        dyb = dyb32.astype(BF16)
        dya_ref[...] = dya
        dyb_ref[...] = dyb
        dp_ref[:, 4 * d:5 * d] = (dya32 * y_a * (1.0 - sa)).astype(BF16)
        dp_ref[:, 5 * d:6 * d] = (dyb32 * y_b * (1.0 - sb)).astype(BF16)
        da_buf[...] = _nt(dya, wpa[...])
        db = _nt(dyb, wpb[...])
        vhat, rs = _ln_stats(proj_ref[:, d:2 * d].astype(F32))
        alg = vec_ref[0:1, :]
        vn_buf[...] = (vhat * alg + vec_ref[1:2, :]).astype(BF16)
        for c in range(tm // CHUNK):
            rs_ = slice(c * CHUNK, (c + 1) * CHUNK)
            for h in range(HEADS):
                cs_ = slice(h * CHUNK, (h + 1) * CHUNK)
                vn_b = vn_buf[rs_, cs_]
                s_b = _nn(wm_ref[h], vn_b) + bs_ref[:, cs_]
                u_b = proj_ref[rs_, cs_].astype(F32)
                da_b = da_buf[rs_, cs_]
                dp_ref[rs_, cs_] = (da_b * s_b).astype(BF16)
                ds_b = da_b * u_b
                dbs_ref[:, cs_] = dbs_ref[:, cs_] + ds_b
                ds_bf = ds_b.astype(BF16)
                dvn_buf[rs_, cs_] = _nn(wmt_ref[h], ds_bf)
                dws_ref[:, cs_] = dws_ref[:, cs_] + _nt(ds_bf, vn_b)
        dvn = dvn_buf[...]
        _add_row(ps_ref, (slice(0, 1), slice(None)), _rowsum(dvn * vhat))
        _add_row(ps_ref, (slice(1, 2), slice(None)), _rowsum(dvn))
        dp_ref[:, d:2 * d] = _ln_bwd(dvn, alg, vhat, rs).astype(BF16)
        dp_ref[:, 2 * d:4 * d] = jnp.zeros((tm, 2 * d), BF16)
        zhat, rsb = _ln_stats(zc_ref[...].astype(F32))
        blg = vec_ref[3:4, :]
        zn = zhat * blg + vec_ref[4:5, :]
        sg = jax.nn.sigmoid(zn)
        dzn = db * (sg * (1.0 + zn * (1.0 - sg)))
        _add_row(ps_ref, (slice(2, 3), slice(None)), _rowsum(dzn * zhat))
        _add_row(ps_ref, (slice(3, 4), slice(None)), _rowsum(dzn))
        dzc = _ln_bwd(dzn, blg, zhat, rsb)
        _add_row(ps_ref, (slice(4, 5), slice(None)), _rowsum(dzc))
        dzc_ref[...] = dzc.astype(BF16)

        @pl.when(i == n - 1)
        def _():
            causal = (lax.broadcasted_iota(jnp.int32, (CHUNK, CHUNK), 0)
                      >= lax.broadcasted_iota(jnp.int32, (CHUNK, CHUNK), 1))
            for h in range(HEADS):
                cs_ = slice(h * CHUNK, (h + 1) * CHUNK)
                dws_ref[:, cs_] = jnp.where(causal, dws_ref[:, cs_], 0.0)
                dbs_ref[:, cs_] = jnp.broadcast_to(jnp.sum(dbs_ref[:, cs_], axis=1, keepdims=True), (CHUNK, CHUNK))

    act = _sds((t, d), BF16)
    return pl.pallas_call(
        body, grid=(n,),
        in_specs=[_row_spec(tm, d), _row_spec(tm, 6 * d), _row_spec(tm, d), _row_spec(tm, d), _row_spec(tm, d),
                  _row_spec(tm, d), _mod_spec(tps, 6 * d), _const_spec((8, d)), _const_spec((HEADS, CHUNK, CHUNK)),
                  _const_spec((HEADS, CHUNK, CHUNK)), _const_spec((CHUNK, d)), ANY],
        out_specs=[_row_spec(tm, 6 * d)] + [_row_spec(tm, d)] * 4
        + [_mstat_spec(tps, d), _const_spec((8, d)), _const_spec((CHUNK, d)), _const_spec((CHUNK, d))],
        out_shape=[_sds((t, 6 * d), BF16)] + [act] * 4
        + [_sds((nb, 8, d), F32), _sds((8, d), F32), _sds((CHUNK, d), F32), _sds((CHUNK, d), F32)],
        scratch_shapes=[pltpu.VMEM((d, d), BF16), pltpu.VMEM((d, d), BF16), pltpu.VMEM((d, d), BF16),
                        pltpu.SemaphoreType.DMA((3 * N_DEV,)),
                        pltpu.VMEM((tm, d), BF16), pltpu.VMEM((tm, d), F32), pltpu.VMEM((tm, d), F32)],
        compiler_params=_params(1), name="bwd_mix",
    )(dx1, proj, ya, yb, o1, zc, mod3, vecs, wm, wmt, bsx, gw)


def _bwd_in(dproj, dzc, proj, x2d, dx1, mod3, g1, cw, gw, pk, seq, tm, part=None):
    t, d = x2d.shape
    tps = seq // tm
    nb = t // seq
    n = t // tm
    hb = tm // HALO
    comm = part is not None

    def body(*refs):
        dpi_ref, dzc_ref, dzn_ref, pp_ref, x_ref, dx1_ref, mod_ref, g_ref, cw_ref, gw_hbm = refs[:10]
        refs = refs[10:]
        if comm:
            part_ref, refs = refs[0], refs[1:]
        dpo_ref, dx_ref, h_ref, ms_ref, ps_ref, dcw_ref = refs[:6]
        refs = refs[6:]
        if comm:
            got_ref, refs = refs[0], refs[1:]
        w_vm, sems, z_buf, dzext, dsh, dz_buf, dcw_acc = refs[:7]
        i = pl.program_id(0)

        if comm:
            rs = functools.partial(_rs_chip_copies, part_ref, got_ref, *refs[7:10])

            @pl.when(i == 0)
            def _():
                local, sends, _ = rs()
                _start_all(local, sends)

        @pl.when(i == 0)
        def _():
            _load_rows(gw_hbm, w_vm, sems, 0, pk.off_in, pk.n_in)
            ps_ref[...] = jnp.zeros(ps_ref.shape, F32)
            dcw_acc[...] = jnp.zeros(dcw_acc.shape, F32)

        last = (i % tps) == tps - 1

        @pl.when((i % tps) == 0)
        def _():
            ms_ref[...] = jnp.zeros(ms_ref.shape, F32)

        pa = pp_ref[:, 0:d].astype(F32)
        sgp = jax.nn.sigmoid(pp_ref[:, d:2 * d].astype(F32))
        z_buf[...] = pa * sgp
        dzext[pl.ds(0, tm), :] = dzc_ref[...].astype(F32)
        dzext[pl.ds(tm, HALO), :] = jnp.where(last, 0.0, dzn_ref[...].astype(F32))
        _shift_copies(dzext, dsh, 0, tm + HALO - 8)
        _conv_wgrad(z_buf, dzext, dsh, dcw_acc, tm, d)
        _conv_taps(dzext, dsh, 0, cw_ref, [CONV_TAPS - 1 - k for k in range(CONV_TAPS)], jnp.zeros((1, d), F32),
                   dz_buf, tm, d)
        dz = dz_buf[...]
        dpa = (dz * sgp).astype(BF16)
        dpg = (dz * pa * sgp * (1.0 - sgp)).astype(BF16)
        dpo_ref[:, 0:d] = dpa
        dpo_ref[:, d:2 * d] = dpg
        dh = (_nn(dpi_ref[:, 0:2 * d], w_vm[0:2 * d, :]) + _nn(dpa, w_vm[2 * d:3 * d, :])
              + _nn(dpg, w_vm[3 * d:4 * d, :]) + _nn(dpi_ref[:, 4 * d:6 * d], w_vm[4 * d:6 * d, :]))
        x = x_ref[...]
        m = mod_ref[0]
        g = g_ref[...]
        sh1, sc1 = m[:, 0:d], m[:, d:2 * d]
        r = lax.rsqrt(_mean(x * x) + EPS)
        nrm = x * r * g
        h_ref[...] = (nrm * (1.0 + sc1) + sh1).astype(BF16)
        _add_row(ms_ref, (0, slice(0, 1), slice(None)), _rowsum(dh))
        _add_row(ms_ref, (0, slice(1, 2), slice(None)), _rowsum(dh * nrm))
        dn = dh * (1.0 + sc1)
        _add_row(ps_ref, (slice(0, 1), slice(None)), _rowsum(dn * x * r))
        dx_ref[...] = dx1_ref[...] + _rms_bwd(dn, g, x, r)

        @pl.when(i == n - 1)
        def _():
            for k in range(CONV_TAPS):
                dcw_ref[k:k + 1, :] = _rowsum(dcw_acc[pl.ds(k * 8, 8), :])
            dcw_ref[CONV_TAPS:HALO, :] = jnp.zeros((HALO - CONV_TAPS, d), F32)

        if comm:
            @pl.when(i == n - 1)
            def _():
                _finish_all(*rs())

    halo_next = pl.BlockSpec((HALO, d), lambda i: (jnp.minimum((i + 1) * hb, t // HALO - 1), 0))
    return pl.pallas_call(
        body, grid=(n,),
        in_specs=[_row_spec(tm, 6 * d), _row_spec(tm, d), halo_next, _row_spec(tm, 2 * d, 1),
                  _row_spec(tm, d), _row_spec(tm, d), _mod_spec(tps, 6 * d), _const_spec((1, d)),
                  _const_spec((8 * HALO, d)), ANY] + ([ANY] if comm else []),
        out_specs=[_row_spec(tm, 2 * d, 1), _row_spec(tm, d), _row_spec(tm, d), _mstat_spec(tps, d),
                   _const_spec((8, d)), _const_spec((HALO, d))] + ([ANY] if comm else []),
        out_shape=[_sds((t, 6 * d), BF16), _sds((t, d), F32), _sds((t, d), BF16), _sds((nb, 8, d), F32),
                   _sds((8, d), F32), _sds((HALO, d), F32)] + ([_sds(part.shape, part.dtype)] if comm else []),
        scratch_shapes=[pltpu.VMEM((6 * d, d), BF16), pltpu.SemaphoreType.DMA((N_DEV,)),
                        pltpu.VMEM((tm, d), F32), pltpu.VMEM((tm + HALO, d), F32),
                        pltpu.VMEM((7, tm + HALO, d), F32),
                        pltpu.VMEM((tm, d), F32), pltpu.VMEM((CONV_TAPS * 8, d), F32)]
        + (_sem_scratch(3, True) if comm else []),
        input_output_aliases={0: 0},
        compiler_params=_params(1), name="bwd_in_rs" if comm else "bwd_in",
    )(dproj, dzc, dzc, proj, x2d, dx1, mod3, g1, cw, gw, *([part] if comm else []))


def _wgrad(a, b, pack, kb, off, relu2, name, part=None):
    t, mo = a.shape
    nn_ = b.shape[1]
    rows = mo // N_DEV
    tk = min(TK_WGRAD_ONE if kb == N_DEV else TK_WGRAD, t)
    nk = t // tk
    ni = N_DEV // kb
    assert off % rows == 0 and N_DEV % kb == 0
    comm = part is not None

    def body(*refs):
        a_ref, b_ref, pack_hbm = refs[:3]
        refs = refs[3:]
        if comm:
            part_ref, refs = refs[0], refs[1:]
        o_ref, refs = refs[0], refs[1:]
        if comm:
            got_ref, refs = refs[0], refs[1:]
        acc = refs[0]
        i = pl.program_id(0)
        k = pl.program_id(1)

        if comm:
            rs = functools.partial(_rs_chip_copies, part_ref, got_ref, *refs[1:4])

            @pl.when((i == 0) & (k == 0))
            def _():
                local, sends, _ = rs()
                _start_all(local, sends)

        @pl.when(k == 0)
        def _():
            acc[...] = jnp.zeros(acc.shape, F32)

        av = a_ref[...]
        if relu2:
            av = jnp.square(jnp.maximum(av, 0.0))
        acc[...] = acc[...] + _tn(av, b_ref[...])

        @pl.when(k == nk - 1)
        def _():
            for j in range(kb):
                o_ref[j] = acc[pl.ds(j * rows, rows), :].astype(BF16)

        if comm:
            @pl.when((i == ni - 1) & (k == nk - 1))
            def _():
                _finish_all(*rs())

    return pl.pallas_call(
        body, grid=(ni, nk),
        in_specs=[pl.BlockSpec((tk, kb * rows), lambda i, k: (k, i)), pl.BlockSpec((tk, nn_), lambda i, k: (k, 0)), ANY]
        + ([ANY] if comm else []),
        out_specs=[pl.BlockSpec((kb, rows, nn_), lambda i, k: (i, off // rows, 0))] + ([ANY] if comm else []),
        out_shape=[_sds(pack.shape, BF16)] + ([_sds(part.shape, part.dtype)] if comm else []),
        scratch_shapes=[pltpu.VMEM((kb * rows, nn_), F32)] + (_sem_scratch(3, True) if comm else []),
        input_output_aliases={2: 0},
        compiler_params=_params(2), name=name,
    )(a, b, pack, *([part] if comm else []))


def _place_rows(pack, rows_blk, off):
    nblk, r, nn_ = rows_blk.shape
    assert off % r == 0

    def body(pack_hbm, s_ref, o_ref):
        o_ref[...] = s_ref[...]

    return pl.pallas_call(
        body, grid=(1,),
        in_specs=[ANY, pl.BlockSpec((nblk, r, nn_), lambda i: (0, 0, 0))],
        out_specs=pl.BlockSpec((nblk, r, nn_), lambda i: (0, off // r, 0)),
        out_shape=_sds(pack.shape, pack.dtype),
        input_output_aliases={0: 0},
        compiler_params=_params(1), name="place_small",
    )(pack, rows_blk)


def _mod_fwd(c_all, w_ada, b_cols):
    nl, d, cols = w_ada.shape
    bsz = c_all.shape[0]

    def body(c_ref, w_ref, b_ref, o_ref):
        cv = c_ref[...]
        ca = cv * jax.nn.sigmoid(cv)
        o_ref[0] = jnp.dot(ca, w_ref[0], preferred_element_type=F32, precision=lax.Precision.HIGHEST) + b_ref[0]

    return pl.pallas_call(
        body, grid=(nl,),
        in_specs=[_const_spec((bsz, d)), pl.BlockSpec((1, d, cols), lambda l: (l, 0, 0)),
                  pl.BlockSpec((1, 1, cols), lambda l: (l, 0, 0))],
        out_specs=pl.BlockSpec((1, bsz, cols), lambda l: (l, 0, 0)),
        out_shape=_sds((nl, bsz, cols), F32),
        compiler_params=_params(1), name="mod_fwd",
    )(c_all, w_ada, b_cols)


def _mod_bwd(c_all, dmod_cols, dmod_all):
    nl, bsz, cols = dmod_cols.shape
    d = c_all.shape[1]
    ncol = dmod_all.shape[2]

    def body(c_ref, dc_ref, da_ref, dw_ref, db_ref):
        cv = c_ref[...]
        ca = cv * jax.nn.sigmoid(cv)
        dw_ref[0] = lax.dot_general(ca, dc_ref[0], (((0,), (0,)), ((), ())), preferred_element_type=F32,
                                    precision=lax.Precision.HIGHEST)
        db_ref[0] = _rowsum(da_ref[0])

    return pl.pallas_call(
        body, grid=(nl,),
        in_specs=[_const_spec((bsz, d)), pl.BlockSpec((1, bsz, cols), lambda l: (l, 0, 0)),
                  pl.BlockSpec((1, bsz, ncol), lambda l: (l, 0, 0))],
        out_specs=[pl.BlockSpec((1, d, cols), lambda l: (l, 0, 0)), pl.BlockSpec((1, 1, ncol), lambda l: (l, 0, 0))],
        out_shape=[_sds((nl, d, cols), F32), _sds((nl, 1, ncol), F32)],
        compiler_params=_params(1), name="mod_bwd",
    )(c_all, dmod_cols, dmod_all)


def _row_tile(rows, cols, nbuf, itemsize=4, budget=24 * 1024 * 1024):
    cap = max(16, budget // (2 * nbuf * cols * itemsize))
    if rows <= cap:
        return rows
    best = None
    for tr in range(16, cap + 1, 16):
        if rows % tr == 0:
            best = tr
    assert best is not None, (rows, cols)
    return best


def _sum_blocks(xs, name):
    nblk, rows, cols = xs.shape
    tr = _row_tile(rows, cols, nblk + 1)

    def body(x_ref, o_ref):
        acc = x_ref[0].astype(F32)
        for j in range(1, nblk):
            acc = acc + x_ref[j].astype(F32)
        o_ref[...] = acc

    return pl.pallas_call(
        body, grid=(rows // tr,),
        in_specs=[pl.BlockSpec((nblk, tr, cols), lambda i: (0, i, 0))],
        out_specs=pl.BlockSpec((tr, cols), lambda i: (i, 0)),
        out_shape=_sds((rows, cols), F32),
        compiler_params=_params(1), name=name,
    )(xs)


def _add_sibling(dp, recv, core):
    nq, _, rows, cols = dp.shape
    tr = _row_tile(rows, cols, 3, itemsize=2)

    def body(c_ref, a_ref, b_ref, o_ref):
        o_ref[...] = (a_ref[...].astype(F32) + b_ref[...].astype(F32)).astype(BF16)

    return pl.pallas_call(
        body,
        grid_spec=pltpu.PrefetchScalarGridSpec(
            num_scalar_prefetch=1, grid=(nq, rows // tr),
            in_specs=[pl.BlockSpec((1, 1, tr, cols), lambda q, i, c: (q, c[0], i, 0)),
                      pl.BlockSpec((1, 1, tr, cols), lambda q, i, c: (q, 0, i, 0))],
            out_specs=pl.BlockSpec((1, 1, tr, cols), lambda q, i, c: (q, 0, i, 0))),
        out_shape=_sds((nq, 1, rows, cols), BF16),
        compiler_params=_params(2), name="add_sibling",
    )(core, dp, recv.reshape(nq, 1, rows, cols)).reshape(nq, rows, cols)


def _adamw(w, g, m, v, name):
    rows, cols = w.shape
    tr = _row_tile(rows, cols, 7)
    c1 = 1.0 - ADAM_B1 ** ADAM_STEP
    c2 = 1.0 - ADAM_B2 ** ADAM_STEP

    def body(w_ref, g_ref, m_ref, v_ref, d_ref, nm_ref, nv_ref):
        gv = g_ref[...]
        nm = ADAM_B1 * m_ref[...] + (1.0 - ADAM_B1) * gv
        nv = ADAM_B2 * v_ref[...] + (1.0 - ADAM_B2) * (gv * gv)
        nm_ref[...] = nm
        nv_ref[...] = nv
        d_ref[...] = -ADAM_LR * ((nm / c1) / (jnp.sqrt(nv / c2) + ADAM_EPS) + ADAM_WD * w_ref[...])

    spec = pl.BlockSpec((tr, cols), lambda i: (i, 0))
    out = _sds((rows, cols), F32)
    return pl.pallas_call(
        body, grid=(rows // tr,), in_specs=[spec] * 4, out_specs=[spec] * 3, out_shape=[out] * 3,
        compiler_params=_params(1), name=name,
    )(w, g, m, v)


def _all_gather(xs, name):
    rows, cols = xs.shape

    def body(x_ref, out_ref, send1, recv1, local_sem, send2, recv2):
        local, first, arrivals = _ag_stage1(x_ref, out_ref, send1, recv1, local_sem)
        _start_all(local, first)
        passed, from_sibling = _ag_stage2(out_ref, out_ref, send2, recv2)
        for arrival, onward in zip(arrivals[1:], passed):
            arrival.wait_recv()
            onward.start()
        arrivals[0].wait_recv()
        _finish_all(local, first + passed, from_sibling)

    return pl.pallas_call(
        body, out_shape=_sds((N_DEV, rows, cols), xs.dtype), in_specs=[ANY], out_specs=ANY,
        scratch_shapes=_sem_scratch(4, True) + _sem_scratch(3, False), name=name,
    )(xs)


def _sibling_exchange(dp):
    nq, _, rows, cols = dp.shape

    def body(x_ref, out_ref, send_sem, recv_sem):
        cp = _sibling_copy(x_ref, out_ref, send_sem, recv_sem)
        cp.start()
        cp.wait()

    return pl.pallas_call(
        body, out_shape=_sds((nq, rows, cols), dp.dtype), in_specs=[ANY], out_specs=ANY,
        scratch_shapes=[pltpu.SemaphoreType.DMA(()), pltpu.SemaphoreType.DMA(())],
        name="rs_sibling",
    )(dp)


def _chip_all_to_all(xs):
    nq, rows, cols = xs.shape

    def body(x_ref, out_ref, send_sems, recv_sems, local_sem):
        local, sends, recvs = _rs_chip_copies(x_ref, out_ref, send_sems, recv_sems, local_sem)
        _start_all(local, sends)
        _finish_all(local, sends, recvs)

    return pl.pallas_call(
        body, out_shape=_sds((nq, rows, cols), xs.dtype), in_specs=[ANY], out_specs=ANY,
        scratch_shapes=_sem_scratch(3, True), name="rs_chips",
    )(xs)


def _pad_rows(a, rows):
    return jnp.pad(a, ((0, rows - a.shape[0]), (0, 0)))


def kernel(x, c, w_ada, b_ada, norm1_g, w_in, a_ln_g, a_ln_b, a_ws, a_bs, w_pa, b_conv_w, b_conv_b, b_ln_g, b_ln_b, w_pb, w_out, norm2_g, w_ff1, w_ff2, final_g, loss_target, m_w_ada, m_b_ada, m_norm1_g, m_w_in, m_a_ln_g, m_a_ln_b, m_a_ws, m_a_bs, m_w_pa, m_b_conv_w, m_b_conv_b, m_b_ln_g, m_b_ln_b, m_w_pb, m_w_out, m_norm2_g, m_w_ff1, m_w_ff2, m_final_g, v_w_ada, v_b_ada, v_norm1_g, v_w_in, v_a_ln_g, v_a_ln_b, v_a_ws, v_a_bs, v_w_pa, v_b_conv_w, v_b_conv_b, v_b_ln_g, v_b_ln_b, v_w_pb, v_w_out, v_norm2_g, v_w_ff1, v_w_ff2, v_final_g):
    nb, seq, d = x.shape
    nl = w_in.shape[0]
    t = nb * seq
    pk = _Pack(d, 0)
    gk = _Pack(d, SMALL_SLOT)
    assert d % (N_DEV * CHUNK) == 0 and d // HEADS == CHUNK and seq % CHUNK == 0
    tm_big = min(TM_BIG, seq)
    tm_mix = min(TM_MIX, seq)
    ax, ay, ac = _position()
    dev = 4 * ax + 2 * ay + ac
    ncol = 6 * d
    cols = ncol // N_DEV
    cpd = d // N_DEV
    bsz = nb * N_DEV

    cw_rows = nl * HALO
    small = jnp.concatenate([
        c.reshape(nb * d // CHUNK, CHUNK),
        jnp.pad(b_conv_w.reshape(nl, CONV_TAPS, cpd), ((0, 0), (0, HALO - CONV_TAPS), (0, 0))).reshape(cw_rows, cpd),
    ], axis=0)
    c_rows = nb * d // CHUNK
    small_all = _all_gather(small, "ag_small")
    c_all = small_all[:, :c_rows].reshape(bsz, d)
    cw_all = small_all[:, c_rows:].reshape(N_DEV, nl, HALO, cpd).transpose(1, 2, 0, 3).reshape(nl, HALO, d)
    cwb_all = jnp.repeat(cw_all, 8, axis=1)
    b_cols = lax.dynamic_slice_in_dim(b_ada, dev * cols, cols, axis=1).reshape(nl, 1, cols)
    mod_cols = _mod_fwd(c_all, w_ada, b_cols)
    mod_all = _all_gather(mod_cols.reshape(nl * bsz, cols), "ag_mod")
    mod_all = mod_all.reshape(N_DEV, nl, bsz, cols).transpose(1, 2, 0, 3).reshape(nl, bsz, ncol)
    mod_mine = lax.dynamic_slice_in_dim(mod_all, dev * nb, nb, axis=1)

    causal = jnp.tril(jnp.ones((CHUNK, CHUNK), bool))
    wm_all = jnp.where(causal[None, None], a_ws, 0.0)
    wm_bf = wm_all.astype(BF16)
    wmt_bf = jnp.swapaxes(wm_all, 2, 3).astype(BF16)
    bsx_all = jnp.broadcast_to(jnp.swapaxes(a_bs, 1, 2)[:, :, :, None], (nl, CHUNK, HEADS, CHUNK)).reshape(nl, CHUNK, d)

    def vec_rows(l):
        return _pad_rows(jnp.stack([a_ln_g[l], a_ln_b[l], b_conv_b[l], b_ln_g[l], b_ln_b[l]]), 8)

    def weight_block(l):
        return jnp.concatenate([
            w_ff1[l].T, w_ff2[l], w_pa[l], w_pb[l], w_out[l], w_in[l].T], axis=0).astype(BF16)

    xs = x.reshape(t, d)
    saved = []
    gw = _all_gather(weight_block(0), "ag_weights")
    for l in range(nl):
        nxt = weight_block(l + 1) if l + 1 < nl else None
        mod3 = mod_mine[l].reshape(nb, 1, ncol)
        vecs = vec_rows(l)
        proj, zc, *gw_next = _fwd_in(xs, mod3, norm1_g[l].reshape(1, d), cwb_all[l], b_conv_b[l].reshape(1, d), gw, pk,
                                     seq, tm_big, nxt)
        x1, ya, yb, o1, aa, ba, mg = _fwd_mix(proj, zc, xs, mod3, vecs, wm_bf[l], bsx_all[l], gw, pk, seq, tm_big)
        x2, f, o2, *gw_next = _fwd_ffn(x1, mod3, norm2_g[l].reshape(1, d), gw, pk, seq, tm_big, *gw_next)
        saved.append((xs, x1, proj, ya, yb, o1, zc, f, o2, gw, mod3, vecs, aa, ba, mg))
        xs = x2
        if gw_next:
            gw = gw_next[0]

    dx, loss_blk, dfg = _loss_head(xs, loss_target.reshape(t, d), final_g.reshape(1, d), tm_big)
    loss = lax.psum(loss_blk[0, 0], ("x", "y", "c"))

    core = ac.reshape(1).astype(jnp.int32)
    wg = {k: [None] * nl for k in ("w_in", "w_ff1", "w_ff2", "w_pa", "w_pb", "w_out")}
    small_red = [None] * nl
    dmod_rows = [None] * nl
    per_layer = 8 + 2 * CHUNK + HALO
    assert per_layer <= N_DEV * SMALL_ROWS <= N_DEV * SMALL_SLOT

    rows_a = gk.off_in

    def reduced(l, red):
        reduced_a(l, red[:rows_a])
        wg["w_in"][l] = red[rows_a:].T

    def reduced_a(l, red):
        wg["w_ff1"][l] = red[gk.off_ff1:gk.off_ff1 + gk.n_ff].T
        wg["w_ff2"][l] = red[gk.off_ff2:gk.off_ff2 + gk.n_ff]
        wg["w_pa"][l] = red[gk.off_pa:gk.off_pa + gk.n_p]
        wg["w_pb"][l] = red[gk.off_pb:gk.off_pb + gk.n_p]
        wg["w_out"][l] = red[gk.off_out:gk.off_out + gk.n_p]
        small_red[l] = red[gk.off_small:gk.off_small + SMALL_ROWS]

    waiting = None
    pending = None
    for l in reversed(range(nl)):
        x0, x1, proj, ya, yb, o1, zc, f, o2, gw, mod3, vecs, aa, ba, mg = saved[l]
        dx1, df, do2, h2, ms2, ps2, *got = _bwd_ffn(dx, x1, f, o2, mod3, norm2_g[l].reshape(1, d), gw, pk, seq, tm_big,
                                                    None if waiting is None else waiting[1])
        if waiting is not None:
            pending = (waiting[0], _add_sibling(waiting[1], got[0], core))
        (dproj, dzc, do1, dya, dyb, ms1, ps1, dws, dbs) = _bwd_mix(
            dx1, proj, ya, yb, o1, zc, mod3, vecs, wm_bf[l], wmt_bf[l], bsx_all[l], gw, pk, seq, tm_mix)
        apart = l == 0
        grads = lax.empty((N_DEV, rows_a if apart else gk.rows, d), BF16)
        grads, = _wgrad(df, h2, grads, 2, gk.off_ff1, False, "wgrad_ff1")
        grads, = _wgrad(f, do2, grads, 2, gk.off_ff2, True, "wgrad_ff2")
        dproj, dx, h, ms0, ps0, dcw, *got = _bwd_in(
            dproj, dzc, proj, x0, dx1, mod3, norm1_g[l].reshape(1, d), cwb_all[l], gw, pk, seq, tm_mix,
            None if pending is None else pending[1])
        if pending is not None:
            reduced(pending[0], _sum_blocks(got[0], "sum_chips"))
        vec_g = jnp.concatenate([ps0[0:1], ps1[0:5], ps2[0:1], jnp.zeros((1, d), F32)], axis=0)
        small = _pad_rows(jnp.concatenate([vec_g, dws, dbs, dcw], axis=0), N_DEV * SMALL_ROWS)
        small = jnp.pad(small.reshape(N_DEV, SMALL_ROWS, d).astype(BF16), ((0, 0), (0, SMALL_SLOT - SMALL_ROWS), (0, 0)))
        grads = _place_rows(grads, small, gk.off_small)
        grads, = _wgrad(aa, dya, grads, N_DEV, gk.off_pa, False, "wgrad_pa")
        grads, = _wgrad(ba, dyb, grads, N_DEV, gk.off_pb, False, "wgrad_pb")
        grads, = _wgrad(mg, do1, grads, N_DEV, gk.off_out, False, "wgrad_out")
        if apart:
            dp_a = grads.reshape(N_CHIP, 2, rows_a, d)
            part_a = _add_sibling(dp_a, _sibling_exchange(dp_a), core)
            g_in, got_a = _wgrad(dproj, h, lax.empty((N_DEV, gk.n_in, d), BF16), 1, 0, False, "wgrad_in_rs", part_a)
            reduced_a(l, _sum_blocks(got_a, "sum_chips_a"))
            dp_b = g_in.reshape(N_CHIP, 2, gk.n_in, d)
            part_b = _add_sibling(dp_b, _sibling_exchange(dp_b), core)
            wg["w_in"][l] = _sum_blocks(_chip_all_to_all(part_b), "sum_chips_b").T
        else:
            grads, = _wgrad(dproj, h, grads, 1, gk.off_in, False, "wgrad_in")
            waiting = (l, grads.reshape(N_CHIP, 2, gk.rows, d))
        pending = None
        dmod_rows[l] = jnp.concatenate([ms0[:, 0], ms0[:, 1], ms1[:, 0], ms2[:, 0], ms2[:, 1], ms2[:, 2]], axis=1)

    n_sm = nl * SMALL_ROWS
    n_dm = nl * nb * 6
    tail = jnp.concatenate(small_red + [jnp.stack(dmod_rows).reshape(n_dm, d), dfg], axis=0)
    tail_all = _all_gather(tail, "ag_small_grads")
    lay = tail_all[:, :n_sm].reshape(N_DEV, nl, SMALL_ROWS, d).transpose(1, 0, 2, 3).reshape(nl, N_DEV * SMALL_ROWS, d)
    tail_all = tail_all[:, n_sm:]
    dmod_all = tail_all[:, :n_dm].reshape(N_DEV, nl, nb, ncol).transpose(1, 0, 2, 3).reshape(nl, bsz, ncol)
    dmod_cols = lax.dynamic_slice_in_dim(dmod_all, dev * cols, cols, axis=2)
    g_w_ada, g_b_ada = _mod_bwd(c_all, dmod_cols, dmod_all)
    g_final = _sum_blocks(tail_all[:, n_dm:], "sum_final_g")[0]

    g_small = {
        "norm1_g": lay[:, 0], "a_ln_g": lay[:, 1], "a_ln_b": lay[:, 2], "b_ln_g": lay[:, 3], "b_ln_b": lay[:, 4],
        "b_conv_b": lay[:, 5], "norm2_g": lay[:, 6],
        "a_ws": lay[:, 8:8 + CHUNK].reshape(nl, CHUNK, HEADS, CHUNK).transpose(0, 2, 1, 3),
        "a_bs": jnp.swapaxes(lay[:, 8 + CHUNK:8 + 2 * CHUNK, ::CHUNK], 1, 2),
        "b_conv_w": lax.dynamic_slice_in_dim(
            lay[:, 8 + 2 * CHUNK:8 + 2 * CHUNK + CONV_TAPS], dev * cpd, cpd, axis=2).reshape(nl, CONV_TAPS, 1, cpd),
        "final_g": g_final,
    }
    grads = dict(g_small)
    grads["w_ada"] = g_w_ada
    grads["b_ada"] = g_b_ada.reshape(nl, ncol)
    for k, v in wg.items():
        grads[k] = jnp.stack(v)

    names = ["w_ada", "b_ada", "norm1_g", "w_in", "a_ln_g", "a_ln_b", "a_ws", "a_bs", "w_pa", "b_conv_w", "b_conv_b",
             "b_ln_g", "b_ln_b", "w_pb", "w_out", "norm2_g", "w_ff1", "w_ff2", "final_g"]
    weights = dict(w_ada=w_ada, b_ada=b_ada, norm1_g=norm1_g, w_in=w_in, a_ln_g=a_ln_g, a_ln_b=a_ln_b, a_ws=a_ws,
                   a_bs=a_bs, w_pa=w_pa, b_conv_w=b_conv_w, b_conv_b=b_conv_b, b_ln_g=b_ln_g, b_ln_b=b_ln_b,
                   w_pb=w_pb, w_out=w_out, norm2_g=norm2_g, w_ff1=w_ff1, w_ff2=w_ff2, final_g=final_g)
    m_in = dict(w_ada=m_w_ada, b_ada=m_b_ada, norm1_g=m_norm1_g, w_in=m_w_in, a_ln_g=m_a_ln_g, a_ln_b=m_a_ln_b,
                a_ws=m_a_ws, a_bs=m_a_bs, w_pa=m_w_pa, b_conv_w=m_b_conv_w, b_conv_b=m_b_conv_b, b_ln_g=m_b_ln_g,
                b_ln_b=m_b_ln_b, w_pb=m_w_pb, w_out=m_w_out, norm2_g=m_norm2_g, w_ff1=m_w_ff1, w_ff2=m_w_ff2,
                final_g=m_final_g)
    v_in = dict(w_ada=v_w_ada, b_ada=v_b_ada, norm1_g=v_norm1_g, w_in=v_w_in, a_ln_g=v_a_ln_g, a_ln_b=v_a_ln_b,
                a_ws=v_a_ws, a_bs=v_a_bs, w_pa=v_w_pa, b_conv_w=v_b_conv_w, b_conv_b=v_b_conv_b, b_ln_g=v_b_ln_g,
                b_ln_b=v_b_ln_b, w_pb=v_w_pb, w_out=v_w_out, norm2_g=v_norm2_g, w_ff1=v_w_ff1, w_ff2=v_w_ff2,
                final_g=v_final_g)

    deltas, new_m, new_v = {}, {}, {}
    for k in names:
        shape = weights[k].shape
        two_d = (-1, shape[-1])
        g2d = grads[k].reshape(shape).reshape(two_d)
        grads[k] = grads[k].reshape(shape)
        dl, nm, nv = _adamw(weights[k].reshape(two_d), g2d, m_in[k].reshape(two_d), v_in[k].reshape(two_d),
                            "adamw_" + k)
        deltas[k], new_m[k], new_v[k] = dl.reshape(shape), nm.reshape(shape), nv.reshape(shape)

    return (loss, dx.reshape(nb, seq, d), *[grads[k] for k in names], *[deltas[k] for k in names],
            *[new_m[k] for k in names], *[new_v[k] for k in names])
```

```python
import functools

import jax
import jax.numpy as jnp
from jax import lax
from jax.experimental import pallas as pl
from jax.experimental.pallas import tpu as pltpu

F32 = jnp.float32
BF16 = jnp.bfloat16
MESH = pl.DeviceIdType.MESH
ANY = pl.BlockSpec(memory_space=pl.ANY)

N_DEV = 8
N_CHIP = 4
EPS = 1e-6
CHUNK = 128
HEADS = 8
CONV_TAPS = 31
HALO = 32
SMALL_ROWS = 40
SMALL_SLOT = 128
CONV_ROWS = 32
CONV_WGRAD_TAPS = 4
TM_BIG = 512
MM_COLS = 512
TM_MIX = 256
CONV_SUB = 256
TK_WGRAD = 4096
TK_WGRAD_ONE = 1024
VMEM_LIMIT = 56 * 1024 * 1024

ADAM_LR = 0.001
ADAM_B1 = 0.9
ADAM_B2 = 0.999
ADAM_EPS = 1e-08
ADAM_WD = 0.01
ADAM_STEP = 10


def _sds(shape, dtype):
    return jax.ShapeDtypeStruct(tuple(shape), dtype)


def _params(n_grid, vmem=VMEM_LIMIT):
    return pltpu.CompilerParams(dimension_semantics=("arbitrary",) * n_grid, vmem_limit_bytes=vmem)


def _nn(a, b):
    return jnp.dot(a, b, preferred_element_type=F32)


def _nt(a, b):
    return lax.dot_general(a, b, (((1,), (1,)), ((), ())), preferred_element_type=F32)


def _tn(a, b):
    return lax.dot_general(a, b, (((0,), (0,)), ((), ())), preferred_element_type=F32)


def _rowsum(v):
    return jnp.sum(v, axis=0, keepdims=True)


def _mean(v):
    return jnp.mean(v, axis=-1, keepdims=True)


def _add_row(ref, idx, val):
    ref[idx] = ref[idx] + val


def _ln_stats(v):
    mu = _mean(v)
    xc = v - mu
    rs = lax.rsqrt(_mean(xc * xc) + EPS)
    return xc * rs, rs


def _ln_bwd(dout, g, vhat, rs):
    dvh = dout * g
    return rs * (dvh - _mean(dvh) - vhat * _mean(dvh * vhat))


def _rms_bwd(dn, g, x, r):
    gd = dn * g
    return r * gd - x * (r * r * r) * _mean(x * gd)


class _Pack:
    def __init__(self, d, small_slot):
        self.n_in = 6 * d // N_DEV
        self.n_ff = 4 * d // N_DEV
        self.n_p = d // N_DEV
        self.off_ff1 = 0
        self.off_ff2 = self.off_ff1 + self.n_ff
        self.off_pa = self.off_ff2 + self.n_ff
        self.off_pb = self.off_pa + self.n_p
        self.off_out = self.off_pb + self.n_p
        self.off_small = self.off_out + self.n_p
        self.off_in = self.off_small + small_slot
        self.rows = self.off_in + self.n_in
        if small_slot:
            assert self.off_in % self.n_in == 0 and self.off_ff2 % self.n_ff == 0 and self.off_small % small_slot == 0


def _load_rows(g_hbm, w_vm, sems, sem0, off, rows):
    cps = [
        pltpu.make_async_copy(g_hbm.at[k, pl.ds(off, rows), :], w_vm.at[pl.ds(k * rows, rows), :], sems.at[sem0 + k])
        for k in range(N_DEV)
    ]
    for cp in cps:
        cp.start()
    for cp in cps:
        cp.wait()


def _row_spec(tm, cols, colblk=0):
    return pl.BlockSpec((tm, cols), lambda i: (i, colblk))


def _const_spec(shape):
    nd = len(shape)
    return pl.BlockSpec(tuple(shape), lambda i: (0,) * nd)


def _mod_spec(tps, cols):
    return pl.BlockSpec((1, 1, cols), lambda i: (i // tps, 0, 0))


def _mstat_spec(tps, d):
    return pl.BlockSpec((1, 8, d), lambda i: (i // tps, 0, 0))


def _position():
    return lax.axis_index("x"), lax.axis_index("y"), lax.axis_index("c")


def _other_chips(x, y):
    return [(1 - x, y), (x, 1 - y), (1 - x, 1 - y)]


def _remote(src, dst, send_sems, recv_sems, k, to):
    return pltpu.make_async_remote_copy(src_ref=src, dst_ref=dst, send_sem=send_sems.at[k], recv_sem=recv_sems.at[k],
                                        device_id=to, device_id_type=MESH)


def _slot(ref, p):
    return ref.at[4 * p[0] + 2 * p[1] + p[2]]


def _ag_stage1(x_ref, out_ref, send_sems, recv_sems, local_sem):
    x, y, c = _position()
    me = (x, y, c)
    peers = [(x, y, 1 - c)] + [(*chip, c) for chip in _other_chips(x, y)]
    sends = [_remote(x_ref, _slot(out_ref, me), send_sems, recv_sems, k, p) for k, p in enumerate(peers)]
    recvs = [_remote(x_ref, _slot(out_ref, p), send_sems, recv_sems, k, p) for k, p in enumerate(peers)]
    return pltpu.make_async_copy(x_ref, _slot(out_ref, me), local_sem), sends, recvs


def _ag_stage2(in_ref, out_ref, send_sems, recv_sems):
    x, y, c = _position()
    sibling = (x, y, 1 - c)
    chips = _other_chips(x, y)
    sends = [_remote(_slot(in_ref, (*ch, c)), _slot(out_ref, (*ch, c)), send_sems, recv_sems, j, sibling)
             for j, ch in enumerate(chips)]
    recvs = [_remote(_slot(in_ref, (*ch, c)), _slot(out_ref, (*ch, 1 - c)), send_sems, recv_sems, j, sibling)
             for j, ch in enumerate(chips)]
    return sends, recvs


def _rs_chip_copies(x_ref, out_ref, send_sems, recv_sems, local_sem):
    x, y, c = _position()
    q_me = 2 * x + y
    chips = _other_chips(x, y)
    sends = [_remote(x_ref.at[2 * px + py], out_ref.at[q_me], send_sems, recv_sems, j, (px, py, c))
             for j, (px, py) in enumerate(chips)]
    recvs = [_remote(x_ref.at[q_me], out_ref.at[2 * px + py], send_sems, recv_sems, j, (px, py, c))
             for j, (px, py) in enumerate(chips)]
    return pltpu.make_async_copy(x_ref.at[q_me], out_ref.at[q_me], local_sem), sends, recvs


def _start_all(local, sends):
    if local is not None:
        local.start()
    for cp in sends:
        cp.start()


def _finish_all(local, sends, recvs):
    for cp in recvs:
        cp.wait_recv()
    for cp in sends:
        cp.wait_send()
    if local is not None:
        local.wait()


def _sem_scratch(n, local):
    out = [pltpu.SemaphoreType.DMA((n,)), pltpu.SemaphoreType.DMA((n,))]
    return out + ([pltpu.SemaphoreType.DMA(())] if local else [])


def _fwd_in(x2d, mod3, g1, cwb, cb, gw, pk, seq, tm, nxt=None):
    t, d = x2d.shape
    nc = 6 * d
    tps = seq // tm
    n = t // tm
    lo = HALO - (CONV_TAPS - 1)
    sub = min(CONV_SUB, tm)
    comm = nxt is not None

    def body(*refs):
        x_ref, mod_ref, g_ref, cw_ref, cb_ref, gw_hbm = refs[:6]
        refs = refs[6:]
        if comm:
            nx_ref, refs = refs[0], refs[1:]
        proj_ref, zc_ref = refs[:2]
        refs = refs[2:]
        if comm:
            gwn_ref, refs = refs[0], refs[1:]
        w_vm, sems, zext, zsh, zc_buf, ztail = refs[:6]
        i = pl.program_id(0)

        if comm:
            ag = functools.partial(_ag_stage1, nx_ref, gwn_ref, *refs[6:9])

            @pl.when(i == 0)
            def _():
                local, sends, _ = ag()
                _start_all(local, sends)

        @pl.when(i == 0)
        def _():
            _load_rows(gw_hbm, w_vm, sems, 0, pk.off_in, pk.n_in)
            ztail[...] = jnp.zeros(ztail.shape, F32)

        x = x_ref[...]
        m = mod_ref[0]
        r = lax.rsqrt(_mean(x * x) + EPS)
        h = (x * r * g_ref[...] * (1.0 + m[:, d:2 * d]) + m[:, 0:d]).astype(BF16)

        def chunk(j):
            js = slice(j * MM_COLS, (j + 1) * MM_COLS)
            proj_ref[:, js] = _nt(h, w_vm[js, :]).astype(BF16)

        glu = range(2 * d // MM_COLS, 4 * d // MM_COLS)
        for j in glu:
            chunk(j)
        z = proj_ref[:, 2 * d:3 * d].astype(F32) * jax.nn.sigmoid(proj_ref[:, 3 * d:4 * d].astype(F32))
        zext[pl.ds(0, HALO), :] = jnp.where((i % tps) == 0, 0.0, ztail[...])
        zext[pl.ds(HALO, tm), :] = z
        ztail[...] = zext[pl.ds(tm, HALO), :]
        for s in range(tm // sub):
            _shift_copies(zext, zsh, s * sub, sub + HALO - 8)
            _conv_taps(zext, zsh, s * sub, cw_ref, [lo + k for k in range(CONV_TAPS)], cb_ref[...], zc_buf, sub, d)
        zc_ref[...] = zc_buf[...].astype(BF16)
        for j in range(nc // MM_COLS):
            if j not in glu:
                chunk(j)

        if comm:
            @pl.when(i == n - 1)
            def _():
                _finish_all(*ag())

    return pl.pallas_call(
        body, grid=(n,),
        in_specs=[_row_spec(tm, d), _mod_spec(tps, nc), _const_spec((1, d)), _const_spec((8 * HALO, d)),
                  _const_spec((1, d)), ANY] + ([ANY] if comm else []),
        out_specs=[_row_spec(tm, nc), _row_spec(tm, d)] + ([ANY] if comm else []),
        out_shape=[_sds((t, nc), BF16), _sds((t, d), BF16)] + ([_sds((N_DEV,) + nxt.shape, nxt.dtype)] if comm else []),
        scratch_shapes=[pltpu.VMEM((nc, d), BF16), pltpu.SemaphoreType.DMA((N_DEV,)),
                        pltpu.VMEM((tm + HALO, d), F32), pltpu.VMEM((7, sub + HALO, d), F32),
                        pltpu.VMEM((tm, d), F32), pltpu.VMEM((HALO, d), F32)]
        + (_sem_scratch(4, True) if comm else []),
        compiler_params=_params(1), name="fwd_in_ag" if comm else "fwd_in",
    )(x2d, mod3, g1, cwb, cb, gw, *([nxt] if comm else []))


def _shift_copies(src, sh, base, rows):
    for r in range(1, 8):
        sh[r - 1, pl.ds(0, rows), :] = src[pl.ds(base + r, rows), :]


def _window(src, sh, base, offset, start, size):
    r, q = offset % 8, offset // 8
    if r == 0:
        return src[pl.ds(base + start + 8 * q, size), :]
    return sh[r - 1, pl.ds(start + 8 * q, size), :]


def _conv_taps(src, sh, base, cwb_ref, offsets, bias, out_ref, rows, d):
    nsub = CONV_ROWS // 8
    for rb in range(rows // CONV_ROWS):
        accs = [jnp.broadcast_to(bias, (8, d))] * nsub
        for k in range(CONV_TAPS):
            w8 = cwb_ref[pl.ds(8 * k, 8), :]
            accs = [a + w8 * _window(src, sh, base, offsets[k], rb * CONV_ROWS + 8 * j, 8) for j, a in enumerate(accs)]
        for j, a in enumerate(accs):
            out_ref[pl.ds(base + rb * CONV_ROWS + 8 * j, 8), :] = a


def _conv_wgrad(z_ref, dsrc, dsh, acc_ref, rows, d):
    for k0 in range(0, CONV_TAPS, CONV_WGRAD_TAPS):
        taps = list(range(k0, min(k0 + CONV_WGRAD_TAPS, CONV_TAPS)))
        accs = [jnp.zeros((8, d), F32)] * len(taps)
        for rb in range(rows // 8):
            zblk = z_ref[pl.ds(rb * 8, 8), :]
            accs = [a + zblk * _window(dsrc, dsh, 0, CONV_TAPS - 1 - k, rb * 8, 8) for a, k in zip(accs, taps)]
        for a, k in zip(accs, taps):
            acc_ref[pl.ds(8 * k, 8), :] = acc_ref[pl.ds(8 * k, 8), :] + a


def _fwd_mix(proj, zc, x2d, mod3, vecs, wm, bsx, gw, pk, seq, tm):
    t, d = x2d.shape
    tps = seq // tm

    def body(proj_ref, zc_ref, x_ref, mod_ref, vec_ref, wm_ref, bs_ref, gw_hbm,
             x1_ref, ya_ref, yb_ref, o1_ref, aa_ref, ba_ref, mg_ref, wpa, wpb, wout, sems, vn_buf):
        @pl.when(pl.program_id(0) == 0)
        def _():
            _load_rows(gw_hbm, wpa, sems, 0, pk.off_pa, pk.n_p)
            _load_rows(gw_hbm, wpb, sems, N_DEV, pk.off_pb, pk.n_p)
            _load_rows(gw_hbm, wout, sems, 2 * N_DEV, pk.off_out, pk.n_p)

        m = mod_ref[0]
        vhat, _ = _ln_stats(proj_ref[:, d:2 * d].astype(F32))
        vn_buf[...] = (vhat * vec_ref[0:1, :] + vec_ref[1:2, :]).astype(BF16)
        for c in range(tm // CHUNK):
            rs_ = slice(c * CHUNK, (c + 1) * CHUNK)
            for h in range(HEADS):
                cs_ = slice(h * CHUNK, (h + 1) * CHUNK)
                s_b = _nn(wm_ref[h], vn_buf[rs_, cs_]) + bs_ref[:, cs_]
                aa_ref[rs_, cs_] = (proj_ref[rs_, cs_].astype(F32) * s_b).astype(BF16)
        y_a = _nn(aa_ref[...], wpa[...])
        ya_ref[...] = y_a.astype(BF16)
        zhat, _ = _ln_stats(zc_ref[...].astype(F32))
        zn = zhat * vec_ref[3:4, :] + vec_ref[4:5, :]
        b_act = (zn * jax.nn.sigmoid(zn)).astype(BF16)
        ba_ref[...] = b_act
        y_b = _nn(b_act, wpb[...])
        yb_ref[...] = y_b.astype(BF16)
        merged = (jax.nn.sigmoid(proj_ref[:, 4 * d:5 * d].astype(F32)) * y_a
                  + jax.nn.sigmoid(proj_ref[:, 5 * d:6 * d].astype(F32)) * y_b).astype(BF16)
        mg_ref[...] = merged
        o1 = _nn(merged, wout[...])
        o1_ref[...] = o1.astype(BF16)
        x1_ref[...] = x_ref[...] + m[:, 2 * d:3 * d] * o1

    act = _sds((t, d), BF16)
    return pl.pallas_call(
        body, grid=(t // tm,),
        in_specs=[_row_spec(tm, 6 * d), _row_spec(tm, d), _row_spec(tm, d), _mod_spec(tps, 6 * d), _const_spec((8, d)),
                  _const_spec((HEADS, CHUNK, CHUNK)), _const_spec((CHUNK, d)), ANY],
        out_specs=[_row_spec(tm, d)] * 7,
        out_shape=[_sds((t, d), F32), act, act, act, act, act, act],
        scratch_shapes=[pltpu.VMEM((d, d), BF16), pltpu.VMEM((d, d), BF16), pltpu.VMEM((d, d), BF16),
                        pltpu.SemaphoreType.DMA((3 * N_DEV,)), pltpu.VMEM((tm, d), BF16)],
        compiler_params=_params(1), name="fwd_mix",
    )(proj, zc, x2d, mod3, vecs, wm, bsx, gw)


def _fwd_ffn(x1, mod3, g2, gw, pk, seq, tm, gw_next=None):
    t, d = x1.shape
    nf = 4 * d
    tps = seq // tm
    n = t // tm
    comm = gw_next is not None

    def body(*refs):
        x_ref, mod_ref, g_ref, gw_hbm = refs[:4]
        refs = refs[4:]
        if comm:
            gwn_in, refs = refs[0], refs[1:]
        x2_ref, f_ref, o2_ref = refs[:3]
        refs = refs[3:]
        if comm:
            gwn_out, refs = refs[0], refs[1:]
        w1, w2, sems = refs[:3]
        i = pl.program_id(0)

        if comm:
            ag = functools.partial(_ag_stage2, gwn_in, gwn_out, *refs[3:5])

            @pl.when(i == 0)
            def _():
                _start_all(None, ag()[0])

        @pl.when(i == 0)
        def _():
            _load_rows(gw_hbm, w1, sems, 0, pk.off_ff1, pk.n_ff)
            _load_rows(gw_hbm, w2, sems, N_DEV, pk.off_ff2, pk.n_ff)

        x = x_ref[...]
        m = mod_ref[0]
        r = lax.rsqrt(_mean(x * x) + EPS)
        h2 = (x * r * g_ref[...] * (1.0 + m[:, 4 * d:5 * d]) + m[:, 3 * d:4 * d]).astype(BF16)
        acc = jnp.zeros(x.shape, F32)
        for j in range(nf // MM_COLS):
            js = slice(j * MM_COLS, (j + 1) * MM_COLS)
            f = _nt(h2, w1[js, :])
            f_ref[:, js] = f.astype(BF16)
            acc = acc + _nn(jnp.square(jnp.maximum(f, 0.0)).astype(BF16), w2[js, :])
        o2_ref[...] = acc.astype(BF16)
        x2_ref[...] = x + m[:, 5 * d:6 * d] * acc

        if comm:
            @pl.when(i == n - 1)
            def _():
                _finish_all(None, *ag())

    return pl.pallas_call(
        body, grid=(n,),
        in_specs=[_row_spec(tm, d), _mod_spec(tps, 6 * d), _const_spec((1, d)), ANY] + ([ANY] if comm else []),
        out_specs=[_row_spec(tm, d), _row_spec(tm, nf), _row_spec(tm, d)] + ([ANY] if comm else []),
        out_shape=[_sds((t, d), F32), _sds((t, nf), BF16), _sds((t, d), BF16)]
        + ([_sds(gw_next.shape, gw_next.dtype)] if comm else []),
        scratch_shapes=[pltpu.VMEM((nf, d), BF16), pltpu.VMEM((nf, d), BF16), pltpu.SemaphoreType.DMA((2 * N_DEV,))]
        + (_sem_scratch(3, False) if comm else []),
        input_output_aliases={4: 3} if comm else {},
        compiler_params=_params(1), name="fwd_ffn_ag" if comm else "fwd_ffn",
    )(x1, mod3, g2, gw, *([gw_next] if comm else []))


def _loss_head(x, tgt, fg, tm):
    t, d = x.shape
    n = t // tm

    def body(x_ref, t_ref, g_ref, dx_ref, loss_ref, dg_ref, lacc):
        i = pl.program_id(0)

        @pl.when(i == 0)
        def _():
            lacc[...] = jnp.zeros(lacc.shape, F32)
            dg_ref[...] = jnp.zeros(dg_ref.shape, F32)

        xv = x_ref[...]
        g = g_ref[...]
        r = lax.rsqrt(_mean(xv * xv) + EPS)
        err = xv * r * g - t_ref[...]
        lacc[...] = lacc[...] + _rowsum(err * err)
        dy = err * (1.0 / d)
        _add_row(dg_ref, (slice(0, 1), slice(None)), _rowsum(dy * xv * r))
        dx_ref[...] = _rms_bwd(dy, g, xv, r)

        @pl.when(i == n - 1)
        def _():
            loss_ref[...] = jnp.broadcast_to(jnp.sum(lacc[...], keepdims=True) * (0.5 / d), loss_ref.shape)

    return pl.pallas_call(
        body, grid=(n,),
        in_specs=[_row_spec(tm, d), _row_spec(tm, d), _const_spec((1, d))],
        out_specs=[_row_spec(tm, d), _const_spec((8, 128)), _const_spec((8, d))],
        out_shape=[_sds((t, d), F32), _sds((8, 128), F32), _sds((8, d), F32)],
        scratch_shapes=[pltpu.VMEM((1, d), F32)],
        compiler_params=_params(1), name="loss_head",
    )(x, tgt, fg)


def _sibling_copy(x_ref, out_ref, send_sem, recv_sem):
    x, y, c = _position()
    return pltpu.make_async_remote_copy(
        src_ref=x_ref.at[pl.ds(0, x_ref.shape[0]), 1 - c], dst_ref=out_ref, send_sem=send_sem, recv_sem=recv_sem,
        device_id=(x, y, 1 - c), device_id_type=MESH)


def _bwd_ffn(dx2, x1, f, o2, mod3, g2, gw, pk, seq, tm, dp=None):
    t, d = x1.shape
    nf = 4 * d
    tps = seq // tm
    nb = t // seq
    steps = t // tm
    comm = dp is not None

    def body(*refs):
        dx2_ref, x_ref, f_ref, o2_ref, mod_ref, g_ref, gw_hbm = refs[:7]
        refs = refs[7:]
        if comm:
            dp_ref, refs = refs[0], refs[1:]
        dx1_ref, df_ref, do2_ref, h2_ref, ms_ref, ps_ref = refs[:6]
        refs = refs[6:]
        if comm:
            got_ref, refs = refs[0], refs[1:]
        w1, w2, sems = refs[:3]
        i = pl.program_id(0)

        if comm:
            swap = functools.partial(_sibling_copy, dp_ref, got_ref, *refs[3:5])

            @pl.when(i == 0)
            def _():
                swap().start()

        @pl.when(i == 0)
        def _():
            _load_rows(gw_hbm, w1, sems, 0, pk.off_ff1, pk.n_ff)
            _load_rows(gw_hbm, w2, sems, N_DEV, pk.off_ff2, pk.n_ff)
            ps_ref[...] = jnp.zeros(ps_ref.shape, F32)

        @pl.when((i % tps) == 0)
        def _():
            ms_ref[...] = jnp.zeros(ms_ref.shape, F32)

        dx2 = dx2_ref[...]
        x = x_ref[...]
        m = mod_ref[0]
        g = g_ref[...]
        sh2, sc2, gt2 = m[:, 3 * d:4 * d], m[:, 4 * d:5 * d], m[:, 5 * d:6 * d]
        _add_row(ms_ref, (0, slice(2, 3), slice(None)), _rowsum(dx2 * o2_ref[...].astype(F32)))
        do2 = (gt2 * dx2).astype(BF16)
        do2_ref[...] = do2
        r = lax.rsqrt(_mean(x * x) + EPS)
        n = x * r * g
        h2_ref[...] = (n * (1.0 + sc2) + sh2).astype(BF16)
        dh = jnp.zeros(x.shape, F32)
        for j in range(nf // MM_COLS):
            js = slice(j * MM_COLS, (j + 1) * MM_COLS)
            dr = _nt(do2, w2[js, :])
            df = (dr * (2.0 * jnp.maximum(f_ref[:, js].astype(F32), 0.0))).astype(BF16)
            df_ref[:, js] = df
            dh = dh + _nn(df, w1[js, :])
        _add_row(ms_ref, (0, slice(0, 1), slice(None)), _rowsum(dh))
        _add_row(ms_ref, (0, slice(1, 2), slice(None)), _rowsum(dh * n))
        dn = dh * (1.0 + sc2)
        _add_row(ps_ref, (slice(0, 1), slice(None)), _rowsum(dn * x * r))
        dx1_ref[...] = dx2 + _rms_bwd(dn, g, x, r)

        if comm:
            @pl.when(i == steps - 1)
            def _():
                swap().wait()

    act = _sds((t, d), BF16)
    return pl.pallas_call(
        body, grid=(steps,),
        in_specs=[_row_spec(tm, d), _row_spec(tm, d), _row_spec(tm, nf), _row_spec(tm, d), _mod_spec(tps, 6 * d),
                  _const_spec((1, d)), ANY] + ([ANY] if comm else []),
        out_specs=[_row_spec(tm, d), _row_spec(tm, nf), _row_spec(tm, d), _row_spec(tm, d), _mstat_spec(tps, d),
                   _const_spec((8, d))] + ([ANY] if comm else []),
        out_shape=[_sds((t, d), F32), _sds((t, nf), BF16), act, act, _sds((nb, 8, d), F32), _sds((8, d), F32)]
        + ([_sds((dp.shape[0],) + dp.shape[2:], dp.dtype)] if comm else []),
        scratch_shapes=[pltpu.VMEM((nf, d), BF16), pltpu.VMEM((nf, d), BF16), pltpu.SemaphoreType.DMA((2 * N_DEV,))]
        + ([pltpu.SemaphoreType.DMA(()), pltpu.SemaphoreType.DMA(())] if comm else []),
        compiler_params=_params(1), name="bwd_ffn_rs" if comm else "bwd_ffn",
    )(dx2, x1, f, o2, mod3, g2, gw, *([dp] if comm else []))


def _bwd_mix(dx1, proj, ya, yb, o1, zc, mod3, vecs, wm, wmt, bsx, gw, pk, seq, tm):
    t, d = dx1.shape
    tps = seq // tm
    nb = t // seq
    n = t // tm

    def body(dx1_ref, proj_ref, ya_ref, yb_ref, o1_ref, zc_ref, mod_ref, vec_ref, wm_ref, wmt_ref, bs_ref, gw_hbm,
             dp_ref, dzc_ref, do1_ref, dya_ref, dyb_ref, ms_ref, ps_ref, dws_ref, dbs_ref,
             wpa, wpb, wout, sems, vn_buf, da_buf, dvn_buf):
        i = pl.program_id(0)

        @pl.when(i == 0)
        def _():
            _load_rows(gw_hbm, wpa, sems, 0, pk.off_pa, pk.n_p)
            _load_rows(gw_hbm, wpb, sems, N_DEV, pk.off_pb, pk.n_p)
            _load_rows(gw_hbm, wout, sems, 2 * N_DEV, pk.off_out, pk.n_p)
            ps_ref[...] = jnp.zeros(ps_ref.shape, F32)
            dws_ref[...] = jnp.zeros(dws_ref.shape, F32)
            dbs_ref[...] = jnp.zeros(dbs_ref.shape, F32)

        @pl.when((i % tps) == 0)
        def _():
            ms_ref[...] = jnp.zeros(ms_ref.shape, F32)

        m = mod_ref[0]
        dx1v = dx1_ref[...]
        _add_row(ms_ref, (0, slice(0, 1), slice(None)), _rowsum(dx1v * o1_ref[...].astype(F32)))
        do1 = (m[:, 2 * d:3 * d] * dx1v).astype(BF16)
        do1_ref[...] = do1
        dmg = _nt(do1, wout[...])
        sa = jax.nn.sigmoid(proj_ref[:, 4 * d:5 * d].astype(F32))
        sb = jax.nn.sigmoid(proj_ref[:, 5 * d:6 * d].astype(F32))
        y_a = ya_ref[...].astype(F32)
        y_b = yb_ref[...].astype(F32)
        dya32 = dmg * sa
        dyb32 = dmg * sb
        dya = dya32.astype(BF16)
        dyb = dyb32.astype(BF16)
        dya_ref[...] = dya
        dyb_ref[...] = dyb
        dp_ref[:, 4 * d:5 * d] = (dya32 * y_a * (1.0 - sa)).astype(BF16)
        dp_ref[:, 5 * d:6 * d] = (dyb32 * y_b * (1.0 - sb)).astype(BF16)
        da_buf[...] = _nt(dya, wpa[...])
        db = _nt(dyb, wpb[...])
        vhat, rs = _ln_stats(proj_ref[:, d:2 * d].astype(F32))
        alg = vec_ref[0:1, :]
        vn_buf[...] = (vhat * alg + vec_ref[1:2, :]).astype(BF16)
        for c in range(tm // CHUNK):
            rs_ = slice(c * CHUNK, (c + 1) * CHUNK)
            for h in range(HEADS):
                cs_ = slice(h * CHUNK, (h + 1) * CHUNK)
                vn_b = vn_buf[rs_, cs_]
                s_b = _nn(wm_ref[h], vn_b) + bs_ref[:, cs_]
                u_b = proj_ref[rs_, cs_].astype(F32)
                da_b = da_buf[rs_, cs_]
                dp_ref[rs_, cs_] = (da_b * s_b).astype(BF16)
                ds_b = da_b * u_b
                dbs_ref[:, cs_] = dbs_ref[:, cs_] + ds_b
                ds_bf = ds_b.astype(BF16)
                dvn_buf[rs_, cs_] = _nn(wmt_ref[h], ds_bf)
                dws_ref[:, cs_] = dws_ref[:, cs_] + _nt(ds_bf, vn_b)
        dvn = dvn_buf[...]
        _add_row(ps_ref, (slice(0, 1), slice(None)), _rowsum(dvn * vhat))
        _add_row(ps_ref, (slice(1, 2), slice(None)), _rowsum(dvn))
        dp_ref[:, d:2 * d] = _ln_bwd(dvn, alg, vhat, rs).astype(BF16)
        dp_ref[:, 2 * d:4 * d] = jnp.zeros((tm, 2 * d), BF16)
        zhat, rsb = _ln_stats(zc_ref[...].astype(F32))
        blg = vec_ref[3:4, :]
        zn = zhat * blg + vec_ref[4:5, :]
        sg = jax.nn.sigmoid(zn)
        dzn = db * (sg * (1.0 + zn * (1.0 - sg)))
        _add_row(ps_ref, (slice(2, 3), slice(None)), _rowsum(dzn * zhat))
        _add_row(ps_ref, (slice(3, 4), slice(None)), _rowsum(dzn))
        dzc = _ln_bwd(dzn, blg, zhat, rsb)
        _add_row(ps_ref, (slice(4, 5), slice(None)), _rowsum(dzc))
        dzc_ref[...] = dzc.astype(BF16)

        @pl.when(i == n - 1)
        def _():
            causal = (lax.broadcasted_iota(jnp.int32, (CHUNK, CHUNK), 0)
                      >= lax.broadcasted_iota(jnp.int32, (CHUNK, CHUNK), 1))
            for h in range(HEADS):
                cs_ = slice(h * CHUNK, (h + 1) * CHUNK)
                dws_ref[:, cs_] = jnp.where(causal, dws_ref[:, cs_], 0.0)
                dbs_ref[:, cs_] = jnp.broadcast_to(jnp.sum(dbs_ref[:, cs_], axis=1, keepdims=True), (CHUNK, CHUNK))

    act = _sds((t, d), BF16)
    return pl.pallas_call(
        body, grid=(n,),
        in_specs=[_row_spec(tm, d), _row_spec(tm, 6 * d), _row_spec(tm, d), _row_spec(tm, d), _row_spec(tm, d),
                  _row_spec(tm, d), _mod_spec(tps, 6 * d), _const_spec((8, d)), _const_spec((HEADS, CHUNK, CHUNK)),
                  _const_spec((HEADS, CHUNK, CHUNK)), _const_spec((CHUNK, d)), ANY],
        out_specs=[_row_spec(tm, 6 * d)] + [_row_spec(tm, d)] * 4
        + [_mstat_spec(tps, d), _const_spec((8, d)), _const_spec((CHUNK, d)), _const_spec((CHUNK, d))],
        out_shape=[_sds((t, 6 * d), BF16)] + [act] * 4
        + [_sds((nb, 8, d), F32), _sds((8, d), F32), _sds((CHUNK, d), F32), _sds((CHUNK, d), F32)],
        scratch_shapes=[pltpu.VMEM((d, d), BF16), pltpu.VMEM((d, d), BF16), pltpu.VMEM((d, d), BF16),
                        pltpu.SemaphoreType.DMA((3 * N_DEV,)),
                        pltpu.VMEM((tm, d), BF16), pltpu.VMEM((tm, d), F32), pltpu.VMEM((tm, d), F32)],
        compiler_params=_params(1), name="bwd_mix",
    )(dx1, proj, ya, yb, o1, zc, mod3, vecs, wm, wmt, bsx, gw)


def _bwd_in(dproj, dzc, proj, x2d, dx1, mod3, g1, cw, gw, pk, seq, tm, part=None):
    t, d = x2d.shape
    tps = seq // tm
    nb = t // seq
    n = t // tm
    hb = tm // HALO
    comm = part is not None

    def body(*refs):
        dpi_ref, dzc_ref, dzn_ref, pp_ref, x_ref, dx1_ref, mod_ref, g_ref, cw_ref, gw_hbm = refs[:10]
        refs = refs[10:]
        if comm:
            part_ref, refs = refs[0], refs[1:]
        dpo_ref, dx_ref, h_ref, ms_ref, ps_ref, dcw_ref = refs[:6]
        refs = refs[6:]
        if comm:
            got_ref, refs = refs[0], refs[1:]
        w_vm, sems, z_buf, dzext, dsh, dz_buf, dcw_acc = refs[:7]
        i = pl.program_id(0)

        if comm:
            rs = functools.partial(_rs_chip_copies, part_ref, got_ref, *refs[7:10])

            @pl.when(i == 0)
            def _():
                local, sends, _ = rs()
                _start_all(local, sends)

        @pl.when(i == 0)
        def _():
            _load_rows(gw_hbm, w_vm, sems, 0, pk.off_in, pk.n_in)
            ps_ref[...] = jnp.zeros(ps_ref.shape, F32)
            dcw_acc[...] = jnp.zeros(dcw_acc.shape, F32)

        last = (i % tps) == tps - 1

        @pl.when((i % tps) == 0)
        def _():
            ms_ref[...] = jnp.zeros(ms_ref.shape, F32)

        pa = pp_ref[:, 0:d].astype(F32)
        sgp = jax.nn.sigmoid(pp_ref[:, d:2 * d].astype(F32))
        z_buf[...] = pa * sgp
        dzext[pl.ds(0, tm), :] = dzc_ref[...].astype(F32)
        dzext[pl.ds(tm, HALO), :] = jnp.where(last, 0.0, dzn_ref[...].astype(F32))
        _shift_copies(dzext, dsh, 0, tm + HALO - 8)
        _conv_wgrad(z_buf, dzext, dsh, dcw_acc, tm, d)
        _conv_taps(dzext, dsh, 0, cw_ref, [CONV_TAPS - 1 - k for k in range(CONV_TAPS)], jnp.zeros((1, d), F32),
                   dz_buf, tm, d)
        dz = dz_buf[...]
        dpa = (dz * sgp).astype(BF16)
        dpg = (dz * pa * sgp * (1.0 - sgp)).astype(BF16)
        dpo_ref[:, 0:d] = dpa
        dpo_ref[:, d:2 * d] = dpg
        dh = (_nn(dpi_ref[:, 0:2 * d], w_vm[0:2 * d, :]) + _nn(dpa, w_vm[2 * d:3 * d, :])
              + _nn(dpg, w_vm[3 * d:4 * d, :]) + _nn(dpi_ref[:, 4 * d:6 * d], w_vm[4 * d:6 * d, :]))
        x = x_ref[...]
        m = mod_ref[0]
        g = g_ref[...]
        sh1, sc1 = m[:, 0:d], m[:, d:2 * d]
        r = lax.rsqrt(_mean(x * x) + EPS)
        nrm = x * r * g
        h_ref[...] = (nrm * (1.0 + sc1) + sh1).astype(BF16)
        _add_row(ms_ref, (0, slice(0, 1), slice(None)), _rowsum(dh))
        _add_row(ms_ref, (0, slice(1, 2), slice(None)), _rowsum(dh * nrm))
        dn = dh * (1.0 + sc1)
        _add_row(ps_ref, (slice(0, 1), slice(None)), _rowsum(dn * x * r))
        dx_ref[...] = dx1_ref[...] + _rms_bwd(dn, g, x, r)

        @pl.when(i == n - 1)
        def _():
            for k in range(CONV_TAPS):
                dcw_ref[k:k + 1, :] = _rowsum(dcw_acc[pl.ds(k * 8, 8), :])
            dcw_ref[CONV_TAPS:HALO, :] = jnp.zeros((HALO - CONV_TAPS, d), F32)

        if comm:
            @pl.when(i == n - 1)
            def _():
                _finish_all(*rs())

    halo_next = pl.BlockSpec((HALO, d), lambda i: (jnp.minimum((i + 1) * hb, t // HALO - 1), 0))
    return pl.pallas_call(
        body, grid=(n,),
        in_specs=[_row_spec(tm, 6 * d), _row_spec(tm, d), halo_next, _row_spec(tm, 2 * d, 1),
                  _row_spec(tm, d), _row_spec(tm, d), _mod_spec(tps, 6 * d), _const_spec((1, d)),
                  _const_spec((8 * HALO, d)), ANY] + ([ANY] if comm else []),
        out_specs=[_row_spec(tm, 2 * d, 1), _row_spec(tm, d), _row_spec(tm, d), _mstat_spec(tps, d),
                   _const_spec((8, d)), _const_spec((HALO, d))] + ([ANY] if comm else []),
        out_shape=[_sds((t, 6 * d), BF16), _sds((t, d), F32), _sds((t, d), BF16), _sds((nb, 8, d), F32),
                   _sds((8, d), F32), _sds((HALO, d), F32)] + ([_sds(part.shape, part.dtype)] if comm else []),
        scratch_shapes=[pltpu.VMEM((6 * d, d), BF16), pltpu.SemaphoreType.DMA((N_DEV,)),
                        pltpu.VMEM((tm, d), F32), pltpu.VMEM((tm + HALO, d), F32),
                        pltpu.VMEM((7, tm + HALO, d), F32),
                        pltpu.VMEM((tm, d), F32), pltpu.VMEM((CONV_TAPS * 8, d), F32)]
        + (_sem_scratch(3, True) if comm else []),
        input_output_aliases={0: 0},
        compiler_params=_params(1), name="bwd_in_rs" if comm else "bwd_in",
    )(dproj, dzc, dzc, proj, x2d, dx1, mod3, g1, cw, gw, *([part] if comm else []))


def _wgrad(a, b, pack, kb, off, relu2, name, part=None):
    t, mo = a.shape
    nn_ = b.shape[1]
    rows = mo // N_DEV
    tk = min(TK_WGRAD_ONE if kb == N_DEV else TK_WGRAD, t)
    nk = t // tk
    ni = N_DEV // kb
    assert off % rows == 0 and N_DEV % kb == 0
    comm = part is not None

    def body(*refs):
        a_ref, b_ref, pack_hbm = refs[:3]
        refs = refs[3:]
        if comm:
            part_ref, refs = refs[0], refs[1:]
        o_ref, refs = refs[0], refs[1:]
        if comm:
            got_ref, refs = refs[0], refs[1:]
        acc = refs[0]
        i = pl.program_id(0)
        k = pl.program_id(1)

        if comm:
            rs = functools.partial(_rs_chip_copies, part_ref, got_ref, *refs[1:4])

            @pl.when((i == 0) & (k == 0))
            def _():
                local, sends, _ = rs()
                _start_all(local, sends)

        @pl.when(k == 0)
        def _():
            acc[...] = jnp.zeros(acc.shape, F32)

        av = a_ref[...]
        if relu2:
            av = jnp.square(jnp.maximum(av, 0.0))
        acc[...] = acc[...] + _tn(av, b_ref[...])

        @pl.when(k == nk - 1)
        def _():
            for j in range(kb):
                o_ref[j] = acc[pl.ds(j * rows, rows), :].astype(BF16)

        if comm:
            @pl.when((i == ni - 1) & (k == nk - 1))
            def _():
                _finish_all(*rs())

    return pl.pallas_call(
        body, grid=(ni, nk),
        in_specs=[pl.BlockSpec((tk, kb * rows), lambda i, k: (k, i)), pl.BlockSpec((tk, nn_), lambda i, k: (k, 0)), ANY]
        + ([ANY] if comm else []),
        out_specs=[pl.BlockSpec((kb, rows, nn_), lambda i, k: (i, off // rows, 0))] + ([ANY] if comm else []),
        out_shape=[_sds(pack.shape, BF16)] + ([_sds(part.shape, part.dtype)] if comm else []),
        scratch_shapes=[pltpu.VMEM((kb * rows, nn_), F32)] + (_sem_scratch(3, True) if comm else []),
        input_output_aliases={2: 0},
        compiler_params=_params(2), name=name,
    )(a, b, pack, *([part] if comm else []))


def _place_rows(pack, rows_blk, off):
    nblk, r, nn_ = rows_blk.shape
    assert off % r == 0

    def body(pack_hbm, s_ref, o_ref):
        o_ref[...] = s_ref[...]

    return pl.pallas_call(
        body, grid=(1,),
        in_specs=[ANY, pl.BlockSpec((nblk, r, nn_), lambda i: (0, 0, 0))],
        out_specs=pl.BlockSpec((nblk, r, nn_), lambda i: (0, off // r, 0)),
        out_shape=_sds(pack.shape, pack.dtype),
        input_output_aliases={0: 0},
        compiler_params=_params(1), name="place_small",
    )(pack, rows_blk)


def _mod_fwd(c_all, w_ada, b_cols):
    nl, d, cols = w_ada.shape
    bsz = c_all.shape[0]

    def body(c_ref, w_ref, b_ref, o_ref):
        cv = c_ref[...]
        ca = cv * jax.nn.sigmoid(cv)
        o_ref[0] = jnp.dot(ca, w_ref[0], preferred_element_type=F32, precision=lax.Precision.HIGHEST) + b_ref[0]

    return pl.pallas_call(
        body, grid=(nl,),
        in_specs=[_const_spec((bsz, d)), pl.BlockSpec((1, d, cols), lambda l: (l, 0, 0)),
                  pl.BlockSpec((1, 1, cols), lambda l: (l, 0, 0))],
        out_specs=pl.BlockSpec((1, bsz, cols), lambda l: (l, 0, 0)),
        out_shape=_sds((nl, bsz, cols), F32),
        compiler_params=_params(1), name="mod_fwd",
    )(c_all, w_ada, b_cols)


def _mod_bwd(c_all, dmod_cols, dmod_all):
    nl, bsz, cols = dmod_cols.shape
    d = c_all.shape[1]
    ncol = dmod_all.shape[2]

    def body(c_ref, dc_ref, da_ref, dw_ref, db_ref):
        cv = c_ref[...]
        ca = cv * jax.nn.sigmoid(cv)
        dw_ref[0] = lax.dot_general(ca, dc_ref[0], (((0,), (0,)), ((), ())), preferred_element_type=F32,
                                    precision=lax.Precision.HIGHEST)
        db_ref[0] = _rowsum(da_ref[0])

    return pl.pallas_call(
        body, grid=(nl,),
        in_specs=[_const_spec((bsz, d)), pl.BlockSpec((1, bsz, cols), lambda l: (l, 0, 0)),
                  pl.BlockSpec((1, bsz, ncol), lambda l: (l, 0, 0))],
        out_specs=[pl.BlockSpec((1, d, cols), lambda l: (l, 0, 0)), pl.BlockSpec((1, 1, ncol), lambda l: (l, 0, 0))],
        out_shape=[_sds((nl, d, cols), F32), _sds((nl, 1, ncol), F32)],
        compiler_params=_params(1), name="mod_bwd",
    )(c_all, dmod_cols, dmod_all)


def _row_tile(rows, cols, nbuf, itemsize=4, budget=24 * 1024 * 1024):
    cap = max(16, budget // (2 * nbuf * cols * itemsize))
    if rows <= cap:
        return rows
    best = None
    for tr in range(16, cap + 1, 16):
        if rows % tr == 0:
            best = tr
    assert best is not None, (rows, cols)
    return best


def _sum_blocks(xs, name):
    nblk, rows, cols = xs.shape
    tr = _row_tile(rows, cols, nblk + 1)

    def body(x_ref, o_ref):
        acc = x_ref[0].astype(F32)
        for j in range(1, nblk):
            acc = acc + x_ref[j].astype(F32)
        o_ref[...] = acc

    return pl.pallas_call(
        body, grid=(rows // tr,),
        in_specs=[pl.BlockSpec((nblk, tr, cols), lambda i: (0, i, 0))],
        out_specs=pl.BlockSpec((tr, cols), lambda i: (i, 0)),
        out_shape=_sds((rows, cols), F32),
        compiler_params=_params(1), name=name,
    )(xs)


def _add_sibling(dp, recv, core):
    nq, _, rows, cols = dp.shape
    tr = _row_tile(rows, cols, 3, itemsize=2)

    def body(c_ref, a_ref, b_ref, o_ref):
        o_ref[...] = (a_ref[...].astype(F32) + b_ref[...].astype(F32)).astype(BF16)

    return pl.pallas_call(
        body,
        grid_spec=pltpu.PrefetchScalarGridSpec(
            num_scalar_prefetch=1, grid=(nq, rows // tr),
            in_specs=[pl.BlockSpec((1, 1, tr, cols), lambda q, i, c: (q, c[0], i, 0)),
                      pl.BlockSpec((1, 1, tr, cols), lambda q, i, c: (q, 0, i, 0))],
            out_specs=pl.BlockSpec((1, 1, tr, cols), lambda q, i, c: (q, 0, i, 0))),
        out_shape=_sds((nq, 1, rows, cols), BF16),
        compiler_params=_params(2), name="add_sibling",
    )(core, dp, recv.reshape(nq, 1, rows, cols)).reshape(nq, rows, cols)


def _adamw(w, g, m, v, name):
    rows, cols = w.shape
    tr = _row_tile(rows, cols, 7)
    c1 = 1.0 - ADAM_B1 ** ADAM_STEP
    c2 = 1.0 - ADAM_B2 ** ADAM_STEP

    def body(w_ref, g_ref, m_ref, v_ref, d_ref, nm_ref, nv_ref):
        gv = g_ref[...]
        nm = ADAM_B1 * m_ref[...] + (1.0 - ADAM_B1) * gv
        nv = ADAM_B2 * v_ref[...] + (1.0 - ADAM_B2) * (gv * gv)
        nm_ref[...] = nm
        nv_ref[...] = nv
        d_ref[...] = -ADAM_LR * ((nm / c1) / (jnp.sqrt(nv / c2) + ADAM_EPS) + ADAM_WD * w_ref[...])

    spec = pl.BlockSpec((tr, cols), lambda i: (i, 0))
    out = _sds((rows, cols), F32)
    return pl.pallas_call(
        body, grid=(rows // tr,), in_specs=[spec] * 4, out_specs=[spec] * 3, out_shape=[out] * 3,
        compiler_params=_params(1), name=name,
    )(w, g, m, v)


def _all_gather(xs, name):
    return _all_gather_many([xs], name)[0]


def _all_gather_many(arrays, name):
    n = len(arrays)

    def body(*refs):
        x_refs, out_refs, sems = refs[:n], refs[n:2 * n], refs[2 * n:]
        gathers = []
        for a in range(n):
            send1, recv1, local_sem, send2, recv2 = sems[5 * a:5 * a + 5]
            local, first, arrivals = _ag_stage1(x_refs[a], out_refs[a], send1, recv1, local_sem)
            _start_all(local, first)
            passed, from_sibling = _ag_stage2(out_refs[a], out_refs[a], send2, recv2)
            gathers.append((local, first, arrivals, passed, from_sibling))
        for local, first, arrivals, passed, from_sibling in gathers:
            for arrival, onward in zip(arrivals[1:], passed):
                arrival.wait_recv()
                onward.start()
        for local, first, arrivals, passed, from_sibling in gathers:
            arrivals[0].wait_recv()
            _finish_all(local, first + passed, from_sibling)

    return pl.pallas_call(
        body, out_shape=[_sds((N_DEV,) + a.shape, a.dtype) for a in arrays], in_specs=[ANY] * n, out_specs=[ANY] * n,
        scratch_shapes=(_sem_scratch(4, True) + _sem_scratch(3, False)) * n, name=name,
    )(*arrays)


def _sibling_exchange(dp):
    nq, _, rows, cols = dp.shape

    def body(x_ref, out_ref, send_sem, recv_sem):
        cp = _sibling_copy(x_ref, out_ref, send_sem, recv_sem)
        cp.start()
        cp.wait()

    return pl.pallas_call(
        body, out_shape=_sds((nq, rows, cols), dp.dtype), in_specs=[ANY], out_specs=ANY,
        scratch_shapes=[pltpu.SemaphoreType.DMA(()), pltpu.SemaphoreType.DMA(())],
        name="rs_sibling",
    )(dp)


def _chip_all_to_all(xs):
    nq, rows, cols = xs.shape

    def body(x_ref, out_ref, send_sems, recv_sems, local_sem):
        local, sends, recvs = _rs_chip_copies(x_ref, out_ref, send_sems, recv_sems, local_sem)
        _start_all(local, sends)
        _finish_all(local, sends, recvs)

    return pl.pallas_call(
        body, out_shape=_sds((nq, rows, cols), xs.dtype), in_specs=[ANY], out_specs=ANY,
        scratch_shapes=_sem_scratch(3, True), name="rs_chips",
    )(xs)


def _pad_rows(a, rows):
    return jnp.pad(a, ((0, rows - a.shape[0]), (0, 0)))


def kernel(x, c, w_ada, b_ada, norm1_g, w_in, a_ln_g, a_ln_b, a_ws, a_bs, w_pa, b_conv_w, b_conv_b, b_ln_g, b_ln_b, w_pb, w_out, norm2_g, w_ff1, w_ff2, final_g, loss_target, m_w_ada, m_b_ada, m_norm1_g, m_w_in, m_a_ln_g, m_a_ln_b, m_a_ws, m_a_bs, m_w_pa, m_b_conv_w, m_b_conv_b, m_b_ln_g, m_b_ln_b, m_w_pb, m_w_out, m_norm2_g, m_w_ff1, m_w_ff2, m_final_g, v_w_ada, v_b_ada, v_norm1_g, v_w_in, v_a_ln_g, v_a_ln_b, v_a_ws, v_a_bs, v_w_pa, v_b_conv_w, v_b_conv_b, v_b_ln_g, v_b_ln_b, v_w_pb, v_w_out, v_norm2_g, v_w_ff1, v_w_ff2, v_final_g):
    nb, seq, d = x.shape
    nl = w_in.shape[0]
    t = nb * seq
    pk = _Pack(d, 0)
    gk = _Pack(d, SMALL_SLOT)
    assert d % (N_DEV * CHUNK) == 0 and d // HEADS == CHUNK and seq % CHUNK == 0
    tm_big = min(TM_BIG, seq)
    tm_mix = min(TM_MIX, seq)
    ax, ay, ac = _position()
    dev = 4 * ax + 2 * ay + ac
    ncol = 6 * d
    cols = ncol // N_DEV
    cpd = d // N_DEV
    bsz = nb * N_DEV

    cw_rows = nl * HALO
    small = jnp.concatenate([
        c.reshape(nb * d // CHUNK, CHUNK),
        jnp.pad(b_conv_w.reshape(nl, CONV_TAPS, cpd), ((0, 0), (0, HALO - CONV_TAPS), (0, 0))).reshape(cw_rows, cpd),
    ], axis=0)
    c_rows = nb * d // CHUNK
    small_all = _all_gather(small, "ag_small")
    c_all = small_all[:, :c_rows].reshape(bsz, d)
    cw_all = small_all[:, c_rows:].reshape(N_DEV, nl, HALO, cpd).transpose(1, 2, 0, 3).reshape(nl, HALO, d)
    cwb_all = jnp.repeat(cw_all, 8, axis=1)
    b_cols = lax.dynamic_slice_in_dim(b_ada, dev * cols, cols, axis=1).reshape(nl, 1, cols)
    mod_cols = _mod_fwd(c_all, w_ada, b_cols)

    causal = jnp.tril(jnp.ones((CHUNK, CHUNK), bool))
    wm_all = jnp.where(causal[None, None], a_ws, 0.0)
    wm_bf = wm_all.astype(BF16)
    wmt_bf = jnp.swapaxes(wm_all, 2, 3).astype(BF16)
    bsx_all = jnp.broadcast_to(jnp.swapaxes(a_bs, 1, 2)[:, :, :, None], (nl, CHUNK, HEADS, CHUNK)).reshape(nl, CHUNK, d)

    def vec_rows(l):
        return _pad_rows(jnp.stack([a_ln_g[l], a_ln_b[l], b_conv_b[l], b_ln_g[l], b_ln_b[l]]), 8)

    def weight_block(l):
        return jnp.concatenate([
            w_ff1[l].T, w_ff2[l], w_pa[l], w_pb[l], w_out[l], w_in[l].T], axis=0).astype(BF16)

    xs = x.reshape(t, d)
    saved = []
    mod_all, gw = _all_gather_many([mod_cols.reshape(nl * bsz, cols), weight_block(0)], "ag_weights")
    mod_all = mod_all.reshape(N_DEV, nl, bsz, cols).transpose(1, 2, 0, 3).reshape(nl, bsz, ncol)
    mod_mine = lax.dynamic_slice_in_dim(mod_all, dev * nb, nb, axis=1)
    for l in range(nl):
        nxt = weight_block(l + 1) if l + 1 < nl else None
        mod3 = mod_mine[l].reshape(nb, 1, ncol)
        vecs = vec_rows(l)
        proj, zc, *gw_next = _fwd_in(xs, mod3, norm1_g[l].reshape(1, d), cwb_all[l], b_conv_b[l].reshape(1, d), gw, pk,
                                     seq, tm_big, nxt)
        x1, ya, yb, o1, aa, ba, mg = _fwd_mix(proj, zc, xs, mod3, vecs, wm_bf[l], bsx_all[l], gw, pk, seq, tm_big)
        x2, f, o2, *gw_next = _fwd_ffn(x1, mod3, norm2_g[l].reshape(1, d), gw, pk, seq, tm_big, *gw_next)
        saved.append((xs, x1, proj, ya, yb, o1, zc, f, o2, gw, mod3, vecs, aa, ba, mg))
        xs = x2
        if gw_next:
            gw = gw_next[0]

    dx, loss_blk, dfg = _loss_head(xs, loss_target.reshape(t, d), final_g.reshape(1, d), tm_big)
    loss = lax.psum(loss_blk[0, 0], ("x", "y", "c"))

    core = ac.reshape(1).astype(jnp.int32)
    wg = {k: [None] * nl for k in ("w_in", "w_ff1", "w_ff2", "w_pa", "w_pb", "w_out")}
    small_red = [None] * nl
    dmod_rows = [None] * nl
    per_layer = 8 + 2 * CHUNK + HALO
    assert per_layer <= N_DEV * SMALL_ROWS <= N_DEV * SMALL_SLOT

    rows_a = gk.off_in

    def reduced(l, red):
        reduced_a(l, red[:rows_a])
        wg["w_in"][l] = red[rows_a:].T

    def reduced_a(l, red):
        wg["w_ff1"][l] = red[gk.off_ff1:gk.off_ff1 + gk.n_ff].T
        wg["w_ff2"][l] = red[gk.off_ff2:gk.off_ff2 + gk.n_ff]
        wg["w_pa"][l] = red[gk.off_pa:gk.off_pa + gk.n_p]
        wg["w_pb"][l] = red[gk.off_pb:gk.off_pb + gk.n_p]
        wg["w_out"][l] = red[gk.off_out:gk.off_out + gk.n_p]
        small_red[l] = red[gk.off_small:gk.off_small + SMALL_ROWS]

    waiting = None
    pending = None
    for l in reversed(range(nl)):
        x0, x1, proj, ya, yb, o1, zc, f, o2, gw, mod3, vecs, aa, ba, mg = saved[l]
        dx1, df, do2, h2, ms2, ps2, *got = _bwd_ffn(dx, x1, f, o2, mod3, norm2_g[l].reshape(1, d), gw, pk, seq, tm_big,
                                                    None if waiting is None else waiting[1])
        if waiting is not None:
            pending = (waiting[0], _add_sibling(waiting[1], got[0], core))
        (dproj, dzc, do1, dya, dyb, ms1, ps1, dws, dbs) = _bwd_mix(
            dx1, proj, ya, yb, o1, zc, mod3, vecs, wm_bf[l], wmt_bf[l], bsx_all[l], gw, pk, seq, tm_mix)
        apart = l == 0
        grads = lax.empty((N_DEV, rows_a if apart else gk.rows, d), BF16)
        grads, = _wgrad(df, h2, grads, 2, gk.off_ff1, False, "wgrad_ff1")
        grads, = _wgrad(f, do2, grads, 2, gk.off_ff2, True, "wgrad_ff2")
        dproj, dx, h, ms0, ps0, dcw, *got = _bwd_in(
            dproj, dzc, proj, x0, dx1, mod3, norm1_g[l].reshape(1, d), cwb_all[l], gw, pk, seq, tm_mix,
            None if pending is None else pending[1])
        if pending is not None:
            reduced(pending[0], _sum_blocks(got[0], "sum_chips"))
        vec_g = jnp.concatenate([ps0[0:1], ps1[0:5], ps2[0:1], jnp.zeros((1, d), F32)], axis=0)
        small = _pad_rows(jnp.concatenate([vec_g, dws, dbs, dcw], axis=0), N_DEV * SMALL_ROWS)
        small = jnp.pad(small.reshape(N_DEV, SMALL_ROWS, d).astype(BF16), ((0, 0), (0, SMALL_SLOT - SMALL_ROWS), (0, 0)))
        grads = _place_rows(grads, small, gk.off_small)
        grads, = _wgrad(aa, dya, grads, N_DEV, gk.off_pa, False, "wgrad_pa")
        grads, = _wgrad(ba, dyb, grads, N_DEV, gk.off_pb, False, "wgrad_pb")
        grads, = _wgrad(mg, do1, grads, N_DEV, gk.off_out, False, "wgrad_out")
        if apart:
            dp_a = grads.reshape(N_CHIP, 2, rows_a, d)
            part_a = _add_sibling(dp_a, _sibling_exchange(dp_a), core)
            g_in, got_a = _wgrad(dproj, h, lax.empty((N_DEV, gk.n_in, d), BF16), 1, 0, False, "wgrad_in_rs", part_a)
            reduced_a(l, _sum_blocks(got_a, "sum_chips_a"))
            dp_b = g_in.reshape(N_CHIP, 2, gk.n_in, d)
            part_b = _add_sibling(dp_b, _sibling_exchange(dp_b), core)
            wg["w_in"][l] = _sum_blocks(_chip_all_to_all(part_b), "sum_chips_b").T
        else:
            grads, = _wgrad(dproj, h, grads, 1, gk.off_in, False, "wgrad_in")
            waiting = (l, grads.reshape(N_CHIP, 2, gk.rows, d))
        pending = None
        dmod_rows[l] = jnp.concatenate([ms0[:, 0], ms0[:, 1], ms1[:, 0], ms2[:, 0], ms2[:, 1], ms2[:, 2]], axis=1)

    n_sm = nl * SMALL_ROWS
    n_dm = nl * nb * 6
    tail = jnp.concatenate(small_red + [jnp.stack(dmod_rows).reshape(n_dm, d), dfg], axis=0)
    tail_all = _all_gather(tail, "ag_small_grads")
    lay = tail_all[:, :n_sm].reshape(N_DEV, nl, SMALL_ROWS, d).transpose(1, 0, 2, 3).reshape(nl, N_DEV * SMALL_ROWS, d)
    tail_all = tail_all[:, n_sm:]
    dmod_all = tail_all[:, :n_dm].reshape(N_DEV, nl, nb, ncol).transpose(1, 0, 2, 3).reshape(nl, bsz, ncol)
    dmod_cols = lax.dynamic_slice_in_dim(dmod_all, dev * cols, cols, axis=2)
    g_w_ada, g_b_ada = _mod_bwd(c_all, dmod_cols, dmod_all)
    g_final = _sum_blocks(tail_all[:, n_dm:], "sum_final_g")[0]

    g_small = {
        "norm1_g": lay[:, 0], "a_ln_g": lay[:, 1], "a_ln_b": lay[:, 2], "b_ln_g": lay[:, 3], "b_ln_b": lay[:, 4],
        "b_conv_b": lay[:, 5], "norm2_g": lay[:, 6],
        "a_ws": lay[:, 8:8 + CHUNK].reshape(nl, CHUNK, HEADS, CHUNK).transpose(0, 2, 1, 3),
        "a_bs": jnp.swapaxes(lay[:, 8 + CHUNK:8 + 2 * CHUNK, ::CHUNK], 1, 2),
        "b_conv_w": lax.dynamic_slice_in_dim(
            lay[:, 8 + 2 * CHUNK:8 + 2 * CHUNK + CONV_TAPS], dev * cpd, cpd, axis=2).reshape(nl, CONV_TAPS, 1, cpd),
        "final_g": g_final,
    }
    grads = dict(g_small)
    grads["w_ada"] = g_w_ada
    grads["b_ada"] = g_b_ada.reshape(nl, ncol)
    for k, v in wg.items():
        grads[k] = jnp.stack(v)

    names = ["w_ada", "b_ada", "norm1_g", "w_in", "a_ln_g", "a_ln_b", "a_ws", "a_bs", "w_pa", "b_conv_w", "b_conv_b",
             "b_ln_g", "b_ln_b", "w_pb", "w_out", "norm2_g", "w_ff1", "w_ff2", "final_g"]
    weights = dict(w_ada=w_ada, b_ada=b_ada, norm1_g=norm1_g, w_in=w_in, a_ln_g=a_ln_g, a_ln_b=a_ln_b, a_ws=a_ws,
                   a_bs=a_bs, w_pa=w_pa, b_conv_w=b_conv_w, b_conv_b=b_conv_b, b_ln_g=b_ln_g, b_ln_b=b_ln_b,
                   w_pb=w_pb, w_out=w_out, norm2_g=norm2_g, w_ff1=w_ff1, w_ff2=w_ff2, final_g=final_g)
    m_in = dict(w_ada=m_w_ada, b_ada=m_b_ada, norm1_g=m_norm1_g, w_in=m_w_in, a_ln_g=m_a_ln_g, a_ln_b=m_a_ln_b,
                a_ws=m_a_ws, a_bs=m_a_bs, w_pa=m_w_pa, b_conv_w=m_b_conv_w, b_conv_b=m_b_conv_b, b_ln_g=m_b_ln_g,
                b_ln_b=m_b_ln_b, w_pb=m_w_pb, w_out=m_w_out, norm2_g=m_norm2_g, w_ff1=m_w_ff1, w_ff2=m_w_ff2,
                final_g=m_final_g)
    v_in = dict(w_ada=v_w_ada, b_ada=v_b_ada, norm1_g=v_norm1_g, w_in=v_w_in, a_ln_g=v_a_ln_g, a_ln_b=v_a_ln_b,
                a_ws=v_a_ws, a_bs=v_a_bs, w_pa=v_w_pa, b_conv_w=v_b_conv_w, b_conv_b=v_b_conv_b, b_ln_g=v_b_ln_g,
                b_ln_b=v_b_ln_b, w_pb=v_w_pb, w_out=v_w_out, norm2_g=v_norm2_g, w_ff1=v_w_ff1, w_ff2=v_w_ff2,
                final_g=v_final_g)

    deltas, new_m, new_v = {}, {}, {}
    for k in names:
        shape = weights[k].shape
        two_d = (-1, shape[-1])
        g2d = grads[k].reshape(shape).reshape(two_d)
        grads[k] = grads[k].reshape(shape)
        dl, nm, nv = _adamw(weights[k].reshape(two_d), g2d, m_in[k].reshape(two_d), v_in[k].reshape(two_d),
                            "adamw_" + k)
        deltas[k], new_m[k], new_v[k] = dl.reshape(shape), nm.reshape(shape), nv.reshape(shape)

    return (loss, dx.reshape(nb, seq, d), *[grads[k] for k in names], *[deltas[k] for k in names],
            *[new_m[k] for k in names], *[new_v[k] for k in names])
```
